```python
import jax, jax.numpy as jnp
from jax import lax
import numpy as np

D_MODEL = 2048
BATCH = 8
SEQ = 4096
DEPTH = 4

N_MIXERS = 3
EXPAND = 2
D_INNER = EXPAND * D_MODEL
CHUNK = 128
SGU_HEADS = 8
SGU_HEAD_DIM = D_INNER // SGU_HEADS
POOL_WINDOWS = (2, 4, 8, 16)
POOL_GROUPS = len(POOL_WINDOWS)
POOL_GROUP_DIM = D_INNER // POOL_GROUPS
CONV_WIDTH = 31
LN_EPS = 1e-5
ALPHA = (2.0 * DEPTH) ** 0.25
BETA = (8.0 * DEPTH) ** -0.25
N_A = (DEPTH + 2) // 3
N_B = (DEPTH + 1) // 3
N_C = DEPTH // 3

kernel_name = 'hybrid_sgu_pool_conformer_deepnorm'


def layer_norm(x, g, b):
    xf = x.astype(jnp.float32)
    mu = jnp.mean(xf, axis=-1, keepdims=True)
    var = jnp.mean(jnp.square(xf - mu), axis=-1, keepdims=True)
    y = (xf - mu) * lax.rsqrt(var + LN_EPS) * g.astype(jnp.float32) + b.astype(jnp.float32)
    return y.astype(x.dtype)


def mixer_sgu(h, w_in, ln_g, ln_b, w_s, b_s, w_out):
    bsz, seq, _ = h.shape
    proj = h @ w_in
    u, v, z = jnp.split(proj, 3, axis=-1)
    u = jax.nn.gelu(u)
    v = layer_norm(jax.nn.gelu(v), ln_g, ln_b)
    vc = v.reshape(bsz, seq // CHUNK, CHUNK, SGU_HEADS, SGU_HEAD_DIM)
    causal = jnp.tril(jnp.ones((CHUNK, CHUNK), dtype=w_s.dtype))
    ws = w_s * causal[None]
    mixed = jnp.einsum('hts,bcshe->bcthe', ws, vc) + b_s.T[:, :, None]
    mixed = mixed.reshape(bsz, seq, D_INNER)
    y = u * mixed * jax.nn.silu(z)
    return y @ w_out


def mixer_pool(h, w_in, w_pool, scale, w_out):
    bsz, seq, _ = h.shape
    proj = h @ w_in
    v, z = jnp.split(proj, 2, axis=-1)
    vf = v.astype(jnp.float32).reshape(bsz, seq, POOL_GROUPS, POOL_GROUP_DIM)
    cs = jnp.cumsum(vf, axis=1)
    pos = jnp.arange(seq)
    outs = []
    for g, w in enumerate(POOL_WINDOWS):
        c = cs[:, :, g]
        lag = jnp.pad(c, ((0, 0), (w, 0), (0, 0)))[:, :seq]
        cnt = jnp.minimum(pos + 1, w).astype(jnp.float32)
        mean = (c - lag) / cnt[None, :, None]
        outs.append(mean - vf[:, :, g])
    p = jnp.stack(outs, axis=2).astype(v.dtype)
    p = jnp.einsum('bsgc,gcd->bsgd', p, w_pool).reshape(bsz, seq, D_INNER) * scale
    return (p * jax.nn.silu(z)) @ w_out


def mixer_conv(h, w_in, conv_w, conv_b, ln_g, ln_b, w_out):
    proj = h @ w_in
    a, gl, z = jnp.split(proj, 3, axis=-1)
    g = a * jax.nn.sigmoid(gl)
    c = lax.conv_general_dilated(
        g, conv_w[:, None, :], window_strides=(1,), padding=[(CONV_WIDTH - 1, 0)],
        dimension_numbers=('NWC', 'WIO', 'NWC'), feature_group_count=D_INNER) + conv_b
    s = jax.nn.silu(layer_norm(c, ln_g, ln_b))
    return (s * jax.nn.silu(z)) @ w_out


def _fwd_setup_inputs(seed: int = 0) -> dict:
    key = jax.random.key(seed)
    ks = jax.random.split(key, 24)
    f32 = jnp.float32
    nrm = lambda k, shape: jax.random.normal(k, shape, dtype=f32)
    E, D = D_INNER, D_MODEL
    out_scale = BETA * E ** -0.5
    return {
        'x': nrm(ks[0], (BATCH, SEQ, D)),
        'a_w_in': nrm(ks[1], (N_A, D, 3 * E)) * D ** -0.5,
        'a_ln_g': 1.0 + 0.02 * nrm(ks[2], (N_A, E)),
        'a_ln_b': 0.02 * nrm(ks[3], (N_A, E)),
        'a_w_s': nrm(ks[4], (N_A, SGU_HEADS, CHUNK, CHUNK)) * CHUNK ** -0.5,
        'a_b_s': 1.0 + 0.02 * nrm(ks[5], (N_A, SGU_HEADS, CHUNK)),
        'a_w_out': nrm(ks[6], (N_A, E, D)) * out_scale,
        'b_w_in': nrm(ks[7], (N_B, D, 2 * E)) * D ** -0.5,
        'b_w_pool': nrm(ks[8], (N_B, POOL_GROUPS, POOL_GROUP_DIM, POOL_GROUP_DIM)) * POOL_GROUP_DIM ** -0.5,
        'b_scale': 1.0 + 0.1 * nrm(ks[9], (N_B, E)),
        'b_w_out': nrm(ks[10], (N_B, E, D)) * out_scale,
        'c_w_in': nrm(ks[11], (N_C, D, 3 * E)) * D ** -0.5,
        'c_conv_w': nrm(ks[12], (N_C, CONV_WIDTH, E)) * CONV_WIDTH ** -0.5,
        'c_conv_b': 0.02 * nrm(ks[13], (N_C, E)),
        'c_ln_g': 1.0 + 0.02 * nrm(ks[14], (N_C, E)),
        'c_ln_b': 0.02 * nrm(ks[15], (N_C, E)),
        'c_w_out': nrm(ks[16], (N_C, E, D)) * out_scale,
        'post_ln_g': 1.0 + 0.02 * nrm(ks[17], (DEPTH, D)),
        'post_ln_b': 0.02 * nrm(ks[18], (DEPTH, D)),
    }


def _fwd_reference(x, a_w_in, a_ln_g, a_ln_b, a_w_s, a_b_s, a_w_out,
              b_w_in, b_w_pool, b_scale, b_w_out,
              c_w_in, c_conv_w, c_conv_b, c_ln_g, c_ln_b, c_w_out,
              post_ln_g, post_ln_b):
    h = x
    for i in range(DEPTH):
        kind, j = i % N_MIXERS, i // N_MIXERS
        if kind == 0:
            y = mixer_sgu(h, a_w_in[j], a_ln_g[j], a_ln_b[j], a_w_s[j], a_b_s[j], a_w_out[j])
        elif kind == 1:
            y = mixer_pool(h, b_w_in[j], b_w_pool[j], b_scale[j], b_w_out[j])
        else:
            y = mixer_conv(h, c_w_in[j], c_conv_w[j], c_conv_b[j], c_ln_g[j], c_ln_b[j], c_w_out[j])
        h = layer_norm(ALPHA * h + y, post_ln_g[i], post_ln_b[i])
    return h


import jax as _jax
import jax.numpy as _jnp

TWIN_FORMAT = 'train_step'
FWD_PARAMS = ['x', 'a_w_in', 'a_ln_g', 'a_ln_b', 'a_w_s', 'a_b_s', 'a_w_out', 'b_w_in', 'b_w_pool', 'b_scale', 'b_w_out', 'c_w_in', 'c_conv_w', 'c_conv_b', 'c_ln_g', 'c_ln_b', 'c_w_out', 'post_ln_g', 'post_ln_b']
TWIN_WEIGHTS = ['a_w_in', 'a_ln_g', 'a_ln_b', 'a_w_s', 'a_b_s', 'a_w_out', 'b_w_in', 'b_w_pool', 'b_scale', 'b_w_out', 'c_w_in', 'c_conv_w', 'c_conv_b', 'c_ln_g', 'c_ln_b', 'c_w_out', 'post_ln_g', 'post_ln_b']
TWIN_DIFF_INPUT = 'x'
TWIN_INPUTS = ['x', 'a_w_in', 'a_ln_g', 'a_ln_b', 'a_w_s', 'a_b_s', 'a_w_out', 'b_w_in', 'b_w_pool', 'b_scale', 'b_w_out', 'c_w_in', 'c_conv_w', 'c_conv_b', 'c_ln_g', 'c_ln_b', 'c_w_out', 'post_ln_g', 'post_ln_b', 'loss_target', 'm_a_w_in', 'm_a_ln_g', 'm_a_ln_b', 'm_a_w_s', 'm_a_b_s', 'm_a_w_out', 'm_b_w_in', 'm_b_w_pool', 'm_b_scale', 'm_b_w_out', 'm_c_w_in', 'm_c_conv_w', 'm_c_conv_b', 'm_c_ln_g', 'm_c_ln_b', 'm_c_w_out', 'm_post_ln_g', 'm_post_ln_b', 'v_a_w_in', 'v_a_ln_g', 'v_a_ln_b', 'v_a_w_s', 'v_a_b_s', 'v_a_w_out', 'v_b_w_in', 'v_b_w_pool', 'v_b_scale', 'v_b_w_out', 'v_c_w_in', 'v_c_conv_w', 'v_c_conv_b', 'v_c_ln_g', 'v_c_ln_b', 'v_c_w_out', 'v_post_ln_g', 'v_post_ln_b']
TWIN_OUTPUTS = ['loss', 'grad_x', 'grad_a_w_in', 'grad_a_ln_g', 'grad_a_ln_b', 'grad_a_w_s', 'grad_a_b_s', 'grad_a_w_out', 'grad_b_w_in', 'grad_b_w_pool', 'grad_b_scale', 'grad_b_w_out', 'grad_c_w_in', 'grad_c_conv_w', 'grad_c_conv_b', 'grad_c_ln_g', 'grad_c_ln_b', 'grad_c_w_out', 'grad_post_ln_g', 'grad_post_ln_b', 'delta_a_w_in', 'delta_a_ln_g', 'delta_a_ln_b', 'delta_a_w_s', 'delta_a_b_s', 'delta_a_w_out', 'delta_b_w_in', 'delta_b_w_pool', 'delta_b_scale', 'delta_b_w_out', 'delta_c_w_in', 'delta_c_conv_w', 'delta_c_conv_b', 'delta_c_ln_g', 'delta_c_ln_b', 'delta_c_w_out', 'delta_post_ln_g', 'delta_post_ln_b', 'new_m_a_w_in', 'new_m_a_ln_g', 'new_m_a_ln_b', 'new_m_a_w_s', 'new_m_a_b_s', 'new_m_a_w_out', 'new_m_b_w_in', 'new_m_b_w_pool', 'new_m_b_scale', 'new_m_b_w_out', 'new_m_c_w_in', 'new_m_c_conv_w', 'new_m_c_conv_b', 'new_m_c_ln_g', 'new_m_c_ln_b', 'new_m_c_w_out', 'new_m_post_ln_g', 'new_m_post_ln_b', 'new_v_a_w_in', 'new_v_a_ln_g', 'new_v_a_ln_b', 'new_v_a_w_s', 'new_v_a_b_s', 'new_v_a_w_out', 'new_v_b_w_in', 'new_v_b_w_pool', 'new_v_b_scale', 'new_v_b_w_out', 'new_v_c_w_in', 'new_v_c_conv_w', 'new_v_c_conv_b', 'new_v_c_ln_g', 'new_v_c_ln_b', 'new_v_c_w_out', 'new_v_post_ln_g', 'new_v_post_ln_b']
TWIN_LEAF_KINDS = {'loss': 'loss', 'grad_x': 'grad_x', 'grad_a_w_in': 'grad_w', 'grad_a_ln_g': 'grad_w', 'grad_a_ln_b': 'grad_w', 'grad_a_w_s': 'grad_w', 'grad_a_b_s': 'grad_w', 'grad_a_w_out': 'grad_w', 'grad_b_w_in': 'grad_w', 'grad_b_w_pool': 'grad_w', 'grad_b_scale': 'grad_w', 'grad_b_w_out': 'grad_w', 'grad_c_w_in': 'grad_w', 'grad_c_conv_w': 'grad_w', 'grad_c_conv_b': 'grad_w', 'grad_c_ln_g': 'grad_w', 'grad_c_ln_b': 'grad_w', 'grad_c_w_out': 'grad_w', 'grad_post_ln_g': 'grad_w', 'grad_post_ln_b': 'grad_w', 'delta_a_w_in': 'delta_w', 'delta_a_ln_g': 'delta_w', 'delta_a_ln_b': 'delta_w', 'delta_a_w_s': 'delta_w', 'delta_a_b_s': 'delta_w', 'delta_a_w_out': 'delta_w', 'delta_b_w_in': 'delta_w', 'delta_b_w_pool': 'delta_w', 'delta_b_scale': 'delta_w', 'delta_b_w_out': 'delta_w', 'delta_c_w_in': 'delta_w', 'delta_c_conv_w': 'delta_w', 'delta_c_conv_b': 'delta_w', 'delta_c_ln_g': 'delta_w', 'delta_c_ln_b': 'delta_w', 'delta_c_w_out': 'delta_w', 'delta_post_ln_g': 'delta_w', 'delta_post_ln_b': 'delta_w', 'new_m_a_w_in': 'new_m', 'new_m_a_ln_g': 'new_m', 'new_m_a_ln_b': 'new_m', 'new_m_a_w_s': 'new_m', 'new_m_a_b_s': 'new_m', 'new_m_a_w_out': 'new_m', 'new_m_b_w_in': 'new_m', 'new_m_b_w_pool': 'new_m', 'new_m_b_scale': 'new_m', 'new_m_b_w_out': 'new_m', 'new_m_c_w_in': 'new_m', 'new_m_c_conv_w': 'new_m', 'new_m_c_conv_b': 'new_m', 'new_m_c_ln_g': 'new_m', 'new_m_c_ln_b': 'new_m', 'new_m_c_w_out': 'new_m', 'new_m_post_ln_g': 'new_m', 'new_m_post_ln_b': 'new_m', 'new_v_a_w_in': 'new_v', 'new_v_a_ln_g': 'new_v', 'new_v_a_ln_b': 'new_v', 'new_v_a_w_s': 'new_v', 'new_v_a_b_s': 'new_v', 'new_v_a_w_out': 'new_v', 'new_v_b_w_in': 'new_v', 'new_v_b_w_pool': 'new_v', 'new_v_b_scale': 'new_v', 'new_v_b_w_out': 'new_v', 'new_v_c_w_in': 'new_v', 'new_v_c_conv_w': 'new_v', 'new_v_c_conv_b': 'new_v', 'new_v_c_ln_g': 'new_v', 'new_v_c_ln_b': 'new_v', 'new_v_c_w_out': 'new_v', 'new_v_post_ln_g': 'new_v', 'new_v_post_ln_b': 'new_v'}


def _forward(args):
    return _fwd_reference(*[args[k] for k in FWD_PARAMS])


def _output_shape():
    def fwd():
        inp = _fwd_setup_inputs(0)
        return _fwd_reference(*[inp[k] for k in FWD_PARAMS])
    out = _jax.eval_shape(fwd)
    return out.shape, out.dtype

N_MICROBATCH = 1
ADAM_LR = 0.001
ADAM_B1 = 0.9
ADAM_B2 = 0.999
ADAM_EPS = 1e-08
ADAM_WD = 0.01
ADAM_STEP = 10
PER_EXAMPLE_BATCH_AXIS = {'x': 0, 'loss_target': 0}
SHARED_INPUTS = []
_WEIGHT_DTYPES = {'a_w_in': _jnp.float32, 'a_ln_g': _jnp.float32, 'a_ln_b': _jnp.float32, 'a_w_s': _jnp.float32, 'a_b_s': _jnp.float32, 'a_w_out': _jnp.float32, 'b_w_in': _jnp.float32, 'b_w_pool': _jnp.float32, 'b_scale': _jnp.float32, 'b_w_out': _jnp.float32, 'c_w_in': _jnp.float32, 'c_conv_w': _jnp.float32, 'c_conv_b': _jnp.float32, 'c_ln_g': _jnp.float32, 'c_ln_b': _jnp.float32, 'c_w_out': _jnp.float32, 'post_ln_g': _jnp.float32, 'post_ln_b': _jnp.float32}
MOMENT_SCALE = {'a_w_in': 6.960243e-03, 'a_ln_g': 4.290003e-03, 'a_ln_b': 4.418376e-03, 'a_w_s': 8.712910e-03, 'a_b_s': 1.249659e-02, 'a_w_out': 2.622875e-02, 'b_w_in': 8.430818e-03, 'b_w_pool': 8.299527e-03, 'b_scale': 8.570237e-03, 'b_w_out': 2.791605e-02, 'c_w_in': 4.935422e-03, 'c_conv_w': 5.896882e-03, 'c_conv_b': 1.254390e-02, 'c_ln_g': 7.254960e-03, 'c_ln_b': 7.025067e-03, 'c_w_out': 1.972578e-02, 'post_ln_g': 8.025524e+00, 'post_ln_b': 4.005026e-01}


def _to_microbatches(a, axis):
    t = _jnp.moveaxis(a, axis, 0)
    t = t.reshape((N_MICROBATCH, t.shape[0] // N_MICROBATCH) + t.shape[1:])
    return _jnp.moveaxis(t, 1, axis + 1)


def setup_inputs(seed: int = 0) -> dict:
    inp = _fwd_setup_inputs(seed)
    key = _jax.random.fold_in(_jax.random.key(seed), 7919)
    shape, _ = _output_shape()
    out = dict(inp)
    out["loss_target"] = _jax.random.normal(_jax.random.fold_in(key, 0), shape, _jnp.float32)
    for i, name in enumerate(TWIN_WEIGHTS):
        w = inp[name].astype(_jnp.float32)
        if MOMENT_SCALE is None:
            s = _jnp.sqrt(_jnp.mean(_jnp.square(w)) + 1e-30)
        else:
            s = MOMENT_SCALE[name]
        km, kv = _jax.random.split(_jax.random.fold_in(key, i + 1))
        out[name] = w
        out["m_" + name] = s * _jax.random.normal(km, w.shape, _jnp.float32)
        out["v_" + name] = (s * s) * _jax.random.uniform(kv, w.shape, _jnp.float32, 0.5, 1.5)
    if N_MICROBATCH > 1:
        for name, axis in PER_EXAMPLE_BATCH_AXIS.items():
            out[name] = _to_microbatches(out[name], axis)
    return {'x': out['x'], 'a_w_in': out['a_w_in'], 'a_ln_g': out['a_ln_g'], 'a_ln_b': out['a_ln_b'], 'a_w_s': out['a_w_s'], 'a_b_s': out['a_b_s'], 'a_w_out': out['a_w_out'], 'b_w_in': out['b_w_in'], 'b_w_pool': out['b_w_pool'], 'b_scale': out['b_scale'], 'b_w_out': out['b_w_out'], 'c_w_in': out['c_w_in'], 'c_conv_w': out['c_conv_w'], 'c_conv_b': out['c_conv_b'], 'c_ln_g': out['c_ln_g'], 'c_ln_b': out['c_ln_b'], 'c_w_out': out['c_w_out'], 'post_ln_g': out['post_ln_g'], 'post_ln_b': out['post_ln_b'], 'loss_target': out['loss_target'], 'm_a_w_in': out['m_a_w_in'], 'm_a_ln_g': out['m_a_ln_g'], 'm_a_ln_b': out['m_a_ln_b'], 'm_a_w_s': out['m_a_w_s'], 'm_a_b_s': out['m_a_b_s'], 'm_a_w_out': out['m_a_w_out'], 'm_b_w_in': out['m_b_w_in'], 'm_b_w_pool': out['m_b_w_pool'], 'm_b_scale': out['m_b_scale'], 'm_b_w_out': out['m_b_w_out'], 'm_c_w_in': out['m_c_w_in'], 'm_c_conv_w': out['m_c_conv_w'], 'm_c_conv_b': out['m_c_conv_b'], 'm_c_ln_g': out['m_c_ln_g'], 'm_c_ln_b': out['m_c_ln_b'], 'm_c_w_out': out['m_c_w_out'], 'm_post_ln_g': out['m_post_ln_g'], 'm_post_ln_b': out['m_post_ln_b'], 'v_a_w_in': out['v_a_w_in'], 'v_a_ln_g': out['v_a_ln_g'], 'v_a_ln_b': out['v_a_ln_b'], 'v_a_w_s': out['v_a_w_s'], 'v_a_b_s': out['v_a_b_s'], 'v_a_w_out': out['v_a_w_out'], 'v_b_w_in': out['v_b_w_in'], 'v_b_w_pool': out['v_b_w_pool'], 'v_b_scale': out['v_b_scale'], 'v_b_w_out': out['v_b_w_out'], 'v_c_w_in': out['v_c_w_in'], 'v_c_conv_w': out['v_c_conv_w'], 'v_c_conv_b': out['v_c_conv_b'], 'v_c_ln_g': out['v_c_ln_g'], 'v_c_ln_b': out['v_c_ln_b'], 'v_c_w_out': out['v_c_w_out'], 'v_post_ln_g': out['v_post_ln_g'], 'v_post_ln_b': out['v_post_ln_b']}


def _loss(weights, diff, rest, loss_target):
    with _jax.named_scope("forward"):
        args = {**rest, TWIN_DIFF_INPUT: diff, **{k: w.astype(_WEIGHT_DTYPES[k]) for k, w in weights.items()}}
        y = _forward(args)
    with _jax.named_scope("loss_head"):
        err = _jnp.square(y.astype(_jnp.float32) - loss_target)
        return 0.5 * _jnp.sum(_jnp.mean(err, axis=-1)) if err.ndim else 0.5 * err


def _adamw(w, g, m, v):
    m = ADAM_B1 * m + (1.0 - ADAM_B1) * g
    v = ADAM_B2 * v + (1.0 - ADAM_B2) * _jnp.square(g)
    m_hat = m / (1.0 - ADAM_B1 ** ADAM_STEP)
    v_hat = v / (1.0 - ADAM_B2 ** ADAM_STEP)
    delta = -ADAM_LR * (m_hat / (_jnp.sqrt(v_hat) + ADAM_EPS) + ADAM_WD * w)
    return delta, m, v


def reference(x, a_w_in, a_ln_g, a_ln_b, a_w_s, a_b_s, a_w_out, b_w_in, b_w_pool, b_scale, b_w_out, c_w_in, c_conv_w, c_conv_b, c_ln_g, c_ln_b, c_w_out, post_ln_g, post_ln_b, loss_target, m_a_w_in, m_a_ln_g, m_a_ln_b, m_a_w_s, m_a_b_s, m_a_w_out, m_b_w_in, m_b_w_pool, m_b_scale, m_b_w_out, m_c_w_in, m_c_conv_w, m_c_conv_b, m_c_ln_g, m_c_ln_b, m_c_w_out, m_post_ln_g, m_post_ln_b, v_a_w_in, v_a_ln_g, v_a_ln_b, v_a_w_s, v_a_b_s, v_a_w_out, v_b_w_in, v_b_w_pool, v_b_scale, v_b_w_out, v_c_w_in, v_c_conv_w, v_c_conv_b, v_c_ln_g, v_c_ln_b, v_c_w_out, v_post_ln_g, v_post_ln_b):
    given = dict(x=x, a_w_in=a_w_in, a_ln_g=a_ln_g, a_ln_b=a_ln_b, a_w_s=a_w_s, a_b_s=a_b_s, a_w_out=a_w_out, b_w_in=b_w_in, b_w_pool=b_w_pool, b_scale=b_scale, b_w_out=b_w_out, c_w_in=c_w_in, c_conv_w=c_conv_w, c_conv_b=c_conv_b, c_ln_g=c_ln_g, c_ln_b=c_ln_b, c_w_out=c_w_out, post_ln_g=post_ln_g, post_ln_b=post_ln_b, loss_target=loss_target, m_a_w_in=m_a_w_in, m_a_ln_g=m_a_ln_g, m_a_ln_b=m_a_ln_b, m_a_w_s=m_a_w_s, m_a_b_s=m_a_b_s, m_a_w_out=m_a_w_out, m_b_w_in=m_b_w_in, m_b_w_pool=m_b_w_pool, m_b_scale=m_b_scale, m_b_w_out=m_b_w_out, m_c_w_in=m_c_w_in, m_c_conv_w=m_c_conv_w, m_c_conv_b=m_c_conv_b, m_c_ln_g=m_c_ln_g, m_c_ln_b=m_c_ln_b, m_c_w_out=m_c_w_out, m_post_ln_g=m_post_ln_g, m_post_ln_b=m_post_ln_b, v_a_w_in=v_a_w_in, v_a_ln_g=v_a_ln_g, v_a_ln_b=v_a_ln_b, v_a_w_s=v_a_w_s, v_a_b_s=v_a_b_s, v_a_w_out=v_a_w_out, v_b_w_in=v_b_w_in, v_b_w_pool=v_b_w_pool, v_b_scale=v_b_scale, v_b_w_out=v_b_w_out, v_c_w_in=v_c_w_in, v_c_conv_w=v_c_conv_w, v_c_conv_b=v_c_conv_b, v_c_ln_g=v_c_ln_g, v_c_ln_b=v_c_ln_b, v_c_w_out=v_c_w_out, v_post_ln_g=v_post_ln_g, v_post_ln_b=v_post_ln_b)
    weights = {n: given[n] for n in TWIN_WEIGHTS}
    shared = {n: given[n] for n in SHARED_INPUTS}
    per_example = {n: given[n] for n in ['x']}
    grad_fn = _jax.value_and_grad(_loss, argnums=(0, 1))

    def one_microbatch(ex, loss_target):
        ex = dict(ex)
        diff = ex.pop(TWIN_DIFF_INPUT)
        return grad_fn(weights, diff, {**shared, **ex}, loss_target)

    if N_MICROBATCH == 1:
        loss, (grad_w, grad_x) = one_microbatch(per_example, given["loss_target"])
    else:
        def body(carry, xs):
            loss_sum, grad_sum = carry
            l_k, (gw_k, gx_k) = one_microbatch(xs[0], xs[1])
            with _jax.named_scope("update"):
                return (loss_sum + l_k, _jax.tree.map(_jnp.add, grad_sum, gw_k)), gx_k

        init = (_jnp.zeros((), _jnp.float32), _jax.tree.map(_jnp.zeros_like, weights))
        (loss, grad_w), grad_x = _jax.lax.scan(body, init, (per_example, given["loss_target"]))
    with _jax.named_scope("update"):
        delta_w, new_m, new_v = {}, {}, {}
        for n in TWIN_WEIGHTS:
            delta_w[n], new_m[n], new_v[n] = _adamw(weights[n], grad_w[n], given["m_" + n], given["v_" + n])
    return (loss, grad_x, *[grad_w[n] for n in TWIN_WEIGHTS], *[delta_w[n] for n in TWIN_WEIGHTS],
            *[new_m[n] for n in TWIN_WEIGHTS], *[new_v[n] for n in TWIN_WEIGHTS])
```

```python
import functools

import jax
import jax.numpy as jnp
from jax import lax
from jax.experimental import pallas as pl
from jax.experimental.pallas import tpu as pltpu

F32 = jnp.float32
BF16 = jnp.bfloat16
MESH = pl.DeviceIdType.MESH

DEPTH = 4
CHUNK = 128
SGU_HEADS = 8
POOL_WINDOWS = (2, 4, 8, 16)
POOL_HALO = 16
CONV_WIDTH = 31
CONV_HALO = 32
LN_EPS = 1e-5
ALPHA = (2.0 * DEPTH) ** 0.25
ADAM_LR, ADAM_B1, ADAM_B2, ADAM_EPS, ADAM_WD, ADAM_STEP = 0.001, 0.9, 0.999, 1e-08, 0.01, 10
N_CHIPS = 4
N_DEV = 8
LANES = 128
VMEM_LIMIT = 56 << 20
GELU_K = 0.7978845608028654
GELU_C = 0.044715


def _params(sem=None, vmem=VMEM_LIMIT):
    return pltpu.CompilerParams(dimension_semantics=sem, vmem_limit_bytes=vmem)


def _tile(dim, pref):
    if dim <= pref:
        return dim
    t = 1 << (pref.bit_length() - 1)
    while dim % t:
        t //= 2
    return t


ROW_BLOCK_BYTES = 2 << 20


def _row_tile(rows, cols):
    return _tile(rows, max(ROW_ALIGN, ROW_BLOCK_BYTES // (4 * cols)))


def _sigmoid(x):
    return 1.0 / (1.0 + jnp.exp(-x))


def _gelu(x):
    return 0.5 * x * (1.0 + jnp.tanh(GELU_K * (x + GELU_C * x * x * x)))


def _gelu_grad(x):
    t = jnp.tanh(GELU_K * (x + GELU_C * x * x * x))
    return 0.5 * (1.0 + t) + 0.5 * x * (1.0 - t * t) * GELU_K * (1.0 + 3.0 * GELU_C * x * x)


def _ln_stats(x):
    mu = jnp.mean(x, axis=-1, keepdims=True)
    d = x - mu
    var = jnp.mean(d * d, axis=-1, keepdims=True)
    rstd = lax.rsqrt(var + LN_EPS)
    return d * rstd, rstd


def _ln_bwd(dy_hat, xhat, rstd):
    m1 = jnp.mean(dy_hat, axis=-1, keepdims=True)
    m2 = jnp.mean(dy_hat * xhat, axis=-1, keepdims=True)
    return rstd * (dy_hat - m1 - xhat * m2)


def _rowsum(x):
    return jnp.sum(x, axis=0, keepdims=True)


def _mm(name, a, b, *, dims, grid, a_spec, b_spec, out_shape, out_specs, acc_shape,
        extra=(), extra_specs=(), epilogue=None, aliases=None):
    nk = grid[2]
    n_extra = len(extra)
    single = not isinstance(out_shape, (list, tuple))
    shapes = [out_shape] if single else list(out_shape)
    specs = [out_specs] if single else list(out_specs)
    n_out = len(shapes)

    def body(*refs):
        a_ref, b_ref = refs[0], refs[1]
        ex = refs[2:2 + n_extra]
        outs = refs[2 + n_extra:2 + n_extra + n_out]
        acc = refs[-1]
        k = pl.program_id(2)

        @pl.when(k == 0)
        def _():
            acc[...] = jnp.zeros_like(acc)

        acc[...] += lax.dot_general(a_ref[...].astype(BF16), b_ref[...].astype(BF16), dims,
                                    preferred_element_type=F32)

        @pl.when(k == nk - 1)
        def _():
            if epilogue is None:
                outs[0][...] = acc[...].astype(outs[0].dtype)
            else:
                epilogue(acc, ex, outs)

    res = pl.pallas_call(
        body, name=name, grid=grid, in_specs=[a_spec, b_spec, *extra_specs], out_specs=specs,
        out_shape=shapes, scratch_shapes=[pltpu.VMEM(acc_shape, F32)],
        input_output_aliases=aliases or {},
        compiler_params=_params(("parallel", "parallel", "arbitrary")),
    )(a, b, *extra)
    return res[0] if single else res


NN = (((1,), (0,)), ((), ()))
NT = (((1,), (1,)), ((), ()))
TN = (((0,), (0,)), ((), ()))


def _mm_in(name, hb, wg, layer):
    s, d = hb.shape
    nsh = wg.shape[3]
    n = N_CHIPS * nsh
    bm, bn, bk = _tile(s, 1024), _tile(nsh, 1024), _tile(d, 1024)
    per = nsh // bn
    return _mm(name, hb, wg, dims=NN, grid=(s // bm, n // bn, d // bk),
               a_spec=pl.BlockSpec((bm, bk), lambda m, j, k: (m, k)),
               b_spec=pl.BlockSpec((None, None, bk, bn), lambda m, j, k: (j // per, layer, k, j % per)),
               out_shape=jax.ShapeDtypeStruct((s, n), F32),
               out_specs=pl.BlockSpec((bm, bn), lambda m, j, k: (m, j)),
               acc_shape=(bm, bn))


def _mm_out_ln(name, yg, wg, layer, res, gp, bp, g, b):
    s, e = yg.shape
    esh, d = wg.shape[2], wg.shape[3]
    bm, bk = _tile(s, 512), _tile(esh, 512)
    per = esh // bk

    def epilogue(acc, ex, outs):
        res_ref, gp_ref, bp_ref, g_ref, b_ref = ex
        xhat_ref, hb_ref, rstd_ref = outs
        r = ALPHA * (res_ref[...] * gp_ref[...] + bp_ref[...]) + acc[...]
        xhat, rstd = _ln_stats(r)
        xhat_ref[...] = xhat
        hb_ref[...] = (xhat * g_ref[...] + b_ref[...]).astype(BF16)
        rstd_ref[...] = jnp.broadcast_to(rstd, rstd_ref.shape)

    vec = pl.BlockSpec((1, d), lambda m, j, k: (0, 0))
    row = pl.BlockSpec((bm, d), lambda m, j, k: (m, 0))
    return _mm(name, yg, wg, dims=NN, grid=(s // bm, 1, e // bk),
               a_spec=pl.BlockSpec((bm, bk), lambda m, j, k: (m, k)),
               b_spec=pl.BlockSpec((None, None, bk, d), lambda m, j, k: (k // per, layer, k % per, 0)),
               extra=(res, gp, bp, g, b), extra_specs=(row, vec, vec, vec, vec),
               out_shape=(jax.ShapeDtypeStruct((s, d), F32), jax.ShapeDtypeStruct((s, d), BF16),
                          jax.ShapeDtypeStruct((s, LANES), F32)),
               out_specs=(row, row, pl.BlockSpec((bm, LANES), lambda m, j, k: (m, 0))),
               acc_shape=(bm, d), epilogue=epilogue)


def _mm_dyg(name, drb, wg, layer):
    s, d = drb.shape
    esh = wg.shape[2]
    e = N_CHIPS * esh
    bm, bn, bk = _tile(s, 1024), _tile(esh, 1024), _tile(d, 1024)
    per = esh // bn
    return _mm(name, drb, wg, dims=NT, grid=(s // bm, e // bn, d // bk),
               a_spec=pl.BlockSpec((bm, bk), lambda m, j, k: (m, k)),
               b_spec=pl.BlockSpec((None, None, bn, bk), lambda m, j, k: (j // per, layer, j % per, k)),
               out_shape=jax.ShapeDtypeStruct((s, e), F32),
               out_specs=pl.BlockSpec((bm, bn), lambda m, j, k: (m, j)),
               acc_shape=(bm, bn))


def _mm_dw_out(name, yg, drb):
    s, e = yg.shape
    d = drb.shape[1]
    bm, bn, bk = _tile(e, 1024), _tile(d, 1024), _tile(s, 1024)
    return _mm(name, yg, drb, dims=TN, grid=(e // bm, d // bn, s // bk),
               a_spec=pl.BlockSpec((bk, bm), lambda m, j, k: (k, m)),
               b_spec=pl.BlockSpec((bk, bn), lambda m, j, k: (k, j)),
               out_shape=jax.ShapeDtypeStruct((e, d), BF16),
               out_specs=pl.BlockSpec((bm, bn), lambda m, j, k: (m, j)),
               acc_shape=(bm, bn))


def _mm_dh(name, dproj, wg, layer, dr):
    s, n = dproj.shape
    d, nsh = wg.shape[2], wg.shape[3]
    bm, bn, bk = _tile(s, 1024), _tile(d, 1024), _tile(nsh, 1024)
    per = nsh // bk

    def epilogue(acc, ex, outs):
        outs[0][...] = ALPHA * ex[0][...] + acc[...]

    blk = pl.BlockSpec((bm, bn), lambda m, j, k: (m, j))
    return _mm(name, dproj, wg, dims=NT, grid=(s // bm, d // bn, n // bk),
               a_spec=pl.BlockSpec((bm, bk), lambda m, j, k: (m, k)),
               b_spec=pl.BlockSpec((None, None, bn, bk), lambda m, j, k: (k // per, layer, j, k % per)),
               extra=(dr,), extra_specs=(blk,),
               out_shape=jax.ShapeDtypeStruct((s, d), F32), out_specs=blk,
               acc_shape=(bm, bn), epilogue=epilogue)


def _mm_dw_in(name, hb, dproj):
    s, d = hb.shape
    n = dproj.shape[1]
    nsh = n // N_CHIPS
    bm, bn, bk = _tile(d, 1024), _tile(nsh, 1024), _tile(s, 1024)
    per = nsh // bn
    return _mm(name, hb, dproj, dims=TN, grid=(d // bm, n // bn, s // bk),
               a_spec=pl.BlockSpec((bk, bm), lambda m, j, k: (k, m)),
               b_spec=pl.BlockSpec((bk, bn), lambda m, j, k: (k, j)),
               out_shape=jax.ShapeDtypeStruct((N_CHIPS, d, nsh), BF16),
               out_specs=pl.BlockSpec((None, bm, bn), lambda m, j, k: (j // per, m, j % per)),
               acc_shape=(bm, bn))


def _causal_mask():
    r = lax.broadcasted_iota(jnp.int32, (CHUNK, CHUNK), 0)
    c = lax.broadcasted_iota(jnp.int32, (CHUNK, CHUNK), 1)
    return r >= c


def _sgu_fwd(name, proj, ln_g, ln_b, w_s, b_st):
    s, n3 = proj.shape
    e = n3 // 3
    hd = e // SGU_HEADS

    def body(proj_ref, g_ref, b_ref, ws_ref, bst_ref, yg_ref):
        xhat, _ = _ln_stats(_gelu(proj_ref[:, e:2 * e]))
        vnb = (xhat * g_ref[...] + b_ref[...]).astype(BF16)
        tri = _causal_mask()
        for h in range(SGU_HEADS):
            hs = slice(h * hd, (h + 1) * hd)
            ws = jnp.where(tri, ws_ref[h], 0.0).astype(BF16)
            mixed = jnp.dot(ws, vnb[:, hs], preferred_element_type=F32) + bst_ref[:, h:h + 1]
            u = proj_ref[:, hs]
            z = proj_ref[:, 2 * e + h * hd:2 * e + (h + 1) * hd]
            yg_ref[:, hs] = (_gelu(u) * mixed * (z * _sigmoid(z))).astype(BF16)

    vec = pl.BlockSpec((1, e), lambda i: (0, 0))
    return pl.pallas_call(
        body, name=name, grid=(s // CHUNK,),
        in_specs=[pl.BlockSpec((CHUNK, n3), lambda i: (i, 0)), vec, vec,
                  pl.BlockSpec((SGU_HEADS, CHUNK, CHUNK), lambda i: (0, 0, 0)),
                  pl.BlockSpec((CHUNK, SGU_HEADS), lambda i: (0, 0))],
        out_specs=pl.BlockSpec((CHUNK, e), lambda i: (i, 0)),
        out_shape=jax.ShapeDtypeStruct((s, e), BF16),
        compiler_params=_params(("parallel",)),
    )(proj, ln_g, ln_b, w_s, b_st)


def _sgu_bwd(name, dyg, proj, ln_g, ln_b, w_s, b_st):
    s, n3 = proj.shape
    e = n3 // 3
    hd = e // SGU_HEADS

    def body(dyg_ref, proj_ref, g_ref, b_ref, ws_ref, bst_ref,
             dproj_ref, dg_ref, db_ref, dws_ref, dbs_ref, dvn_ref):
        @pl.when(pl.program_id(0) == 0)
        def _():
            dg_ref[...] = jnp.zeros_like(dg_ref)
            db_ref[...] = jnp.zeros_like(db_ref)
            dws_ref[...] = jnp.zeros_like(dws_ref)
            dbs_ref[...] = jnp.zeros_like(dbs_ref)

        v = proj_ref[:, e:2 * e]
        xhat, rstd = _ln_stats(_gelu(v))
        g = g_ref[...]
        vnb = (xhat * g + b_ref[...]).astype(BF16)
        tri = _causal_mask()
        for h in range(SGU_HEADS):
            hs = slice(h * hd, (h + 1) * hd)
            zs = slice(2 * e + h * hd, 2 * e + (h + 1) * hd)
            ws = jnp.where(tri, ws_ref[h], 0.0).astype(BF16)
            mixed = jnp.dot(ws, vnb[:, hs], preferred_element_type=F32) + bst_ref[:, h:h + 1]
            u = proj_ref[:, hs]
            z = proj_ref[:, zs]
            gu = _gelu(u)
            sig = _sigmoid(z)
            dy = dyg_ref[:, hs]
            t = dy * (z * sig)
            dmixed = t * gu
            dproj_ref[:, hs] = (t * mixed * _gelu_grad(u)).astype(BF16)
            dproj_ref[:, zs] = (dy * gu * mixed * (sig * (1.0 + z * (1.0 - sig)))).astype(BF16)
            dmb = dmixed.astype(BF16)
            dvn_ref[:, hs] = lax.dot_general(ws, dmb, TN, preferred_element_type=F32)
            dws = lax.dot_general(dmb, vnb[:, hs], NT, preferred_element_type=F32)
            dws_ref[h] += jnp.where(tri, dws, 0.0)
            dbs_ref[:, h:h + 1] += jnp.sum(dmixed, axis=-1, keepdims=True)
        dvn = dvn_ref[...]
        dg_ref[...] += _rowsum(dvn * xhat)
        db_ref[...] += _rowsum(dvn)
        dgv = _ln_bwd(dvn * g, xhat, rstd)
        dproj_ref[:, e:2 * e] = (dgv * _gelu_grad(v)).astype(BF16)

    vec = pl.BlockSpec((1, e), lambda i: (0, 0))
    wsp = pl.BlockSpec((SGU_HEADS, CHUNK, CHUNK), lambda i: (0, 0, 0))
    bsp = pl.BlockSpec((CHUNK, SGU_HEADS), lambda i: (0, 0))
    return pl.pallas_call(
        body, name=name, grid=(s // CHUNK,),
        in_specs=[pl.BlockSpec((CHUNK, e), lambda i: (i, 0)), pl.BlockSpec((CHUNK, n3), lambda i: (i, 0)),
                  vec, vec, wsp, bsp],
        out_specs=[pl.BlockSpec((CHUNK, n3), lambda i: (i, 0)), vec, vec, wsp, bsp],
        out_shape=[jax.ShapeDtypeStruct((s, n3), BF16), jax.ShapeDtypeStruct((1, e), F32),
                   jax.ShapeDtypeStruct((1, e), F32),
                   jax.ShapeDtypeStruct((SGU_HEADS, CHUNK, CHUNK), F32),
                   jax.ShapeDtypeStruct((CHUNK, SGU_HEADS), F32)],
        scratch_shapes=[pltpu.VMEM((CHUNK, e), F32)],
        compiler_params=_params(("arbitrary",)),
    )(dyg, proj, ln_g, ln_b, w_s, b_st)


def _pool_counts(pos, w):
    return jnp.minimum(pos + 1, w).astype(F32)


def _pool_fwd(name, proj):
    s, n2 = proj.shape
    e = n2 // 2
    gd = e // len(POOL_WINDOWS)
    ts = _row_tile(s, e)
    hb = ts // POOL_HALO

    def body(v_ref, halo_ref, p_ref, ext):
        i = pl.program_id(0)
        ext[POOL_HALO:, :] = v_ref[...]
        ext[:POOL_HALO, :] = jnp.where(i > 0, halo_ref[...], 0.0)
        pos = i * ts + lax.broadcasted_iota(jnp.int32, (ts, 1), 0)
        for g, w in enumerate(POOL_WINDOWS):
            cs = slice(g * gd, (g + 1) * gd)
            acc = ext[POOL_HALO:, cs]
            for j in range(1, w):
                acc = acc + ext[POOL_HALO - j:POOL_HALO - j + ts, cs]
            p_ref[:, cs] = (acc / _pool_counts(pos, w) - v_ref[:, cs]).astype(BF16)

    return pl.pallas_call(
        body, name=name, grid=(s // ts,),
        in_specs=[pl.BlockSpec((ts, e), lambda i: (i, 0)),
                  pl.BlockSpec((POOL_HALO, e), lambda i: (jnp.maximum(i * hb - 1, 0), 0))],
        out_specs=pl.BlockSpec((ts, e), lambda i: (i, 0)),
        out_shape=jax.ShapeDtypeStruct((s, e), BF16),
        scratch_shapes=[pltpu.VMEM((ts + POOL_HALO, e), F32)],
        compiler_params=_params(("parallel",)),
    )(proj, proj)


def _pool_mm(name, p, wg, proj, scale):
    s, e = p.shape
    ng, rsh, gd = wg.shape[1], wg.shape[2], wg.shape[3]
    bm = _tile(s, 1024)

    def epilogue(acc, ex, outs):
        z = ex[0][...]
        q = acc[...]
        outs[0][...] = q
        outs[1][...] = (q * ex[1][...] * (z * _sigmoid(z))).astype(BF16)

    blk = pl.BlockSpec((bm, gd), lambda m, g, k: (m, g))
    return _mm(name, p, wg, dims=NN, grid=(s // bm, ng, N_CHIPS),
               a_spec=pl.BlockSpec((bm, rsh), lambda m, g, k: (m, g * N_CHIPS + k)),
               b_spec=pl.BlockSpec((None, None, rsh, gd), lambda m, g, k: (k, g, 0, 0)),
               extra=(proj, scale),
               extra_specs=(pl.BlockSpec((bm, gd), lambda m, g, k: (m, ng + g)),
                            pl.BlockSpec((1, gd), lambda m, g, k: (0, g))),
               out_shape=(jax.ShapeDtypeStruct((s, e), F32), jax.ShapeDtypeStruct((s, e), BF16)),
               out_specs=(blk, blk), acc_shape=(bm, gd), epilogue=epilogue)


def _pool_bwd1(name, dyg, q, proj, scale):
    s, e = q.shape
    ts = _row_tile(s, e)

    def body(dyg_ref, q_ref, z_ref, sc_ref, dq_ref, dz_ref, dsc_ref):
        @pl.when(pl.program_id(0) == 0)
        def _():
            dsc_ref[...] = jnp.zeros_like(dsc_ref)

        z = z_ref[...]
        sig = _sigmoid(z)
        dy = dyg_ref[...]
        qv = q_ref[...]
        sc = sc_ref[...]
        dp2 = dy * (z * sig)
        dsc_ref[...] += _rowsum(dp2 * qv)
        dq_ref[...] = (dp2 * sc).astype(BF16)
        dz_ref[...] = (dy * (qv * sc) * (sig * (1.0 + z * (1.0 - sig)))).astype(BF16)

    blk = pl.BlockSpec((ts, e), lambda i: (i, 0))
    vec = pl.BlockSpec((1, e), lambda i: (0, 0))
    return pl.pallas_call(
        body, name=name, grid=(s // ts,),
        in_specs=[blk, blk, pl.BlockSpec((ts, e), lambda i: (i, 1)), vec],
        out_specs=[blk, pl.BlockSpec((ts, e), lambda i: (i, 1)), vec],
        out_shape=[jax.ShapeDtypeStruct((s, e), BF16), jax.ShapeDtypeStruct((s, 2 * e), BF16),
                   jax.ShapeDtypeStruct((1, e), F32)],
        compiler_params=_params(("arbitrary",)),
    )(dyg, q, proj, scale)


def _pool_mm_bwd(name, dq, wg):
    s, e = dq.shape
    ng, rsh, gd = wg.shape[1], wg.shape[2], wg.shape[3]
    bm = _tile(s, 1024)
    return _mm(name, dq, wg, dims=NT, grid=(s // bm, ng * N_CHIPS, 1),
               a_spec=pl.BlockSpec((bm, gd), lambda m, j, k: (m, j // N_CHIPS)),
               b_spec=pl.BlockSpec((None, None, rsh, gd), lambda m, j, k: (j % N_CHIPS, j // N_CHIPS, 0, 0)),
               out_shape=jax.ShapeDtypeStruct((s, e), F32),
               out_specs=pl.BlockSpec((bm, rsh), lambda m, j, k: (m, j)),
               acc_shape=(bm, rsh))


def _pool_bwd2(name, dp, dproj):
    s, e = dp.shape
    gd = e // len(POOL_WINDOWS)
    ts = _row_tile(s, e)
    n = s // ts
    hb = ts // POOL_HALO

    def body(dp_ref, nxt_ref, _, dv_ref, ext):
        i = pl.program_id(0)
        pos = i * ts + lax.broadcasted_iota(jnp.int32, (ts + POOL_HALO, 1), 0)
        for g, w in enumerate(POOL_WINDOWS):
            cs = slice(g * gd, (g + 1) * gd)
            cnt = _pool_counts(pos, w)
            ext[:ts, cs] = dp_ref[:, cs] / cnt[:ts]
            ext[ts:, cs] = jnp.where(i < n - 1, nxt_ref[:, cs] / cnt[ts:], 0.0)
        for g, w in enumerate(POOL_WINDOWS):
            cs = slice(g * gd, (g + 1) * gd)
            acc = ext[:ts, cs]
            for j in range(1, w):
                acc = acc + ext[j:j + ts, cs]
            dv_ref[:, cs] = (acc - dp_ref[:, cs]).astype(BF16)

    return pl.pallas_call(
        body, name=name, grid=(n,),
        in_specs=[pl.BlockSpec((ts, e), lambda i: (i, 0)),
                  pl.BlockSpec((POOL_HALO, e), lambda i: (jnp.minimum((i + 1) * hb, s // POOL_HALO - 1), 0)),
                  pl.BlockSpec(memory_space=pl.ANY)],
        out_specs=pl.BlockSpec((ts, e), lambda i: (i, 0)),
        out_shape=jax.ShapeDtypeStruct((s, 2 * e), BF16),
        scratch_shapes=[pltpu.VMEM((ts + POOL_HALO, e), F32)],
        input_output_aliases={2: 0},
        compiler_params=_params(("parallel",)),
    )(dp, dp, dproj)


def _mm_dw_pool(name, p, dq, ng):
    s, e = p.shape
    gd = e // ng
    rsh = gd // N_CHIPS
    bk = _tile(s, 1024)
    return _mm(name, p, dq, dims=TN, grid=(N_CHIPS, ng, s // bk),
               a_spec=pl.BlockSpec((bk, rsh), lambda m, g, k: (k, g * N_CHIPS + m)),
               b_spec=pl.BlockSpec((bk, gd), lambda m, g, k: (k, g)),
               out_shape=jax.ShapeDtypeStruct((N_CHIPS, ng, rsh, gd), BF16),
               out_specs=pl.BlockSpec((None, None, rsh, gd), lambda m, g, k: (m, g, 0, 0)),
               acc_shape=(rsh, gd))


def _conv_fwd(name, proj, conv_w, conv_b):
    s, n3 = proj.shape
    e = n3 // 3
    ts, tc = _tile(s, 128), _tile(e, 512)
    ne = e // tc
    hb = ts // CONV_HALO
    off = CONV_HALO - (CONV_WIDTH - 1)

    def body(a_ref, gl_ref, ha_ref, hgl_ref, w_ref, b_ref, c_ref, gext):
        i = pl.program_id(1)
        gext[CONV_HALO:, :] = a_ref[...] * _sigmoid(gl_ref[...])
        gext[:CONV_HALO, :] = jnp.where(i > 0, ha_ref[...] * _sigmoid(hgl_ref[...]), 0.0)
        acc = jnp.broadcast_to(b_ref[...], (ts, tc))
        for k in range(CONV_WIDTH):
            acc = acc + w_ref[k:k + 1, :] * gext[off + k:off + k + ts, :]
        c_ref[...] = acc

    halo = lambda i: jnp.maximum(i * hb - 1, 0)
    return pl.pallas_call(
        body, name=name, grid=(ne, s // ts),
        in_specs=[pl.BlockSpec((ts, tc), lambda j, i: (i, j)),
                  pl.BlockSpec((ts, tc), lambda j, i: (i, ne + j)),
                  pl.BlockSpec((CONV_HALO, tc), lambda j, i: (halo(i), j)),
                  pl.BlockSpec((CONV_HALO, tc), lambda j, i: (halo(i), ne + j)),
                  pl.BlockSpec((CONV_WIDTH, tc), lambda j, i: (0, j)),
                  pl.BlockSpec((1, tc), lambda j, i: (0, j))],
        out_specs=pl.BlockSpec((ts, tc), lambda j, i: (i, j)),
        out_shape=jax.ShapeDtypeStruct((s, e), F32),
        scratch_shapes=[pltpu.VMEM((ts + CONV_HALO, tc), F32)],
        compiler_params=_params(("parallel", "parallel")),
    )(proj, proj, proj, proj, conv_w, conv_b)


def _conv_gate_fwd(name, c, proj, ln_g, ln_b):
    s, e = c.shape
    ts = _row_tile(s, e)

    def body(c_ref, z_ref, g_ref, b_ref, yg_ref):
        xhat, _ = _ln_stats(c_ref[...])
        ln = xhat * g_ref[...] + b_ref[...]
        z = z_ref[...]
        yg_ref[...] = ((ln * _sigmoid(ln)) * (z * _sigmoid(z))).astype(BF16)

    blk = pl.BlockSpec((ts, e), lambda i: (i, 0))
    vec = pl.BlockSpec((1, e), lambda i: (0, 0))
    return pl.pallas_call(
        body, name=name, grid=(s // ts,),
        in_specs=[blk, pl.BlockSpec((ts, e), lambda i: (i, 2)), vec, vec],
        out_specs=blk, out_shape=jax.ShapeDtypeStruct((s, e), BF16),
        compiler_params=_params(("parallel",)),
    )(c, proj, ln_g, ln_b)


def _conv_bwd1(name, dyg, c, proj, ln_g, ln_b):
    s, e = c.shape
    ts = _row_tile(s, e)

    def body(dyg_ref, c_ref, z_ref, g_ref, b_ref, dc_ref, dz_ref, dg_ref, db_ref, dcb_ref):
        @pl.when(pl.program_id(0) == 0)
        def _():
            dg_ref[...] = jnp.zeros_like(dg_ref)
            db_ref[...] = jnp.zeros_like(db_ref)
            dcb_ref[...] = jnp.zeros_like(dcb_ref)

        xhat, rstd = _ln_stats(c_ref[...])
        g = g_ref[...]
        ln = xhat * g + b_ref[...]
        sl = _sigmoid(ln)
        z = z_ref[...]
        sz = _sigmoid(z)
        dy = dyg_ref[...]
        dz_ref[...] = (dy * (ln * sl) * (sz * (1.0 + z * (1.0 - sz)))).astype(BF16)
        dln = dy * (z * sz) * (sl * (1.0 + ln * (1.0 - sl)))
        dg_ref[...] += _rowsum(dln * xhat)
        db_ref[...] += _rowsum(dln)
        dc = _ln_bwd(dln * g, xhat, rstd)
        dc_ref[...] = dc
        dcb_ref[...] += _rowsum(dc)

    blk = pl.BlockSpec((ts, e), lambda i: (i, 0))
    zblk = pl.BlockSpec((ts, e), lambda i: (i, 2))
    vec = pl.BlockSpec((1, e), lambda i: (0, 0))
    vshape = jax.ShapeDtypeStruct((1, e), F32)
    return pl.pallas_call(
        body, name=name, grid=(s // ts,),
        in_specs=[blk, blk, zblk, vec, vec],
        out_specs=[blk, zblk, vec, vec, vec],
        out_shape=[jax.ShapeDtypeStruct((s, e), F32), jax.ShapeDtypeStruct((s, 3 * e), BF16),
                   vshape, vshape, vshape],
        compiler_params=_params(("arbitrary",)),
    )(dyg, c, proj, ln_g, ln_b)


def _conv_bwd2(name, dc, proj, conv_w):
    s, e = dc.shape
    ts, tc = _tile(s, 128), _tile(e, 512)
    ne = e // tc
    n = s // ts
    hb = ts // CONV_HALO
    off = CONV_HALO - (CONV_WIDTH - 1)

    def body(dc_ref, nxt_ref, a_ref, gl_ref, ha_ref, hgl_ref, w_ref, dg_ref, dw_ref, gext, dcext):
        i = pl.program_id(1)

        @pl.when(i == 0)
        def _():
            dw_ref[...] = jnp.zeros_like(dw_ref)

        gext[CONV_HALO:, :] = a_ref[...] * _sigmoid(gl_ref[...])
        gext[:CONV_HALO, :] = jnp.where(i > 0, ha_ref[...] * _sigmoid(hgl_ref[...]), 0.0)
        dc_t = dc_ref[...]
        dcext[:ts, :] = dc_t
        dcext[ts:, :] = jnp.where(i < n - 1, nxt_ref[...], 0.0)
        dg = jnp.zeros((ts, tc), F32)
        for k in range(CONV_WIDTH):
            sh = CONV_WIDTH - 1 - k
            dg = dg + w_ref[k:k + 1, :] * dcext[sh:sh + ts, :]
            dw_ref[k:k + 1, :] += _rowsum(dc_t * gext[off + k:off + k + ts, :])
        dg_ref[...] = dg

    halo = lambda i: jnp.maximum(i * hb - 1, 0)
    nxt = lambda i: jnp.minimum((i + 1) * hb, s // CONV_HALO - 1)
    return pl.pallas_call(
        body, name=name, grid=(ne, n),
        in_specs=[pl.BlockSpec((ts, tc), lambda j, i: (i, j)),
                  pl.BlockSpec((CONV_HALO, tc), lambda j, i: (nxt(i), j)),
                  pl.BlockSpec((ts, tc), lambda j, i: (i, j)),
                  pl.BlockSpec((ts, tc), lambda j, i: (i, ne + j)),
                  pl.BlockSpec((CONV_HALO, tc), lambda j, i: (halo(i), j)),
                  pl.BlockSpec((CONV_HALO, tc), lambda j, i: (halo(i), ne + j)),
                  pl.BlockSpec((CONV_WIDTH, tc), lambda j, i: (0, j))],
        out_specs=[pl.BlockSpec((ts, tc), lambda j, i: (i, j)),
                   pl.BlockSpec((CONV_HALO, tc), lambda j, i: (0, j))],
        out_shape=[jax.ShapeDtypeStruct((s, e), F32), jax.ShapeDtypeStruct((CONV_HALO, e), F32)],
        scratch_shapes=[pltpu.VMEM((ts + CONV_HALO, tc), F32), pltpu.VMEM((ts + CONV_HALO, tc), F32)],
        compiler_params=_params(("parallel", "arbitrary")),
    )(dc, dc, proj, proj, proj, proj, conv_w)


def _conv_bwd3(name, dg, proj, dproj):
    s, e = dg.shape
    ts = _row_tile(s, 2 * e)

    def body(dg_ref, agl_ref, _, out_ref):
        a = agl_ref[:, :e]
        sig = _sigmoid(agl_ref[:, e:])
        dgv = dg_ref[...]
        out_ref[:, :e] = (dgv * sig).astype(BF16)
        out_ref[:, e:] = (dgv * a * sig * (1.0 - sig)).astype(BF16)

    wide = pl.BlockSpec((ts, 2 * e), lambda i: (i, 0))
    return pl.pallas_call(
        body, name=name, grid=(s // ts,),
        in_specs=[pl.BlockSpec((ts, e), lambda i: (i, 0)), wide, pl.BlockSpec(memory_space=pl.ANY)],
        out_specs=wide, out_shape=jax.ShapeDtypeStruct((s, 3 * e), BF16),
        input_output_aliases={2: 0},
        compiler_params=_params(("parallel",)),
    )(dg, proj, dproj)


def _post_ln_bwd(name, dy_or_target, xhat, rstd, g, b, loss_head):
    s, d = xhat.shape
    ts = _row_tile(s, d)

    def body(dy_ref, xhat_ref, rstd_ref, g_ref, b_ref, dr_ref, drb_ref, dg_ref, db_ref, loss_ref):
        @pl.when(pl.program_id(0) == 0)
        def _():
            dg_ref[...] = jnp.zeros_like(dg_ref)
            db_ref[...] = jnp.zeros_like(db_ref)
            loss_ref[...] = jnp.zeros_like(loss_ref)

        xh = xhat_ref[...]
        gv = g_ref[...]
        if loss_head:
            err = (xh * gv + b_ref[...]) - dy_ref[...]
            per_row = jnp.mean(err * err, axis=-1, keepdims=True)
            loss_ref[...] += 0.5 * jnp.broadcast_to(_rowsum(per_row), loss_ref.shape)
            dy = err / d
        else:
            dy = dy_ref[...]
        dg_ref[...] += _rowsum(dy * xh)
        db_ref[...] += _rowsum(dy)
        dr = _ln_bwd(dy * gv, xh, rstd_ref[:, 0:1])
        dr_ref[...] = dr
        drb_ref[...] = dr.astype(BF16)

    blk = pl.BlockSpec((ts, d), lambda i: (i, 0))
    vec = pl.BlockSpec((1, d), lambda i: (0, 0))
    return pl.pallas_call(
        body, name=name, grid=(s // ts,),
        in_specs=[blk, blk, pl.BlockSpec((ts, LANES), lambda i: (i, 0)), vec, vec],
        out_specs=[blk, blk, vec, vec, pl.BlockSpec((1, LANES), lambda i: (0, 0))],
        out_shape=[jax.ShapeDtypeStruct((s, d), F32), jax.ShapeDtypeStruct((s, d), BF16),
                   jax.ShapeDtypeStruct((1, d), F32), jax.ShapeDtypeStruct((1, d), F32),
                   jax.ShapeDtypeStruct((1, LANES), F32)],
        compiler_params=_params(("arbitrary",)),
    )(dy_or_target, xhat, rstd, g, b)


def _as2d(a):
    return a.reshape(-1, a.shape[-1])


def _cast_bf16(name, a):
    r, c = a.shape
    tr = _row_tile(r, c)

    def body(a_ref, o_ref):
        o_ref[...] = a_ref[...].astype(BF16)

    blk = pl.BlockSpec((tr, c), lambda i: (i, 0))
    return pl.pallas_call(body, name=name, grid=(r // tr,), in_specs=[blk], out_specs=blk,
                          out_shape=jax.ShapeDtypeStruct((r, c), BF16),
                          compiler_params=_params(("parallel",)))(a)


def _pair_sum(name, sp, own, recv):
    nsh, r, c = own.shape
    h = r // 2
    tr = _row_tile(h, c)
    nb = h // tr

    def body(sp_ref, own_ref, recv_ref, o_ref):
        o_ref[...] = (own_ref[...].astype(F32) + recv_ref[...].astype(F32)).astype(BF16)

    blk = pl.BlockSpec((None, tr, c), lambda j, i, sp: (j, i, 0))
    grid_spec = pltpu.PrefetchScalarGridSpec(
        num_scalar_prefetch=1, grid=(nsh, nb),
        in_specs=[pl.BlockSpec((None, tr, c), lambda j, i, sp: (j, sp[1] * nb + i, 0)), blk],
        out_specs=blk)
    return pl.pallas_call(body, name=name, grid_spec=grid_spec,
                          out_shape=jax.ShapeDtypeStruct((nsh, h, c), BF16),
                          compiler_params=_params(("parallel", "parallel")))(sp, own, recv)


def _final_sum(name, sp, own, recv, got, layer, n_layers, prev):
    _, r, c = own.shape
    h = r // 2
    tr = _row_tile(h, c)
    nb = h // tr

    def body(sp_ref, own_ref, recv_ref, q0_ref, q1_ref, q2_ref, *rest):
        o_ref = rest[-1]
        acc = own_ref[...].astype(F32) + recv_ref[...].astype(F32)
        acc = acc + q0_ref[...].astype(F32)
        acc = acc + q1_ref[...].astype(F32)
        o_ref[...] = acc + q2_ref[...].astype(F32)

    in_specs = [pl.BlockSpec((None, tr, c), lambda i, sp: (sp[0], sp[1] * nb + i, 0)),
                pl.BlockSpec((None, tr, c), lambda i, sp: (sp[0], i, 0))]
    in_specs += [pl.BlockSpec((None, tr, c), functools.partial(lambda i, sp, k: (k, i, 0), k=k)) for k in range(3)]
    operands = [sp, own, recv, got, got, got]
    aliases = {}
    if prev is not None:
        in_specs.append(pl.BlockSpec(memory_space=pl.ANY))
        operands.append(prev)
        aliases = {6: 0}
    grid_spec = pltpu.PrefetchScalarGridSpec(
        num_scalar_prefetch=1, grid=(nb,), in_specs=in_specs,
        out_specs=pl.BlockSpec((None, tr, c), lambda i, sp: (layer, sp[1] * nb + i, 0)))
    return pl.pallas_call(body, name=name, grid_spec=grid_spec,
                          out_shape=jax.ShapeDtypeStruct((n_layers, r, c), F32),
                          input_output_aliases=aliases,
                          compiler_params=_params(("parallel",)))(*operands)


def _adam(name, w, g, m, v):
    r, c = w.shape
    tr = _row_tile(r, c)
    c1 = 1.0 / (1.0 - ADAM_B1 ** ADAM_STEP)
    c2 = 1.0 / (1.0 - ADAM_B2 ** ADAM_STEP)

    def body(w_ref, g_ref, m_ref, v_ref, d_ref, nm_ref, nv_ref):
        gv = g_ref[...]
        nm = ADAM_B1 * m_ref[...] + (1.0 - ADAM_B1) * gv
        nv = ADAM_B2 * v_ref[...] + (1.0 - ADAM_B2) * (gv * gv)
        d_ref[...] = -ADAM_LR * ((nm * c1) / (jnp.sqrt(nv * c2) + ADAM_EPS) + ADAM_WD * w_ref[...])
        nm_ref[...] = nm
        nv_ref[...] = nv

    blk = pl.BlockSpec((tr, c), lambda i: (i, 0))
    shp = jax.ShapeDtypeStruct((r, c), F32)
    return pl.pallas_call(body, name=name, grid=(r // tr,), in_specs=[blk] * 4, out_specs=[blk] * 3,
                          out_shape=[shp] * 3, compiler_params=_params(("parallel",)))(w, g, m, v)


def _sum_devices(name, parts):
    _, r, c = parts.shape
    tr = _row_tile(r, N_DEV * c)

    def body(p_ref, o_ref):
        acc = p_ref[0]
        for k in range(1, N_DEV):
            acc = acc + p_ref[k]
        o_ref[...] = acc

    return pl.pallas_call(body, name=name, grid=(r // tr,),
                          in_specs=[pl.BlockSpec((N_DEV, tr, c), lambda i: (0, i, 0))],
                          out_specs=pl.BlockSpec((tr, c), lambda i: (i, 0)),
                          out_shape=jax.ShapeDtypeStruct((r, c), F32),
                          compiler_params=_params(("parallel",)))(parts)


def _coords():
    return lax.axis_index("x"), lax.axis_index("y"), lax.axis_index("c")


def _chip_peers(x, y):
    return [(1 - x, y, 2 * (1 - x) + y), (x, 1 - y, 2 * x + 1 - y), (1 - x, 1 - y, 2 * (1 - x) + 1 - y)]


def _remote(src, dst, ssem, rsem, dev):
    return pltpu.make_async_remote_copy(src_ref=src, dst_ref=dst, send_sem=ssem, recv_sem=rsem,
                                        device_id=dev, device_id_type=MESH)


ANY = pl.BlockSpec(memory_space=pl.ANY)


def _gather_weights(bigs, small):
    n = len(bigs)

    def body(*refs):
        ins, s_in = refs[:n], refs[n]
        outs, s_out = refs[n + 1:2 * n + 1], refs[2 * n + 1]
        ssem, rsem, fssem, frsem, lsem, sssem, srsem = refs[2 * n + 2:]
        x, y, c = _coords()
        j = 2 * x + y
        peers = _chip_peers(x, y)
        local = [pltpu.make_async_copy(ins[i], outs[i].at[j], lsem.at[i]) for i in range(n)]
        local.append(pltpu.make_async_copy(s_in, s_out.at[j], lsem.at[n]))
        for cp in local:
            cp.start()
        sends = []
        for k, (px, py, pj) in enumerate(peers):
            sends.append(_remote(s_in, s_out.at[j], sssem.at[k], srsem.at[k], (px, py, c)))
            for i in range(n):
                h = ins[i].shape[0] // 2
                rows = pl.ds(c * h, h)
                sends.append(_remote(ins[i].at[rows], outs[i].at[j, rows],
                                     ssem.at[i * 3 + k], rsem.at[i * 3 + k], (px, py, c)))
        for cp in sends:
            cp.start()
        passed = []
        for i in range(n):
            h = ins[i].shape[0] // 2
            rows = pl.ds(c * h, h)
            for k, (px, py, pj) in enumerate(peers):
                landed = outs[i].at[pj, rows]
                _remote(landed, landed, ssem.at[i * 3 + k], rsem.at[i * 3 + k], (px, py, c)).wait_recv()
                fwd = _remote(landed, landed, fssem.at[i * 3 + k], frsem.at[i * 3 + k], (x, y, 1 - c))
                fwd.start()
                passed.append(fwd)
        for i in range(n):
            h = ins[i].shape[0] // 2
            other = pl.ds((1 - c) * h, h)
            for k, (px, py, pj) in enumerate(peers):
                landed = outs[i].at[pj, other]
                _remote(landed, landed, fssem.at[i * 3 + k], frsem.at[i * 3 + k], (x, y, 1 - c)).wait_recv()
        for k, (px, py, pj) in enumerate(peers):
            _remote(s_in, s_out.at[pj], sssem.at[k], srsem.at[k], (px, py, c)).wait_recv()
        for cp in sends + passed:
            cp.wait_send()
        for cp in local:
            cp.wait()

    out_shape = [jax.ShapeDtypeStruct((N_CHIPS,) + a.shape, a.dtype) for a in bigs]
    out_shape.append(jax.ShapeDtypeStruct((N_CHIPS,) + small.shape, small.dtype))
    res = pl.pallas_call(
        body, name="gather_weights", in_specs=[ANY] * (n + 1), out_specs=[ANY] * (n + 1), out_shape=out_shape,
        scratch_shapes=[pltpu.SemaphoreType.DMA((3 * n,))] * 4
        + [pltpu.SemaphoreType.DMA((n + 1,)), pltpu.SemaphoreType.DMA((3,)), pltpu.SemaphoreType.DMA((3,))],
    )(*bigs, small)
    return res[:n], res[n]


def _pair_send(dws):
    n = len(dws)

    def body(*refs):
        ins, outs = refs[:n], refs[n:2 * n]
        ssem, rsem = refs[2 * n:]
        x, y, c = _coords()
        cps = []
        for i in range(n):
            h = ins[i].shape[1] // 2
            cps.append(_remote(ins[i].at[:, pl.ds((1 - c) * h, h)], outs[i], ssem.at[i], rsem.at[i], (x, y, 1 - c)))
        for cp in cps:
            cp.start()
        for cp in cps:
            cp.wait()

    out_shape = [jax.ShapeDtypeStruct((a.shape[0], a.shape[1] // 2, a.shape[2]), a.dtype) for a in dws]
    return pl.pallas_call(body, name="pair_send", in_specs=[ANY] * n, out_specs=[ANY] * n, out_shape=out_shape,
                          scratch_shapes=[pltpu.SemaphoreType.DMA((n,))] * 2)(*dws)


def _scatter_send(tbs):
    n = len(tbs)

    def body(*refs):
        ins, outs = refs[:n], refs[n:2 * n]
        ssem, rsem = refs[2 * n:]
        x, y, c = _coords()
        cps = []
        for i in range(n):
            for k, (px, py, pj) in enumerate(_chip_peers(x, y)):
                cps.append(_remote(ins[i].at[pj], outs[i].at[k], ssem.at[i * 3 + k], rsem.at[i * 3 + k], (px, py, c)))
        for cp in cps:
            cp.start()
        for cp in cps:
            cp.wait()

    out_shape = [jax.ShapeDtypeStruct((3,) + a.shape[1:], a.dtype) for a in tbs]
    return pl.pallas_call(body, name="scatter_send", in_specs=[ANY] * n, out_specs=[ANY] * n, out_shape=out_shape,
                          scratch_shapes=[pltpu.SemaphoreType.DMA((3 * n,))] * 2)(*tbs)


def _half_swap(gs):
    n = len(gs)

    def body(*refs):
        bufs = refs[n:2 * n]
        ssem, rsem = refs[2 * n:]
        x, y, c = _coords()
        cps = []
        for i in range(n):
            h = bufs[i].shape[1] // 2
            mine = bufs[i].at[:, pl.ds(c * h, h)]
            cps.append(_remote(mine, mine, ssem.at[i], rsem.at[i], (x, y, 1 - c)))
        for cp in cps:
            cp.start()
        for i, cp in enumerate(cps):
            h = bufs[i].shape[1] // 2
            theirs = bufs[i].at[:, pl.ds((1 - c) * h, h)]
            cp.wait_send()
            _remote(theirs, theirs, ssem.at[i], rsem.at[i], (x, y, 1 - c)).wait_recv()

    return pl.pallas_call(body, name="half_swap", in_specs=[ANY] * n, out_specs=[ANY] * n,
                          out_shape=[jax.ShapeDtypeStruct(a.shape, a.dtype) for a in gs],
                          input_output_aliases={i: i for i in range(n)},
                          scratch_shapes=[pltpu.SemaphoreType.DMA((n,))] * 2)(*gs)


def _allgather_small(part):
    def body(in_ref, out_ref, ssem, rsem, lsem):
        x, y, c = _coords()
        me = 4 * x + 2 * y + c
        local = pltpu.make_async_copy(in_ref, out_ref.at[me], lsem)
        local.start()
        cps, peers = [], []
        for r in range(1, N_DEV):
            px = 1 - x if r & 4 else x
            py = 1 - y if r & 2 else y
            pc = 1 - c if r & 1 else c
            peers.append((4 * px + 2 * py + pc, (px, py, pc)))
            cps.append(_remote(in_ref, out_ref.at[me], ssem.at[r - 1], rsem.at[r - 1], (px, py, pc)))
        for cp in cps:
            cp.start()
        for r, cp in enumerate(cps):
            cp.wait_send()
            slot = out_ref.at[peers[r][0]]
            _remote(slot, slot, ssem.at[r], rsem.at[r], peers[r][1]).wait_recv()
        local.wait()

    return pl.pallas_call(body, name="allgather_small", in_specs=[ANY], out_specs=ANY,
                          out_shape=jax.ShapeDtypeStruct((N_DEV,) + part.shape, part.dtype),
                          scratch_shapes=[pltpu.SemaphoreType.DMA((N_DEV - 1,)), pltpu.SemaphoreType.DMA((N_DEV - 1,)),
                                          pltpu.SemaphoreType.DMA])(part)


ROW_ALIGN = 8
BIG = ("a_w_in", "a_w_out", "b_w_in", "b_w_pool", "b_w_out", "c_w_in", "c_w_out")
SMALL_SHARDED = ("a_ln_g", "a_ln_b", "c_conv_w", "c_conv_b", "c_ln_g", "c_ln_b")
SMALL_ORDER = ("a_ln_g", "a_ln_b", "a_w_s", "a_b_s", "b_scale", "c_conv_w", "c_conv_b", "c_ln_g", "c_ln_b",
               "post_ln_g", "post_ln_b")
WEIGHTS = ("a_w_in", "a_ln_g", "a_ln_b", "a_w_s", "a_b_s", "a_w_out", "b_w_in", "b_w_pool", "b_scale", "b_w_out",
           "c_w_in", "c_conv_w", "c_conv_b", "c_ln_g", "c_ln_b", "c_w_out", "post_ln_g", "post_ln_b")


def _pad_rows(a):
    pad = -a.shape[0] % ROW_ALIGN
    return jnp.pad(a, ((0, pad), (0, 0))) if pad else a


def _pack(parts, width):
    return jnp.concatenate([_pad_rows(p.reshape(-1, width)) for p in parts], axis=0)


def _unpack(packed, shapes, width):
    out, row = [], 0
    for shp in shapes:
        size = 1
        for n in shp:
            size *= n
        rows = size // width
        out.append(packed[row:row + rows].reshape(shp))
        row += rows + (-rows % ROW_ALIGN)
    return out


def _step(x, tgt, w, m, v):
    s, d = x.shape
    cx, cy, cc = _coords()
    j_me = 2 * cx + cy
    sp = jnp.stack([j_me, cc]).astype(jnp.int32)
    es = w["a_ln_g"].shape[1]
    e = N_CHIPS * es

    shard_bf = [_cast_bf16("cast_" + k, _as2d(w[k])) for k in BIG]
    small_in = _pack([w[k] for k in SMALL_SHARDED], es)
    gathered, small_all = _gather_weights(shard_bf, small_in)
    wg = dict(zip(BIG, gathered))
    n_a, n_b, n_c = w["a_w_in"].shape[0], w["b_w_in"].shape[0], w["c_w_in"].shape[0]
    wg["a_w_in"] = wg["a_w_in"].reshape(N_CHIPS, n_a, d, -1)
    wg["b_w_in"] = wg["b_w_in"].reshape(N_CHIPS, n_b, d, -1)
    wg["c_w_in"] = wg["c_w_in"].reshape(N_CHIPS, n_c, d, -1)
    for k, nl in (("a_w_out", n_a), ("b_w_out", n_b), ("c_w_out", n_c)):
        wg[k] = wg[k].reshape(N_CHIPS, nl, es, d)
    ng = w["b_w_pool"].shape[1]
    wg["b_w_pool"] = wg["b_w_pool"].reshape(N_CHIPS, ng, -1, w["b_w_pool"].shape[3])
    small_full = jnp.swapaxes(small_all, 0, 1).reshape(small_all.shape[1], e)
    full_shapes = [(n_a, e), (n_a, e), (CONV_WIDTH, e), (1, e), (1, e), (1, e)]
    a_ln_g, a_ln_b, conv_w, conv_b, c_ln_g, c_ln_b = _unpack(small_full, full_shapes, e)

    post_g, post_b = w["post_ln_g"], w["post_ln_b"]
    b_scale = w["b_scale"]

    saved = []
    res, gp, bp, hb = x, jnp.ones((1, d), F32), jnp.zeros((1, d), F32), x
    for i in range(DEPTH):
        kind, l = i % 3, i // 3
        tag = "L%d" % i
        if kind == 0:
            proj = _mm_in(tag + "_in", hb, wg["a_w_in"], l)
            b_st = w["a_b_s"][l].T
            yg = _sgu_fwd(tag + "_sgu", proj, a_ln_g[l:l + 1], a_ln_b[l:l + 1], w["a_w_s"][l], b_st)
            extra = (b_st,)
            w_out = wg["a_w_out"]
        elif kind == 1:
            proj = _mm_in(tag + "_in", hb, wg["b_w_in"], l)
            p = _pool_fwd(tag + "_pool", proj)
            q, yg = _pool_mm(tag + "_poolmm", p, wg["b_w_pool"], proj, b_scale[l:l + 1])
            extra = (p, q)
            w_out = wg["b_w_out"]
        else:
            proj = _mm_in(tag + "_in", hb, wg["c_w_in"], l)
            cv = _conv_fwd(tag + "_conv", proj, conv_w, conv_b)
            yg = _conv_gate_fwd(tag + "_gate", cv, proj, c_ln_g, c_ln_b)
            extra = (cv,)
            w_out = wg["c_w_out"]
        xhat, hb_next, rstd = _mm_out_ln(tag + "_out", yg, w_out, l, res, gp, bp, post_g[i:i + 1], post_b[i:i + 1])
        saved.append((hb, proj, yg, xhat, rstd, extra))
        res, gp, bp, hb = xhat, post_g[i:i + 1], post_b[i:i + 1], hb_next

    sg = {k: [None] * w[k].shape[0] for k in ("a_ln_g", "a_ln_b", "a_w_s", "a_b_s", "post_ln_g", "post_ln_b")}
    dws = {}
    dr, drb, dg, db, loss_vec = _post_ln_bwd("L3_lnbwd", tgt, saved[-1][3], saved[-1][4],
                                             post_g[DEPTH - 1:DEPTH], post_b[DEPTH - 1:DEPTH], True)
    grad_x = None
    for i in reversed(range(DEPTH)):
        kind, l = i % 3, i // 3
        tag = "L%d" % i
        hb_in, proj, yg, _, _, extra = saved[i]
        sg["post_ln_g"][i], sg["post_ln_b"][i] = dg, db
        if kind == 0:
            dyg = _mm_dyg(tag + "_dyg", drb, wg["a_w_out"], l)
            dproj, dlg, dlb, dws_s, dbs_t = _sgu_bwd(tag + "_sgubwd", dyg, proj, a_ln_g[l:l + 1], a_ln_b[l:l + 1],
                                                     w["a_w_s"][l], extra[0])
            sg["a_ln_g"][l], sg["a_ln_b"][l], sg["a_w_s"][l], sg["a_b_s"][l] = dlg, dlb, dws_s, dbs_t.T
            w_in, key = wg["a_w_in"], "a"
        elif kind == 1:
            p, q = extra
            dyg = _mm_dyg(tag + "_dyg", drb, wg["b_w_out"], l)
            dq, dproj, dscale = _pool_bwd1(tag + "_poolbwd1", dyg, q, proj, b_scale[l:l + 1])
            dp = _pool_mm_bwd(tag + "_poolmmbwd", dq, wg["b_w_pool"])
            dproj = _pool_bwd2(tag + "_poolbwd2", dp, dproj)
            dws["b_w_pool"] = [_mm_dw_pool(tag + "_dwpool", p, dq, ng).reshape(N_CHIPS, -1, dq.shape[1] // ng)]
            sg["b_scale"] = [dscale]
            w_in, key = wg["b_w_in"], "b"
        else:
            cv = extra[0]
            dyg = _mm_dyg(tag + "_dyg", drb, wg["c_w_out"], l)
            dc, dproj, dlg, dlb, dcb = _conv_bwd1(tag + "_convbwd1", dyg, cv, proj, c_ln_g, c_ln_b)
            dgc, dcw = _conv_bwd2(tag + "_convbwd2", dc, proj, conv_w)
            dproj = _conv_bwd3(tag + "_convbwd3", dgc, proj, dproj)
            sg["c_ln_g"], sg["c_ln_b"], sg["c_conv_b"], sg["c_conv_w"] = [dlg], [dlb], [dcb], [dcw[:CONV_WIDTH]]
            w_in, key = wg["c_w_in"], "c"
        dws.setdefault(key + "_w_out", {})[l] = _mm_dw_out(tag + "_dwout", yg, drb).reshape(N_CHIPS, es, d)
        dh = _mm_dh(tag + "_dh", dproj, w_in, l, dr)
        dws.setdefault(key + "_w_in", {})[l] = _mm_dw_in(tag + "_dwin", hb_in, dproj)
        if i > 0:
            prev = saved[i - 1]
            dr, drb, dg, db, _ = _post_ln_bwd("L%d_lnbwd" % (i - 1), dh, prev[3], prev[4],
                                              post_g[i - 1:i], post_b[i - 1:i], False)
        else:
            grad_x = dh

    flat = []
    for k in BIG:
        if k == "b_w_pool":
            flat.append((k, 0, dws[k][0]))
        else:
            for l in sorted(dws[k]):
                flat.append((k, l, dws[k][l]))
    recv = _pair_send([a for _, _, a in flat])
    pair = [_pair_sum("pairsum_%s%d" % (k, l), sp, a, r) for (k, l, a), r in zip(flat, recv)]
    got = _scatter_send(pair)
    reduced = {}
    for (k, l, a), r, q3 in zip(flat, recv, got):
        nl = w[k].shape[0]
        reduced[k] = _final_sum("finalsum_%s%d" % (k, l), sp, a, r, q3, l, nl, reduced.get(k))
    swapped = _half_swap([reduced[k] for k in BIG])
    grads = {k: g.reshape(w[k].shape) for k, g in zip(BIG, swapped)}

    small_parts = []
    for k in SMALL_ORDER:
        small_parts.append(jnp.concatenate([a.reshape(1, -1) for a in sg[k]], axis=0))
    width = 8 * LANES
    packed = _pack(small_parts, width)
    summed = _sum_devices("sum_small", _allgather_small(packed))
    full_small_shapes = {"a_ln_g": (n_a, e), "a_ln_b": (n_a, e), "a_w_s": w["a_w_s"].shape, "a_b_s": w["a_b_s"].shape,
                         "b_scale": w["b_scale"].shape, "c_conv_w": (n_c, CONV_WIDTH, e), "c_conv_b": (n_c, e),
                         "c_ln_g": (n_c, e), "c_ln_b": (n_c, e), "post_ln_g": post_g.shape, "post_ln_b": post_b.shape}
    for k, g in zip(SMALL_ORDER, _unpack(summed, [full_small_shapes[k] for k in SMALL_ORDER], width)):
        if k in SMALL_SHARDED:
            g = lax.dynamic_slice_in_dim(g, j_me * es, es, axis=g.ndim - 1)
        grads[k] = g

    delta, new_m, new_v = {}, {}, {}
    for k in WEIGHTS:
        shp = w[k].shape
        two_d = (lambda a: a.reshape(-1, shp[-1])) if shp[-1] % LANES == 0 else (lambda a: a.reshape(1, -1))
        dl, nm, nv = _adam("adam_" + k, two_d(w[k]), two_d(grads[k]), two_d(m[k]), two_d(v[k]))
        delta[k], new_m[k], new_v[k] = dl.reshape(shp), nm.reshape(shp), nv.reshape(shp)

    loss = lax.psum(loss_vec[0, 0], ("x", "y", "c"))
    return loss, grad_x, grads, delta, new_m, new_v


def kernel(x, a_w_in, a_ln_g, a_ln_b, a_w_s, a_b_s, a_w_out, b_w_in, b_w_pool, b_scale, b_w_out, c_w_in, c_conv_w, c_conv_b, c_ln_g, c_ln_b, c_w_out, post_ln_g, post_ln_b, loss_target, m_a_w_in, m_a_ln_g, m_a_ln_b, m_a_w_s, m_a_b_s, m_a_w_out, m_b_w_in, m_b_w_pool, m_b_scale, m_b_w_out, m_c_w_in, m_c_conv_w, m_c_conv_b, m_c_ln_g, m_c_ln_b, m_c_w_out, m_post_ln_g, m_post_ln_b, v_a_w_in, v_a_ln_g, v_a_ln_b, v_a_w_s, v_a_b_s, v_a_w_out, v_b_w_in, v_b_w_pool, v_b_scale, v_b_w_out, v_c_w_in, v_c_conv_w, v_c_conv_b, v_c_ln_g, v_c_ln_b, v_c_w_out, v_post_ln_g, v_post_ln_b):
    args = locals()
    w = {k: args[k] for k in WEIGHTS}
    m = {k: args["m_" + k] for k in WEIGHTS}
    v = {k: args["v_" + k] for k in WEIGHTS}
    loss, grad_x, grads, delta, new_m, new_v = _step(x[0], loss_target[0], w, m, v)
    out = [loss, grad_x[None]]
    for group in (grads, delta, new_m, new_v):
        out.extend(group[k] for k in WEIGHTS)
    return tuple(out)
```

```python
import functools

import jax
import jax.numpy as jnp
from jax import lax
from jax.experimental import pallas as pl
from jax.experimental.pallas import tpu as pltpu

F32 = jnp.float32
BF16 = jnp.bfloat16
MESH = pl.DeviceIdType.MESH

DEPTH = 4
CHUNK = 128
SGU_HEADS = 8
POOL_WINDOWS = (2, 4, 8, 16)
POOL_HALO = 16
CONV_WIDTH = 31
CONV_HALO = 32
LN_EPS = 1e-5
ALPHA = (2.0 * DEPTH) ** 0.25
ADAM_LR, ADAM_B1, ADAM_B2, ADAM_EPS, ADAM_WD, ADAM_STEP = 0.001, 0.9, 0.999, 1e-08, 0.01, 10
N_CHIPS = 4
N_DEV = 8
LANES = 128
SUBLANES = 8
CONV_STRIDE = 4
CONV_SUB = SUBLANES * CONV_STRIDE
VMEM_LIMIT = 56 << 20
GELU_K = 0.7978845608028654
GELU_C = 0.044715


def _params(sem=None, vmem=VMEM_LIMIT):
    return pltpu.CompilerParams(dimension_semantics=sem, vmem_limit_bytes=vmem)


def _tile(dim, pref):
    if dim <= pref:
        return dim
    t = 1 << (pref.bit_length() - 1)
    while dim % t:
        t //= 2
    return t


ROW_BLOCK_BYTES = 2 << 20


def _row_tile(rows, cols):
    return _tile(rows, max(ROW_ALIGN, ROW_BLOCK_BYTES // (4 * cols)))


def _sigmoid(x):
    return 1.0 / (1.0 + jnp.exp(-x))


def _gelu(x):
    return 0.5 * x * (1.0 + jnp.tanh(GELU_K * (x + GELU_C * x * x * x)))


def _gelu_grad(x):
    t = jnp.tanh(GELU_K * (x + GELU_C * x * x * x))
    return 0.5 * (1.0 + t) + 0.5 * x * (1.0 - t * t) * GELU_K * (1.0 + 3.0 * GELU_C * x * x)


def _ln_stats(x):
    mu = jnp.mean(x, axis=-1, keepdims=True)
    d = x - mu
    var = jnp.mean(d * d, axis=-1, keepdims=True)
    rstd = lax.rsqrt(var + LN_EPS)
    return d * rstd, rstd


def _ln_bwd(dy_hat, xhat, rstd):
    m1 = jnp.mean(dy_hat, axis=-1, keepdims=True)
    m2 = jnp.mean(dy_hat * xhat, axis=-1, keepdims=True)
    return rstd * (dy_hat - m1 - xhat * m2)


def _rowsum(x):
    return jnp.sum(x, axis=0, keepdims=True)


def _mm(name, a, b, *, dims, grid, a_spec, b_spec, out_shape, out_specs, acc_shape,
        extra=(), extra_specs=(), epilogue=None, aliases=None):
    nk = grid[2]
    n_extra = len(extra)
    single = not isinstance(out_shape, (list, tuple))
    shapes = [out_shape] if single else list(out_shape)
    specs = [out_specs] if single else list(out_specs)
    n_out = len(shapes)

    def body(*refs):
        a_ref, b_ref = refs[0], refs[1]
        ex = refs[2:2 + n_extra]
        outs = refs[2 + n_extra:2 + n_extra + n_out]
        acc = refs[-1]
        k = pl.program_id(2)

        @pl.when(k == 0)
        def _():
            acc[...] = jnp.zeros_like(acc)

        acc[...] += lax.dot_general(a_ref[...].astype(BF16), b_ref[...].astype(BF16), dims,
                                    preferred_element_type=F32)

        @pl.when(k == nk - 1)
        def _():
            if epilogue is None:
                outs[0][...] = acc[...].astype(outs[0].dtype)
            else:
                epilogue(acc, ex, outs)

    res = pl.pallas_call(
        body, name=name, grid=grid, in_specs=[a_spec, b_spec, *extra_specs], out_specs=specs,
        out_shape=shapes, scratch_shapes=[pltpu.VMEM(acc_shape, F32)],
        input_output_aliases=aliases or {},
        compiler_params=_params(("parallel", "parallel", "arbitrary")),
    )(a, b, *extra)
    return res[0] if single else res


NN = (((1,), (0,)), ((), ()))
NT = (((1,), (1,)), ((), ()))
TN = (((0,), (0,)), ((), ()))


def _mm_in(name, hb, wg, layer):
    s, d = hb.shape
    nsh = wg.shape[3]
    n = N_CHIPS * nsh
    bm, bn, bk = _tile(s, 1024), _tile(nsh, 1024), _tile(d, 1024)
    per = nsh // bn
    return _mm(name, hb, wg, dims=NN, grid=(s // bm, n // bn, d // bk),
               a_spec=pl.BlockSpec((bm, bk), lambda m, j, k: (m, k)),
               b_spec=pl.BlockSpec((None, None, bk, bn), lambda m, j, k: (j // per, layer, k, j % per)),
               out_shape=jax.ShapeDtypeStruct((s, n), F32),
               out_specs=pl.BlockSpec((bm, bn), lambda m, j, k: (m, j)),
               acc_shape=(bm, bn))


def _mm_out_ln(name, yg, wg, layer, res, gp, bp, g, b):
    s, e = yg.shape
    esh, d = wg.shape[2], wg.shape[3]
    bm, bk = _tile(s, 512), _tile(esh, 512)
    per = esh // bk

    def epilogue(acc, ex, outs):
        res_ref, gp_ref, bp_ref, g_ref, b_ref = ex
        xhat_ref, hb_ref, rstd_ref = outs
        r = ALPHA * (res_ref[...] * gp_ref[...] + bp_ref[...]) + acc[...]
        xhat, rstd = _ln_stats(r)
        xhat_ref[...] = xhat
        hb_ref[...] = (xhat * g_ref[...] + b_ref[...]).astype(BF16)
        rstd_ref[...] = jnp.broadcast_to(rstd, rstd_ref.shape)

    vec = pl.BlockSpec((1, d), lambda m, j, k: (0, 0))
    row = pl.BlockSpec((bm, d), lambda m, j, k: (m, 0))
    return _mm(name, yg, wg, dims=NN, grid=(s // bm, 1, e // bk),
               a_spec=pl.BlockSpec((bm, bk), lambda m, j, k: (m, k)),
               b_spec=pl.BlockSpec((None, None, bk, d), lambda m, j, k: (k // per, layer, k % per, 0)),
               extra=(res, gp, bp, g, b), extra_specs=(row, vec, vec, vec, vec),
               out_shape=(jax.ShapeDtypeStruct((s, d), F32), jax.ShapeDtypeStruct((s, d), BF16),
                          jax.ShapeDtypeStruct((s, LANES), F32)),
               out_specs=(row, row, pl.BlockSpec((bm, LANES), lambda m, j, k: (m, 0))),
               acc_shape=(bm, d), epilogue=epilogue)


def _mm_dyg(name, drb, wg, layer):
    s, d = drb.shape
    esh = wg.shape[2]
    e = N_CHIPS * esh
    bm, bn, bk = _tile(s, 1024), _tile(esh, 1024), _tile(d, 1024)
    per = esh // bn
    return _mm(name, drb, wg, dims=NT, grid=(s // bm, e // bn, d // bk),
               a_spec=pl.BlockSpec((bm, bk), lambda m, j, k: (m, k)),
               b_spec=pl.BlockSpec((None, None, bn, bk), lambda m, j, k: (j // per, layer, j % per, k)),
               out_shape=jax.ShapeDtypeStruct((s, e), F32),
               out_specs=pl.BlockSpec((bm, bn), lambda m, j, k: (m, j)),
               acc_shape=(bm, bn))


def _mm_dw_out(name, yg, drb):
    s, e = yg.shape
    d = drb.shape[1]
    bm, bn, bk = _tile(e, 1024), _tile(d, 1024), _tile(s, 1024)
    return _mm(name, yg, drb, dims=TN, grid=(e // bm, d // bn, s // bk),
               a_spec=pl.BlockSpec((bk, bm), lambda m, j, k: (k, m)),
               b_spec=pl.BlockSpec((bk, bn), lambda m, j, k: (k, j)),
               out_shape=jax.ShapeDtypeStruct((e, d), BF16),
               out_specs=pl.BlockSpec((bm, bn), lambda m, j, k: (m, j)),
               acc_shape=(bm, bn))


def _mm_dh(name, dproj, wg, layer, dr):
    s, n = dproj.shape
    d, nsh = wg.shape[2], wg.shape[3]
    bm, bn, bk = _tile(s, 1024), _tile(d, 1024), _tile(nsh, 1024)
    per = nsh // bk

    def epilogue(acc, ex, outs):
        outs[0][...] = ALPHA * ex[0][...] + acc[...]

    blk = pl.BlockSpec((bm, bn), lambda m, j, k: (m, j))
    return _mm(name, dproj, wg, dims=NT, grid=(s // bm, d // bn, n // bk),
               a_spec=pl.BlockSpec((bm, bk), lambda m, j, k: (m, k)),
               b_spec=pl.BlockSpec((None, None, bn, bk), lambda m, j, k: (k // per, layer, j, k % per)),
               extra=(dr,), extra_specs=(blk,),
               out_shape=jax.ShapeDtypeStruct((s, d), F32), out_specs=blk,
               acc_shape=(bm, bn), epilogue=epilogue)


def _mm_dw_in(name, hb, dproj):
    s, d = hb.shape
    n = dproj.shape[1]
    nsh = n // N_CHIPS
    bm, bn, bk = _tile(d, 1024), _tile(nsh, 1024), _tile(s, 1024)
    per = nsh // bn
    return _mm(name, hb, dproj, dims=TN, grid=(d // bm, n // bn, s // bk),
               a_spec=pl.BlockSpec((bk, bm), lambda m, j, k: (k, m)),
               b_spec=pl.BlockSpec((bk, bn), lambda m, j, k: (k, j)),
               out_shape=jax.ShapeDtypeStruct((N_CHIPS, d, nsh), BF16),
               out_specs=pl.BlockSpec((None, bm, bn), lambda m, j, k: (j // per, m, j % per)),
               acc_shape=(bm, bn))


def _causal_mask():
    r = lax.broadcasted_iota(jnp.int32, (CHUNK, CHUNK), 0)
    c = lax.broadcasted_iota(jnp.int32, (CHUNK, CHUNK), 1)
    return r >= c


def _sgu_fwd(name, proj, ln_g, ln_b, w_s, b_st):
    s, n3 = proj.shape
    e = n3 // 3
    hd = e // SGU_HEADS

    def body(proj_ref, g_ref, b_ref, ws_ref, bst_ref, yg_ref):
        xhat, _ = _ln_stats(_gelu(proj_ref[:, e:2 * e]))
        vnb = (xhat * g_ref[...] + b_ref[...]).astype(BF16)
        tri = _causal_mask()
        for h in range(SGU_HEADS):
            hs = slice(h * hd, (h + 1) * hd)
            ws = jnp.where(tri, ws_ref[h], 0.0).astype(BF16)
            mixed = jnp.dot(ws, vnb[:, hs], preferred_element_type=F32) + bst_ref[:, h:h + 1]
            u = proj_ref[:, hs]
            z = proj_ref[:, 2 * e + h * hd:2 * e + (h + 1) * hd]
            yg_ref[:, hs] = (_gelu(u) * mixed * (z * _sigmoid(z))).astype(BF16)

    vec = pl.BlockSpec((1, e), lambda i: (0, 0))
    return pl.pallas_call(
        body, name=name, grid=(s // CHUNK,),
        in_specs=[pl.BlockSpec((CHUNK, n3), lambda i: (i, 0)), vec, vec,
                  pl.BlockSpec((SGU_HEADS, CHUNK, CHUNK), lambda i: (0, 0, 0)),
                  pl.BlockSpec((CHUNK, SGU_HEADS), lambda i: (0, 0))],
        out_specs=pl.BlockSpec((CHUNK, e), lambda i: (i, 0)),
        out_shape=jax.ShapeDtypeStruct((s, e), BF16),
        compiler_params=_params(("parallel",)),
    )(proj, ln_g, ln_b, w_s, b_st)


def _sgu_bwd(name, dyg, proj, ln_g, ln_b, w_s, b_st):
    s, n3 = proj.shape
    e = n3 // 3
    hd = e // SGU_HEADS

    def body(dyg_ref, proj_ref, g_ref, b_ref, ws_ref, bst_ref,
             dproj_ref, dg_ref, db_ref, dws_ref, dbs_ref, dvn_ref):
        @pl.when(pl.program_id(0) == 0)
        def _():
            dg_ref[...] = jnp.zeros_like(dg_ref)
            db_ref[...] = jnp.zeros_like(db_ref)
            dws_ref[...] = jnp.zeros_like(dws_ref)
            dbs_ref[...] = jnp.zeros_like(dbs_ref)

        v = proj_ref[:, e:2 * e]
        xhat, rstd = _ln_stats(_gelu(v))
        g = g_ref[...]
        vnb = (xhat * g + b_ref[...]).astype(BF16)
        tri = _causal_mask()
        for h in range(SGU_HEADS):
            hs = slice(h * hd, (h + 1) * hd)
            zs = slice(2 * e + h * hd, 2 * e + (h + 1) * hd)
            ws = jnp.where(tri, ws_ref[h], 0.0).astype(BF16)
            mixed = jnp.dot(ws, vnb[:, hs], preferred_element_type=F32) + bst_ref[:, h:h + 1]
            u = proj_ref[:, hs]
            z = proj_ref[:, zs]
            gu = _gelu(u)
            sig = _sigmoid(z)
            dy = dyg_ref[:, hs]
            t = dy * (z * sig)
            dmixed = t * gu
            dproj_ref[:, hs] = (t * mixed * _gelu_grad(u)).astype(BF16)
            dproj_ref[:, zs] = (dy * gu * mixed * (sig * (1.0 + z * (1.0 - sig)))).astype(BF16)
            dmb = dmixed.astype(BF16)
            dvn_ref[:, hs] = lax.dot_general(ws, dmb, TN, preferred_element_type=F32)
            dws = lax.dot_general(dmb, vnb[:, hs], NT, preferred_element_type=F32)
            dws_ref[h] += jnp.where(tri, dws, 0.0)
            dbs_ref[:, h:h + 1] += jnp.sum(dmixed, axis=-1, keepdims=True)
        dvn = dvn_ref[...]
        dg_ref[...] += _rowsum(dvn * xhat)
        db_ref[...] += _rowsum(dvn)
        dgv = _ln_bwd(dvn * g, xhat, rstd)
        dproj_ref[:, e:2 * e] = (dgv * _gelu_grad(v)).astype(BF16)

    vec = pl.BlockSpec((1, e), lambda i: (0, 0))
    wsp = pl.BlockSpec((SGU_HEADS, CHUNK, CHUNK), lambda i: (0, 0, 0))
    bsp = pl.BlockSpec((CHUNK, SGU_HEADS), lambda i: (0, 0))
    return pl.pallas_call(
        body, name=name, grid=(s // CHUNK,),
        in_specs=[pl.BlockSpec((CHUNK, e), lambda i: (i, 0)), pl.BlockSpec((CHUNK, n3), lambda i: (i, 0)),
                  vec, vec, wsp, bsp],
        out_specs=[pl.BlockSpec((CHUNK, n3), lambda i: (i, 0)), vec, vec, wsp, bsp],
        out_shape=[jax.ShapeDtypeStruct((s, n3), BF16), jax.ShapeDtypeStruct((1, e), F32),
                   jax.ShapeDtypeStruct((1, e), F32),
                   jax.ShapeDtypeStruct((SGU_HEADS, CHUNK, CHUNK), F32),
                   jax.ShapeDtypeStruct((CHUNK, SGU_HEADS), F32)],
        scratch_shapes=[pltpu.VMEM((CHUNK, e), F32)],
        compiler_params=_params(("arbitrary",)),
    )(dyg, proj, ln_g, ln_b, w_s, b_st)


def _pool_counts(pos, w):
    return jnp.minimum(pos + 1, w).astype(F32)


def _pool_fwd(name, proj):
    s, n2 = proj.shape
    e = n2 // 2
    gd = e // len(POOL_WINDOWS)
    ts = _row_tile(s, e)
    hb = ts // POOL_HALO

    def body(v_ref, halo_ref, p_ref, ext):
        i = pl.program_id(0)
        ext[POOL_HALO:, :] = v_ref[...]
        ext[:POOL_HALO, :] = jnp.where(i > 0, halo_ref[...], 0.0)
        pos = i * ts + lax.broadcasted_iota(jnp.int32, (ts, 1), 0)
        for g, w in enumerate(POOL_WINDOWS):
            cs = slice(g * gd, (g + 1) * gd)
            acc = ext[POOL_HALO:, cs]
            for j in range(1, w):
                acc = acc + ext[POOL_HALO - j:POOL_HALO - j + ts, cs]
            p_ref[:, cs] = (acc / _pool_counts(pos, w) - v_ref[:, cs]).astype(BF16)

    return pl.pallas_call(
        body, name=name, grid=(s // ts,),
        in_specs=[pl.BlockSpec((ts, e), lambda i: (i, 0)),
                  pl.BlockSpec((POOL_HALO, e), lambda i: (jnp.maximum(i * hb - 1, 0), 0))],
        out_specs=pl.BlockSpec((ts, e), lambda i: (i, 0)),
        out_shape=jax.ShapeDtypeStruct((s, e), BF16),
        scratch_shapes=[pltpu.VMEM((ts + POOL_HALO, e), F32)],
        compiler_params=_params(("parallel",)),
    )(proj, proj)


def _pool_mm(name, p, wg, proj, scale):
    s, e = p.shape
    ng, rsh, gd = wg.shape[1], wg.shape[2], wg.shape[3]
    bm = _tile(s, 1024)

    def epilogue(acc, ex, outs):
        z = ex[0][...]
        q = acc[...]
        outs[0][...] = q
        outs[1][...] = (q * ex[1][...] * (z * _sigmoid(z))).astype(BF16)

    blk = pl.BlockSpec((bm, gd), lambda m, g, k: (m, g))
    return _mm(name, p, wg, dims=NN, grid=(s // bm, ng, N_CHIPS),
               a_spec=pl.BlockSpec((bm, rsh), lambda m, g, k: (m, g * N_CHIPS + k)),
               b_spec=pl.BlockSpec((None, None, rsh, gd), lambda m, g, k: (k, g, 0, 0)),
               extra=(proj, scale),
               extra_specs=(pl.BlockSpec((bm, gd), lambda m, g, k: (m, ng + g)),
                            pl.BlockSpec((1, gd), lambda m, g, k: (0, g))),
               out_shape=(jax.ShapeDtypeStruct((s, e), F32), jax.ShapeDtypeStruct((s, e), BF16)),
               out_specs=(blk, blk), acc_shape=(bm, gd), epilogue=epilogue)


def _pool_bwd1(name, dyg, q, proj, scale):
    s, e = q.shape
    ts = _row_tile(s, e)

    def body(dyg_ref, q_ref, z_ref, sc_ref, dq_ref, dz_ref, dsc_ref):
        @pl.when(pl.program_id(0) == 0)
        def _():
            dsc_ref[...] = jnp.zeros_like(dsc_ref)

        z = z_ref[...]
        sig = _sigmoid(z)
        dy = dyg_ref[...]
        qv = q_ref[...]
        sc = sc_ref[...]
        dp2 = dy * (z * sig)
        dsc_ref[...] += _rowsum(dp2 * qv)
        dq_ref[...] = (dp2 * sc).astype(BF16)
        dz_ref[...] = (dy * (qv * sc) * (sig * (1.0 + z * (1.0 - sig)))).astype(BF16)

    blk = pl.BlockSpec((ts, e), lambda i: (i, 0))
    vec = pl.BlockSpec((1, e), lambda i: (0, 0))
    return pl.pallas_call(
        body, name=name, grid=(s // ts,),
        in_specs=[blk, blk, pl.BlockSpec((ts, e), lambda i: (i, 1)), vec],
        out_specs=[blk, pl.BlockSpec((ts, e), lambda i: (i, 1)), vec],
        out_shape=[jax.ShapeDtypeStruct((s, e), BF16), jax.ShapeDtypeStruct((s, 2 * e), BF16),
                   jax.ShapeDtypeStruct((1, e), F32)],
        compiler_params=_params(("arbitrary",)),
    )(dyg, q, proj, scale)


def _pool_mm_bwd(name, dq, wg):
    s, e = dq.shape
    ng, rsh, gd = wg.shape[1], wg.shape[2], wg.shape[3]
    bm = _tile(s, 1024)
    return _mm(name, dq, wg, dims=NT, grid=(s // bm, ng * N_CHIPS, 1),
               a_spec=pl.BlockSpec((bm, gd), lambda m, j, k: (m, j // N_CHIPS)),
               b_spec=pl.BlockSpec((None, None, rsh, gd), lambda m, j, k: (j % N_CHIPS, j // N_CHIPS, 0, 0)),
               out_shape=jax.ShapeDtypeStruct((s, e), F32),
               out_specs=pl.BlockSpec((bm, rsh), lambda m, j, k: (m, j)),
               acc_shape=(bm, rsh))


def _pool_bwd2(name, dp, dproj):
    s, e = dp.shape
    gd = e // len(POOL_WINDOWS)
    ts = _row_tile(s, e)
    n = s // ts
    hb = ts // POOL_HALO

    def body(dp_ref, nxt_ref, _, dv_ref, ext):
        i = pl.program_id(0)
        pos = i * ts + lax.broadcasted_iota(jnp.int32, (ts + POOL_HALO, 1), 0)
        for g, w in enumerate(POOL_WINDOWS):
            cs = slice(g * gd, (g + 1) * gd)
            cnt = _pool_counts(pos, w)
            ext[:ts, cs] = dp_ref[:, cs] / cnt[:ts]
            ext[ts:, cs] = jnp.where(i < n - 1, nxt_ref[:, cs] / cnt[ts:], 0.0)
        for g, w in enumerate(POOL_WINDOWS):
            cs = slice(g * gd, (g + 1) * gd)
            acc = ext[:ts, cs]
            for j in range(1, w):
                acc = acc + ext[j:j + ts, cs]
            dv_ref[:, cs] = (acc - dp_ref[:, cs]).astype(BF16)

    return pl.pallas_call(
        body, name=name, grid=(n,),
        in_specs=[pl.BlockSpec((ts, e), lambda i: (i, 0)),
                  pl.BlockSpec((POOL_HALO, e), lambda i: (jnp.minimum((i + 1) * hb, s // POOL_HALO - 1), 0)),
                  pl.BlockSpec(memory_space=pl.ANY)],
        out_specs=pl.BlockSpec((ts, e), lambda i: (i, 0)),
        out_shape=jax.ShapeDtypeStruct((s, 2 * e), BF16),
        scratch_shapes=[pltpu.VMEM((ts + POOL_HALO, e), F32)],
        input_output_aliases={2: 0},
        compiler_params=_params(("parallel",)),
    )(dp, dp, dproj)


def _mm_dw_pool(name, p, dq, ng):
    s, e = p.shape
    gd = e // ng
    rsh = gd // N_CHIPS
    bk = _tile(s, 1024)
    return _mm(name, p, dq, dims=TN, grid=(N_CHIPS, ng, s // bk),
               a_spec=pl.BlockSpec((bk, rsh), lambda m, g, k: (k, g * N_CHIPS + m)),
               b_spec=pl.BlockSpec((bk, gd), lambda m, g, k: (k, g)),
               out_shape=jax.ShapeDtypeStruct((N_CHIPS, ng, rsh, gd), BF16),
               out_specs=pl.BlockSpec((None, None, rsh, gd), lambda m, g, k: (m, g, 0, 0)),
               acc_shape=(rsh, gd))


def _conv_fwd(name, proj, conv_w, conv_b):
    s, n3 = proj.shape
    e = n3 // 3
    ts, tc = _tile(s, 512), LANES
    ne = e // tc
    hb = ts // CONV_HALO
    off = CONV_HALO - (CONV_WIDTH - 1)

    def body(a_ref, gl_ref, ha_ref, hgl_ref, w_ref, b_ref, c_ref, gext):
        i = pl.program_id(1)
        gext[CONV_HALO:, :] = a_ref[...] * _sigmoid(gl_ref[...])
        gext[:CONV_HALO, :] = jnp.where(i > 0, ha_ref[...] * _sigmoid(hgl_ref[...]), 0.0)
        taps = [jnp.broadcast_to(w_ref[k:k + 1, :], (SUBLANES, LANES)) for k in range(CONV_WIDTH)]
        bias = jnp.broadcast_to(b_ref[...], (SUBLANES, LANES))

        def sub_block(rb, carry):
            r0 = pl.multiple_of(rb * CONV_SUB, CONV_SUB)
            acc = [bias] * CONV_STRIDE
            for m in range(CONV_WIDTH + CONV_STRIDE - 1):
                g_m = gext[pl.ds(r0 + off + m, SUBLANES, stride=CONV_STRIDE), :]
                for q in range(CONV_STRIDE):
                    if 0 <= m - q < CONV_WIDTH:
                        acc[q] = acc[q] + taps[m - q] * g_m
            for q in range(CONV_STRIDE):
                c_ref[pl.ds(r0 + q, SUBLANES, stride=CONV_STRIDE), :] = acc[q]
            return carry

        lax.fori_loop(0, ts // CONV_SUB, sub_block, 0)

    halo = lambda i: jnp.maximum(i * hb - 1, 0)
    return pl.pallas_call(
        body, name=name, grid=(ne, s // ts),
        in_specs=[pl.BlockSpec((ts, tc), lambda j, i: (i, j)),
                  pl.BlockSpec((ts, tc), lambda j, i: (i, ne + j)),
                  pl.BlockSpec((CONV_HALO, tc), lambda j, i: (halo(i), j)),
                  pl.BlockSpec((CONV_HALO, tc), lambda j, i: (halo(i), ne + j)),
                  pl.BlockSpec((CONV_WIDTH, tc), lambda j, i: (0, j)),
                  pl.BlockSpec((1, tc), lambda j, i: (0, j))],
        out_specs=pl.BlockSpec((ts, tc), lambda j, i: (i, j)),
        out_shape=jax.ShapeDtypeStruct((s, e), F32),
        scratch_shapes=[pltpu.VMEM((ts + CONV_HALO, tc), F32)],
        compiler_params=_params(("parallel", "parallel")),
    )(proj, proj, proj, proj, conv_w, conv_b)


def _conv_gate_fwd(name, c, proj, ln_g, ln_b):
    s, e = c.shape
    ts = _row_tile(s, e)

    def body(c_ref, z_ref, g_ref, b_ref, yg_ref):
        xhat, _ = _ln_stats(c_ref[...])
        ln = xhat * g_ref[...] + b_ref[...]
        z = z_ref[...]
        yg_ref[...] = ((ln * _sigmoid(ln)) * (z * _sigmoid(z))).astype(BF16)

    blk = pl.BlockSpec((ts, e), lambda i: (i, 0))
    vec = pl.BlockSpec((1, e), lambda i: (0, 0))
    return pl.pallas_call(
        body, name=name, grid=(s // ts,),
        in_specs=[blk, pl.BlockSpec((ts, e), lambda i: (i, 2)), vec, vec],
        out_specs=blk, out_shape=jax.ShapeDtypeStruct((s, e), BF16),
        compiler_params=_params(("parallel",)),
    )(c, proj, ln_g, ln_b)


def _conv_bwd1(name, dyg, c, proj, ln_g, ln_b):
    s, e = c.shape
    ts = _row_tile(s, e)

    def body(dyg_ref, c_ref, z_ref, g_ref, b_ref, dc_ref, dz_ref, dg_ref, db_ref, dcb_ref):
        @pl.when(pl.program_id(0) == 0)
        def _():
            dg_ref[...] = jnp.zeros_like(dg_ref)
            db_ref[...] = jnp.zeros_like(db_ref)
            dcb_ref[...] = jnp.zeros_like(dcb_ref)

        xhat, rstd = _ln_stats(c_ref[...])
        g = g_ref[...]
        ln = xhat * g + b_ref[...]
        sl = _sigmoid(ln)
        z = z_ref[...]
        sz = _sigmoid(z)
        dy = dyg_ref[...]
        dz_ref[...] = (dy * (ln * sl) * (sz * (1.0 + z * (1.0 - sz)))).astype(BF16)
        dln = dy * (z * sz) * (sl * (1.0 + ln * (1.0 - sl)))
        dg_ref[...] += _rowsum(dln * xhat)
        db_ref[...] += _rowsum(dln)
        dc = _ln_bwd(dln * g, xhat, rstd)
        dc_ref[...] = dc
        dcb_ref[...] += _rowsum(dc)

    blk = pl.BlockSpec((ts, e), lambda i: (i, 0))
    zblk = pl.BlockSpec((ts, e), lambda i: (i, 2))
    vec = pl.BlockSpec((1, e), lambda i: (0, 0))
    vshape = jax.ShapeDtypeStruct((1, e), F32)
    return pl.pallas_call(
        body, name=name, grid=(s // ts,),
        in_specs=[blk, blk, zblk, vec, vec],
        out_specs=[blk, zblk, vec, vec, vec],
        out_shape=[jax.ShapeDtypeStruct((s, e), F32), jax.ShapeDtypeStruct((s, 3 * e), BF16),
                   vshape, vshape, vshape],
        compiler_params=_params(("arbitrary",)),
    )(dyg, c, proj, ln_g, ln_b)


def _conv_bwd2(name, dc, proj, conv_w):
    s, e = dc.shape
    ts, tc = _tile(s, 512), LANES
    ne = e // tc
    n = s // ts
    hb = ts // CONV_HALO
    off = CONV_HALO - (CONV_WIDTH - 1)
    span = CONV_WIDTH + CONV_STRIDE - 1

    def body(dc_ref, nxt_ref, a_ref, gl_ref, ha_ref, hgl_ref, w_ref, dg_ref, dw_ref, gext, dcext, dwacc):
        i = pl.program_id(1)

        @pl.when(i == 0)
        def _():
            dwacc[...] = jnp.zeros_like(dwacc)

        gext[CONV_HALO:, :] = a_ref[...] * _sigmoid(gl_ref[...])
        gext[:CONV_HALO, :] = jnp.where(i > 0, ha_ref[...] * _sigmoid(hgl_ref[...]), 0.0)
        dcext[:ts, :] = dc_ref[...]
        dcext[ts:, :] = jnp.where(i < n - 1, nxt_ref[...], 0.0)
        taps = [jnp.broadcast_to(w_ref[k:k + 1, :], (SUBLANES, LANES)) for k in range(CONV_WIDTH)]

        def sub_block(rb, carry):
            r0 = pl.multiple_of(rb * CONV_SUB, CONV_SUB)
            dc_m = [dcext[pl.ds(r0 + m, SUBLANES, stride=CONV_STRIDE), :] for m in range(span)]
            dg = [jnp.zeros((SUBLANES, LANES), F32)] * CONV_STRIDE
            for k in range(CONV_WIDTH):
                for q in range(CONV_STRIDE):
                    dg[q] = dg[q] + taps[k] * dc_m[CONV_WIDTH - 1 - k + q]
            for q in range(CONV_STRIDE):
                dg_ref[pl.ds(r0 + q, SUBLANES, stride=CONV_STRIDE), :] = dg[q]
            g_m = [gext[pl.ds(r0 + off + m, SUBLANES, stride=CONV_STRIDE), :] for m in range(span)]
            for k in range(CONV_WIDTH):
                part = dc_m[0] * g_m[k]
                for q in range(1, CONV_STRIDE):
                    part = part + dc_m[q] * g_m[k + q]
                dwacc[k] += part
            return carry

        lax.fori_loop(0, ts // CONV_SUB, sub_block, 0)

        @pl.when(i == n - 1)
        def _():
            dw_ref[...] = jnp.sum(dwacc[...], axis=1)

    halo = lambda i: jnp.maximum(i * hb - 1, 0)
    nxt = lambda i: jnp.minimum((i + 1) * hb, s // CONV_HALO - 1)
    return pl.pallas_call(
        body, name=name, grid=(ne, n),
        in_specs=[pl.BlockSpec((ts, tc), lambda j, i: (i, j)),
                  pl.BlockSpec((CONV_HALO, tc), lambda j, i: (nxt(i), j)),
                  pl.BlockSpec((ts, tc), lambda j, i: (i, j)),
                  pl.BlockSpec((ts, tc), lambda j, i: (i, ne + j)),
                  pl.BlockSpec((CONV_HALO, tc), lambda j, i: (halo(i), j)),
                  pl.BlockSpec((CONV_HALO, tc), lambda j, i: (halo(i), ne + j)),
                  pl.BlockSpec((CONV_WIDTH, tc), lambda j, i: (0, j))],
        out_specs=[pl.BlockSpec((ts, tc), lambda j, i: (i, j)),
                   pl.BlockSpec((CONV_HALO, tc), lambda j, i: (0, j))],
        out_shape=[jax.ShapeDtypeStruct((s, e), F32), jax.ShapeDtypeStruct((CONV_HALO, e), F32)],
        scratch_shapes=[pltpu.VMEM((ts + CONV_HALO, tc), F32), pltpu.VMEM((ts + CONV_HALO, tc), F32),
                        pltpu.VMEM((CONV_HALO, SUBLANES, LANES), F32)],
        compiler_params=_params(("parallel", "arbitrary")),
    )(dc, dc, proj, proj, proj, proj, conv_w)


def _conv_bwd3(name, dg, proj, dproj):
    s, e = dg.shape
    ts = _row_tile(s, 2 * e)

    def body(dg_ref, agl_ref, _, out_ref):
        a = agl_ref[:, :e]
        sig = _sigmoid(agl_ref[:, e:])
        dgv = dg_ref[...]
        out_ref[:, :e] = (dgv * sig).astype(BF16)
        out_ref[:, e:] = (dgv * a * sig * (1.0 - sig)).astype(BF16)

    wide = pl.BlockSpec((ts, 2 * e), lambda i: (i, 0))
    return pl.pallas_call(
        body, name=name, grid=(s // ts,),
        in_specs=[pl.BlockSpec((ts, e), lambda i: (i, 0)), wide, pl.BlockSpec(memory_space=pl.ANY)],
        out_specs=wide, out_shape=jax.ShapeDtypeStruct((s, 3 * e), BF16),
        input_output_aliases={2: 0},
        compiler_params=_params(("parallel",)),
    )(dg, proj, dproj)


def _post_ln_bwd(name, dy_or_target, xhat, rstd, g, b, loss_head, dep=None):
    s, d = xhat.shape
    ts = _row_tile(s, d)

    def body(dy_ref, xhat_ref, rstd_ref, g_ref, b_ref, *rest):
        dr_ref, drb_ref, dg_ref, db_ref, loss_ref = rest[-5:]

        @pl.when(pl.program_id(0) == 0)
        def _():
            dg_ref[...] = jnp.zeros_like(dg_ref)
            db_ref[...] = jnp.zeros_like(db_ref)
            loss_ref[...] = jnp.zeros_like(loss_ref)

        xh = xhat_ref[...]
        gv = g_ref[...]
        if loss_head:
            err = (xh * gv + b_ref[...]) - dy_ref[...]
            per_row = jnp.mean(err * err, axis=-1, keepdims=True)
            loss_ref[...] += 0.5 * jnp.broadcast_to(_rowsum(per_row), loss_ref.shape)
            dy = err / d
        else:
            dy = dy_ref[...]
        dg_ref[...] += _rowsum(dy * xh)
        db_ref[...] += _rowsum(dy)
        dr = _ln_bwd(dy * gv, xh, rstd_ref[:, 0:1])
        dr_ref[...] = dr
        drb_ref[...] = dr.astype(BF16)

    blk = pl.BlockSpec((ts, d), lambda i: (i, 0))
    vec = pl.BlockSpec((1, d), lambda i: (0, 0))
    deps = [] if dep is None else [dep]
    return pl.pallas_call(
        body, name=name, grid=(s // ts,),
        in_specs=[blk, blk, pl.BlockSpec((ts, LANES), lambda i: (i, 0)), vec, vec] + [ANY] * len(deps),
        out_specs=[blk, blk, vec, vec, pl.BlockSpec((1, LANES), lambda i: (0, 0))],
        out_shape=[jax.ShapeDtypeStruct((s, d), F32), jax.ShapeDtypeStruct((s, d), BF16),
                   jax.ShapeDtypeStruct((1, d), F32), jax.ShapeDtypeStruct((1, d), F32),
                   jax.ShapeDtypeStruct((1, LANES), F32)],
        compiler_params=_params(("arbitrary",)),
    )(dy_or_target, xhat, rstd, g, b, *deps)


def _as2d(a):
    return a.reshape(-1, a.shape[-1])


def _cast_layer(name, a, layer):
    _, r, c = a.shape
    tr = _row_tile(r, c)

    def body(a_ref, o_ref):
        o_ref[...] = a_ref[...].astype(BF16)

    return pl.pallas_call(body, name=name, grid=(r // tr,),
                          in_specs=[pl.BlockSpec((None, tr, c), lambda i: (layer, i, 0))],
                          out_specs=pl.BlockSpec((tr, c), lambda i: (i, 0)),
                          out_shape=jax.ShapeDtypeStruct((r, c), BF16),
                          compiler_params=_params(("parallel",)))(a)


def _my_shard():
    return 2 * lax.axis_index("x") + lax.axis_index("y")


def _pair_sum(name, own, recv):
    nsh, r, c = own.shape
    h = r // 2
    tr = _row_tile(h, c)
    nb = h // tr

    def body(own_ref, recv_ref, o_ref):
        o_ref[...] = (own_ref[...].astype(F32) + recv_ref[...].astype(F32)).astype(BF16)

    blk = pl.BlockSpec((None, tr, c), lambda j, i: (j, i, 0))
    mine = pl.BlockSpec((None, tr, c), lambda j, i: (j, lax.axis_index("c") * nb + i, 0))
    return pl.pallas_call(body, name=name, grid=(nsh, nb), in_specs=[mine, blk], out_specs=blk,
                          out_shape=jax.ShapeDtypeStruct((nsh, h, c), BF16),
                          compiler_params=_params(("parallel", "parallel")))(own, recv)


def _final_sum(name, own, recv, got, layer, n_layers, prev):
    _, r, c = own.shape
    h = r // 2
    tr = _row_tile(h, c)
    nb = h // tr

    def body(own_ref, recv_ref, q0_ref, q1_ref, q2_ref, *rest):
        o_ref = rest[-1]
        acc = own_ref[...].astype(F32) + recv_ref[...].astype(F32)
        acc = acc + q0_ref[...].astype(F32)
        acc = acc + q1_ref[...].astype(F32)
        o_ref[...] = acc + q2_ref[...].astype(F32)

    in_specs = [pl.BlockSpec((None, tr, c), lambda i: (_my_shard(), lax.axis_index("c") * nb + i, 0)),
                pl.BlockSpec((None, tr, c), lambda i: (_my_shard(), i, 0))]
    in_specs += [pl.BlockSpec((None, tr, c), functools.partial(lambda i, k: (k, i, 0), k=k)) for k in range(3)]
    operands = [own, recv, got, got, got]
    aliases = {}
    if prev is not None:
        in_specs.append(pl.BlockSpec(memory_space=pl.ANY))
        operands.append(prev)
        aliases = {5: 0}
    return pl.pallas_call(body, name=name, grid=(nb,), in_specs=in_specs,
                          out_specs=pl.BlockSpec((None, tr, c), lambda i: (layer, lax.axis_index("c") * nb + i, 0)),
                          out_shape=jax.ShapeDtypeStruct((n_layers, r, c), F32),
                          input_output_aliases=aliases,
                          compiler_params=_params(("parallel",)))(*operands)


def _adam(name, w, g, m, v):
    r, c = w.shape
    tr = _row_tile(r, c)
    c1 = 1.0 / (1.0 - ADAM_B1 ** ADAM_STEP)
    c2 = 1.0 / (1.0 - ADAM_B2 ** ADAM_STEP)

    def body(w_ref, g_ref, m_ref, v_ref, d_ref, nm_ref, nv_ref):
        gv = g_ref[...]
        nm = ADAM_B1 * m_ref[...] + (1.0 - ADAM_B1) * gv
        nv = ADAM_B2 * v_ref[...] + (1.0 - ADAM_B2) * (gv * gv)
        d_ref[...] = -ADAM_LR * ((nm * c1) / (jnp.sqrt(nv * c2) + ADAM_EPS) + ADAM_WD * w_ref[...])
        nm_ref[...] = nm
        nv_ref[...] = nv

    blk = pl.BlockSpec((tr, c), lambda i: (i, 0))
    shp = jax.ShapeDtypeStruct((r, c), F32)
    return pl.pallas_call(body, name=name, grid=(r // tr,), in_specs=[blk] * 4, out_specs=[blk] * 3,
                          out_shape=[shp] * 3, compiler_params=_params(("parallel",)))(w, g, m, v)


def _sum_devices(name, parts):
    _, r, c = parts.shape
    tr = _row_tile(r, N_DEV * c)

    def body(p_ref, o_ref):
        acc = p_ref[0]
        for k in range(1, N_DEV):
            acc = acc + p_ref[k]
        o_ref[...] = acc

    return pl.pallas_call(body, name=name, grid=(r // tr,),
                          in_specs=[pl.BlockSpec((N_DEV, tr, c), lambda i: (0, i, 0))],
                          out_specs=pl.BlockSpec((tr, c), lambda i: (i, 0)),
                          out_shape=jax.ShapeDtypeStruct((r, c), F32),
                          compiler_params=_params(("parallel",)))(parts)


def _coords():
    return lax.axis_index("x"), lax.axis_index("y"), lax.axis_index("c")


def _chip_peers(x, y):
    return [(1 - x, y, 2 * (1 - x) + y), (x, 1 - y, 2 * x + 1 - y), (1 - x, 1 - y, 2 * (1 - x) + 1 - y)]


def _remote(src, dst, ssem, rsem, dev):
    return pltpu.make_async_remote_copy(src_ref=src, dst_ref=dst, send_sem=ssem, recv_sem=rsem,
                                        device_id=dev, device_id_type=MESH)


ANY = pl.BlockSpec(memory_space=pl.ANY)


SEM = pl.BlockSpec(memory_space=pltpu.SEMAPHORE)
HBM = pl.BlockSpec(memory_space=pltpu.HBM)
DATAFLOW = pltpu.SideEffectType.DATAFLOW_SIDE_EFFECTING


def _hbm(a):
    return pltpu.with_memory_space_constraint(a, pltpu.HBM)


def _split_start(name, make_copies, n_copies, srcs, land_shapes, dep):
    n = len(srcs)

    def body(*refs):
        src_refs, land_refs = refs[:n], refs[n:2 * n]
        ssem, rsem = refs[2 * n + 1], refs[2 * n + 2]
        token = refs[-1]
        for send, _ in make_copies(src_refs, land_refs, ssem, rsem):
            send.start()
        token[...] = jnp.zeros_like(token)

    lands = [_hbm(lax.empty(shp, a.dtype)) for shp, a in zip(land_shapes, srcs)]
    out_shape = [pltpu.SemaphoreType.DMA((n_copies,)), pltpu.SemaphoreType.DMA((n_copies,))]
    out_shape += [pltpu.HBM(a.shape, a.dtype) for a in list(srcs) + lands]
    out_shape.append(jax.ShapeDtypeStruct((ROW_ALIGN, LANES), F32))
    res = pl.pallas_call(
        body, name=name, out_shape=out_shape,
        in_specs=[HBM] * (2 * n) + [ANY],
        out_specs=[SEM, SEM] + [HBM] * (2 * n) + [pl.BlockSpec(memory_space=pltpu.VMEM)],
        input_output_aliases={i: 2 + i for i in range(2 * n)},
        compiler_params=pltpu.CompilerParams(has_side_effects=DATAFLOW),
    )(*[_hbm(a) for a in srcs], *lands, dep)
    return res[0], res[1], res[2:2 + n], res[2 + n:2 + 2 * n], res[-1]


def _split_wait(name, make_copies, ssem, rsem, srcs, lands, after):
    n = len(srcs)

    def body(*refs):
        src_refs, land_refs = refs[:n], refs[n:2 * n]
        for send, recv in make_copies(src_refs, land_refs, refs[2 * n], refs[2 * n + 1]):
            send.wait_send()
            recv.wait_recv()

    res = pl.pallas_call(
        body, name=name, out_shape=[pltpu.HBM(a.shape, a.dtype) for a in list(srcs) + list(lands)],
        in_specs=[HBM] * (2 * n) + [SEM, SEM, ANY], out_specs=[HBM] * (2 * n),
        input_output_aliases={i: i for i in range(2 * n)},
        compiler_params=pltpu.CompilerParams(has_side_effects=DATAFLOW),
    )(*srcs, *lands, ssem, rsem, after)
    return res[:n], res[n:]


def _gather_copies(src_refs, land_refs, ssem, rsem):
    x, y, c = _coords()
    j = 2 * x + y
    out = []
    for i, (src, land) in enumerate(zip(src_refs, land_refs)):
        h = src.shape[0] // 2
        rows = pl.ds(c * h, h)
        for k, (px, py, pj) in enumerate(_chip_peers(x, y)):
            sems = (ssem.at[3 * i + k], rsem.at[3 * i + k], (px, py, c))
            out.append((_remote(src.at[rows], land.at[j, rows], *sems),
                        _remote(src.at[rows], land.at[pj, rows], *sems)))
    return out


def _scatter_copies(src_refs, land_refs, ssem, rsem):
    x, y, c = _coords()
    out = []
    for i, (src, land) in enumerate(zip(src_refs, land_refs)):
        for k, (px, py, pj) in enumerate(_chip_peers(x, y)):
            cp = _remote(src.at[pj], land.at[k], ssem.at[3 * i + k], rsem.at[3 * i + k], (px, py, c))
            out.append((cp, cp))
    return out


def _sibling_pass(name, lands, shards, dep):
    n = len(lands)

    def body(*refs):
        shard_refs, bufs = refs[n:2 * n], refs[2 * n + 1:3 * n + 1]
        ssem, rsem, lsem = refs[3 * n + 1:]
        x, y, c = _coords()
        j = 2 * x + y
        local = [pltpu.make_async_copy(shard_refs[i], bufs[i].at[j], lsem.at[i]) for i in range(n)]
        for cp in local:
            cp.start()
        sends = []
        for i in range(n):
            h = bufs[i].shape[1] // 2
            for k, (px, py, pj) in enumerate(_chip_peers(x, y)):
                mine = bufs[i].at[pj, pl.ds(c * h, h)]
                sends.append(_remote(mine, mine, ssem.at[3 * i + k], rsem.at[3 * i + k], (x, y, 1 - c)))
        for cp in sends:
            cp.start()
        for i in range(n):
            h = bufs[i].shape[1] // 2
            for k, (px, py, pj) in enumerate(_chip_peers(x, y)):
                theirs = bufs[i].at[pj, pl.ds((1 - c) * h, h)]
                _remote(theirs, theirs, ssem.at[3 * i + k], rsem.at[3 * i + k], (x, y, 1 - c)).wait_recv()
        for cp in sends:
            cp.wait_send()
        for cp in local:
            cp.wait()

    return pl.pallas_call(
        body, name=name, in_specs=[ANY] * (2 * n + 1), out_specs=[ANY] * n,
        out_shape=[jax.ShapeDtypeStruct(a.shape, a.dtype) for a in lands],
        input_output_aliases={i: i for i in range(n)},
        scratch_shapes=[pltpu.SemaphoreType.DMA((3 * n,)), pltpu.SemaphoreType.DMA((3 * n,)),
                        pltpu.SemaphoreType.DMA((n,))],
    )(*lands, *shards, dep)


def _gather_small(small):
    def body(in_ref, out_ref, ssem, rsem, lsem):
        x, y, c = _coords()
        j = 2 * x + y
        local = pltpu.make_async_copy(in_ref, out_ref.at[j], lsem)
        local.start()
        peers = _chip_peers(x, y)
        sends = [_remote(in_ref, out_ref.at[j], ssem.at[k], rsem.at[k], (px, py, c))
                 for k, (px, py, pj) in enumerate(peers)]
        for cp in sends:
            cp.start()
        for k, (px, py, pj) in enumerate(peers):
            sends[k].wait_send()
            _remote(in_ref, out_ref.at[pj], ssem.at[k], rsem.at[k], (px, py, c)).wait_recv()
        local.wait()

    return pl.pallas_call(body, name="gather_small", in_specs=[ANY], out_specs=ANY,
                          out_shape=jax.ShapeDtypeStruct((N_CHIPS,) + small.shape, small.dtype),
                          scratch_shapes=[pltpu.SemaphoreType.DMA((3,)), pltpu.SemaphoreType.DMA((3,)),
                                          pltpu.SemaphoreType.DMA])(small)


def _pair_send(name, dws):
    n = len(dws)

    def body(*refs):
        ins, outs = refs[:n], refs[n:2 * n]
        ssem, rsem = refs[2 * n:]
        x, y, c = _coords()
        cps = []
        for i in range(n):
            h = ins[i].shape[1] // 2
            cps.append(_remote(ins[i].at[:, pl.ds((1 - c) * h, h)], outs[i], ssem.at[i], rsem.at[i], (x, y, 1 - c)))
        for cp in cps:
            cp.start()
        for cp in cps:
            cp.wait()

    out_shape = [jax.ShapeDtypeStruct((a.shape[0], a.shape[1] // 2, a.shape[2]), a.dtype) for a in dws]
    return pl.pallas_call(body, name=name, in_specs=[ANY] * n, out_specs=[ANY] * n, out_shape=out_shape,
                          scratch_shapes=[pltpu.SemaphoreType.DMA((n,))] * 2)(*dws)


def _half_swap(name, gs, layers):
    n = len(gs)

    def body(*refs):
        bufs = refs[n:2 * n]
        ssem, rsem = refs[2 * n:]
        x, y, c = _coords()
        cps = []
        for i in range(n):
            h = bufs[i].shape[1] // 2
            mine = bufs[i].at[layers[i], pl.ds(c * h, h)]
            cps.append(_remote(mine, mine, ssem.at[i], rsem.at[i], (x, y, 1 - c)))
        for cp in cps:
            cp.start()
        for i, cp in enumerate(cps):
            h = bufs[i].shape[1] // 2
            theirs = bufs[i].at[layers[i], pl.ds((1 - c) * h, h)]
            cp.wait_send()
            _remote(theirs, theirs, ssem.at[i], rsem.at[i], (x, y, 1 - c)).wait_recv()

    return pl.pallas_call(body, name=name, in_specs=[ANY] * n, out_specs=[ANY] * n,
                          out_shape=[jax.ShapeDtypeStruct(a.shape, a.dtype) for a in gs],
                          input_output_aliases={i: i for i in range(n)},
                          scratch_shapes=[pltpu.SemaphoreType.DMA((n,))] * 2)(*gs)


def _allgather_small(part):
    def body(in_ref, out_ref, ssem, rsem, lsem):
        x, y, c = _coords()
        me = 4 * x + 2 * y + c
        local = pltpu.make_async_copy(in_ref, out_ref.at[me], lsem)
        local.start()
        cps, peers = [], []
        for r in range(1, N_DEV):
            px = 1 - x if r & 4 else x
            py = 1 - y if r & 2 else y
            pc = 1 - c if r & 1 else c
            peers.append((4 * px + 2 * py + pc, (px, py, pc)))
            cps.append(_remote(in_ref, out_ref.at[me], ssem.at[r - 1], rsem.at[r - 1], (px, py, pc)))
        for cp in cps:
            cp.start()
        for r, cp in enumerate(cps):
            cp.wait_send()
            slot = out_ref.at[peers[r][0]]
            _remote(slot, slot, ssem.at[r], rsem.at[r], peers[r][1]).wait_recv()
        local.wait()

    return pl.pallas_call(body, name="allgather_small", in_specs=[ANY], out_specs=ANY,
                          out_shape=jax.ShapeDtypeStruct((N_DEV,) + part.shape, part.dtype),
                          scratch_shapes=[pltpu.SemaphoreType.DMA((N_DEV - 1,)), pltpu.SemaphoreType.DMA((N_DEV - 1,)),
                                          pltpu.SemaphoreType.DMA])(part)


ROW_ALIGN = SUBLANES
LAYER_WEIGHTS = (("a_w_in", "a_w_out"), ("b_w_in", "b_w_pool", "b_w_out"), ("c_w_in", "c_w_out"))
SMALL_SHARDED = ("a_ln_g", "a_ln_b", "c_conv_w", "c_conv_b", "c_ln_g", "c_ln_b")
SMALL_ORDER = ("a_ln_g", "a_ln_b", "a_w_s", "a_b_s", "b_scale", "c_conv_w", "c_conv_b", "c_ln_g", "c_ln_b",
               "post_ln_g", "post_ln_b")
WEIGHTS = ("a_w_in", "a_ln_g", "a_ln_b", "a_w_s", "a_b_s", "a_w_out", "b_w_in", "b_w_pool", "b_scale", "b_w_out",
           "c_w_in", "c_conv_w", "c_conv_b", "c_ln_g", "c_ln_b", "c_w_out", "post_ln_g", "post_ln_b")


def _pad_rows(a):
    pad = -a.shape[0] % ROW_ALIGN
    return jnp.pad(a, ((0, pad), (0, 0))) if pad else a


def _pack(parts, width):
    return jnp.concatenate([_pad_rows(p.reshape(-1, width)) for p in parts], axis=0)


def _unpack(packed, shapes, width):
    out, row = [], 0
    for shp in shapes:
        size = 1
        for n in shp:
            size *= n
        rows = size // width
        out.append(packed[row:row + rows].reshape(shp))
        row += rows + (-rows % ROW_ALIGN)
    return out


def _step(x, tgt, w, m, v):
    s, d = x.shape
    cx, cy, cc = _coords()
    j_me = 2 * cx + cy
    es = w["a_ln_g"].shape[1]
    e = N_CHIPS * es

    n_a, n_c = w["a_w_in"].shape[0], w["c_w_in"].shape[0]
    ng = w["b_w_pool"].shape[1]
    as3d = lambda a: a.reshape(a.shape[0], -1, a.shape[-1])
    started, tok = [], x
    for i in range(DEPTH):
        names = LAYER_WEIGHTS[i % 3]
        shards = [_cast_layer("cast_%s%d" % (k, i // 3), as3d(w[k]), i // 3) for k in names]
        ssem, rsem, srcs, lands, tok = _split_start("gather_start%d" % i, _gather_copies, 3 * len(names), shards,
                                                    [(N_CHIPS,) + a.shape for a in shards], tok)
        started.append((ssem, rsem, srcs, lands))
    small_all = _gather_small(_pack([w[k] for k in SMALL_SHARDED], es))
    small_full = jnp.swapaxes(small_all, 0, 1).reshape(small_all.shape[1], e)
    full_shapes = [(n_a, e), (n_a, e), (CONV_WIDTH, e), (1, e), (1, e), (1, e)]
    a_ln_g, a_ln_b, conv_w, conv_b, c_ln_g, c_ln_b = _unpack(small_full, full_shapes, e)

    post_g, post_b = w["post_ln_g"], w["post_ln_b"]
    b_scale = w["b_scale"]

    saved = []
    res, gp, bp, hb = x, jnp.ones((1, d), F32), jnp.zeros((1, d), F32), x
    for i in range(DEPTH):
        kind, l = i % 3, i // 3
        tag = "L%d" % i
        ssem, rsem, srcs, lands = started[i]
        srcs, lands = _split_wait("gather_wait%d" % i, _gather_copies, ssem, rsem, srcs, lands, tok if i == 0 else res)
        full = _sibling_pass("sibling_pass%d" % i, lands, srcs, tok)
        wl = dict(zip(LAYER_WEIGHTS[kind], full))
        w_in = wl[LAYER_WEIGHTS[kind][0]].reshape(N_CHIPS, 1, d, -1)
        w_out = wl[LAYER_WEIGHTS[kind][-1]].reshape(N_CHIPS, 1, es, d)
        proj = _mm_in(tag + "_in", hb, w_in, 0)
        if kind == 0:
            b_st = w["a_b_s"][l].T
            yg = _sgu_fwd(tag + "_sgu", proj, a_ln_g[l:l + 1], a_ln_b[l:l + 1], w["a_w_s"][l], b_st)
            extra = (b_st,)
        elif kind == 1:
            w_pool = wl["b_w_pool"].reshape(N_CHIPS, ng, -1, w["b_w_pool"].shape[3])
            p = _pool_fwd(tag + "_pool", proj)
            q, yg = _pool_mm(tag + "_poolmm", p, w_pool, proj, b_scale[l:l + 1])
            extra = (p, q, w_pool)
        else:
            cv = _conv_fwd(tag + "_conv", proj, conv_w, conv_b)
            yg = _conv_gate_fwd(tag + "_gate", cv, proj, c_ln_g, c_ln_b)
            extra = (cv,)
        xhat, hb_next, rstd = _mm_out_ln(tag + "_out", yg, w_out, 0, res, gp, bp, post_g[i:i + 1], post_b[i:i + 1])
        saved.append((hb, proj, yg, xhat, rstd, extra, w_in, w_out))
        res, gp, bp, hb = xhat, post_g[i:i + 1], post_b[i:i + 1], hb_next

    sg = {k: [None] * w[k].shape[0] for k in ("a_ln_g", "a_ln_b", "a_w_s", "a_b_s", "post_ln_g", "post_ln_b")}
    scat = {}
    dr, drb, dg, db, loss_vec = _post_ln_bwd("L3_lnbwd", tgt, saved[-1][3], saved[-1][4],
                                             post_g[DEPTH - 1:DEPTH], post_b[DEPTH - 1:DEPTH], True)
    grad_x = None
    for i in reversed(range(DEPTH)):
        kind, l = i % 3, i // 3
        tag = "L%d" % i
        hb_in, proj, yg, _, _, extra, w_in, w_out = saved[i]
        sg["post_ln_g"][i], sg["post_ln_b"][i] = dg, db
        dyg = _mm_dyg(tag + "_dyg", drb, w_out, 0)
        dw_mid = []
        if kind == 0:
            dproj, dlg, dlb, dws_s, dbs_t = _sgu_bwd(tag + "_sgubwd", dyg, proj, a_ln_g[l:l + 1], a_ln_b[l:l + 1],
                                                     w["a_w_s"][l], extra[0])
            sg["a_ln_g"][l], sg["a_ln_b"][l], sg["a_w_s"][l], sg["a_b_s"][l] = dlg, dlb, dws_s, dbs_t.T
        elif kind == 1:
            p, q, w_pool = extra
            dq, dproj, dscale = _pool_bwd1(tag + "_poolbwd1", dyg, q, proj, b_scale[l:l + 1])
            dp = _pool_mm_bwd(tag + "_poolmmbwd", dq, w_pool)
            dproj = _pool_bwd2(tag + "_poolbwd2", dp, dproj)
            dw_mid = [_mm_dw_pool(tag + "_dwpool", p, dq, ng).reshape(N_CHIPS, -1, dq.shape[1] // ng)]
            sg["b_scale"] = [dscale]
        else:
            cv = extra[0]
            dc, dproj, dlg, dlb, dcb = _conv_bwd1(tag + "_convbwd1", dyg, cv, proj, c_ln_g, c_ln_b)
            dgc, dcw = _conv_bwd2(tag + "_convbwd2", dc, proj, conv_w)
            dproj = _conv_bwd3(tag + "_convbwd3", dgc, proj, dproj)
            sg["c_ln_g"], sg["c_ln_b"], sg["c_conv_b"], sg["c_conv_w"] = [dlg], [dlb], [dcb], [dcw[:CONV_WIDTH]]
        dw_out = _mm_dw_out(tag + "_dwout", yg, drb).reshape(N_CHIPS, es, d)
        dh = _mm_dh(tag + "_dh", dproj, w_in, 0, dr)
        dw_in = _mm_dw_in(tag + "_dwin", hb_in, dproj)
        own = [dw_in] + dw_mid + [dw_out]
        recv = _pair_send("pair_send%d" % i, own)
        pair = [_pair_sum("pairsum%d_%d" % (i, n), a, r) for n, (a, r) in enumerate(zip(own, recv))]
        ssem, rsem, srcs, lands, tok = _split_start("scatter_start%d" % i, _scatter_copies, 3 * len(pair), pair,
                                                    [(3,) + a.shape[1:] for a in pair], tok)
        scat[i] = (own, recv, ssem, rsem, srcs, lands)
        if i > 0:
            prev = saved[i - 1]
            dr, drb, dg, db, _ = _post_ln_bwd("L%d_lnbwd" % (i - 1), dh, prev[3], prev[4],
                                              post_g[i - 1:i], post_b[i - 1:i], False, dep=tok)
        else:
            grad_x = dh

    grads, delta, new_m, new_v = {}, {}, {}, {}

    def adam(k):
        shp = w[k].shape
        two_d = (lambda a: a.reshape(-1, shp[-1])) if shp[-1] % LANES == 0 else (lambda a: a.reshape(1, -1))
        dl, nm, nv = _adam("adam_" + k, two_d(w[k]), two_d(grads[k]), two_d(m[k]), two_d(v[k]))
        delta[k], new_m[k], new_v[k] = dl.reshape(shp), nm.reshape(shp), nv.reshape(shp)
        return dl

    reduced = {}
    after = grad_x
    for i in reversed(range(DEPTH)):
        kind, l = i % 3, i // 3
        names = LAYER_WEIGHTS[kind]
        own, recv, ssem, rsem, srcs, lands = scat[i]
        _, got = _split_wait("scatter_wait%d" % i, _scatter_copies, ssem, rsem, srcs, lands, after)
        for n, k in enumerate(names):
            reduced[k] = _final_sum("finalsum%d_%d" % (i, n), own[n], recv[n], got[n], l, w[k].shape[0],
                                    reduced.get(k))
        swapped = _half_swap("half_swap%d" % i, [reduced[k] for k in names], [l] * len(names))
        for k, g in zip(names, swapped):
            reduced[k] = g
            if l == 0:
                grads[k] = g.reshape(w[k].shape)
                after = adam(k)

    small_parts = []
    for k in SMALL_ORDER:
        small_parts.append(jnp.concatenate([a.reshape(1, -1) for a in sg[k]], axis=0))
    width = 8 * LANES
    packed = _pack(small_parts, width)
    summed = _sum_devices("sum_small", _allgather_small(packed))
    full_small_shapes = {"a_ln_g": (n_a, e), "a_ln_b": (n_a, e), "a_w_s": w["a_w_s"].shape, "a_b_s": w["a_b_s"].shape,
                         "b_scale": w["b_scale"].shape, "c_conv_w": (n_c, CONV_WIDTH, e), "c_conv_b": (n_c, e),
                         "c_ln_g": (n_c, e), "c_ln_b": (n_c, e), "post_ln_g": post_g.shape, "post_ln_b": post_b.shape}
    for k, g in zip(SMALL_ORDER, _unpack(summed, [full_small_shapes[k] for k in SMALL_ORDER], width)):
        if k in SMALL_SHARDED:
            g = lax.dynamic_slice_in_dim(g, j_me * es, es, axis=g.ndim - 1)
        grads[k] = g
        adam(k)

    loss = lax.psum(loss_vec[0, 0], ("x", "y", "c"))
    return loss, grad_x, grads, delta, new_m, new_v


def kernel(x, a_w_in, a_ln_g, a_ln_b, a_w_s, a_b_s, a_w_out, b_w_in, b_w_pool, b_scale, b_w_out, c_w_in, c_conv_w, c_conv_b, c_ln_g, c_ln_b, c_w_out, post_ln_g, post_ln_b, loss_target, m_a_w_in, m_a_ln_g, m_a_ln_b, m_a_w_s, m_a_b_s, m_a_w_out, m_b_w_in, m_b_w_pool, m_b_scale, m_b_w_out, m_c_w_in, m_c_conv_w, m_c_conv_b, m_c_ln_g, m_c_ln_b, m_c_w_out, m_post_ln_g, m_post_ln_b, v_a_w_in, v_a_ln_g, v_a_ln_b, v_a_w_s, v_a_b_s, v_a_w_out, v_b_w_in, v_b_w_pool, v_b_scale, v_b_w_out, v_c_w_in, v_c_conv_w, v_c_conv_b, v_c_ln_g, v_c_ln_b, v_c_w_out, v_post_ln_g, v_post_ln_b):
    args = locals()
    w = {k: args[k] for k in WEIGHTS}
    m = {k: args["m_" + k] for k in WEIGHTS}
    v = {k: args["v_" + k] for k in WEIGHTS}
    loss, grad_x, grads, delta, new_m, new_v = _step(x[0], loss_target[0], w, m, v)
    out = [loss, grad_x[None]]
    for group in (grads, delta, new_m, new_v):
        out.extend(group[k] for k in WEIGHTS)
    return tuple(out)
```

```python
import functools

import jax
import jax.numpy as jnp
from jax import lax
from jax.experimental import pallas as pl
from jax.experimental.pallas import tpu as pltpu

F32 = jnp.float32
BF16 = jnp.bfloat16
MESH = pl.DeviceIdType.MESH

DEPTH = 4
CHUNK = 128
SGU_HEADS = 8
POOL_WINDOWS = (2, 4, 8, 16)
POOL_HALO = 16
CONV_WIDTH = 31
CONV_HALO = 32
LN_EPS = 1e-5
ALPHA = (2.0 * DEPTH) ** 0.25
ADAM_LR, ADAM_B1, ADAM_B2, ADAM_EPS, ADAM_WD, ADAM_STEP = 0.001, 0.9, 0.999, 1e-08, 0.01, 10
N_CHIPS = 4
N_DEV = 8
LANES = 128
SUBLANES = 8
CONV_STRIDE = 4
CONV_SUB = SUBLANES * CONV_STRIDE
VMEM_LIMIT = 56 << 20
GELU_K = 0.7978845608028654
GELU_C = 0.044715


def _params(sem=None, vmem=VMEM_LIMIT):
    return pltpu.CompilerParams(dimension_semantics=sem, vmem_limit_bytes=vmem)


def _tile(dim, pref):
    if dim <= pref:
        return dim
    t = 1 << (pref.bit_length() - 1)
    while dim % t:
        t //= 2
    return t


ROW_BLOCK_BYTES = 2 << 20


def _row_tile(rows, cols):
    return _tile(rows, max(ROW_ALIGN, ROW_BLOCK_BYTES // (4 * cols)))


def _sigmoid(x):
    return 1.0 / (1.0 + jnp.exp(-x))


def _gelu(x):
    return 0.5 * x * (1.0 + jnp.tanh(GELU_K * (x + GELU_C * x * x * x)))


def _gelu_grad(x):
    t = jnp.tanh(GELU_K * (x + GELU_C * x * x * x))
    return 0.5 * (1.0 + t) + 0.5 * x * (1.0 - t * t) * GELU_K * (1.0 + 3.0 * GELU_C * x * x)


def _ln_stats(x):
    mu = jnp.mean(x, axis=-1, keepdims=True)
    d = x - mu
    var = jnp.mean(d * d, axis=-1, keepdims=True)
    rstd = lax.rsqrt(var + LN_EPS)
    return d * rstd, rstd


def _ln_bwd(dy_hat, xhat, rstd):
    m1 = jnp.mean(dy_hat, axis=-1, keepdims=True)
    m2 = jnp.mean(dy_hat * xhat, axis=-1, keepdims=True)
    return rstd * (dy_hat - m1 - xhat * m2)


def _rowsum(x):
    return jnp.sum(x, axis=0, keepdims=True)


def _mm(name, a, b, *, dims, grid, a_spec, b_spec, out_shape, out_specs, acc_shape,
        extra=(), extra_specs=(), epilogue=None, dep=None):
    nk = grid[2]
    n_extra = len(extra)
    single = not isinstance(out_shape, (list, tuple))
    shapes = [out_shape] if single else list(out_shape)
    specs = [out_specs] if single else list(out_specs)
    n_out = len(shapes)
    deps = [] if dep is None else [dep]
    first_out = 2 + n_extra + len(deps)

    def body(*refs):
        a_ref, b_ref = refs[0], refs[1]
        ex = refs[2:2 + n_extra]
        outs = refs[first_out:first_out + n_out]

        def finish(acc):
            if epilogue is None:
                outs[0][...] = acc.astype(outs[0].dtype)
            else:
                epilogue(acc, ex, outs)

        part = lax.dot_general(a_ref[...].astype(BF16), b_ref[...].astype(BF16), dims, preferred_element_type=F32)
        if nk == 1:
            finish(part)
            return
        acc_ref = refs[-1]
        k = pl.program_id(2)

        @pl.when(k == 0)
        def _():
            acc_ref[...] = part

        @pl.when(jnp.logical_and(k > 0, k < nk - 1))
        def _():
            acc_ref[...] += part

        @pl.when(k == nk - 1)
        def _():
            finish(acc_ref[...] + part)

    res = pl.pallas_call(
        body, name=name, grid=grid, in_specs=[a_spec, b_spec, *extra_specs] + [ANY] * len(deps), out_specs=specs,
        out_shape=shapes, scratch_shapes=[pltpu.VMEM(acc_shape, F32)] if nk > 1 else [],
        compiler_params=_params(("parallel", "parallel", "arbitrary")),
    )(a, b, *extra, *deps)
    return res[0] if single else res


NN = (((1,), (0,)), ((), ()))
NT = (((1,), (1,)), ((), ()))
TN = (((0,), (0,)), ((), ()))


def _mm_in(name, hb, wg, layer, dep=None):
    s, d = hb.shape
    nsh = wg.shape[3]
    n = N_CHIPS * nsh
    bm, bn, bk = _tile(s, 1024), _tile(nsh, 1024), _tile(d, 2048)
    per = nsh // bn
    return _mm(name, hb, wg, dims=NN, grid=(s // bm, n // bn, d // bk),
               a_spec=pl.BlockSpec((bm, bk), lambda m, j, k: (m, k)),
               b_spec=pl.BlockSpec((None, None, bk, bn), lambda m, j, k: (j // per, layer, k, j % per)),
               out_shape=jax.ShapeDtypeStruct((s, n), F32),
               out_specs=pl.BlockSpec((bm, bn), lambda m, j, k: (m, j)),
               acc_shape=(bm, bn), dep=dep)


def _mm_out_ln(name, yg, wg, layer, res, gp, bp, g, b):
    s, e = yg.shape
    esh, d = wg.shape[2], wg.shape[3]
    bm, bk = _tile(s, 512), _tile(esh, 1024)
    per = esh // bk

    def epilogue(acc, ex, outs):
        res_ref, gp_ref, bp_ref, g_ref, b_ref = ex
        xhat_ref, hb_ref, rstd_ref = outs
        r = ALPHA * (res_ref[...] * gp_ref[...] + bp_ref[...]) + acc
        xhat, rstd = _ln_stats(r)
        xhat_ref[...] = xhat
        hb_ref[...] = (xhat * g_ref[...] + b_ref[...]).astype(BF16)
        rstd_ref[...] = jnp.broadcast_to(rstd, rstd_ref.shape)

    vec = pl.BlockSpec((1, d), lambda m, j, k: (0, 0))
    row = pl.BlockSpec((bm, d), lambda m, j, k: (m, 0))
    return _mm(name, yg, wg, dims=NN, grid=(s // bm, 1, e // bk),
               a_spec=pl.BlockSpec((bm, bk), lambda m, j, k: (m, k)),
               b_spec=pl.BlockSpec((None, None, bk, d), lambda m, j, k: (k // per, layer, k % per, 0)),
               extra=(res, gp, bp, g, b), extra_specs=(row, vec, vec, vec, vec),
               out_shape=(jax.ShapeDtypeStruct((s, d), F32), jax.ShapeDtypeStruct((s, d), BF16),
                          jax.ShapeDtypeStruct((s, LANES), F32)),
               out_specs=(row, row, pl.BlockSpec((bm, LANES), lambda m, j, k: (m, 0))),
               acc_shape=(bm, d), epilogue=epilogue)


def _mm_dyg(name, drb, wg, layer):
    s, d = drb.shape
    esh = wg.shape[2]
    e = N_CHIPS * esh
    bm, bn, bk = _tile(s, 1024), _tile(esh, 1024), _tile(d, 2048)
    per = esh // bn
    return _mm(name, drb, wg, dims=NT, grid=(s // bm, e // bn, d // bk),
               a_spec=pl.BlockSpec((bm, bk), lambda m, j, k: (m, k)),
               b_spec=pl.BlockSpec((None, None, bn, bk), lambda m, j, k: (j // per, layer, j % per, k)),
               out_shape=jax.ShapeDtypeStruct((s, e), F32),
               out_specs=pl.BlockSpec((bm, bn), lambda m, j, k: (m, j)),
               acc_shape=(bm, bn))


def _mm_dw_out(name, yg, drb):
    s, e = yg.shape
    d = drb.shape[1]
    bm, bn, bk = _tile(e, 1024), _tile(d, 1024), _tile(s, 4096)
    return _mm(name, yg, drb, dims=TN, grid=(e // bm, d // bn, s // bk),
               a_spec=pl.BlockSpec((bk, bm), lambda m, j, k: (k, m)),
               b_spec=pl.BlockSpec((bk, bn), lambda m, j, k: (k, j)),
               out_shape=jax.ShapeDtypeStruct((e, d), BF16),
               out_specs=pl.BlockSpec((bm, bn), lambda m, j, k: (m, j)),
               acc_shape=(bm, bn))


def _mm_dh(name, dproj, wg, layer, dr, dep=None):
    s, n = dproj.shape
    d, nsh = wg.shape[2], wg.shape[3]
    bm, bn = _tile(s, 1024), _tile(d, 1024)
    bk = nsh if nsh <= 2048 else nsh // 2
    per = nsh // bk

    def epilogue(acc, ex, outs):
        outs[0][...] = ALPHA * ex[0][...] + acc

    blk = pl.BlockSpec((bm, bn), lambda m, j, k: (m, j))
    return _mm(name, dproj, wg, dims=NT, grid=(s // bm, d // bn, n // bk),
               a_spec=pl.BlockSpec((bm, bk), lambda m, j, k: (m, k)),
               b_spec=pl.BlockSpec((None, None, bn, bk), lambda m, j, k: (k // per, layer, j, k % per)),
               extra=(dr,), extra_specs=(blk,),
               out_shape=jax.ShapeDtypeStruct((s, d), F32), out_specs=blk,
               acc_shape=(bm, bn), epilogue=epilogue, dep=dep)


def _mm_dw_in(name, hb, dproj):
    s, d = hb.shape
    n = dproj.shape[1]
    nsh = n // N_CHIPS
    bm, bn, bk = _tile(d, 1024), _tile(nsh, 1024), _tile(s, 4096)
    per = nsh // bn
    return _mm(name, hb, dproj, dims=TN, grid=(d // bm, n // bn, s // bk),
               a_spec=pl.BlockSpec((bk, bm), lambda m, j, k: (k, m)),
               b_spec=pl.BlockSpec((bk, bn), lambda m, j, k: (k, j)),
               out_shape=jax.ShapeDtypeStruct((N_CHIPS, d, nsh), BF16),
               out_specs=pl.BlockSpec((None, bm, bn), lambda m, j, k: (j // per, m, j % per)),
               acc_shape=(bm, bn))


def _causal_mask():
    r = lax.broadcasted_iota(jnp.int32, (CHUNK, CHUNK), 0)
    c = lax.broadcasted_iota(jnp.int32, (CHUNK, CHUNK), 1)
    return r >= c


def _sgu_fwd(name, proj, ln_g, ln_b, w_s, b_st):
    s, n3 = proj.shape
    e = n3 // 3
    hd = e // SGU_HEADS

    def body(proj_ref, g_ref, b_ref, ws_ref, bst_ref, yg_ref):
        xhat, _ = _ln_stats(_gelu(proj_ref[:, e:2 * e]))
        vnb = (xhat * g_ref[...] + b_ref[...]).astype(BF16)
        tri = _causal_mask()
        for h in range(SGU_HEADS):
            hs = slice(h * hd, (h + 1) * hd)
            ws = jnp.where(tri, ws_ref[h], 0.0).astype(BF16)
            mixed = jnp.dot(ws, vnb[:, hs], preferred_element_type=F32) + bst_ref[:, h:h + 1]
            u = proj_ref[:, hs]
            z = proj_ref[:, 2 * e + h * hd:2 * e + (h + 1) * hd]
            yg_ref[:, hs] = (_gelu(u) * mixed * (z * _sigmoid(z))).astype(BF16)

    vec = pl.BlockSpec((1, e), lambda i: (0, 0))
    return pl.pallas_call(
        body, name=name, grid=(s // CHUNK,),
        in_specs=[pl.BlockSpec((CHUNK, n3), lambda i: (i, 0)), vec, vec,
                  pl.BlockSpec((SGU_HEADS, CHUNK, CHUNK), lambda i: (0, 0, 0)),
                  pl.BlockSpec((CHUNK, SGU_HEADS), lambda i: (0, 0))],
        out_specs=pl.BlockSpec((CHUNK, e), lambda i: (i, 0)),
        out_shape=jax.ShapeDtypeStruct((s, e), BF16),
        compiler_params=_params(("parallel",)),
    )(proj, ln_g, ln_b, w_s, b_st)


def _sgu_bwd(name, dyg, proj, ln_g, ln_b, w_s, b_st):
    s, n3 = proj.shape
    e = n3 // 3
    hd = e // SGU_HEADS

    def body(dyg_ref, proj_ref, g_ref, b_ref, ws_ref, bst_ref,
             dproj_ref, dg_ref, db_ref, dws_ref, dbs_ref, dvn_ref):
        @pl.when(pl.program_id(0) == 0)
        def _():
            dg_ref[...] = jnp.zeros_like(dg_ref)
            db_ref[...] = jnp.zeros_like(db_ref)
            dws_ref[...] = jnp.zeros_like(dws_ref)
            dbs_ref[...] = jnp.zeros_like(dbs_ref)

        v = proj_ref[:, e:2 * e]
        xhat, rstd = _ln_stats(_gelu(v))
        g = g_ref[...]
        vnb = (xhat * g + b_ref[...]).astype(BF16)
        tri = _causal_mask()
        for h in range(SGU_HEADS):
            hs = slice(h * hd, (h + 1) * hd)
            zs = slice(2 * e + h * hd, 2 * e + (h + 1) * hd)
            ws = jnp.where(tri, ws_ref[h], 0.0).astype(BF16)
            mixed = jnp.dot(ws, vnb[:, hs], preferred_element_type=F32) + bst_ref[:, h:h + 1]
            u = proj_ref[:, hs]
            z = proj_ref[:, zs]
            gu = _gelu(u)
            sig = _sigmoid(z)
            dy = dyg_ref[:, hs]
            t = dy * (z * sig)
            dmixed = t * gu
            dproj_ref[:, hs] = (t * mixed * _gelu_grad(u)).astype(BF16)
            dproj_ref[:, zs] = (dy * gu * mixed * (sig * (1.0 + z * (1.0 - sig)))).astype(BF16)
            dmb = dmixed.astype(BF16)
            dvn_ref[:, hs] = lax.dot_general(ws, dmb, TN, preferred_element_type=F32)
            dws = lax.dot_general(dmb, vnb[:, hs], NT, preferred_element_type=F32)
            dws_ref[h] += jnp.where(tri, dws, 0.0)
            dbs_ref[:, h:h + 1] += jnp.sum(dmixed, axis=-1, keepdims=True)
        dvn = dvn_ref[...]
        dg_ref[...] += _rowsum(dvn * xhat)
        db_ref[...] += _rowsum(dvn)
        dgv = _ln_bwd(dvn * g, xhat, rstd)
        dproj_ref[:, e:2 * e] = (dgv * _gelu_grad(v)).astype(BF16)

    vec = pl.BlockSpec((1, e), lambda i: (0, 0))
    wsp = pl.BlockSpec((SGU_HEADS, CHUNK, CHUNK), lambda i: (0, 0, 0))
    bsp = pl.BlockSpec((CHUNK, SGU_HEADS), lambda i: (0, 0))
    return pl.pallas_call(
        body, name=name, grid=(s // CHUNK,),
        in_specs=[pl.BlockSpec((CHUNK, e), lambda i: (i, 0)), pl.BlockSpec((CHUNK, n3), lambda i: (i, 0)),
                  vec, vec, wsp, bsp],
        out_specs=[pl.BlockSpec((CHUNK, n3), lambda i: (i, 0)), vec, vec, wsp, bsp],
        out_shape=[jax.ShapeDtypeStruct((s, n3), BF16), jax.ShapeDtypeStruct((1, e), F32),
                   jax.ShapeDtypeStruct((1, e), F32),
                   jax.ShapeDtypeStruct((SGU_HEADS, CHUNK, CHUNK), F32),
                   jax.ShapeDtypeStruct((CHUNK, SGU_HEADS), F32)],
        scratch_shapes=[pltpu.VMEM((CHUNK, e), F32)],
        compiler_params=_params(("arbitrary",)),
    )(dyg, proj, ln_g, ln_b, w_s, b_st)


def _pool_counts(pos, w):
    return jnp.minimum(pos + 1, w).astype(F32)


def _pool_fwd(name, proj):
    s, n2 = proj.shape
    e = n2 // 2
    gd = e // len(POOL_WINDOWS)
    ts = _row_tile(s, e)
    hb = ts // POOL_HALO

    def body(v_ref, halo_ref, p_ref, ext):
        i = pl.program_id(0)
        ext[POOL_HALO:, :] = v_ref[...]
        ext[:POOL_HALO, :] = jnp.where(i > 0, halo_ref[...], 0.0)
        pos = i * ts + lax.broadcasted_iota(jnp.int32, (ts, 1), 0)
        for g, w in enumerate(POOL_WINDOWS):
            cs = slice(g * gd, (g + 1) * gd)
            acc = ext[POOL_HALO:, cs]
            for j in range(1, w):
                acc = acc + ext[POOL_HALO - j:POOL_HALO - j + ts, cs]
            p_ref[:, cs] = (acc / _pool_counts(pos, w) - v_ref[:, cs]).astype(BF16)

    return pl.pallas_call(
        body, name=name, grid=(s // ts,),
        in_specs=[pl.BlockSpec((ts, e), lambda i: (i, 0)),
                  pl.BlockSpec((POOL_HALO, e), lambda i: (jnp.maximum(i * hb - 1, 0), 0))],
        out_specs=pl.BlockSpec((ts, e), lambda i: (i, 0)),
        out_shape=jax.ShapeDtypeStruct((s, e), BF16),
        scratch_shapes=[pltpu.VMEM((ts + POOL_HALO, e), F32)],
        compiler_params=_params(("parallel",)),
    )(proj, proj)


def _pool_mm(name, p, wg, proj, scale):
    s, e = p.shape
    ng, rsh, gd = wg.shape[1], wg.shape[2], wg.shape[3]
    bm = _tile(s, 1024)

    def body(p_ref, w0, w1, w2, w3, z_ref, sc_ref, q_ref, yg_ref):
        q = None
        for j, w_ref in enumerate((w0, w1, w2, w3)):
            part = jnp.dot(p_ref[:, j * rsh:(j + 1) * rsh], w_ref[...], preferred_element_type=F32)
            q = part if q is None else q + part
        z = z_ref[...]
        q_ref[...] = q
        yg_ref[...] = (q * sc_ref[...] * (z * _sigmoid(z))).astype(BF16)

    blk = pl.BlockSpec((bm, gd), lambda m, g: (m, g))
    w_specs = [pl.BlockSpec((None, None, rsh, gd), functools.partial(lambda m, g, j: (j, g, 0, 0), j=j))
               for j in range(N_CHIPS)]
    return pl.pallas_call(
        body, name=name, grid=(s // bm, ng),
        in_specs=[blk] + w_specs + [pl.BlockSpec((bm, gd), lambda m, g: (m, ng + g)),
                                    pl.BlockSpec((1, gd), lambda m, g: (0, g))],
        out_specs=[blk, blk],
        out_shape=[jax.ShapeDtypeStruct((s, e), F32), jax.ShapeDtypeStruct((s, e), BF16)],
        compiler_params=_params(("parallel", "parallel")),
    )(p, wg, wg, wg, wg, proj, scale)


def _pool_bwd1(name, dyg, q, proj, scale):
    s, e = q.shape
    ts = _row_tile(s, e)

    def body(dyg_ref, q_ref, z_ref, sc_ref, dq_ref, dz_ref, dsc_ref):
        @pl.when(pl.program_id(0) == 0)
        def _():
            dsc_ref[...] = jnp.zeros_like(dsc_ref)

        z = z_ref[...]
        sig = _sigmoid(z)
        dy = dyg_ref[...]
        qv = q_ref[...]
        sc = sc_ref[...]
        dp2 = dy * (z * sig)
        dsc_ref[...] += _rowsum(dp2 * qv)
        dq_ref[...] = (dp2 * sc).astype(BF16)
        dz_ref[...] = (dy * (qv * sc) * (sig * (1.0 + z * (1.0 - sig)))).astype(BF16)

    blk = pl.BlockSpec((ts, e), lambda i: (i, 0))
    vec = pl.BlockSpec((1, e), lambda i: (0, 0))
    return pl.pallas_call(
        body, name=name, grid=(s // ts,),
        in_specs=[blk, blk, pl.BlockSpec((ts, e), lambda i: (i, 1)), vec],
        out_specs=[blk, pl.BlockSpec((ts, e), lambda i: (i, 1)), vec],
        out_shape=[jax.ShapeDtypeStruct((s, e), BF16), jax.ShapeDtypeStruct((s, 2 * e), BF16),
                   jax.ShapeDtypeStruct((1, e), F32)],
        compiler_params=_params(("arbitrary",)),
    )(dyg, q, proj, scale)


def _pool_mm_bwd(name, dq, wg):
    s, e = dq.shape
    ng, rsh, gd = wg.shape[1], wg.shape[2], wg.shape[3]
    bm = _tile(s, 1024)
    return _mm(name, dq, wg, dims=NT, grid=(s // bm, ng * N_CHIPS, 1),
               a_spec=pl.BlockSpec((bm, gd), lambda m, j, k: (m, j // N_CHIPS)),
               b_spec=pl.BlockSpec((None, None, rsh, gd), lambda m, j, k: (j % N_CHIPS, j // N_CHIPS, 0, 0)),
               out_shape=jax.ShapeDtypeStruct((s, e), F32),
               out_specs=pl.BlockSpec((bm, rsh), lambda m, j, k: (m, j)),
               acc_shape=(bm, rsh))


def _pool_bwd2(name, dp, dproj):
    s, e = dp.shape
    gd = e // len(POOL_WINDOWS)
    ts = _row_tile(s, e)
    n = s // ts
    hb = ts // POOL_HALO

    def body(dp_ref, nxt_ref, _, dv_ref, ext):
        i = pl.program_id(0)
        pos = i * ts + lax.broadcasted_iota(jnp.int32, (ts + POOL_HALO, 1), 0)
        for g, w in enumerate(POOL_WINDOWS):
            cs = slice(g * gd, (g + 1) * gd)
            cnt = _pool_counts(pos, w)
            ext[:ts, cs] = dp_ref[:, cs] / cnt[:ts]
            ext[ts:, cs] = jnp.where(i < n - 1, nxt_ref[:, cs] / cnt[ts:], 0.0)
        for g, w in enumerate(POOL_WINDOWS):
            cs = slice(g * gd, (g + 1) * gd)
            acc = ext[:ts, cs]
            for j in range(1, w):
                acc = acc + ext[j:j + ts, cs]
            dv_ref[:, cs] = (acc - dp_ref[:, cs]).astype(BF16)

    return pl.pallas_call(
        body, name=name, grid=(n,),
        in_specs=[pl.BlockSpec((ts, e), lambda i: (i, 0)),
                  pl.BlockSpec((POOL_HALO, e), lambda i: (jnp.minimum((i + 1) * hb, s // POOL_HALO - 1), 0)),
                  pl.BlockSpec(memory_space=pl.ANY)],
        out_specs=pl.BlockSpec((ts, e), lambda i: (i, 0)),
        out_shape=jax.ShapeDtypeStruct((s, 2 * e), BF16),
        scratch_shapes=[pltpu.VMEM((ts + POOL_HALO, e), F32)],
        input_output_aliases={2: 0},
        compiler_params=_params(("parallel",)),
    )(dp, dp, dproj)


def _mm_dw_pool(name, p, dq, ng):
    s, e = p.shape
    gd = e // ng
    rsh = gd // N_CHIPS
    bk = _tile(s, 4096)
    return _mm(name, p, dq, dims=TN, grid=(N_CHIPS, ng, s // bk),
               a_spec=pl.BlockSpec((bk, rsh), lambda m, g, k: (k, g * N_CHIPS + m)),
               b_spec=pl.BlockSpec((bk, gd), lambda m, g, k: (k, g)),
               out_shape=jax.ShapeDtypeStruct((N_CHIPS, ng, rsh, gd), BF16),
               out_specs=pl.BlockSpec((None, None, rsh, gd), lambda m, g, k: (m, g, 0, 0)),
               acc_shape=(rsh, gd))


def _conv_fwd(name, proj, conv_w, conv_b):
    s, n3 = proj.shape
    e = n3 // 3
    ts, tc = _tile(s, 512), LANES
    ne = e // tc
    hb = ts // CONV_HALO
    off = CONV_HALO - (CONV_WIDTH - 1)

    def body(a_ref, gl_ref, ha_ref, hgl_ref, w_ref, b_ref, c_ref, gext):
        i = pl.program_id(1)
        gext[CONV_HALO:, :] = a_ref[...] * _sigmoid(gl_ref[...])
        gext[:CONV_HALO, :] = jnp.where(i > 0, ha_ref[...] * _sigmoid(hgl_ref[...]), 0.0)
        taps = [jnp.broadcast_to(w_ref[k:k + 1, :], (SUBLANES, LANES)) for k in range(CONV_WIDTH)]
        bias = jnp.broadcast_to(b_ref[...], (SUBLANES, LANES))

        def sub_block(rb, carry):
            r0 = pl.multiple_of(rb * CONV_SUB, CONV_SUB)
            acc = [bias] * CONV_STRIDE
            for m in range(CONV_WIDTH + CONV_STRIDE - 1):
                g_m = gext[pl.ds(r0 + off + m, SUBLANES, stride=CONV_STRIDE), :]
                for q in range(CONV_STRIDE):
                    if 0 <= m - q < CONV_WIDTH:
                        acc[q] = acc[q] + taps[m - q] * g_m
            for q in range(CONV_STRIDE):
                c_ref[pl.ds(r0 + q, SUBLANES, stride=CONV_STRIDE), :] = acc[q]
            return carry

        lax.fori_loop(0, ts // CONV_SUB, sub_block, 0)

    halo = lambda i: jnp.maximum(i * hb - 1, 0)
    return pl.pallas_call(
        body, name=name, grid=(ne, s // ts),
        in_specs=[pl.BlockSpec((ts, tc), lambda j, i: (i, j)),
                  pl.BlockSpec((ts, tc), lambda j, i: (i, ne + j)),
                  pl.BlockSpec((CONV_HALO, tc), lambda j, i: (halo(i), j)),
                  pl.BlockSpec((CONV_HALO, tc), lambda j, i: (halo(i), ne + j)),
                  pl.BlockSpec((CONV_WIDTH, tc), lambda j, i: (0, j)),
                  pl.BlockSpec((1, tc), lambda j, i: (0, j))],
        out_specs=pl.BlockSpec((ts, tc), lambda j, i: (i, j)),
        out_shape=jax.ShapeDtypeStruct((s, e), F32),
        scratch_shapes=[pltpu.VMEM((ts + CONV_HALO, tc), F32)],
        compiler_params=_params(("parallel", "parallel")),
    )(proj, proj, proj, proj, conv_w, conv_b)


def _conv_gate_fwd(name, c, proj, ln_g, ln_b):
    s, e = c.shape
    ts = _row_tile(s, e)

    def body(c_ref, z_ref, g_ref, b_ref, yg_ref):
        xhat, _ = _ln_stats(c_ref[...])
        ln = xhat * g_ref[...] + b_ref[...]
        z = z_ref[...]
        yg_ref[...] = ((ln * _sigmoid(ln)) * (z * _sigmoid(z))).astype(BF16)

    blk = pl.BlockSpec((ts, e), lambda i: (i, 0))
    vec = pl.BlockSpec((1, e), lambda i: (0, 0))
    return pl.pallas_call(
        body, name=name, grid=(s // ts,),
        in_specs=[blk, pl.BlockSpec((ts, e), lambda i: (i, 2)), vec, vec],
        out_specs=blk, out_shape=jax.ShapeDtypeStruct((s, e), BF16),
        compiler_params=_params(("parallel",)),
    )(c, proj, ln_g, ln_b)


def _conv_bwd1(name, dyg, c, proj, ln_g, ln_b):
    s, e = c.shape
    ts = _row_tile(s, e)

    def body(dyg_ref, c_ref, z_ref, g_ref, b_ref, dc_ref, dz_ref, dg_ref, db_ref, dcb_ref):
        @pl.when(pl.program_id(0) == 0)
        def _():
            dg_ref[...] = jnp.zeros_like(dg_ref)
            db_ref[...] = jnp.zeros_like(db_ref)
            dcb_ref[...] = jnp.zeros_like(dcb_ref)

        xhat, rstd = _ln_stats(c_ref[...])
        g = g_ref[...]
        ln = xhat * g + b_ref[...]
        sl = _sigmoid(ln)
        z = z_ref[...]
        sz = _sigmoid(z)
        dy = dyg_ref[...]
        dz_ref[...] = (dy * (ln * sl) * (sz * (1.0 + z * (1.0 - sz)))).astype(BF16)
        dln = dy * (z * sz) * (sl * (1.0 + ln * (1.0 - sl)))
        dg_ref[...] += _rowsum(dln * xhat)
        db_ref[...] += _rowsum(dln)
        dc = _ln_bwd(dln * g, xhat, rstd)
        dc_ref[...] = dc
        dcb_ref[...] += _rowsum(dc)

    blk = pl.BlockSpec((ts, e), lambda i: (i, 0))
    zblk = pl.BlockSpec((ts, e), lambda i: (i, 2))
    vec = pl.BlockSpec((1, e), lambda i: (0, 0))
    vshape = jax.ShapeDtypeStruct((1, e), F32)
    return pl.pallas_call(
        body, name=name, grid=(s // ts,),
        in_specs=[blk, blk, zblk, vec, vec],
        out_specs=[blk, zblk, vec, vec, vec],
        out_shape=[jax.ShapeDtypeStruct((s, e), F32), jax.ShapeDtypeStruct((s, 3 * e), BF16),
                   vshape, vshape, vshape],
        compiler_params=_params(("arbitrary",)),
    )(dyg, c, proj, ln_g, ln_b)


def _conv_bwd2(name, dc, proj, conv_w):
    s, e = dc.shape
    ts, tc = _tile(s, 512), LANES
    ne = e // tc
    n = s // ts
    hb = ts // CONV_HALO
    off = CONV_HALO - (CONV_WIDTH - 1)
    span = CONV_WIDTH + CONV_STRIDE - 1

    def body(dc_ref, nxt_ref, a_ref, gl_ref, ha_ref, hgl_ref, w_ref, dg_ref, dw_ref, gext, dcext, dwacc):
        i = pl.program_id(1)

        @pl.when(i == 0)
        def _():
            dwacc[...] = jnp.zeros_like(dwacc)

        gext[CONV_HALO:, :] = a_ref[...] * _sigmoid(gl_ref[...])
        gext[:CONV_HALO, :] = jnp.where(i > 0, ha_ref[...] * _sigmoid(hgl_ref[...]), 0.0)
        dcext[:ts, :] = dc_ref[...]
        dcext[ts:, :] = jnp.where(i < n - 1, nxt_ref[...], 0.0)
        taps = [jnp.broadcast_to(w_ref[k:k + 1, :], (SUBLANES, LANES)) for k in range(CONV_WIDTH)]

        def sub_block(rb, carry):
            r0 = pl.multiple_of(rb * CONV_SUB, CONV_SUB)
            dc_m = [dcext[pl.ds(r0 + m, SUBLANES, stride=CONV_STRIDE), :] for m in range(span)]
            dg = [jnp.zeros((SUBLANES, LANES), F32)] * CONV_STRIDE
            for k in range(CONV_WIDTH):
                for q in range(CONV_STRIDE):
                    dg[q] = dg[q] + taps[k] * dc_m[CONV_WIDTH - 1 - k + q]
            for q in range(CONV_STRIDE):
                dg_ref[pl.ds(r0 + q, SUBLANES, stride=CONV_STRIDE), :] = dg[q]
            g_m = [gext[pl.ds(r0 + off + m, SUBLANES, stride=CONV_STRIDE), :] for m in range(span)]
            for k in range(CONV_WIDTH):
                part = dc_m[0] * g_m[k]
                for q in range(1, CONV_STRIDE):
                    part = part + dc_m[q] * g_m[k + q]
                dwacc[k] += part
            return carry

        lax.fori_loop(0, ts // CONV_SUB, sub_block, 0)

        @pl.when(i == n - 1)
        def _():
            dw_ref[...] = jnp.sum(dwacc[...], axis=1)

    halo = lambda i: jnp.maximum(i * hb - 1, 0)
    nxt = lambda i: jnp.minimum((i + 1) * hb, s // CONV_HALO - 1)
    return pl.pallas_call(
        body, name=name, grid=(ne, n),
        in_specs=[pl.BlockSpec((ts, tc), lambda j, i: (i, j)),
                  pl.BlockSpec((CONV_HALO, tc), lambda j, i: (nxt(i), j)),
                  pl.BlockSpec((ts, tc), lambda j, i: (i, j)),
                  pl.BlockSpec((ts, tc), lambda j, i: (i, ne + j)),
                  pl.BlockSpec((CONV_HALO, tc), lambda j, i: (halo(i), j)),
                  pl.BlockSpec((CONV_HALO, tc), lambda j, i: (halo(i), ne + j)),
                  pl.BlockSpec((CONV_WIDTH, tc), lambda j, i: (0, j))],
        out_specs=[pl.BlockSpec((ts, tc), lambda j, i: (i, j)),
                   pl.BlockSpec((CONV_HALO, tc), lambda j, i: (0, j))],
        out_shape=[jax.ShapeDtypeStruct((s, e), F32), jax.ShapeDtypeStruct((CONV_HALO, e), F32)],
        scratch_shapes=[pltpu.VMEM((ts + CONV_HALO, tc), F32), pltpu.VMEM((ts + CONV_HALO, tc), F32),
                        pltpu.VMEM((CONV_HALO, SUBLANES, LANES), F32)],
        compiler_params=_params(("parallel", "arbitrary")),
    )(dc, dc, proj, proj, proj, proj, conv_w)


def _conv_bwd3(name, dg, proj, dproj):
    s, e = dg.shape
    ts = _row_tile(s, 2 * e)

    def body(dg_ref, agl_ref, _, out_ref):
        a = agl_ref[:, :e]
        sig = _sigmoid(agl_ref[:, e:])
        dgv = dg_ref[...]
        out_ref[:, :e] = (dgv * sig).astype(BF16)
        out_ref[:, e:] = (dgv * a * sig * (1.0 - sig)).astype(BF16)

    wide = pl.BlockSpec((ts, 2 * e), lambda i: (i, 0))
    return pl.pallas_call(
        body, name=name, grid=(s // ts,),
        in_specs=[pl.BlockSpec((ts, e), lambda i: (i, 0)), wide, pl.BlockSpec(memory_space=pl.ANY)],
        out_specs=wide, out_shape=jax.ShapeDtypeStruct((s, 3 * e), BF16),
        input_output_aliases={2: 0},
        compiler_params=_params(("parallel",)),
    )(dg, proj, dproj)


def _post_ln_bwd(name, dy_or_target, xhat, rstd, g, b, loss_head, dep=None):
    s, d = xhat.shape
    ts = _row_tile(s, d)

    def body(dy_ref, xhat_ref, rstd_ref, g_ref, b_ref, *rest):
        dr_ref, drb_ref, dg_ref, db_ref, loss_ref = rest[-5:]

        @pl.when(pl.program_id(0) == 0)
        def _():
            dg_ref[...] = jnp.zeros_like(dg_ref)
            db_ref[...] = jnp.zeros_like(db_ref)
            loss_ref[...] = jnp.zeros_like(loss_ref)

        xh = xhat_ref[...]
        gv = g_ref[...]
        if loss_head:
            err = (xh * gv + b_ref[...]) - dy_ref[...]
            per_row = jnp.mean(err * err, axis=-1, keepdims=True)
            loss_ref[...] += 0.5 * jnp.broadcast_to(_rowsum(per_row), loss_ref.shape)
            dy = err / d
        else:
            dy = dy_ref[...]
        dg_ref[...] += _rowsum(dy * xh)
        db_ref[...] += _rowsum(dy)
        dr = _ln_bwd(dy * gv, xh, rstd_ref[:, 0:1])
        dr_ref[...] = dr
        drb_ref[...] = dr.astype(BF16)

    blk = pl.BlockSpec((ts, d), lambda i: (i, 0))
    vec = pl.BlockSpec((1, d), lambda i: (0, 0))
    deps = [] if dep is None else [dep]
    return pl.pallas_call(
        body, name=name, grid=(s // ts,),
        in_specs=[blk, blk, pl.BlockSpec((ts, LANES), lambda i: (i, 0)), vec, vec] + [ANY] * len(deps),
        out_specs=[blk, blk, vec, vec, pl.BlockSpec((1, LANES), lambda i: (0, 0))],
        out_shape=[jax.ShapeDtypeStruct((s, d), F32), jax.ShapeDtypeStruct((s, d), BF16),
                   jax.ShapeDtypeStruct((1, d), F32), jax.ShapeDtypeStruct((1, d), F32),
                   jax.ShapeDtypeStruct((1, LANES), F32)],
        compiler_params=_params(("arbitrary",)),
    )(dy_or_target, xhat, rstd, g, b, *deps)


def _as2d(a):
    return a.reshape(-1, a.shape[-1])


def _cast_layer(name, a, layer):
    _, r, c = a.shape
    tr = _row_tile(r, c)

    def body(a_ref, o_ref):
        o_ref[...] = a_ref[...].astype(BF16)

    return pl.pallas_call(body, name=name, grid=(r // tr,),
                          in_specs=[pl.BlockSpec((None, tr, c), lambda i: (layer, i, 0))],
                          out_specs=pl.BlockSpec((tr, c), lambda i: (i, 0)),
                          out_shape=jax.ShapeDtypeStruct((r, c), BF16),
                          compiler_params=_params(("parallel",)))(a)


def _my_shard():
    return 2 * lax.axis_index("x") + lax.axis_index("y")


def _pair_sum(name, own, recv):
    nsh, r, c = own.shape
    h = r // 2
    tr = _row_tile(h, c)
    nb = h // tr

    def body(own_ref, recv_ref, o_ref):
        o_ref[...] = (own_ref[...].astype(F32) + recv_ref[...].astype(F32)).astype(BF16)

    blk = pl.BlockSpec((None, tr, c), lambda j, i: (j, i, 0))
    mine = pl.BlockSpec((None, tr, c), lambda j, i: (j, lax.axis_index("c") * nb + i, 0))
    return pl.pallas_call(body, name=name, grid=(nsh, nb), in_specs=[mine, blk], out_specs=blk,
                          out_shape=jax.ShapeDtypeStruct((nsh, h, c), BF16),
                          compiler_params=_params(("parallel", "parallel")))(own, recv)


def _final_sum(name, own, recv, got, layer, n_layers, prev):
    _, r, c = own.shape
    h = r // 2
    tr = _row_tile(h, c)
    nb = h // tr

    def body(own_ref, recv_ref, q0_ref, q1_ref, q2_ref, *rest):
        o_ref = rest[-1]
        acc = own_ref[...].astype(F32) + recv_ref[...].astype(F32)
        acc = acc + q0_ref[...].astype(F32)
        acc = acc + q1_ref[...].astype(F32)
        o_ref[...] = acc + q2_ref[...].astype(F32)

    in_specs = [pl.BlockSpec((None, tr, c), lambda i: (_my_shard(), lax.axis_index("c") * nb + i, 0)),
                pl.BlockSpec((None, tr, c), lambda i: (_my_shard(), i, 0))]
    in_specs += [pl.BlockSpec((None, tr, c), functools.partial(lambda i, k: (k, i, 0), k=k)) for k in range(3)]
    operands = [own, recv, got, got, got]
    aliases = {}
    if prev is not None:
        in_specs.append(pl.BlockSpec(memory_space=pl.ANY))
        operands.append(prev)
        aliases = {5: 0}
    return pl.pallas_call(body, name=name, grid=(nb,), in_specs=in_specs,
                          out_specs=pl.BlockSpec((None, tr, c), lambda i: (layer, lax.axis_index("c") * nb + i, 0)),
                          out_shape=jax.ShapeDtypeStruct((n_layers, r, c), F32),
                          input_output_aliases=aliases,
                          compiler_params=_params(("parallel",)))(*operands)


def _adam(name, w, g, m, v):
    r, c = w.shape
    tr = _row_tile(r, c)
    c1 = 1.0 / (1.0 - ADAM_B1 ** ADAM_STEP)
    c2 = 1.0 / (1.0 - ADAM_B2 ** ADAM_STEP)

    def body(w_ref, g_ref, m_ref, v_ref, d_ref, nm_ref, nv_ref):
        gv = g_ref[...]
        nm = ADAM_B1 * m_ref[...] + (1.0 - ADAM_B1) * gv
        nv = ADAM_B2 * v_ref[...] + (1.0 - ADAM_B2) * (gv * gv)
        d_ref[...] = -ADAM_LR * ((nm * c1) / (jnp.sqrt(nv * c2) + ADAM_EPS) + ADAM_WD * w_ref[...])
        nm_ref[...] = nm
        nv_ref[...] = nv

    blk = pl.BlockSpec((tr, c), lambda i: (i, 0))
    shp = jax.ShapeDtypeStruct((r, c), F32)
    return pl.pallas_call(body, name=name, grid=(r // tr,), in_specs=[blk] * 4, out_specs=[blk] * 3,
                          out_shape=[shp] * 3, compiler_params=_params(("parallel",)))(w, g, m, v)


def _sum_devices(name, parts):
    _, r, c = parts.shape
    tr = _row_tile(r, N_DEV * c)

    def body(p_ref, o_ref):
        acc = p_ref[0]
        for k in range(1, N_DEV):
            acc = acc + p_ref[k]
        o_ref[...] = acc

    return pl.pallas_call(body, name=name, grid=(r // tr,),
                          in_specs=[pl.BlockSpec((N_DEV, tr, c), lambda i: (0, i, 0))],
                          out_specs=pl.BlockSpec((tr, c), lambda i: (i, 0)),
                          out_shape=jax.ShapeDtypeStruct((r, c), F32),
                          compiler_params=_params(("parallel",)))(parts)


def _coords():
    return lax.axis_index("x"), lax.axis_index("y"), lax.axis_index("c")


def _chip_peers(x, y):
    return [(1 - x, y, 2 * (1 - x) + y), (x, 1 - y, 2 * x + 1 - y), (1 - x, 1 - y, 2 * (1 - x) + 1 - y)]


def _remote(src, dst, ssem, rsem, dev):
    return pltpu.make_async_remote_copy(src_ref=src, dst_ref=dst, send_sem=ssem, recv_sem=rsem,
                                        device_id=dev, device_id_type=MESH)


ANY = pl.BlockSpec(memory_space=pl.ANY)


SEM = pl.BlockSpec(memory_space=pltpu.SEMAPHORE)
HBM = pl.BlockSpec(memory_space=pltpu.HBM)
DATAFLOW = pltpu.SideEffectType.DATAFLOW_SIDE_EFFECTING


def _hbm(a):
    return pltpu.with_memory_space_constraint(a, pltpu.HBM)


def _split_start(name, make_copies, n_copies, srcs, land_shapes, dep):
    n = len(srcs)

    def body(*refs):
        src_refs, land_refs = refs[:n], refs[n:2 * n]
        ssem, rsem = refs[2 * n + 1], refs[2 * n + 2]
        token = refs[-1]
        for send, _ in make_copies(src_refs, land_refs, ssem, rsem):
            send.start()
        token[...] = jnp.zeros_like(token)

    lands = [_hbm(lax.empty(shp, a.dtype)) for shp, a in zip(land_shapes, srcs)]
    out_shape = [pltpu.SemaphoreType.DMA((n_copies,)), pltpu.SemaphoreType.DMA((n_copies,))]
    out_shape += [pltpu.HBM(a.shape, a.dtype) for a in list(srcs) + lands]
    out_shape.append(jax.ShapeDtypeStruct((ROW_ALIGN, LANES), F32))
    res = pl.pallas_call(
        body, name=name, out_shape=out_shape,
        in_specs=[HBM] * (2 * n) + [ANY],
        out_specs=[SEM, SEM] + [HBM] * (2 * n) + [pl.BlockSpec(memory_space=pltpu.VMEM)],
        input_output_aliases={i: 2 + i for i in range(2 * n)},
        compiler_params=pltpu.CompilerParams(has_side_effects=DATAFLOW),
    )(*[_hbm(a) for a in srcs], *lands, dep)
    return res[0], res[1], res[2:2 + n], res[2 + n:2 + 2 * n], res[-1]


def _split_wait(name, make_copies, ssem, rsem, srcs, lands, after):
    n = len(srcs)

    def body(*refs):
        src_refs, land_refs = refs[:n], refs[n:2 * n]
        for send, recv in make_copies(src_refs, land_refs, refs[2 * n], refs[2 * n + 1]):
            send.wait_send()
            recv.wait_recv()

    res = pl.pallas_call(
        body, name=name, out_shape=[pltpu.HBM(a.shape, a.dtype) for a in list(srcs) + list(lands)],
        in_specs=[HBM] * (2 * n) + [SEM, SEM, ANY], out_specs=[HBM] * (2 * n),
        input_output_aliases={i: i for i in range(2 * n)},
        compiler_params=pltpu.CompilerParams(has_side_effects=DATAFLOW),
    )(*srcs, *lands, ssem, rsem, after)
    return res[:n], res[n:]


def _gather_copies(src_refs, land_refs, ssem, rsem):
    x, y, c = _coords()
    j = 2 * x + y
    out = []
    for i, (src, land) in enumerate(zip(src_refs, land_refs)):
        h = src.shape[0] // 2
        rows = pl.ds(c * h, h)
        for k, (px, py, pj) in enumerate(_chip_peers(x, y)):
            sems = (ssem.at[3 * i + k], rsem.at[3 * i + k], (px, py, c))
            out.append((_remote(src.at[rows], land.at[j, rows], *sems),
                        _remote(src.at[rows], land.at[pj, rows], *sems)))
    return out


def _scatter_copies(src_refs, land_refs, ssem, rsem):
    x, y, c = _coords()
    out = []
    for i, (src, land) in enumerate(zip(src_refs, land_refs)):
        for k, (px, py, pj) in enumerate(_chip_peers(x, y)):
            cp = _remote(src.at[pj], land.at[k], ssem.at[3 * i + k], rsem.at[3 * i + k], (px, py, c))
            out.append((cp, cp))
    return out


def _sibling_pass(name, lands, shards, dep):
    n = len(lands)

    def body(*refs):
        shard_refs, bufs = refs[n:2 * n], refs[2 * n + 1:3 * n + 1]
        ssem, rsem, lsem = refs[3 * n + 1:]
        x, y, c = _coords()
        j = 2 * x + y
        local = [pltpu.make_async_copy(shard_refs[i], bufs[i].at[j], lsem.at[i]) for i in range(n)]
        for cp in local:
            cp.start()
        sends = []
        for i in range(n):
            h = bufs[i].shape[1] // 2
            for k, (px, py, pj) in enumerate(_chip_peers(x, y)):
                mine = bufs[i].at[pj, pl.ds(c * h, h)]
                sends.append(_remote(mine, mine, ssem.at[3 * i + k], rsem.at[3 * i + k], (x, y, 1 - c)))
        for cp in sends:
            cp.start()
        for i in range(n):
            h = bufs[i].shape[1] // 2
            for k, (px, py, pj) in enumerate(_chip_peers(x, y)):
                theirs = bufs[i].at[pj, pl.ds((1 - c) * h, h)]
                _remote(theirs, theirs, ssem.at[3 * i + k], rsem.at[3 * i + k], (x, y, 1 - c)).wait_recv()
        for cp in sends:
            cp.wait_send()
        for cp in local:
            cp.wait()

    return pl.pallas_call(
        body, name=name, in_specs=[ANY] * (2 * n + 1), out_specs=[ANY] * n,
        out_shape=[jax.ShapeDtypeStruct(a.shape, a.dtype) for a in lands],
        input_output_aliases={i: i for i in range(n)},
        scratch_shapes=[pltpu.SemaphoreType.DMA((3 * n,)), pltpu.SemaphoreType.DMA((3 * n,)),
                        pltpu.SemaphoreType.DMA((n,))],
    )(*lands, *shards, dep)


def _gather_small(small):
    def body(in_ref, out_ref, ssem, rsem, lsem):
        x, y, c = _coords()
        j = 2 * x + y
        local = pltpu.make_async_copy(in_ref, out_ref.at[j], lsem)
        local.start()
        peers = _chip_peers(x, y)
        sends = [_remote(in_ref, out_ref.at[j], ssem.at[k], rsem.at[k], (px, py, c))
                 for k, (px, py, pj) in enumerate(peers)]
        for cp in sends:
            cp.start()
        for k, (px, py, pj) in enumerate(peers):
            sends[k].wait_send()
            _remote(in_ref, out_ref.at[pj], ssem.at[k], rsem.at[k], (px, py, c)).wait_recv()
        local.wait()

    return pl.pallas_call(body, name="gather_small", in_specs=[ANY], out_specs=ANY,
                          out_shape=jax.ShapeDtypeStruct((N_CHIPS,) + small.shape, small.dtype),
                          scratch_shapes=[pltpu.SemaphoreType.DMA((3,)), pltpu.SemaphoreType.DMA((3,)),
                                          pltpu.SemaphoreType.DMA])(small)


def _pair_send(name, dws):
    n = len(dws)

    def body(*refs):
        ins, outs = refs[:n], refs[n:2 * n]
        ssem, rsem = refs[2 * n:]
        x, y, c = _coords()
        cps = []
        for i in range(n):
            h = ins[i].shape[1] // 2
            cps.append(_remote(ins[i].at[:, pl.ds((1 - c) * h, h)], outs[i], ssem.at[i], rsem.at[i], (x, y, 1 - c)))
        for cp in cps:
            cp.start()
        for cp in cps:
            cp.wait()

    out_shape = [jax.ShapeDtypeStruct((a.shape[0], a.shape[1] // 2, a.shape[2]), a.dtype) for a in dws]
    return pl.pallas_call(body, name=name, in_specs=[ANY] * n, out_specs=[ANY] * n, out_shape=out_shape,
                          scratch_shapes=[pltpu.SemaphoreType.DMA((n,))] * 2)(*dws)


def _half_swap(name, gs, layers):
    n = len(gs)

    def body(*refs):
        bufs = refs[n:2 * n]
        ssem, rsem = refs[2 * n:]
        x, y, c = _coords()
        cps = []
        for i in range(n):
            h = bufs[i].shape[1] // 2
            mine = bufs[i].at[layers[i], pl.ds(c * h, h)]
            cps.append(_remote(mine, mine, ssem.at[i], rsem.at[i], (x, y, 1 - c)))
        for cp in cps:
            cp.start()
        for i, cp in enumerate(cps):
            h = bufs[i].shape[1] // 2
            theirs = bufs[i].at[layers[i], pl.ds((1 - c) * h, h)]
            cp.wait_send()
            _remote(theirs, theirs, ssem.at[i], rsem.at[i], (x, y, 1 - c)).wait_recv()

    return pl.pallas_call(body, name=name, in_specs=[ANY] * n, out_specs=[ANY] * n,
                          out_shape=[jax.ShapeDtypeStruct(a.shape, a.dtype) for a in gs],
                          input_output_aliases={i: i for i in range(n)},
                          scratch_shapes=[pltpu.SemaphoreType.DMA((n,))] * 2)(*gs)


def _allgather_small(part):
    def body(in_ref, out_ref, ssem, rsem, lsem):
        x, y, c = _coords()
        sibling = (x, y, 1 - c)
        chips = _chip_peers(x, y)

        def slot(px, py, pc):
            return out_ref.at[4 * px + 2 * py + pc]

        local = pltpu.make_async_copy(in_ref, slot(x, y, c), lsem)
        local.start()
        first = [_remote(in_ref, slot(x, y, c), ssem.at[0], rsem.at[0], sibling)]
        first += [_remote(in_ref, slot(x, y, c), ssem.at[1 + k], rsem.at[1 + k], (px, py, c))
                  for k, (px, py, _) in enumerate(chips)]
        for cp in first:
            cp.start()
        passed = []
        for k, (px, py, _) in enumerate(chips):
            landed = slot(px, py, c)
            _remote(landed, landed, ssem.at[1 + k], rsem.at[1 + k], (px, py, c)).wait_recv()
            fwd = _remote(landed, landed, ssem.at[4 + k], rsem.at[4 + k], sibling)
            fwd.start()
            passed.append(fwd)
        theirs = slot(x, y, 1 - c)
        _remote(theirs, theirs, ssem.at[0], rsem.at[0], sibling).wait_recv()
        for k, (px, py, _) in enumerate(chips):
            landed = slot(px, py, 1 - c)
            _remote(landed, landed, ssem.at[4 + k], rsem.at[4 + k], sibling).wait_recv()
        for cp in first + passed:
            cp.wait_send()
        local.wait()

    return pl.pallas_call(body, name="allgather_small", in_specs=[ANY], out_specs=ANY,
                          out_shape=jax.ShapeDtypeStruct((N_DEV,) + part.shape, part.dtype),
                          scratch_shapes=[pltpu.SemaphoreType.DMA((N_DEV - 1,)), pltpu.SemaphoreType.DMA((N_DEV - 1,)),
                                          pltpu.SemaphoreType.DMA])(part)


ROW_ALIGN = SUBLANES
LAYER_WEIGHTS = (("a_w_in", "a_w_out"), ("b_w_in", "b_w_pool", "b_w_out"), ("c_w_in", "c_w_out"))
SMALL_SHARDED = ("a_ln_g", "a_ln_b", "c_conv_w", "c_conv_b", "c_ln_g", "c_ln_b")
SMALL_ORDER = ("a_ln_g", "a_ln_b", "a_w_s", "a_b_s", "b_scale", "c_conv_w", "c_conv_b", "c_ln_g", "c_ln_b",
               "post_ln_g", "post_ln_b")
WEIGHTS = ("a_w_in", "a_ln_g", "a_ln_b", "a_w_s", "a_b_s", "a_w_out", "b_w_in", "b_w_pool", "b_scale", "b_w_out",
           "c_w_in", "c_conv_w", "c_conv_b", "c_ln_g", "c_ln_b", "c_w_out", "post_ln_g", "post_ln_b")


def _pad_rows(a):
    pad = -a.shape[0] % ROW_ALIGN
    return jnp.pad(a, ((0, pad), (0, 0))) if pad else a


def _pack(parts, width):
    return jnp.concatenate([_pad_rows(p.reshape(-1, width)) for p in parts], axis=0)


def _unpack(packed, shapes, width):
    out, row = [], 0
    for shp in shapes:
        size = 1
        for n in shp:
            size *= n
        rows = size // width
        out.append(packed[row:row + rows].reshape(shp))
        row += rows + (-rows % ROW_ALIGN)
    return out


def _step(x, tgt, w, m, v):
    s, d = x.shape
    cx, cy, cc = _coords()
    j_me = 2 * cx + cy
    es = w["a_ln_g"].shape[1]
    e = N_CHIPS * es

    n_a, n_c = w["a_w_in"].shape[0], w["c_w_in"].shape[0]
    ng = w["b_w_pool"].shape[1]
    as3d = lambda a: a.reshape(a.shape[0], -1, a.shape[-1])

    def gather_start(i, dep):
        names = LAYER_WEIGHTS[i % 3]
        shards = [_cast_layer("cast_%s%d" % (k, i // 3), as3d(w[k]), i // 3) for k in names]
        return _split_start("gather_start%d" % i, _gather_copies, 3 * len(names), shards,
                            [(N_CHIPS,) + a.shape for a in shards], dep)

    small_all = _gather_small(_pack([w[k] for k in SMALL_SHARDED], es))
    ssem, rsem, srcs, lands, tok = gather_start(0, small_all)
    xb = _cast_layer("cast_x", x[None], 0)
    small_full = jnp.swapaxes(small_all, 0, 1).reshape(small_all.shape[1], e)
    full_shapes = [(n_a, e), (n_a, e), (CONV_WIDTH, e), (1, e), (1, e), (1, e)]
    a_ln_g, a_ln_b, conv_w, conv_b, c_ln_g, c_ln_b = _unpack(small_full, full_shapes, e)

    post_g, post_b = w["post_ln_g"], w["post_ln_b"]
    b_scale = w["b_scale"]

    saved = []
    res, gp, bp, hb = x, jnp.ones((1, d), F32), jnp.zeros((1, d), F32), xb
    for i in range(DEPTH):
        kind, l = i % 3, i // 3
        tag = "L%d" % i
        srcs, lands = _split_wait("gather_wait%d" % i, _gather_copies, ssem, rsem, srcs, lands, xb if i == 0 else res)
        full = _sibling_pass("sibling_pass%d" % i, lands, srcs, tok)
        if i + 1 < DEPTH:
            ssem, rsem, srcs, lands, tok = gather_start(i + 1, full[0])
        wl = dict(zip(LAYER_WEIGHTS[kind], full))
        w_in = wl[LAYER_WEIGHTS[kind][0]].reshape(N_CHIPS, 1, d, -1)
        w_out = wl[LAYER_WEIGHTS[kind][-1]].reshape(N_CHIPS, 1, es, d)
        proj = _mm_in(tag + "_in", hb, w_in, 0, dep=tok)
        if kind == 0:
            b_st = w["a_b_s"][l].T
            yg = _sgu_fwd(tag + "_sgu", proj, a_ln_g[l:l + 1], a_ln_b[l:l + 1], w["a_w_s"][l], b_st)
            extra = (b_st,)
        elif kind == 1:
            w_pool = wl["b_w_pool"].reshape(N_CHIPS, ng, -1, w["b_w_pool"].shape[3])
            p = _pool_fwd(tag + "_pool", proj)
            q, yg = _pool_mm(tag + "_poolmm", p, w_pool, proj, b_scale[l:l + 1])
            extra = (p, q, w_pool)
        else:
            cv = _conv_fwd(tag + "_conv", proj, conv_w, conv_b)
            yg = _conv_gate_fwd(tag + "_gate", cv, proj, c_ln_g, c_ln_b)
            extra = (cv,)
        xhat, hb_next, rstd = _mm_out_ln(tag + "_out", yg, w_out, 0, res, gp, bp, post_g[i:i + 1], post_b[i:i + 1])
        saved.append((hb, proj, yg, xhat, rstd, extra, w_in, w_out))
        res, gp, bp, hb = xhat, post_g[i:i + 1], post_b[i:i + 1], hb_next

    sg = {k: [None] * w[k].shape[0] for k in ("a_ln_g", "a_ln_b", "a_w_s", "a_b_s", "post_ln_g", "post_ln_b")}
    scat = {}
    dr, drb, dg, db, loss_vec = _post_ln_bwd("L3_lnbwd", tgt, saved[-1][3], saved[-1][4],
                                             post_g[DEPTH - 1:DEPTH], post_b[DEPTH - 1:DEPTH], True)
    grad_x = None
    for i in reversed(range(DEPTH)):
        kind, l = i % 3, i // 3
        tag = "L%d" % i
        hb_in, proj, yg, _, _, extra, w_in, w_out = saved[i]
        sg["post_ln_g"][i], sg["post_ln_b"][i] = dg, db
        dyg = _mm_dyg(tag + "_dyg", drb, w_out, 0)
        dw_mid = []
        if kind == 0:
            dproj, dlg, dlb, dws_s, dbs_t = _sgu_bwd(tag + "_sgubwd", dyg, proj, a_ln_g[l:l + 1], a_ln_b[l:l + 1],
                                                     w["a_w_s"][l], extra[0])
            sg["a_ln_g"][l], sg["a_ln_b"][l], sg["a_w_s"][l], sg["a_b_s"][l] = dlg, dlb, dws_s, dbs_t.T
        elif kind == 1:
            p, q, w_pool = extra
            dq, dproj, dscale = _pool_bwd1(tag + "_poolbwd1", dyg, q, proj, b_scale[l:l + 1])
            dp = _pool_mm_bwd(tag + "_poolmmbwd", dq, w_pool)
            dproj = _pool_bwd2(tag + "_poolbwd2", dp, dproj)
            dw_mid = [_mm_dw_pool(tag + "_dwpool", p, dq, ng).reshape(N_CHIPS, -1, dq.shape[1] // ng)]
            sg["b_scale"] = [dscale]
        else:
            cv = extra[0]
            dc, dproj, dlg, dlb, dcb = _conv_bwd1(tag + "_convbwd1", dyg, cv, proj, c_ln_g, c_ln_b)
            dgc, dcw = _conv_bwd2(tag + "_convbwd2", dc, proj, conv_w)
            dproj = _conv_bwd3(tag + "_convbwd3", dgc, proj, dproj)
            sg["c_ln_g"], sg["c_ln_b"], sg["c_conv_b"], sg["c_conv_w"] = [dlg], [dlb], [dcb], [dcw[:CONV_WIDTH]]
        dw_out = _mm_dw_out(tag + "_dwout", yg, drb).reshape(N_CHIPS, es, d)
        if i > 0:
            dh = _mm_dh(tag + "_dh", dproj, w_in, 0, dr)
        dw_in = _mm_dw_in(tag + "_dwin", hb_in, dproj)
        own = [dw_in] + dw_mid + [dw_out]
        recv = _pair_send("pair_send%d" % i, own)
        pair = [_pair_sum("pairsum%d_%d" % (i, n), a, r) for n, (a, r) in enumerate(zip(own, recv))]
        ssem, rsem, srcs, lands, tok = _split_start("scatter_start%d" % i, _scatter_copies, 3 * len(pair), pair,
                                                    [(3,) + a.shape[1:] for a in pair], tok)
        scat[i] = (own, recv, ssem, rsem, srcs, lands)
        if i > 0:
            prev = saved[i - 1]
            dr, drb, dg, db, _ = _post_ln_bwd("L%d_lnbwd" % (i - 1), dh, prev[3], prev[4],
                                              post_g[i - 1:i], post_b[i - 1:i], False, dep=tok)
        else:
            grad_x = _mm_dh(tag + "_dh", dproj, w_in, 0, dr, dep=tok)

    grads, delta, new_m, new_v = {}, {}, {}, {}

    def adam(k):
        shp = w[k].shape
        two_d = (lambda a: a.reshape(-1, shp[-1])) if shp[-1] % LANES == 0 else (lambda a: a.reshape(1, -1))
        dl, nm, nv = _adam("adam_" + k, two_d(w[k]), two_d(grads[k]), two_d(m[k]), two_d(v[k]))
        delta[k], new_m[k], new_v[k] = dl.reshape(shp), nm.reshape(shp), nv.reshape(shp)
        return dl

    reduced = {}
    after = grad_x
    for i in reversed(range(DEPTH)):
        kind, l = i % 3, i // 3
        names = LAYER_WEIGHTS[kind]
        own, recv, ssem, rsem, srcs, lands = scat[i]
        _, got = _split_wait("scatter_wait%d" % i, _scatter_copies, ssem, rsem, srcs, lands, after)
        for n, k in enumerate(names):
            reduced[k] = _final_sum("finalsum%d_%d" % (i, n), own[n], recv[n], got[n], l, w[k].shape[0],
                                    reduced.get(k))
        swapped = _half_swap("half_swap%d" % i, [reduced[k] for k in names], [l] * len(names))
        for k, g in zip(names, swapped):
            reduced[k] = g
            if l == 0:
                grads[k] = g.reshape(w[k].shape)
                after = adam(k)

    small_parts = []
    for k in SMALL_ORDER:
        small_parts.append(jnp.concatenate([a.reshape(1, -1) for a in sg[k]], axis=0))
    width = 8 * LANES
    packed = _pack(small_parts, width)
    summed = _sum_devices("sum_small", _allgather_small(packed))
    full_small_shapes = {"a_ln_g": (n_a, e), "a_ln_b": (n_a, e), "a_w_s": w["a_w_s"].shape, "a_b_s": w["a_b_s"].shape,
                         "b_scale": w["b_scale"].shape, "c_conv_w": (n_c, CONV_WIDTH, e), "c_conv_b": (n_c, e),
                         "c_ln_g": (n_c, e), "c_ln_b": (n_c, e), "post_ln_g": post_g.shape, "post_ln_b": post_b.shape}
    for k, g in zip(SMALL_ORDER, _unpack(summed, [full_small_shapes[k] for k in SMALL_ORDER], width)):
        if k in SMALL_SHARDED:
            g = lax.dynamic_slice_in_dim(g, j_me * es, es, axis=g.ndim - 1)
        grads[k] = g
        adam(k)

    loss = lax.psum(loss_vec[0, 0], ("x", "y", "c"))
    return loss, grad_x, grads, delta, new_m, new_v


def kernel(x, a_w_in, a_ln_g, a_ln_b, a_w_s, a_b_s, a_w_out, b_w_in, b_w_pool, b_scale, b_w_out, c_w_in, c_conv_w, c_conv_b, c_ln_g, c_ln_b, c_w_out, post_ln_g, post_ln_b, loss_target, m_a_w_in, m_a_ln_g, m_a_ln_b, m_a_w_s, m_a_b_s, m_a_w_out, m_b_w_in, m_b_w_pool, m_b_scale, m_b_w_out, m_c_w_in, m_c_conv_w, m_c_conv_b, m_c_ln_g, m_c_ln_b, m_c_w_out, m_post_ln_g, m_post_ln_b, v_a_w_in, v_a_ln_g, v_a_ln_b, v_a_w_s, v_a_b_s, v_a_w_out, v_b_w_in, v_b_w_pool, v_b_scale, v_b_w_out, v_c_w_in, v_c_conv_w, v_c_conv_b, v_c_ln_g, v_c_ln_b, v_c_w_out, v_post_ln_g, v_post_ln_b):
    args = locals()
    w = {k: args[k] for k in WEIGHTS}
    m = {k: args["m_" + k] for k in WEIGHTS}
    v = {k: args["v_" + k] for k in WEIGHTS}
    loss, grad_x, grads, delta, new_m, new_v = _step(x[0], loss_target[0], w, m, v)
    out = [loss, grad_x[None]]
    for group in (grads, delta, new_m, new_v):
        out.extend(group[k] for k in WEIGHTS)
    return tuple(out)
```

```python
import functools

import jax
import jax.numpy as jnp
from jax import lax
from jax.experimental import pallas as pl
from jax.experimental.pallas import tpu as pltpu

F32 = jnp.float32
BF16 = jnp.bfloat16
MESH = pl.DeviceIdType.MESH

DEPTH = 4
CHUNK = 128
SGU_HEADS = 8
POOL_WINDOWS = (2, 4, 8, 16)
POOL_HALO = 16
CONV_WIDTH = 31
CONV_HALO = 32
LN_EPS = 1e-5
ALPHA = (2.0 * DEPTH) ** 0.25
ADAM_LR, ADAM_B1, ADAM_B2, ADAM_EPS, ADAM_WD, ADAM_STEP = 0.001, 0.9, 0.999, 1e-08, 0.01, 10
N_CHIPS = 4
N_DEV = 8
LANES = 128
SUBLANES = 8
CONV_STRIDE = 4
CONV_SUB = SUBLANES * CONV_STRIDE
VMEM_LIMIT = 56 << 20
GELU_K = 0.7978845608028654
GELU_C = 0.044715


def _params(sem=None, vmem=VMEM_LIMIT):
    return pltpu.CompilerParams(dimension_semantics=sem, vmem_limit_bytes=vmem)


def _tile(dim, pref):
    if dim <= pref:
        return dim
    t = 1 << (pref.bit_length() - 1)
    while dim % t:
        t //= 2
    return t


ROW_BLOCK_BYTES = 2 << 20


def _row_tile(rows, cols):
    return _tile(rows, max(ROW_ALIGN, ROW_BLOCK_BYTES // (4 * cols)))


def _sigmoid(x):
    return 1.0 / (1.0 + jnp.exp(-x))


def _gelu(x):
    return 0.5 * x * (1.0 + jnp.tanh(GELU_K * (x + GELU_C * x * x * x)))


def _gelu_grad(x):
    t = jnp.tanh(GELU_K * (x + GELU_C * x * x * x))
    return 0.5 * (1.0 + t) + 0.5 * x * (1.0 - t * t) * GELU_K * (1.0 + 3.0 * GELU_C * x * x)


def _ln_stats(x):
    mu = jnp.mean(x, axis=-1, keepdims=True)
    d = x - mu
    var = jnp.mean(d * d, axis=-1, keepdims=True)
    rstd = lax.rsqrt(var + LN_EPS)
    return d * rstd, rstd


def _ln_bwd(dy_hat, xhat, rstd):
    m1 = jnp.mean(dy_hat, axis=-1, keepdims=True)
    m2 = jnp.mean(dy_hat * xhat, axis=-1, keepdims=True)
    return rstd * (dy_hat - m1 - xhat * m2)


def _rowsum(x):
    return jnp.sum(x, axis=0, keepdims=True)


def _mm(name, a, b, *, dims, grid, a_spec, b_spec, out_shape, out_specs, acc_shape,
        extra=(), extra_specs=(), epilogue=None, dep=None):
    nk = grid[2]
    n_extra = len(extra)
    single = not isinstance(out_shape, (list, tuple))
    shapes = [out_shape] if single else list(out_shape)
    specs = [out_specs] if single else list(out_specs)
    n_out = len(shapes)
    deps = [] if dep is None else [dep]
    first_out = 2 + n_extra + len(deps)

    def body(*refs):
        a_ref, b_ref = refs[0], refs[1]
        ex = refs[2:2 + n_extra]
        outs = refs[first_out:first_out + n_out]

        def finish(acc):
            if epilogue is None:
                outs[0][...] = acc.astype(outs[0].dtype)
            else:
                epilogue(acc, ex, outs)

        part = lax.dot_general(a_ref[...].astype(BF16), b_ref[...].astype(BF16), dims, preferred_element_type=F32)
        if nk == 1:
            finish(part)
            return
        acc_ref = refs[-1]
        k = pl.program_id(2)

        @pl.when(k == 0)
        def _():
            acc_ref[...] = part

        @pl.when(jnp.logical_and(k > 0, k < nk - 1))
        def _():
            acc_ref[...] += part

        @pl.when(k == nk - 1)
        def _():
            finish(acc_ref[...] + part)

    res = pl.pallas_call(
        body, name=name, grid=grid, in_specs=[a_spec, b_spec, *extra_specs] + [ANY] * len(deps), out_specs=specs,
        out_shape=shapes, scratch_shapes=[pltpu.VMEM(acc_shape, F32)] if nk > 1 else [],
        compiler_params=_params(("parallel", "parallel", "arbitrary")),
    )(a, b, *extra, *deps)
    return res[0] if single else res


NN = (((1,), (0,)), ((), ()))
NT = (((1,), (1,)), ((), ()))
TN = (((0,), (0,)), ((), ()))


def _mm_in(name, hb, wg, layer, dep=None):
    s, d = hb.shape
    nsh = wg.shape[3]
    n = N_CHIPS * nsh
    bm, bn, bk = _tile(s, 1024), _tile(nsh, 1024), _tile(d, 2048)
    per = nsh // bn
    return _mm(name, hb, wg, dims=NN, grid=(s // bm, n // bn, d // bk),
               a_spec=pl.BlockSpec((bm, bk), lambda m, j, k: (m, k)),
               b_spec=pl.BlockSpec((None, None, bk, bn), lambda m, j, k: (j // per, layer, k, j % per)),
               out_shape=jax.ShapeDtypeStruct((s, n), F32),
               out_specs=pl.BlockSpec((bm, bn), lambda m, j, k: (m, j)),
               acc_shape=(bm, bn), dep=dep)


def _mm_out_ln(name, yg, wg, layer, res, gp, bp, g, b):
    s, e = yg.shape
    esh, d = wg.shape[2], wg.shape[3]
    bm, bk = _tile(s, 512), _tile(esh, 1024)
    per = esh // bk

    def epilogue(acc, ex, outs):
        res_ref, gp_ref, bp_ref, g_ref, b_ref = ex
        xhat_ref, hb_ref, rstd_ref = outs
        r = ALPHA * (res_ref[...] * gp_ref[...] + bp_ref[...]) + acc
        xhat, rstd = _ln_stats(r)
        xhat_ref[...] = xhat
        hb_ref[...] = (xhat * g_ref[...] + b_ref[...]).astype(BF16)
        rstd_ref[...] = jnp.broadcast_to(rstd, rstd_ref.shape)

    vec = pl.BlockSpec((1, d), lambda m, j, k: (0, 0))
    row = pl.BlockSpec((bm, d), lambda m, j, k: (m, 0))
    return _mm(name, yg, wg, dims=NN, grid=(s // bm, 1, e // bk),
               a_spec=pl.BlockSpec((bm, bk), lambda m, j, k: (m, k)),
               b_spec=pl.BlockSpec((None, None, bk, d), lambda m, j, k: (k // per, layer, k % per, 0)),
               extra=(res, gp, bp, g, b), extra_specs=(row, vec, vec, vec, vec),
               out_shape=(jax.ShapeDtypeStruct((s, d), F32), jax.ShapeDtypeStruct((s, d), BF16),
                          jax.ShapeDtypeStruct((s, LANES), F32)),
               out_specs=(row, row, pl.BlockSpec((bm, LANES), lambda m, j, k: (m, 0))),
               acc_shape=(bm, d), epilogue=epilogue)


def _mm_dyg(name, drb, wg, layer):
    s, d = drb.shape
    esh = wg.shape[2]
    e = N_CHIPS * esh
    bm, bn, bk = _tile(s, 1024), _tile(esh, 1024), _tile(d, 2048)
    per = esh // bn
    return _mm(name, drb, wg, dims=NT, grid=(s // bm, e // bn, d // bk),
               a_spec=pl.BlockSpec((bm, bk), lambda m, j, k: (m, k)),
               b_spec=pl.BlockSpec((None, None, bn, bk), lambda m, j, k: (j // per, layer, j % per, k)),
               out_shape=jax.ShapeDtypeStruct((s, e), F32),
               out_specs=pl.BlockSpec((bm, bn), lambda m, j, k: (m, j)),
               acc_shape=(bm, bn))


def _mm_dw_out(name, yg, drb):
    s, e = yg.shape
    d = drb.shape[1]
    bm, bn, bk = _tile(e, 1024), _tile(d, 1024), _tile(s, 4096)
    return _mm(name, yg, drb, dims=TN, grid=(e // bm, d // bn, s // bk),
               a_spec=pl.BlockSpec((bk, bm), lambda m, j, k: (k, m)),
               b_spec=pl.BlockSpec((bk, bn), lambda m, j, k: (k, j)),
               out_shape=jax.ShapeDtypeStruct((e, d), BF16),
               out_specs=pl.BlockSpec((bm, bn), lambda m, j, k: (m, j)),
               acc_shape=(bm, bn))


def _mm_dh(name, dproj, wg, layer, dr, dep=None):
    s, n = dproj.shape
    d, nsh = wg.shape[2], wg.shape[3]
    bm, bn = _tile(s, 2048), _tile(d, 1024)
    bk = max(t for t in range(LANES, 769, LANES) if nsh % t == 0)
    per = nsh // bk

    def epilogue(acc, ex, outs):
        outs[0][...] = ALPHA * ex[0][...] + acc

    blk = pl.BlockSpec((bm, bn), lambda m, j, k: (m, j))
    return _mm(name, dproj, wg, dims=NT, grid=(s // bm, d // bn, n // bk),
               a_spec=pl.BlockSpec((bm, bk), lambda m, j, k: (m, k)),
               b_spec=pl.BlockSpec((None, None, bn, bk), lambda m, j, k: (k // per, layer, j, k % per)),
               extra=(dr,), extra_specs=(pl.BlockSpec((bm, bn), lambda m, j, k: (m, j), pipeline_mode=pl.Buffered(1)),),
               out_shape=jax.ShapeDtypeStruct((s, d), F32), out_specs=blk,
               acc_shape=(bm, bn), epilogue=epilogue, dep=dep)


def _mm_dw_in(name, hb, dproj):
    s, d = hb.shape
    n = dproj.shape[1]
    nsh = n // N_CHIPS
    bm, bn, bk = _tile(d, 1024), _tile(nsh, 1024), _tile(s, 4096)
    per = nsh // bn
    return _mm(name, hb, dproj, dims=TN, grid=(d // bm, n // bn, s // bk),
               a_spec=pl.BlockSpec((bk, bm), lambda m, j, k: (k, m)),
               b_spec=pl.BlockSpec((bk, bn), lambda m, j, k: (k, j)),
               out_shape=jax.ShapeDtypeStruct((N_CHIPS, d, nsh), BF16),
               out_specs=pl.BlockSpec((None, bm, bn), lambda m, j, k: (j // per, m, j % per)),
               acc_shape=(bm, bn))


def _causal_mask():
    r = lax.broadcasted_iota(jnp.int32, (CHUNK, CHUNK), 0)
    c = lax.broadcasted_iota(jnp.int32, (CHUNK, CHUNK), 1)
    return r >= c


def _sgu_fwd(name, proj, ln_g, ln_b, w_s, b_st):
    s, n3 = proj.shape
    e = n3 // 3
    hd = e // SGU_HEADS

    def body(proj_ref, g_ref, b_ref, ws_ref, bst_ref, yg_ref):
        xhat, _ = _ln_stats(_gelu(proj_ref[:, e:2 * e]))
        vnb = (xhat * g_ref[...] + b_ref[...]).astype(BF16)
        tri = _causal_mask()
        for h in range(SGU_HEADS):
            hs = slice(h * hd, (h + 1) * hd)
            ws = jnp.where(tri, ws_ref[h], 0.0).astype(BF16)
            mixed = jnp.dot(ws, vnb[:, hs], preferred_element_type=F32) + bst_ref[:, h:h + 1]
            u = proj_ref[:, hs]
            z = proj_ref[:, 2 * e + h * hd:2 * e + (h + 1) * hd]
            yg_ref[:, hs] = (_gelu(u) * mixed * (z * _sigmoid(z))).astype(BF16)

    vec = pl.BlockSpec((1, e), lambda i: (0, 0))
    return pl.pallas_call(
        body, name=name, grid=(s // CHUNK,),
        in_specs=[pl.BlockSpec((CHUNK, n3), lambda i: (i, 0)), vec, vec,
                  pl.BlockSpec((SGU_HEADS, CHUNK, CHUNK), lambda i: (0, 0, 0)),
                  pl.BlockSpec((CHUNK, SGU_HEADS), lambda i: (0, 0))],
        out_specs=pl.BlockSpec((CHUNK, e), lambda i: (i, 0)),
        out_shape=jax.ShapeDtypeStruct((s, e), BF16),
        compiler_params=_params(("parallel",)),
    )(proj, ln_g, ln_b, w_s, b_st)


def _sgu_bwd(name, dyg, proj, ln_g, ln_b, w_s, b_st):
    s, n3 = proj.shape
    e = n3 // 3
    hd = e // SGU_HEADS

    def body(dyg_ref, proj_ref, g_ref, b_ref, ws_ref, bst_ref,
             dproj_ref, dg_ref, db_ref, dws_ref, dbs_ref, dvn_ref):
        @pl.when(pl.program_id(0) == 0)
        def _():
            dg_ref[...] = jnp.zeros_like(dg_ref)
            db_ref[...] = jnp.zeros_like(db_ref)
            dws_ref[...] = jnp.zeros_like(dws_ref)
            dbs_ref[...] = jnp.zeros_like(dbs_ref)

        v = proj_ref[:, e:2 * e]
        xhat, rstd = _ln_stats(_gelu(v))
        g = g_ref[...]
        vnb = (xhat * g + b_ref[...]).astype(BF16)
        tri = _causal_mask()
        for h in range(SGU_HEADS):
            hs = slice(h * hd, (h + 1) * hd)
            zs = slice(2 * e + h * hd, 2 * e + (h + 1) * hd)
            ws = jnp.where(tri, ws_ref[h], 0.0).astype(BF16)
            mixed = jnp.dot(ws, vnb[:, hs], preferred_element_type=F32) + bst_ref[:, h:h + 1]
            u = proj_ref[:, hs]
            z = proj_ref[:, zs]
            gu = _gelu(u)
            sig = _sigmoid(z)
            dy = dyg_ref[:, hs]
            t = dy * (z * sig)
            dmixed = t * gu
            dproj_ref[:, hs] = (t * mixed * _gelu_grad(u)).astype(BF16)
            dproj_ref[:, zs] = (dy * gu * mixed * (sig * (1.0 + z * (1.0 - sig)))).astype(BF16)
            dmb = dmixed.astype(BF16)
            dvn_ref[:, hs] = lax.dot_general(ws, dmb, TN, preferred_element_type=F32)
            dws = lax.dot_general(dmb, vnb[:, hs], NT, preferred_element_type=F32)
            dws_ref[h] += jnp.where(tri, dws, 0.0)
            dbs_ref[:, h:h + 1] += jnp.sum(dmixed, axis=-1, keepdims=True)
        dvn = dvn_ref[...]
        dg_ref[...] += _rowsum(dvn * xhat)
        db_ref[...] += _rowsum(dvn)
        dgv = _ln_bwd(dvn * g, xhat, rstd)
        dproj_ref[:, e:2 * e] = (dgv * _gelu_grad(v)).astype(BF16)

    vec = pl.BlockSpec((1, e), lambda i: (0, 0))
    wsp = pl.BlockSpec((SGU_HEADS, CHUNK, CHUNK), lambda i: (0, 0, 0))
    bsp = pl.BlockSpec((CHUNK, SGU_HEADS), lambda i: (0, 0))
    return pl.pallas_call(
        body, name=name, grid=(s // CHUNK,),
        in_specs=[pl.BlockSpec((CHUNK, e), lambda i: (i, 0)), pl.BlockSpec((CHUNK, n3), lambda i: (i, 0)),
                  vec, vec, wsp, bsp],
        out_specs=[pl.BlockSpec((CHUNK, n3), lambda i: (i, 0)), vec, vec, wsp, bsp],
        out_shape=[jax.ShapeDtypeStruct((s, n3), BF16), jax.ShapeDtypeStruct((1, e), F32),
                   jax.ShapeDtypeStruct((1, e), F32),
                   jax.ShapeDtypeStruct((SGU_HEADS, CHUNK, CHUNK), F32),
                   jax.ShapeDtypeStruct((CHUNK, SGU_HEADS), F32)],
        scratch_shapes=[pltpu.VMEM((CHUNK, e), F32)],
        compiler_params=_params(("arbitrary",)),
    )(dyg, proj, ln_g, ln_b, w_s, b_st)


def _pool_counts(pos, w):
    return jnp.minimum(pos + 1, w).astype(F32)


def _pool_fwd(name, proj):
    s, n2 = proj.shape
    e = n2 // 2
    gd = e // len(POOL_WINDOWS)
    ts = _row_tile(s, e)
    hb = ts // POOL_HALO

    def body(v_ref, halo_ref, p_ref, ext):
        i = pl.program_id(0)
        ext[POOL_HALO:, :] = v_ref[...]
        ext[:POOL_HALO, :] = jnp.where(i > 0, halo_ref[...], 0.0)
        pos = i * ts + lax.broadcasted_iota(jnp.int32, (ts, 1), 0)
        for g, w in enumerate(POOL_WINDOWS):
            cs = slice(g * gd, (g + 1) * gd)
            acc = ext[POOL_HALO:, cs]
            for j in range(1, w):
                acc = acc + ext[POOL_HALO - j:POOL_HALO - j + ts, cs]
            p_ref[:, cs] = (acc / _pool_counts(pos, w) - v_ref[:, cs]).astype(BF16)

    return pl.pallas_call(
        body, name=name, grid=(s // ts,),
        in_specs=[pl.BlockSpec((ts, e), lambda i: (i, 0)),
                  pl.BlockSpec((POOL_HALO, e), lambda i: (jnp.maximum(i * hb - 1, 0), 0))],
        out_specs=pl.BlockSpec((ts, e), lambda i: (i, 0)),
        out_shape=jax.ShapeDtypeStruct((s, e), BF16),
        scratch_shapes=[pltpu.VMEM((ts + POOL_HALO, e), F32)],
        compiler_params=_params(("parallel",)),
    )(proj, proj)


def _pool_mm(name, p, wg, proj, scale):
    s, e = p.shape
    ng, rsh, gd = wg.shape[1], wg.shape[2], wg.shape[3]
    bm = _tile(s, 1024)

    def body(p_ref, w0, w1, w2, w3, z_ref, sc_ref, q_ref, yg_ref):
        q = None
        for j, w_ref in enumerate((w0, w1, w2, w3)):
            part = jnp.dot(p_ref[:, j * rsh:(j + 1) * rsh], w_ref[...], preferred_element_type=F32)
            q = part if q is None else q + part
        z = z_ref[...]
        q_ref[...] = q
        yg_ref[...] = (q * sc_ref[...] * (z * _sigmoid(z))).astype(BF16)

    blk = pl.BlockSpec((bm, gd), lambda m, g: (m, g))
    w_specs = [pl.BlockSpec((None, None, rsh, gd), functools.partial(lambda m, g, j: (j, g, 0, 0), j=j))
               for j in range(N_CHIPS)]
    return pl.pallas_call(
        body, name=name, grid=(s // bm, ng),
        in_specs=[blk] + w_specs + [pl.BlockSpec((bm, gd), lambda m, g: (m, ng + g)),
                                    pl.BlockSpec((1, gd), lambda m, g: (0, g))],
        out_specs=[blk, blk],
        out_shape=[jax.ShapeDtypeStruct((s, e), F32), jax.ShapeDtypeStruct((s, e), BF16)],
        compiler_params=_params(("parallel", "parallel")),
    )(p, wg, wg, wg, wg, proj, scale)


def _pool_bwd1(name, dyg, q, proj, scale):
    s, e = q.shape
    ts = _row_tile(s, e)

    def body(dyg_ref, q_ref, z_ref, sc_ref, dq_ref, dz_ref, dsc_ref):
        @pl.when(pl.program_id(0) == 0)
        def _():
            dsc_ref[...] = jnp.zeros_like(dsc_ref)

        z = z_ref[...]
        sig = _sigmoid(z)
        dy = dyg_ref[...]
        qv = q_ref[...]
        sc = sc_ref[...]
        dp2 = dy * (z * sig)
        dsc_ref[...] += _rowsum(dp2 * qv)
        dq_ref[...] = (dp2 * sc).astype(BF16)
        dz_ref[...] = (dy * (qv * sc) * (sig * (1.0 + z * (1.0 - sig)))).astype(BF16)

    blk = pl.BlockSpec((ts, e), lambda i: (i, 0))
    vec = pl.BlockSpec((1, e), lambda i: (0, 0))
    return pl.pallas_call(
        body, name=name, grid=(s // ts,),
        in_specs=[blk, blk, pl.BlockSpec((ts, e), lambda i: (i, 1)), vec],
        out_specs=[blk, pl.BlockSpec((ts, e), lambda i: (i, 1)), vec],
        out_shape=[jax.ShapeDtypeStruct((s, e), BF16), jax.ShapeDtypeStruct((s, 2 * e), BF16),
                   jax.ShapeDtypeStruct((1, e), F32)],
        compiler_params=_params(("arbitrary",)),
    )(dyg, q, proj, scale)


def _pool_mm_bwd(name, dq, wg):
    s, e = dq.shape
    ng, rsh, gd = wg.shape[1], wg.shape[2], wg.shape[3]
    bm = _tile(s, 1024)
    return _mm(name, dq, wg, dims=NT, grid=(s // bm, ng * N_CHIPS, 1),
               a_spec=pl.BlockSpec((bm, gd), lambda m, j, k: (m, j // N_CHIPS)),
               b_spec=pl.BlockSpec((None, None, rsh, gd), lambda m, j, k: (j % N_CHIPS, j // N_CHIPS, 0, 0)),
               out_shape=jax.ShapeDtypeStruct((s, e), F32),
               out_specs=pl.BlockSpec((bm, rsh), lambda m, j, k: (m, j)),
               acc_shape=(bm, rsh))


def _pool_bwd2(name, dp, dproj):
    s, e = dp.shape
    gd = e // len(POOL_WINDOWS)
    ts = _row_tile(s, e)
    n = s // ts
    hb = ts // POOL_HALO

    def body(dp_ref, nxt_ref, _, dv_ref, ext):
        i = pl.program_id(0)
        pos = i * ts + lax.broadcasted_iota(jnp.int32, (ts + POOL_HALO, 1), 0)
        for g, w in enumerate(POOL_WINDOWS):
            cs = slice(g * gd, (g + 1) * gd)
            cnt = _pool_counts(pos, w)
            ext[:ts, cs] = dp_ref[:, cs] / cnt[:ts]
            ext[ts:, cs] = jnp.where(i < n - 1, nxt_ref[:, cs] / cnt[ts:], 0.0)
        for g, w in enumerate(POOL_WINDOWS):
            cs = slice(g * gd, (g + 1) * gd)
            acc = ext[:ts, cs]
            for j in range(1, w):
                acc = acc + ext[j:j + ts, cs]
            dv_ref[:, cs] = (acc - dp_ref[:, cs]).astype(BF16)

    return pl.pallas_call(
        body, name=name, grid=(n,),
        in_specs=[pl.BlockSpec((ts, e), lambda i: (i, 0)),
                  pl.BlockSpec((POOL_HALO, e), lambda i: (jnp.minimum((i + 1) * hb, s // POOL_HALO - 1), 0)),
                  pl.BlockSpec(memory_space=pl.ANY)],
        out_specs=pl.BlockSpec((ts, e), lambda i: (i, 0)),
        out_shape=jax.ShapeDtypeStruct((s, 2 * e), BF16),
        scratch_shapes=[pltpu.VMEM((ts + POOL_HALO, e), F32)],
        input_output_aliases={2: 0},
        compiler_params=_params(("parallel",)),
    )(dp, dp, dproj)


def _mm_dw_pool(name, p, dq, ng):
    s, e = p.shape
    gd = e // ng
    rsh = gd // N_CHIPS
    bk = _tile(s, 4096)
    return _mm(name, p, dq, dims=TN, grid=(N_CHIPS, ng, s // bk),
               a_spec=pl.BlockSpec((bk, rsh), lambda m, g, k: (k, g * N_CHIPS + m)),
               b_spec=pl.BlockSpec((bk, gd), lambda m, g, k: (k, g)),
               out_shape=jax.ShapeDtypeStruct((N_CHIPS, ng, rsh, gd), BF16),
               out_specs=pl.BlockSpec((None, None, rsh, gd), lambda m, g, k: (m, g, 0, 0)),
               acc_shape=(rsh, gd))


def _conv_fwd(name, proj, conv_w, conv_b):
    s, n3 = proj.shape
    e = n3 // 3
    ts, tc = _tile(s, 512), LANES
    ne = e // tc
    hb = ts // CONV_HALO
    off = CONV_HALO - (CONV_WIDTH - 1)

    def body(a_ref, gl_ref, ha_ref, hgl_ref, w_ref, b_ref, c_ref, gext):
        i = pl.program_id(1)
        gext[CONV_HALO:, :] = a_ref[...] * _sigmoid(gl_ref[...])
        gext[:CONV_HALO, :] = jnp.where(i > 0, ha_ref[...] * _sigmoid(hgl_ref[...]), 0.0)
        taps = [jnp.broadcast_to(w_ref[k:k + 1, :], (SUBLANES, LANES)) for k in range(CONV_WIDTH)]
        bias = jnp.broadcast_to(b_ref[...], (SUBLANES, LANES))

        def sub_block(rb, carry):
            r0 = pl.multiple_of(rb * CONV_SUB, CONV_SUB)
            acc = [bias] * CONV_STRIDE
            for m in range(CONV_WIDTH + CONV_STRIDE - 1):
                g_m = gext[pl.ds(r0 + off + m, SUBLANES, stride=CONV_STRIDE), :]
                for q in range(CONV_STRIDE):
                    if 0 <= m - q < CONV_WIDTH:
                        acc[q] = acc[q] + taps[m - q] * g_m
            for q in range(CONV_STRIDE):
                c_ref[pl.ds(r0 + q, SUBLANES, stride=CONV_STRIDE), :] = acc[q]
            return carry

        lax.fori_loop(0, ts // CONV_SUB, sub_block, 0)

    halo = lambda i: jnp.maximum(i * hb - 1, 0)
    return pl.pallas_call(
        body, name=name, grid=(ne, s // ts),
        in_specs=[pl.BlockSpec((ts, tc), lambda j, i: (i, j)),
                  pl.BlockSpec((ts, tc), lambda j, i: (i, ne + j)),
                  pl.BlockSpec((CONV_HALO, tc), lambda j, i: (halo(i), j)),
                  pl.BlockSpec((CONV_HALO, tc), lambda j, i: (halo(i), ne + j)),
                  pl.BlockSpec((CONV_WIDTH, tc), lambda j, i: (0, j)),
                  pl.BlockSpec((1, tc), lambda j, i: (0, j))],
        out_specs=pl.BlockSpec((ts, tc), lambda j, i: (i, j)),
        out_shape=jax.ShapeDtypeStruct((s, e), F32),
        scratch_shapes=[pltpu.VMEM((ts + CONV_HALO, tc), F32)],
        compiler_params=_params(("parallel", "parallel")),
    )(proj, proj, proj, proj, conv_w, conv_b)


def _conv_gate_fwd(name, c, proj, ln_g, ln_b):
    s, e = c.shape
    ts = _row_tile(s, e)

    def body(c_ref, z_ref, g_ref, b_ref, yg_ref):
        xhat, _ = _ln_stats(c_ref[...])
        ln = xhat * g_ref[...] + b_ref[...]
        z = z_ref[...]
        yg_ref[...] = ((ln * _sigmoid(ln)) * (z * _sigmoid(z))).astype(BF16)

    blk = pl.BlockSpec((ts, e), lambda i: (i, 0))
    vec = pl.BlockSpec((1, e), lambda i: (0, 0))
    return pl.pallas_call(
        body, name=name, grid=(s // ts,),
        in_specs=[blk, pl.BlockSpec((ts, e), lambda i: (i, 2)), vec, vec],
        out_specs=blk, out_shape=jax.ShapeDtypeStruct((s, e), BF16),
        compiler_params=_params(("parallel",)),
    )(c, proj, ln_g, ln_b)


def _conv_bwd1(name, dyg, c, proj, ln_g, ln_b):
    s, e = c.shape
    ts = _row_tile(s, e)

    def body(dyg_ref, c_ref, z_ref, g_ref, b_ref, dc_ref, dz_ref, dg_ref, db_ref, dcb_ref):
        @pl.when(pl.program_id(0) == 0)
        def _():
            dg_ref[...] = jnp.zeros_like(dg_ref)
            db_ref[...] = jnp.zeros_like(db_ref)
            dcb_ref[...] = jnp.zeros_like(dcb_ref)

        xhat, rstd = _ln_stats(c_ref[...])
        g = g_ref[...]
        ln = xhat * g + b_ref[...]
        sl = _sigmoid(ln)
        z = z_ref[...]
        sz = _sigmoid(z)
        dy = dyg_ref[...]
        dz_ref[...] = (dy * (ln * sl) * (sz * (1.0 + z * (1.0 - sz)))).astype(BF16)
        dln = dy * (z * sz) * (sl * (1.0 + ln * (1.0 - sl)))
        dg_ref[...] += _rowsum(dln * xhat)
        db_ref[...] += _rowsum(dln)
        dc = _ln_bwd(dln * g, xhat, rstd)
        dc_ref[...] = dc
        dcb_ref[...] += _rowsum(dc)

    blk = pl.BlockSpec((ts, e), lambda i: (i, 0))
    zblk = pl.BlockSpec((ts, e), lambda i: (i, 2))
    vec = pl.BlockSpec((1, e), lambda i: (0, 0))
    vshape = jax.ShapeDtypeStruct((1, e), F32)
    return pl.pallas_call(
        body, name=name, grid=(s // ts,),
        in_specs=[blk, blk, zblk, vec, vec],
        out_specs=[blk, zblk, vec, vec, vec],
        out_shape=[jax.ShapeDtypeStruct((s, e), F32), jax.ShapeDtypeStruct((s, 3 * e), BF16),
                   vshape, vshape, vshape],
        compiler_params=_params(("arbitrary",)),
    )(dyg, c, proj, ln_g, ln_b)


def _conv_bwd2(name, dc, proj, conv_w):
    s, e = dc.shape
    ts, tc = _tile(s, 512), LANES
    ne = e // tc
    n = s // ts
    hb = ts // CONV_HALO
    off = CONV_HALO - (CONV_WIDTH - 1)
    span = CONV_WIDTH + CONV_STRIDE - 1

    def body(dc_ref, nxt_ref, a_ref, gl_ref, ha_ref, hgl_ref, w_ref, dg_ref, dw_ref, gext, dcext, dwacc):
        i = pl.program_id(1)

        @pl.when(i == 0)
        def _():
            dwacc[...] = jnp.zeros_like(dwacc)

        gext[CONV_HALO:, :] = a_ref[...] * _sigmoid(gl_ref[...])
        gext[:CONV_HALO, :] = jnp.where(i > 0, ha_ref[...] * _sigmoid(hgl_ref[...]), 0.0)
        dcext[:ts, :] = dc_ref[...]
        dcext[ts:, :] = jnp.where(i < n - 1, nxt_ref[...], 0.0)
        taps = [jnp.broadcast_to(w_ref[k:k + 1, :], (SUBLANES, LANES)) for k in range(CONV_WIDTH)]

        def sub_block(rb, carry):
            r0 = pl.multiple_of(rb * CONV_SUB, CONV_SUB)
            dc_m = [dcext[pl.ds(r0 + m, SUBLANES, stride=CONV_STRIDE), :] for m in range(span)]
            dg = [jnp.zeros((SUBLANES, LANES), F32)] * CONV_STRIDE
            for k in range(CONV_WIDTH):
                for q in range(CONV_STRIDE):
                    dg[q] = dg[q] + taps[k] * dc_m[CONV_WIDTH - 1 - k + q]
            for q in range(CONV_STRIDE):
                dg_ref[pl.ds(r0 + q, SUBLANES, stride=CONV_STRIDE), :] = dg[q]
            g_m = [gext[pl.ds(r0 + off + m, SUBLANES, stride=CONV_STRIDE), :] for m in range(span)]
            for k in range(CONV_WIDTH):
                part = dc_m[0] * g_m[k]
                for q in range(1, CONV_STRIDE):
                    part = part + dc_m[q] * g_m[k + q]
                dwacc[k] += part
            return carry

        lax.fori_loop(0, ts // CONV_SUB, sub_block, 0)

        @pl.when(i == n - 1)
        def _():
            dw_ref[...] = jnp.sum(dwacc[...], axis=1)

    halo = lambda i: jnp.maximum(i * hb - 1, 0)
    nxt = lambda i: jnp.minimum((i + 1) * hb, s // CONV_HALO - 1)
    return pl.pallas_call(
        body, name=name, grid=(ne, n),
        in_specs=[pl.BlockSpec((ts, tc), lambda j, i: (i, j)),
                  pl.BlockSpec((CONV_HALO, tc), lambda j, i: (nxt(i), j)),
                  pl.BlockSpec((ts, tc), lambda j, i: (i, j)),
                  pl.BlockSpec((ts, tc), lambda j, i: (i, ne + j)),
                  pl.BlockSpec((CONV_HALO, tc), lambda j, i: (halo(i), j)),
                  pl.BlockSpec((CONV_HALO, tc), lambda j, i: (halo(i), ne + j)),
                  pl.BlockSpec((CONV_WIDTH, tc), lambda j, i: (0, j))],
        out_specs=[pl.BlockSpec((ts, tc), lambda j, i: (i, j)),
                   pl.BlockSpec((CONV_HALO, tc), lambda j, i: (0, j))],
        out_shape=[jax.ShapeDtypeStruct((s, e), F32), jax.ShapeDtypeStruct((CONV_HALO, e), F32)],
        scratch_shapes=[pltpu.VMEM((ts + CONV_HALO, tc), F32), pltpu.VMEM((ts + CONV_HALO, tc), F32),
                        pltpu.VMEM((CONV_HALO, SUBLANES, LANES), F32)],
        compiler_params=_params(("parallel", "arbitrary")),
    )(dc, dc, proj, proj, proj, proj, conv_w)


def _conv_bwd3(name, dg, proj, dproj):
    s, e = dg.shape
    ts = _row_tile(s, 2 * e)

    def body(dg_ref, agl_ref, _, out_ref):
        a = agl_ref[:, :e]
        sig = _sigmoid(agl_ref[:, e:])
        dgv = dg_ref[...]
        out_ref[:, :e] = (dgv * sig).astype(BF16)
        out_ref[:, e:] = (dgv * a * sig * (1.0 - sig)).astype(BF16)

    wide = pl.BlockSpec((ts, 2 * e), lambda i: (i, 0))
    return pl.pallas_call(
        body, name=name, grid=(s // ts,),
        in_specs=[pl.BlockSpec((ts, e), lambda i: (i, 0)), wide, pl.BlockSpec(memory_space=pl.ANY)],
        out_specs=wide, out_shape=jax.ShapeDtypeStruct((s, 3 * e), BF16),
        input_output_aliases={2: 0},
        compiler_params=_params(("parallel",)),
    )(dg, proj, dproj)


def _post_ln_bwd(name, dy_or_target, xhat, rstd, g, b, loss_head, dep=None):
    s, d = xhat.shape
    ts = _row_tile(s, d)

    def body(dy_ref, xhat_ref, rstd_ref, g_ref, b_ref, *rest):
        dr_ref, drb_ref, dg_ref, db_ref, loss_ref = rest[-5:]

        @pl.when(pl.program_id(0) == 0)
        def _():
            dg_ref[...] = jnp.zeros_like(dg_ref)
            db_ref[...] = jnp.zeros_like(db_ref)
            loss_ref[...] = jnp.zeros_like(loss_ref)

        xh = xhat_ref[...]
        gv = g_ref[...]
        if loss_head:
            err = (xh * gv + b_ref[...]) - dy_ref[...]
            per_row = jnp.mean(err * err, axis=-1, keepdims=True)
            loss_ref[...] += 0.5 * jnp.broadcast_to(_rowsum(per_row), loss_ref.shape)
            dy = err / d
        else:
            dy = dy_ref[...]
        dg_ref[...] += _rowsum(dy * xh)
        db_ref[...] += _rowsum(dy)
        dr = _ln_bwd(dy * gv, xh, rstd_ref[:, 0:1])
        dr_ref[...] = dr
        drb_ref[...] = dr.astype(BF16)

    blk = pl.BlockSpec((ts, d), lambda i: (i, 0))
    vec = pl.BlockSpec((1, d), lambda i: (0, 0))
    deps = [] if dep is None else [dep]
    return pl.pallas_call(
        body, name=name, grid=(s // ts,),
        in_specs=[blk, blk, pl.BlockSpec((ts, LANES), lambda i: (i, 0)), vec, vec] + [ANY] * len(deps),
        out_specs=[blk, blk, vec, vec, pl.BlockSpec((1, LANES), lambda i: (0, 0))],
        out_shape=[jax.ShapeDtypeStruct((s, d), F32), jax.ShapeDtypeStruct((s, d), BF16),
                   jax.ShapeDtypeStruct((1, d), F32), jax.ShapeDtypeStruct((1, d), F32),
                   jax.ShapeDtypeStruct((1, LANES), F32)],
        compiler_params=_params(("arbitrary",)),
    )(dy_or_target, xhat, rstd, g, b, *deps)


def _as2d(a):
    return a.reshape(-1, a.shape[-1])


def _cast_layer(name, a, layer, slabs=0):
    _, r, c = a.shape
    tr = _row_tile(r, c)

    def body(a_ref, o_ref):
        o_ref[...] = a_ref[...].astype(BF16)

    if slabs:
        out_spec = pl.BlockSpec((None, tr, c), lambda i: (_my_shard(), i, 0))
        out_shape = jax.ShapeDtypeStruct((slabs, r, c), BF16)
    else:
        out_spec = pl.BlockSpec((tr, c), lambda i: (i, 0))
        out_shape = jax.ShapeDtypeStruct((r, c), BF16)
    return pl.pallas_call(body, name=name, grid=(r // tr,),
                          in_specs=[pl.BlockSpec((None, tr, c), lambda i: (layer, i, 0))],
                          out_specs=out_spec, out_shape=out_shape,
                          compiler_params=_params(("parallel",)))(a)


def _my_shard():
    return 2 * lax.axis_index("x") + lax.axis_index("y")


def _pair_sum(name, own, recv):
    nsh, r, c = own.shape
    h = r // 2
    tr = _row_tile(h, c)
    nb = h // tr

    def body(own_ref, recv_ref, o_ref):
        o_ref[...] = (own_ref[...].astype(F32) + recv_ref[...].astype(F32)).astype(BF16)

    blk = pl.BlockSpec((None, tr, c), lambda j, i: (j, i, 0))
    mine = pl.BlockSpec((None, tr, c), lambda j, i: (j, lax.axis_index("c") * nb + i, 0))
    return pl.pallas_call(body, name=name, grid=(nsh, nb), in_specs=[mine, blk], out_specs=blk,
                          out_shape=jax.ShapeDtypeStruct((nsh, h, c), BF16),
                          compiler_params=_params(("parallel", "parallel")))(own, recv)


def _final_sum(name, own, recv, got, layer, n_layers, prev):
    _, r, c = own.shape
    h = r // 2
    tr = _row_tile(h, c)
    nb = h // tr

    def body(own_ref, recv_ref, q0_ref, q1_ref, q2_ref, *rest):
        o_ref = rest[-1]
        acc = own_ref[...].astype(F32) + recv_ref[...].astype(F32)
        acc = acc + q0_ref[...].astype(F32)
        acc = acc + q1_ref[...].astype(F32)
        o_ref[...] = acc + q2_ref[...].astype(F32)

    in_specs = [pl.BlockSpec((None, tr, c), lambda i: (_my_shard(), lax.axis_index("c") * nb + i, 0)),
                pl.BlockSpec((None, tr, c), lambda i: (_my_shard(), i, 0))]
    in_specs += [pl.BlockSpec((None, tr, c), functools.partial(lambda i, k: (k, i, 0), k=k)) for k in range(3)]
    operands = [own, recv, got, got, got]
    aliases = {}
    if prev is not None:
        in_specs.append(pl.BlockSpec(memory_space=pl.ANY))
        operands.append(prev)
        aliases = {5: 0}
    return pl.pallas_call(body, name=name, grid=(nb,), in_specs=in_specs,
                          out_specs=pl.BlockSpec((None, tr, c), lambda i: (layer, lax.axis_index("c") * nb + i, 0)),
                          out_shape=jax.ShapeDtypeStruct((n_layers, r, c), F32),
                          input_output_aliases=aliases,
                          compiler_params=_params(("parallel",)))(*operands)


def _adam(name, w, g, m, v):
    r, c = w.shape
    tr = _row_tile(r, c)
    c1 = 1.0 / (1.0 - ADAM_B1 ** ADAM_STEP)
    c2 = 1.0 / (1.0 - ADAM_B2 ** ADAM_STEP)

    def body(w_ref, g_ref, m_ref, v_ref, d_ref, nm_ref, nv_ref):
        gv = g_ref[...]
        nm = ADAM_B1 * m_ref[...] + (1.0 - ADAM_B1) * gv
        nv = ADAM_B2 * v_ref[...] + (1.0 - ADAM_B2) * (gv * gv)
        d_ref[...] = -ADAM_LR * ((nm * c1) / (jnp.sqrt(nv * c2) + ADAM_EPS) + ADAM_WD * w_ref[...])
        nm_ref[...] = nm
        nv_ref[...] = nv

    blk = pl.BlockSpec((tr, c), lambda i: (i, 0))
    shp = jax.ShapeDtypeStruct((r, c), F32)
    return pl.pallas_call(body, name=name, grid=(r // tr,), in_specs=[blk] * 4, out_specs=[blk] * 3,
                          out_shape=[shp] * 3, compiler_params=_params(("parallel",)))(w, g, m, v)


def _sum_devices(name, parts):
    _, r, c = parts.shape
    tr = _row_tile(r, N_DEV * c)

    def body(p_ref, o_ref):
        acc = p_ref[0]
        for k in range(1, N_DEV):
            acc = acc + p_ref[k]
        o_ref[...] = acc

    return pl.pallas_call(body, name=name, grid=(r // tr,),
                          in_specs=[pl.BlockSpec((N_DEV, tr, c), lambda i: (0, i, 0))],
                          out_specs=pl.BlockSpec((tr, c), lambda i: (i, 0)),
                          out_shape=jax.ShapeDtypeStruct((r, c), F32),
                          compiler_params=_params(("parallel",)))(parts)


def _coords():
    return lax.axis_index("x"), lax.axis_index("y"), lax.axis_index("c")


def _chip_peers(x, y):
    return [(1 - x, y, 2 * (1 - x) + y), (x, 1 - y, 2 * x + 1 - y), (1 - x, 1 - y, 2 * (1 - x) + 1 - y)]


def _remote(src, dst, ssem, rsem, dev):
    return pltpu.make_async_remote_copy(src_ref=src, dst_ref=dst, send_sem=ssem, recv_sem=rsem,
                                        device_id=dev, device_id_type=MESH)


ANY = pl.BlockSpec(memory_space=pl.ANY)


SEM = pl.BlockSpec(memory_space=pltpu.SEMAPHORE)
HBM = pl.BlockSpec(memory_space=pltpu.HBM)
DATAFLOW = pltpu.SideEffectType.DATAFLOW_SIDE_EFFECTING


def _hbm(a):
    return pltpu.with_memory_space_constraint(a, pltpu.HBM)


def _split_start(name, make_copies, n_copies, srcs, land_shapes, dep):
    n, nl = len(srcs), len(land_shapes)
    nb = n + nl

    def body(*refs):
        src_refs, land_refs = refs[:n], refs[n:nb]
        ssem, rsem = refs[nb + 1], refs[nb + 2]
        token = refs[-1]
        for send, _ in make_copies(src_refs, land_refs, ssem, rsem):
            send.start()
        token[...] = jnp.zeros_like(token)

    lands = [_hbm(lax.empty(shp, dt)) for shp, dt in land_shapes]
    out_shape = [pltpu.SemaphoreType.DMA((n_copies,)), pltpu.SemaphoreType.DMA((n_copies,))]
    out_shape += [pltpu.HBM(a.shape, a.dtype) for a in list(srcs) + lands]
    out_shape.append(jax.ShapeDtypeStruct((ROW_ALIGN, LANES), F32))
    res = pl.pallas_call(
        body, name=name, out_shape=out_shape,
        in_specs=[HBM] * nb + [ANY],
        out_specs=[SEM, SEM] + [HBM] * nb + [pl.BlockSpec(memory_space=pltpu.VMEM)],
        input_output_aliases={i: 2 + i for i in range(nb)},
        compiler_params=pltpu.CompilerParams(has_side_effects=DATAFLOW),
    )(*[_hbm(a) for a in srcs], *lands, dep)
    return res[0], res[1], res[2:2 + n], res[2 + n:2 + nb], res[-1]


def _split_wait(name, make_copies, ssem, rsem, srcs, lands, after):
    n, nb = len(srcs), len(srcs) + len(lands)

    def body(*refs):
        src_refs, land_refs = refs[:n], refs[n:nb]
        for send, recv in make_copies(src_refs, land_refs, refs[nb], refs[nb + 1]):
            send.wait_send()
            recv.wait_recv()

    res = pl.pallas_call(
        body, name=name, out_shape=[pltpu.HBM(a.shape, a.dtype) for a in list(srcs) + list(lands)],
        in_specs=[HBM] * nb + [SEM, SEM, ANY], out_specs=[HBM] * nb,
        input_output_aliases={i: i for i in range(nb)},
        compiler_params=pltpu.CompilerParams(has_side_effects=DATAFLOW),
    )(*srcs, *lands, ssem, rsem, after)
    return res[:n], res[n:]


def _gather_copies(buf_refs, _, ssem, rsem):
    x, y, c = _coords()
    j = 2 * x + y
    out = []
    for i, buf in enumerate(buf_refs):
        h = buf.shape[1] // 2
        rows = pl.ds(c * h, h)
        for k, (px, py, pj) in enumerate(_chip_peers(x, y)):
            sems = (ssem.at[3 * i + k], rsem.at[3 * i + k], (px, py, c))
            out.append((_remote(buf.at[j, rows], buf.at[j, rows], *sems),
                        _remote(buf.at[pj, rows], buf.at[pj, rows], *sems)))
    return out


def _scatter_copies(src_refs, land_refs, ssem, rsem):
    x, y, c = _coords()
    out = []
    for i, (src, land) in enumerate(zip(src_refs, land_refs)):
        for k, (px, py, pj) in enumerate(_chip_peers(x, y)):
            cp = _remote(src.at[pj], land.at[k], ssem.at[3 * i + k], rsem.at[3 * i + k], (px, py, c))
            out.append((cp, cp))
    return out


def _sibling_pass(name, lands, dep):
    n = len(lands)

    def body(*refs):
        bufs = refs[n + 1:2 * n + 1]
        ssem, rsem = refs[2 * n + 1:]
        x, y, c = _coords()
        sends = []
        for i in range(n):
            h = bufs[i].shape[1] // 2
            for k, (px, py, pj) in enumerate(_chip_peers(x, y)):
                mine = bufs[i].at[pj, pl.ds(c * h, h)]
                sends.append(_remote(mine, mine, ssem.at[3 * i + k], rsem.at[3 * i + k], (x, y, 1 - c)))
        for cp in sends:
            cp.start()
        for i in range(n):
            h = bufs[i].shape[1] // 2
            for k, (px, py, pj) in enumerate(_chip_peers(x, y)):
                theirs = bufs[i].at[pj, pl.ds((1 - c) * h, h)]
                _remote(theirs, theirs, ssem.at[3 * i + k], rsem.at[3 * i + k], (x, y, 1 - c)).wait_recv()
        for cp in sends:
            cp.wait_send()

    return pl.pallas_call(
        body, name=name, in_specs=[ANY] * (n + 1), out_specs=[ANY] * n,
        out_shape=[jax.ShapeDtypeStruct(a.shape, a.dtype) for a in lands],
        input_output_aliases={i: i for i in range(n)},
        scratch_shapes=[pltpu.SemaphoreType.DMA((3 * n,)), pltpu.SemaphoreType.DMA((3 * n,))],
    )(*lands, dep)


def _gather_small(small):
    def body(in_ref, out_ref, ssem, rsem, lsem):
        x, y, c = _coords()
        j = 2 * x + y
        local = pltpu.make_async_copy(in_ref, out_ref.at[j], lsem)
        local.start()
        peers = _chip_peers(x, y)
        sends = [_remote(in_ref, out_ref.at[j], ssem.at[k], rsem.at[k], (px, py, c))
                 for k, (px, py, pj) in enumerate(peers)]
        for cp in sends:
            cp.start()
        for k, (px, py, pj) in enumerate(peers):
            sends[k].wait_send()
            _remote(in_ref, out_ref.at[pj], ssem.at[k], rsem.at[k], (px, py, c)).wait_recv()
        local.wait()

    return pl.pallas_call(body, name="gather_small", in_specs=[ANY], out_specs=ANY,
                          out_shape=jax.ShapeDtypeStruct((N_CHIPS,) + small.shape, small.dtype),
                          scratch_shapes=[pltpu.SemaphoreType.DMA((3,)), pltpu.SemaphoreType.DMA((3,)),
                                          pltpu.SemaphoreType.DMA])(small)


def _pair_send(name, dws):
    n = len(dws)

    def body(*refs):
        ins, outs = refs[:n], refs[n:2 * n]
        ssem, rsem = refs[2 * n:]
        x, y, c = _coords()
        cps = []
        for i in range(n):
            h = ins[i].shape[1] // 2
            cps.append(_remote(ins[i].at[:, pl.ds((1 - c) * h, h)], outs[i], ssem.at[i], rsem.at[i], (x, y, 1 - c)))
        for cp in cps:
            cp.start()
        for cp in cps:
            cp.wait()

    out_shape = [jax.ShapeDtypeStruct((a.shape[0], a.shape[1] // 2, a.shape[2]), a.dtype) for a in dws]
    return pl.pallas_call(body, name=name, in_specs=[ANY] * n, out_specs=[ANY] * n, out_shape=out_shape,
                          scratch_shapes=[pltpu.SemaphoreType.DMA((n,))] * 2)(*dws)


def _half_swap(name, gs, layers):
    n = len(gs)

    def body(*refs):
        bufs = refs[n:2 * n]
        ssem, rsem = refs[2 * n:]
        x, y, c = _coords()
        cps = []
        for i in range(n):
            h = bufs[i].shape[1] // 2
            mine = bufs[i].at[layers[i], pl.ds(c * h, h)]
            cps.append(_remote(mine, mine, ssem.at[i], rsem.at[i], (x, y, 1 - c)))
        for cp in cps:
            cp.start()
        for i, cp in enumerate(cps):
            h = bufs[i].shape[1] // 2
            theirs = bufs[i].at[layers[i], pl.ds((1 - c) * h, h)]
            cp.wait_send()
            _remote(theirs, theirs, ssem.at[i], rsem.at[i], (x, y, 1 - c)).wait_recv()

    return pl.pallas_call(body, name=name, in_specs=[ANY] * n, out_specs=[ANY] * n,
                          out_shape=[jax.ShapeDtypeStruct(a.shape, a.dtype) for a in gs],
                          input_output_aliases={i: i for i in range(n)},
                          scratch_shapes=[pltpu.SemaphoreType.DMA((n,))] * 2)(*gs)


def _allgather_small(part):
    def body(in_ref, out_ref, ssem, rsem, lsem):
        x, y, c = _coords()
        sibling = (x, y, 1 - c)
        chips = _chip_peers(x, y)

        def slot(px, py, pc):
            return out_ref.at[4 * px + 2 * py + pc]

        local = pltpu.make_async_copy(in_ref, slot(x, y, c), lsem)
        local.start()
        first = [_remote(in_ref, slot(x, y, c), ssem.at[0], rsem.at[0], sibling)]
        first += [_remote(in_ref, slot(x, y, c), ssem.at[1 + k], rsem.at[1 + k], (px, py, c))
                  for k, (px, py, _) in enumerate(chips)]
        for cp in first:
            cp.start()
        passed = []
        for k, (px, py, _) in enumerate(chips):
            landed = slot(px, py, c)
            _remote(landed, landed, ssem.at[1 + k], rsem.at[1 + k], (px, py, c)).wait_recv()
            fwd = _remote(landed, landed, ssem.at[4 + k], rsem.at[4 + k], sibling)
            fwd.start()
            passed.append(fwd)
        theirs = slot(x, y, 1 - c)
        _remote(theirs, theirs, ssem.at[0], rsem.at[0], sibling).wait_recv()
        for k, (px, py, _) in enumerate(chips):
            landed = slot(px, py, 1 - c)
            _remote(landed, landed, ssem.at[4 + k], rsem.at[4 + k], sibling).wait_recv()
        for cp in first + passed:
            cp.wait_send()
        local.wait()

    return pl.pallas_call(body, name="allgather_small", in_specs=[ANY], out_specs=ANY,
                          out_shape=jax.ShapeDtypeStruct((N_DEV,) + part.shape, part.dtype),
                          scratch_shapes=[pltpu.SemaphoreType.DMA((N_DEV - 1,)), pltpu.SemaphoreType.DMA((N_DEV - 1,)),
                                          pltpu.SemaphoreType.DMA])(part)


ROW_ALIGN = SUBLANES
LAYER_WEIGHTS = (("a_w_in", "a_w_out"), ("b_w_in", "b_w_pool", "b_w_out"), ("c_w_in", "c_w_out"))
SMALL_SHARDED = ("a_ln_g", "a_ln_b", "c_conv_w", "c_conv_b", "c_ln_g", "c_ln_b")
SMALL_ORDER = ("a_ln_g", "a_ln_b", "a_w_s", "a_b_s", "b_scale", "c_conv_w", "c_conv_b", "c_ln_g", "c_ln_b",
               "post_ln_g", "post_ln_b")
WEIGHTS = ("a_w_in", "a_ln_g", "a_ln_b", "a_w_s", "a_b_s", "a_w_out", "b_w_in", "b_w_pool", "b_scale", "b_w_out",
           "c_w_in", "c_conv_w", "c_conv_b", "c_ln_g", "c_ln_b", "c_w_out", "post_ln_g", "post_ln_b")


def _pad_rows(a):
    pad = -a.shape[0] % ROW_ALIGN
    return jnp.pad(a, ((0, pad), (0, 0))) if pad else a


def _pack(parts, width):
    return jnp.concatenate([_pad_rows(p.reshape(-1, width)) for p in parts], axis=0)


def _unpack(packed, shapes, width):
    out, row = [], 0
    for shp in shapes:
        size = 1
        for n in shp:
            size *= n
        rows = size // width
        out.append(packed[row:row + rows].reshape(shp))
        row += rows + (-rows % ROW_ALIGN)
    return out


def _step(x, tgt, w, m, v):
    s, d = x.shape
    cx, cy, cc = _coords()
    j_me = 2 * cx + cy
    es = w["a_ln_g"].shape[1]
    e = N_CHIPS * es

    n_a, n_c = w["a_w_in"].shape[0], w["c_w_in"].shape[0]
    ng = w["b_w_pool"].shape[1]
    as3d = lambda a: a.reshape(a.shape[0], -1, a.shape[-1])

    zones = [[_cast_layer("cast_%s%d" % (k, i // 3), as3d(w[k]), i // 3, N_CHIPS) for k in LAYER_WEIGHTS[i % 3]]
             for i in range(DEPTH)]

    def gather_start(i, dep):
        return _split_start("gather_start%d" % i, _gather_copies, 3 * len(zones[i]), zones[i], [], dep)

    small_all = _gather_small(_pack([w[k] for k in SMALL_SHARDED], es))
    ssem, rsem, lands, _, tok = gather_start(0, small_all)
    xb = _cast_layer("cast_x", x[None], 0)
    small_full = jnp.swapaxes(small_all, 0, 1).reshape(small_all.shape[1], e)
    full_shapes = [(n_a, e), (n_a, e), (CONV_WIDTH, e), (1, e), (1, e), (1, e)]
    a_ln_g, a_ln_b, conv_w, conv_b, c_ln_g, c_ln_b = _unpack(small_full, full_shapes, e)

    post_g, post_b = w["post_ln_g"], w["post_ln_b"]
    b_scale = w["b_scale"]

    saved = []
    res, gp, bp, hb = x, jnp.ones((1, d), F32), jnp.zeros((1, d), F32), xb
    for i in range(DEPTH):
        kind, l = i % 3, i // 3
        tag = "L%d" % i
        after = zones[DEPTH - 1][-1] if i == 0 else res
        lands, _ = _split_wait("gather_wait%d" % i, _gather_copies, ssem, rsem, lands, [], after)
        full = _sibling_pass("sibling_pass%d" % i, lands, tok)
        if i + 1 < DEPTH:
            ssem, rsem, lands, _, tok = gather_start(i + 1, full[0])
        wl = dict(zip(LAYER_WEIGHTS[kind], full))
        w_in = wl[LAYER_WEIGHTS[kind][0]].reshape(N_CHIPS, 1, d, -1)
        w_out = wl[LAYER_WEIGHTS[kind][-1]].reshape(N_CHIPS, 1, es, d)
        proj = _mm_in(tag + "_in", hb, w_in, 0, dep=tok)
        if kind == 0:
            b_st = w["a_b_s"][l].T
            yg = _sgu_fwd(tag + "_sgu", proj, a_ln_g[l:l + 1], a_ln_b[l:l + 1], w["a_w_s"][l], b_st)
            extra = (b_st,)
        elif kind == 1:
            w_pool = wl["b_w_pool"].reshape(N_CHIPS, ng, -1, w["b_w_pool"].shape[3])
            p = _pool_fwd(tag + "_pool", proj)
            q, yg = _pool_mm(tag + "_poolmm", p, w_pool, proj, b_scale[l:l + 1])
            extra = (p, q, w_pool)
        else:
            cv = _conv_fwd(tag + "_conv", proj, conv_w, conv_b)
            yg = _conv_gate_fwd(tag + "_gate", cv, proj, c_ln_g, c_ln_b)
            extra = (cv,)
        xhat, hb_next, rstd = _mm_out_ln(tag + "_out", yg, w_out, 0, res, gp, bp, post_g[i:i + 1], post_b[i:i + 1])
        saved.append((hb, proj, yg, xhat, rstd, extra, w_in, w_out))
        res, gp, bp, hb = xhat, post_g[i:i + 1], post_b[i:i + 1], hb_next

    sg = {k: [None] * w[k].shape[0] for k in ("a_ln_g", "a_ln_b", "a_w_s", "a_b_s", "post_ln_g", "post_ln_b")}
    scat = {}
    dr, drb, dg, db, loss_vec = _post_ln_bwd("L3_lnbwd", tgt, saved[-1][3], saved[-1][4],
                                             post_g[DEPTH - 1:DEPTH], post_b[DEPTH - 1:DEPTH], True)
    grad_x = None
    for i in reversed(range(DEPTH)):
        kind, l = i % 3, i // 3
        tag = "L%d" % i
        hb_in, proj, yg, _, _, extra, w_in, w_out = saved[i]
        sg["post_ln_g"][i], sg["post_ln_b"][i] = dg, db
        dyg = _mm_dyg(tag + "_dyg", drb, w_out, 0)
        dw_mid = []
        if kind == 0:
            dproj, dlg, dlb, dws_s, dbs_t = _sgu_bwd(tag + "_sgubwd", dyg, proj, a_ln_g[l:l + 1], a_ln_b[l:l + 1],
                                                     w["a_w_s"][l], extra[0])
            sg["a_ln_g"][l], sg["a_ln_b"][l], sg["a_w_s"][l], sg["a_b_s"][l] = dlg, dlb, dws_s, dbs_t.T
        elif kind == 1:
            p, q, w_pool = extra
            dq, dproj, dscale = _pool_bwd1(tag + "_poolbwd1", dyg, q, proj, b_scale[l:l + 1])
            dp = _pool_mm_bwd(tag + "_poolmmbwd", dq, w_pool)
            dproj = _pool_bwd2(tag + "_poolbwd2", dp, dproj)
            dw_mid = [_mm_dw_pool(tag + "_dwpool", p, dq, ng).reshape(N_CHIPS, -1, dq.shape[1] // ng)]
            sg["b_scale"] = [dscale]
        else:
            cv = extra[0]
            dc, dproj, dlg, dlb, dcb = _conv_bwd1(tag + "_convbwd1", dyg, cv, proj, c_ln_g, c_ln_b)
            dgc, dcw = _conv_bwd2(tag + "_convbwd2", dc, proj, conv_w)
            dproj = _conv_bwd3(tag + "_convbwd3", dgc, proj, dproj)
            sg["c_ln_g"], sg["c_ln_b"], sg["c_conv_b"], sg["c_conv_w"] = [dlg], [dlb], [dcb], [dcw[:CONV_WIDTH]]
        dw_out = _mm_dw_out(tag + "_dwout", yg, drb).reshape(N_CHIPS, es, d)
        if i > 0:
            dh = _mm_dh(tag + "_dh", dproj, w_in, 0, dr)
        dw_in = _mm_dw_in(tag + "_dwin", hb_in, dproj)
        own = [dw_in] + dw_mid + [dw_out]
        recv = _pair_send("pair_send%d" % i, own)
        pair = [_pair_sum("pairsum%d_%d" % (i, n), a, r) for n, (a, r) in enumerate(zip(own, recv))]
        ssem, rsem, srcs, lands, tok = _split_start("scatter_start%d" % i, _scatter_copies, 3 * len(pair), pair,
                                                    [((3,) + a.shape[1:], a.dtype) for a in pair], tok)
        scat[i] = (own, recv, ssem, rsem, srcs, lands)
        if i > 0:
            prev = saved[i - 1]
            dr, drb, dg, db, _ = _post_ln_bwd("L%d_lnbwd" % (i - 1), dh, prev[3], prev[4],
                                              post_g[i - 1:i], post_b[i - 1:i], False, dep=tok)
        else:
            grad_x = _mm_dh(tag + "_dh", dproj, w_in, 0, dr, dep=tok)

    grads, delta, new_m, new_v = {}, {}, {}, {}

    def adam(k):
        shp = w[k].shape
        two_d = (lambda a: a.reshape(-1, shp[-1])) if shp[-1] % LANES == 0 else (lambda a: a.reshape(1, -1))
        dl, nm, nv = _adam("adam_" + k, two_d(w[k]), two_d(grads[k]), two_d(m[k]), two_d(v[k]))
        delta[k], new_m[k], new_v[k] = dl.reshape(shp), nm.reshape(shp), nv.reshape(shp)
        return dl

    reduced = {}
    after = grad_x
    for i in reversed(range(DEPTH)):
        kind, l = i % 3, i // 3
        names = LAYER_WEIGHTS[kind]
        own, recv, ssem, rsem, srcs, lands = scat[i]
        _, got = _split_wait("scatter_wait%d" % i, _scatter_copies, ssem, rsem, srcs, lands, after)
        for n, k in enumerate(names):
            reduced[k] = _final_sum("finalsum%d_%d" % (i, n), own[n], recv[n], got[n], l, w[k].shape[0],
                                    reduced.get(k))
        swapped = _half_swap("half_swap%d" % i, [reduced[k] for k in names], [l] * len(names))
        for k, g in zip(names, swapped):
            reduced[k] = g
            if l == 0:
                grads[k] = g.reshape(w[k].shape)
                after = adam(k)

    small_parts = []
    for k in SMALL_ORDER:
        small_parts.append(jnp.concatenate([a.reshape(1, -1) for a in sg[k]], axis=0))
    width = 8 * LANES
    packed = _pack(small_parts, width)
    summed = _sum_devices("sum_small", _allgather_small(packed))
    full_small_shapes = {"a_ln_g": (n_a, e), "a_ln_b": (n_a, e), "a_w_s": w["a_w_s"].shape, "a_b_s": w["a_b_s"].shape,
                         "b_scale": w["b_scale"].shape, "c_conv_w": (n_c, CONV_WIDTH, e), "c_conv_b": (n_c, e),
                         "c_ln_g": (n_c, e), "c_ln_b": (n_c, e), "post_ln_g": post_g.shape, "post_ln_b": post_b.shape}
    for k, g in zip(SMALL_ORDER, _unpack(summed, [full_small_shapes[k] for k in SMALL_ORDER], width)):
        if k in SMALL_SHARDED:
            g = lax.dynamic_slice_in_dim(g, j_me * es, es, axis=g.ndim - 1)
        grads[k] = g
        adam(k)

    loss = lax.psum(loss_vec[0, 0], ("x", "y", "c"))
    return loss, grad_x, grads, delta, new_m, new_v


def kernel(x, a_w_in, a_ln_g, a_ln_b, a_w_s, a_b_s, a_w_out, b_w_in, b_w_pool, b_scale, b_w_out, c_w_in, c_conv_w, c_conv_b, c_ln_g, c_ln_b, c_w_out, post_ln_g, post_ln_b, loss_target, m_a_w_in, m_a_ln_g, m_a_ln_b, m_a_w_s, m_a_b_s, m_a_w_out, m_b_w_in, m_b_w_pool, m_b_scale, m_b_w_out, m_c_w_in, m_c_conv_w, m_c_conv_b, m_c_ln_g, m_c_ln_b, m_c_w_out, m_post_ln_g, m_post_ln_b, v_a_w_in, v_a_ln_g, v_a_ln_b, v_a_w_s, v_a_b_s, v_a_w_out, v_b_w_in, v_b_w_pool, v_b_scale, v_b_w_out, v_c_w_in, v_c_conv_w, v_c_conv_b, v_c_ln_g, v_c_ln_b, v_c_w_out, v_post_ln_g, v_post_ln_b):
    args = locals()
    w = {k: args[k] for k in WEIGHTS}
    m = {k: args["m_" + k] for k in WEIGHTS}
    v = {k: args["v_" + k] for k in WEIGHTS}
    loss, grad_x, grads, delta, new_m, new_v = _step(x[0], loss_target[0], w, m, v)
    out = [loss, grad_x[None]]
    for group in (grads, delta, new_m, new_v):
        out.extend(group[k] for k in WEIGHTS)
    return tuple(out)
```

```python
import functools

import jax
import jax.numpy as jnp
from jax import lax
from jax.experimental import pallas as pl
from jax.experimental.pallas import tpu as pltpu

F32 = jnp.float32
BF16 = jnp.bfloat16
MESH = pl.DeviceIdType.MESH

DEPTH = 4
CHUNK = 128
SGU_HEADS = 8
POOL_WINDOWS = (2, 4, 8, 16)
POOL_HALO = 16
CONV_WIDTH = 31
CONV_HALO = 32
LN_EPS = 1e-5
ALPHA = (2.0 * DEPTH) ** 0.25
ADAM_LR, ADAM_B1, ADAM_B2, ADAM_EPS, ADAM_WD, ADAM_STEP = 0.001, 0.9, 0.999, 1e-08, 0.01, 10
N_CHIPS = 4
N_DEV = 8
LANES = 128
SUBLANES = 8
CONV_STRIDE = 4
CONV_SUB = SUBLANES * CONV_STRIDE
VMEM_LIMIT = 56 << 20
GELU_K = 0.7978845608028654
GELU_C = 0.044715


def _params(sem=None, vmem=VMEM_LIMIT):
    return pltpu.CompilerParams(dimension_semantics=sem, vmem_limit_bytes=vmem)


def _tile(dim, pref):
    if dim <= pref:
        return dim
    t = 1 << (pref.bit_length() - 1)
    while dim % t:
        t //= 2
    return t


ROW_BLOCK_BYTES = 2 << 20


def _row_tile(rows, cols):
    return _tile(rows, max(ROW_ALIGN, ROW_BLOCK_BYTES // (4 * cols)))


def _sigmoid(x):
    return 1.0 / (1.0 + jnp.exp(-x))


def _gelu_gate(x):
    x2 = x * x
    return _sigmoid((2.0 * GELU_K) * x * (1.0 + GELU_C * x2)), x2


def _gelu(x):
    return 0.5 * x * (1.0 + jnp.tanh(GELU_K * (x + GELU_C * x * x * x)))


def _gelu_and_grad(x):
    s, x2 = _gelu_gate(x)
    return x * s, s + x * (s * (1.0 - s)) * ((2.0 * GELU_K) * (1.0 + (3.0 * GELU_C) * x2))


def _ln_stats(x):
    mu = jnp.mean(x, axis=-1, keepdims=True)
    d = x - mu
    var = jnp.mean(d * d, axis=-1, keepdims=True)
    rstd = lax.rsqrt(var + LN_EPS)
    return d * rstd, rstd


def _ln_bwd(dy_hat, xhat, rstd):
    m1 = jnp.mean(dy_hat, axis=-1, keepdims=True)
    m2 = jnp.mean(dy_hat * xhat, axis=-1, keepdims=True)
    return rstd * (dy_hat - m1 - xhat * m2)


def _rowsum(x):
    return jnp.sum(x, axis=0, keepdims=True)


def _mm(name, a, b, *, dims, grid, a_spec, b_spec, out_shape, out_specs, acc_shape,
        extra=(), extra_specs=(), epilogue=None, dep=None):
    nk = grid[2]
    n_extra = len(extra)
    single = not isinstance(out_shape, (list, tuple))
    shapes = [out_shape] if single else list(out_shape)
    specs = [out_specs] if single else list(out_specs)
    n_out = len(shapes)
    deps = [] if dep is None else [dep]
    first_out = 2 + n_extra + len(deps)

    def body(*refs):
        a_ref, b_ref = refs[0], refs[1]
        ex = refs[2:2 + n_extra]
        outs = refs[first_out:first_out + n_out]

        def finish(acc):
            if epilogue is None:
                outs[0][...] = acc.astype(outs[0].dtype)
            else:
                epilogue(acc, ex, outs)

        part = lax.dot_general(a_ref[...].astype(BF16), b_ref[...].astype(BF16), dims, preferred_element_type=F32)
        if nk == 1:
            finish(part)
            return
        acc_ref = refs[-1]
        k = pl.program_id(2)

        @pl.when(k == 0)
        def _():
            acc_ref[...] = part

        @pl.when(jnp.logical_and(k > 0, k < nk - 1))
        def _():
            acc_ref[...] += part

        @pl.when(k == nk - 1)
        def _():
            finish(acc_ref[...] + part)

    res = pl.pallas_call(
        body, name=name, grid=grid, in_specs=[a_spec, b_spec, *extra_specs] + [ANY] * len(deps), out_specs=specs,
        out_shape=shapes, scratch_shapes=[pltpu.VMEM(acc_shape, F32)] if nk > 1 else [],
        compiler_params=_params(("parallel", "parallel", "arbitrary")),
    )(a, b, *extra, *deps)
    return res[0] if single else res


NN = (((1,), (0,)), ((), ()))
NT = (((1,), (1,)), ((), ()))
TN = (((0,), (0,)), ((), ()))


def _mm_in(name, hb, wg, layer, dep=None):
    s, d = hb.shape
    nsh = wg.shape[3]
    n = N_CHIPS * nsh
    bm, bn, bk = _tile(s, 1024), _tile(nsh, 1024), _tile(d, 2048)
    per = nsh // bn
    return _mm(name, hb, wg, dims=NN, grid=(s // bm, n // bn, d // bk),
               a_spec=pl.BlockSpec((bm, bk), lambda m, j, k: (m, k)),
               b_spec=pl.BlockSpec((None, None, bk, bn), lambda m, j, k: (j // per, layer, k, j % per)),
               out_shape=jax.ShapeDtypeStruct((s, n), F32),
               out_specs=pl.BlockSpec((bm, bn), lambda m, j, k: (m, j)),
               acc_shape=(bm, bn), dep=dep)


def _mm_out_ln(name, yg, wg, layer, res, gp, bp, g, b):
    s, e = yg.shape
    esh, d = wg.shape[2], wg.shape[3]
    bm, bk = _tile(s, 512), _tile(esh, 1024)
    per = esh // bk

    def epilogue(acc, ex, outs):
        res_ref, gp_ref, bp_ref, g_ref, b_ref = ex
        xhat_ref, hb_ref, rstd_ref = outs
        r = ALPHA * (res_ref[...] * gp_ref[...] + bp_ref[...]) + acc
        xhat, rstd = _ln_stats(r)
        xhat_ref[...] = xhat
        hb_ref[...] = (xhat * g_ref[...] + b_ref[...]).astype(BF16)
        rstd_ref[...] = jnp.broadcast_to(rstd, rstd_ref.shape)

    vec = pl.BlockSpec((1, d), lambda m, j, k: (0, 0))
    row = pl.BlockSpec((bm, d), lambda m, j, k: (m, 0))
    return _mm(name, yg, wg, dims=NN, grid=(s // bm, 1, e // bk),
               a_spec=pl.BlockSpec((bm, bk), lambda m, j, k: (m, k)),
               b_spec=pl.BlockSpec((None, None, bk, d), lambda m, j, k: (k // per, layer, k % per, 0)),
               extra=(res, gp, bp, g, b), extra_specs=(row, vec, vec, vec, vec),
               out_shape=(jax.ShapeDtypeStruct((s, d), F32), jax.ShapeDtypeStruct((s, d), BF16),
                          jax.ShapeDtypeStruct((s, LANES), F32)),
               out_specs=(row, row, pl.BlockSpec((bm, LANES), lambda m, j, k: (m, 0))),
               acc_shape=(bm, d), epilogue=epilogue)


def _mm_dyg(name, drb, wg, layer):
    s, d = drb.shape
    esh = wg.shape[2]
    e = N_CHIPS * esh
    bm, bn, bk = _tile(s, 1024), _tile(esh, 1024), _tile(d, 2048)
    per = esh // bn
    return _mm(name, drb, wg, dims=NT, grid=(s // bm, e // bn, d // bk),
               a_spec=pl.BlockSpec((bm, bk), lambda m, j, k: (m, k)),
               b_spec=pl.BlockSpec((None, None, bn, bk), lambda m, j, k: (j // per, layer, j % per, k)),
               out_shape=jax.ShapeDtypeStruct((s, e), F32),
               out_specs=pl.BlockSpec((bm, bn), lambda m, j, k: (m, j)),
               acc_shape=(bm, bn))


def _mm_dw_out(name, yg, drb, dep=None):
    s, e = yg.shape
    d = drb.shape[1]
    bm, bn, bk = _tile(e, 1024), _tile(d, 1024), _tile(s, 4096)
    return _mm(name, yg, drb, dims=TN, grid=(e // bm, d // bn, s // bk),
               a_spec=pl.BlockSpec((bk, bm), lambda m, j, k: (k, m)),
               b_spec=pl.BlockSpec((bk, bn), lambda m, j, k: (k, j)),
               out_shape=jax.ShapeDtypeStruct((e, d), BF16),
               out_specs=pl.BlockSpec((bm, bn), lambda m, j, k: (m, j)),
               acc_shape=(bm, bn), dep=dep)


def _mm_dh(name, dproj, wg, layer, dr, dep=None):
    s, n = dproj.shape
    d, nsh = wg.shape[2], wg.shape[3]
    bm, bn = _tile(s, 1024), _tile(d, 1024)
    bk = max(t for t in range(LANES, 3073, LANES) if nsh % t == 0)
    per = nsh // bk

    def epilogue(acc, ex, outs):
        outs[0][...] = ALPHA * ex[0][...] + acc

    blk = pl.BlockSpec((bm, bn), lambda m, j, k: (m, j))
    return _mm(name, dproj, wg, dims=NT, grid=(s // bm, d // bn, n // bk),
               a_spec=pl.BlockSpec((bm, bk), lambda m, j, k: (m, k)),
               b_spec=pl.BlockSpec((None, None, bn, bk), lambda m, j, k: (k // per, layer, j, k % per)),
               extra=(dr,), extra_specs=(blk,),
               out_shape=jax.ShapeDtypeStruct((s, d), F32), out_specs=blk,
               acc_shape=(bm, bn), epilogue=epilogue, dep=dep)


def _mm_dw_in(name, hb, dproj):
    s, d = hb.shape
    n = dproj.shape[1]
    nsh = n // N_CHIPS
    bm, bn, bk = _tile(d, 1024), _tile(nsh, 1024), _tile(s, 4096)
    per = nsh // bn
    return _mm(name, hb, dproj, dims=TN, grid=(d // bm, n // bn, s // bk),
               a_spec=pl.BlockSpec((bk, bm), lambda m, j, k: (k, m)),
               b_spec=pl.BlockSpec((bk, bn), lambda m, j, k: (k, j)),
               out_shape=jax.ShapeDtypeStruct((N_CHIPS, d, nsh), BF16),
               out_specs=pl.BlockSpec((None, bm, bn), lambda m, j, k: (j // per, m, j % per)),
               acc_shape=(bm, bn))


def _causal_mask():
    r = lax.broadcasted_iota(jnp.int32, (CHUNK, CHUNK), 0)
    c = lax.broadcasted_iota(jnp.int32, (CHUNK, CHUNK), 1)
    return r >= c


def _sgu_fwd(name, proj, ln_g, ln_b, w_s, b_st):
    s, n3 = proj.shape
    e = n3 // 3
    hd = e // SGU_HEADS

    def body(proj_ref, g_ref, b_ref, ws_ref, bst_ref, yg_ref):
        xhat, _ = _ln_stats(_gelu(proj_ref[:, e:2 * e]))
        vnb = (xhat * g_ref[...] + b_ref[...]).astype(BF16)
        tri = _causal_mask()
        for h in range(SGU_HEADS):
            hs = slice(h * hd, (h + 1) * hd)
            ws = jnp.where(tri, ws_ref[h], 0.0).astype(BF16)
            mixed = jnp.dot(ws, vnb[:, hs], preferred_element_type=F32) + bst_ref[:, h:h + 1]
            u = proj_ref[:, hs]
            z = proj_ref[:, 2 * e + h * hd:2 * e + (h + 1) * hd]
            yg_ref[:, hs] = (_gelu(u) * mixed * (z * _sigmoid(z))).astype(BF16)

    vec = pl.BlockSpec((1, e), lambda i: (0, 0))
    return pl.pallas_call(
        body, name=name, grid=(s // CHUNK,),
        in_specs=[pl.BlockSpec((CHUNK, n3), lambda i: (i, 0)), vec, vec,
                  pl.BlockSpec((SGU_HEADS, CHUNK, CHUNK), lambda i: (0, 0, 0)),
                  pl.BlockSpec((CHUNK, SGU_HEADS), lambda i: (0, 0))],
        out_specs=pl.BlockSpec((CHUNK, e), lambda i: (i, 0)),
        out_shape=jax.ShapeDtypeStruct((s, e), BF16),
        compiler_params=_params(("parallel",)),
    )(proj, ln_g, ln_b, w_s, b_st)


def _sgu_bwd(name, dyg, proj, ln_g, ln_b, w_s, b_st):
    s, n3 = proj.shape
    e = n3 // 3
    hd = e // SGU_HEADS

    def body(dyg_ref, proj_ref, g_ref, b_ref, ws_ref, bst_ref,
             dproj_ref, dg_ref, db_ref, dws_ref, dbs_ref, dvn_ref):
        @pl.when(pl.program_id(0) == 0)
        def _():
            dg_ref[...] = jnp.zeros_like(dg_ref)
            db_ref[...] = jnp.zeros_like(db_ref)
            dws_ref[...] = jnp.zeros_like(dws_ref)
            dbs_ref[...] = jnp.zeros_like(dbs_ref)

        gv, gv_grad = _gelu_and_grad(proj_ref[:, e:2 * e])
        xhat, rstd = _ln_stats(gv)
        g = g_ref[...]
        vnb = (xhat * g + b_ref[...]).astype(BF16)
        tri = _causal_mask()
        for h in range(SGU_HEADS):
            hs = slice(h * hd, (h + 1) * hd)
            zs = slice(2 * e + h * hd, 2 * e + (h + 1) * hd)
            ws = jnp.where(tri, ws_ref[h], 0.0).astype(BF16)
            mixed = jnp.dot(ws, vnb[:, hs], preferred_element_type=F32) + bst_ref[:, h:h + 1]
            u = proj_ref[:, hs]
            z = proj_ref[:, zs]
            gu, gu_grad = _gelu_and_grad(u)
            sig = _sigmoid(z)
            dy = dyg_ref[:, hs]
            t = dy * (z * sig)
            dmixed = t * gu
            dproj_ref[:, hs] = (t * mixed * gu_grad).astype(BF16)
            dproj_ref[:, zs] = (dy * gu * mixed * (sig * (1.0 + z * (1.0 - sig)))).astype(BF16)
            dmb = dmixed.astype(BF16)
            dvn_ref[:, hs] = lax.dot_general(ws, dmb, TN, preferred_element_type=F32)
            dws = lax.dot_general(dmb, vnb[:, hs], NT, preferred_element_type=F32)
            dws_ref[h] += jnp.where(tri, dws, 0.0)
            dbs_ref[:, h:h + 1] += jnp.sum(dmixed, axis=-1, keepdims=True)
        dvn = dvn_ref[...]
        dg_ref[...] += _rowsum(dvn * xhat)
        db_ref[...] += _rowsum(dvn)
        dgv = _ln_bwd(dvn * g, xhat, rstd)
        dproj_ref[:, e:2 * e] = (dgv * gv_grad).astype(BF16)

    vec = pl.BlockSpec((1, e), lambda i: (0, 0))
    wsp = pl.BlockSpec((SGU_HEADS, CHUNK, CHUNK), lambda i: (0, 0, 0))
    bsp = pl.BlockSpec((CHUNK, SGU_HEADS), lambda i: (0, 0))
    return pl.pallas_call(
        body, name=name, grid=(s // CHUNK,),
        in_specs=[pl.BlockSpec((CHUNK, e), lambda i: (i, 0)), pl.BlockSpec((CHUNK, n3), lambda i: (i, 0)),
                  vec, vec, wsp, bsp],
        out_specs=[pl.BlockSpec((CHUNK, n3), lambda i: (i, 0)), vec, vec, wsp, bsp],
        out_shape=[jax.ShapeDtypeStruct((s, n3), BF16), jax.ShapeDtypeStruct((1, e), F32),
                   jax.ShapeDtypeStruct((1, e), F32),
                   jax.ShapeDtypeStruct((SGU_HEADS, CHUNK, CHUNK), F32),
                   jax.ShapeDtypeStruct((CHUNK, SGU_HEADS), F32)],
        scratch_shapes=[pltpu.VMEM((CHUNK, e), F32)],
        compiler_params=_params(("arbitrary",)),
    )(dyg, proj, ln_g, ln_b, w_s, b_st)


def _pool_counts(pos, w):
    return jnp.minimum(pos + 1, w).astype(F32)


def _pool_fwd(name, proj):
    s, n2 = proj.shape
    e = n2 // 2
    gd = e // len(POOL_WINDOWS)
    ts = _row_tile(s, e)
    hb = ts // POOL_HALO

    def body(v_ref, halo_ref, p_ref, ext):
        i = pl.program_id(0)
        ext[POOL_HALO:, :] = v_ref[...]
        ext[:POOL_HALO, :] = jnp.where(i > 0, halo_ref[...], 0.0)
        pos = i * ts + lax.broadcasted_iota(jnp.int32, (ts, 1), 0)
        for g, w in enumerate(POOL_WINDOWS):
            cs = slice(g * gd, (g + 1) * gd)
            acc = ext[POOL_HALO:, cs]
            for j in range(1, w):
                acc = acc + ext[POOL_HALO - j:POOL_HALO - j + ts, cs]
            p_ref[:, cs] = (acc / _pool_counts(pos, w) - v_ref[:, cs]).astype(BF16)

    return pl.pallas_call(
        body, name=name, grid=(s // ts,),
        in_specs=[pl.BlockSpec((ts, e), lambda i: (i, 0)),
                  pl.BlockSpec((POOL_HALO, e), lambda i: (jnp.maximum(i * hb - 1, 0), 0))],
        out_specs=pl.BlockSpec((ts, e), lambda i: (i, 0)),
        out_shape=jax.ShapeDtypeStruct((s, e), BF16),
        scratch_shapes=[pltpu.VMEM((ts + POOL_HALO, e), F32)],
        compiler_params=_params(("parallel",)),
    )(proj, proj)


def _pool_mm(name, p, wg, proj, scale):
    s, e = p.shape
    ng, rsh, gd = wg.shape[1], wg.shape[2], wg.shape[3]
    bm = _tile(s, 1024)

    def body(p_ref, w0, w1, w2, w3, z_ref, sc_ref, q_ref, yg_ref):
        q = None
        for j, w_ref in enumerate((w0, w1, w2, w3)):
            part = jnp.dot(p_ref[:, j * rsh:(j + 1) * rsh], w_ref[...], preferred_element_type=F32)
            q = part if q is None else q + part
        z = z_ref[...]
        q_ref[...] = q
        yg_ref[...] = (q * sc_ref[...] * (z * _sigmoid(z))).astype(BF16)

    blk = pl.BlockSpec((bm, gd), lambda m, g: (m, g))
    w_specs = [pl.BlockSpec((None, None, rsh, gd), functools.partial(lambda m, g, j: (j, g, 0, 0), j=j))
               for j in range(N_CHIPS)]
    return pl.pallas_call(
        body, name=name, grid=(s // bm, ng),
        in_specs=[blk] + w_specs + [pl.BlockSpec((bm, gd), lambda m, g: (m, ng + g)),
                                    pl.BlockSpec((1, gd), lambda m, g: (0, g))],
        out_specs=[blk, blk],
        out_shape=[jax.ShapeDtypeStruct((s, e), F32), jax.ShapeDtypeStruct((s, e), BF16)],
        compiler_params=_params(("parallel", "parallel")),
    )(p, wg, wg, wg, wg, proj, scale)


def _pool_bwd1(name, dyg, q, proj, scale):
    s, e = q.shape
    ts = _row_tile(s, e)

    def body(dyg_ref, q_ref, z_ref, sc_ref, dq_ref, dz_ref, dsc_ref):
        @pl.when(pl.program_id(0) == 0)
        def _():
            dsc_ref[...] = jnp.zeros_like(dsc_ref)

        z = z_ref[...]
        sig = _sigmoid(z)
        dy = dyg_ref[...]
        qv = q_ref[...]
        sc = sc_ref[...]
        dp2 = dy * (z * sig)
        dsc_ref[...] += _rowsum(dp2 * qv)
        dq_ref[...] = (dp2 * sc).astype(BF16)
        dz_ref[...] = (dy * (qv * sc) * (sig * (1.0 + z * (1.0 - sig)))).astype(BF16)

    blk = pl.BlockSpec((ts, e), lambda i: (i, 0))
    vec = pl.BlockSpec((1, e), lambda i: (0, 0))
    return pl.pallas_call(
        body, name=name, grid=(s // ts,),
        in_specs=[blk, blk, pl.BlockSpec((ts, e), lambda i: (i, 1)), vec],
        out_specs=[blk, pl.BlockSpec((ts, e), lambda i: (i, 1)), vec],
        out_shape=[jax.ShapeDtypeStruct((s, e), BF16), jax.ShapeDtypeStruct((s, 2 * e), BF16),
                   jax.ShapeDtypeStruct((1, e), F32)],
        compiler_params=_params(("arbitrary",)),
    )(dyg, q, proj, scale)


def _pool_mm_bwd(name, dq, wg):
    s, e = dq.shape
    ng, rsh, gd = wg.shape[1], wg.shape[2], wg.shape[3]
    bm = _tile(s, 1024)
    return _mm(name, dq, wg, dims=NT, grid=(s // bm, ng * N_CHIPS, 1),
               a_spec=pl.BlockSpec((bm, gd), lambda m, j, k: (m, j // N_CHIPS)),
               b_spec=pl.BlockSpec((None, None, rsh, gd), lambda m, j, k: (j % N_CHIPS, j // N_CHIPS, 0, 0)),
               out_shape=jax.ShapeDtypeStruct((s, e), F32),
               out_specs=pl.BlockSpec((bm, rsh), lambda m, j, k: (m, j)),
               acc_shape=(bm, rsh))


def _pool_bwd2(name, dp, dproj):
    s, e = dp.shape
    gd = e // len(POOL_WINDOWS)
    ts = _row_tile(s, e)
    n = s // ts
    hb = ts // POOL_HALO

    def body(dp_ref, nxt_ref, _, dv_ref, ext):
        i = pl.program_id(0)
        pos = i * ts + lax.broadcasted_iota(jnp.int32, (ts + POOL_HALO, 1), 0)
        for g, w in enumerate(POOL_WINDOWS):
            cs = slice(g * gd, (g + 1) * gd)
            cnt = _pool_counts(pos, w)
            ext[:ts, cs] = dp_ref[:, cs] / cnt[:ts]
            ext[ts:, cs] = jnp.where(i < n - 1, nxt_ref[:, cs] / cnt[ts:], 0.0)
        for g, w in enumerate(POOL_WINDOWS):
            cs = slice(g * gd, (g + 1) * gd)
            acc = ext[:ts, cs]
            for j in range(1, w):
                acc = acc + ext[j:j + ts, cs]
            dv_ref[:, cs] = (acc - dp_ref[:, cs]).astype(BF16)

    return pl.pallas_call(
        body, name=name, grid=(n,),
        in_specs=[pl.BlockSpec((ts, e), lambda i: (i, 0)),
                  pl.BlockSpec((POOL_HALO, e), lambda i: (jnp.minimum((i + 1) * hb, s // POOL_HALO - 1), 0)),
                  pl.BlockSpec(memory_space=pl.ANY)],
        out_specs=pl.BlockSpec((ts, e), lambda i: (i, 0)),
        out_shape=jax.ShapeDtypeStruct((s, 2 * e), BF16),
        scratch_shapes=[pltpu.VMEM((ts + POOL_HALO, e), F32)],
        input_output_aliases={2: 0},
        compiler_params=_params(("parallel",)),
    )(dp, dp, dproj)


def _mm_dw_pool(name, p, dq, ng):
    s, e = p.shape
    gd = e // ng
    rsh = gd // N_CHIPS
    bk = _tile(s, 4096)
    return _mm(name, p, dq, dims=TN, grid=(N_CHIPS, ng, s // bk),
               a_spec=pl.BlockSpec((bk, rsh), lambda m, g, k: (k, g * N_CHIPS + m)),
               b_spec=pl.BlockSpec((bk, gd), lambda m, g, k: (k, g)),
               out_shape=jax.ShapeDtypeStruct((N_CHIPS, ng, rsh, gd), BF16),
               out_specs=pl.BlockSpec((None, None, rsh, gd), lambda m, g, k: (m, g, 0, 0)),
               acc_shape=(rsh, gd))


def _conv_fwd(name, proj, conv_w, conv_b):
    s, n3 = proj.shape
    e = n3 // 3
    ts, tc = _tile(s, 512), LANES
    ne = e // tc
    hb = ts // CONV_HALO
    off = CONV_HALO - (CONV_WIDTH - 1)

    def body(a_ref, gl_ref, ha_ref, hgl_ref, w_ref, b_ref, c_ref, gext):
        i = pl.program_id(1)
        gext[CONV_HALO:, :] = a_ref[...] * _sigmoid(gl_ref[...])
        gext[:CONV_HALO, :] = jnp.where(i > 0, ha_ref[...] * _sigmoid(hgl_ref[...]), 0.0)
        taps = [jnp.broadcast_to(w_ref[k:k + 1, :], (SUBLANES, LANES)) for k in range(CONV_WIDTH)]
        bias = jnp.broadcast_to(b_ref[...], (SUBLANES, LANES))

        def sub_block(rb, carry):
            r0 = pl.multiple_of(rb * CONV_SUB, CONV_SUB)
            acc = [bias] * CONV_STRIDE
            for m in range(CONV_WIDTH + CONV_STRIDE - 1):
                g_m = gext[pl.ds(r0 + off + m, SUBLANES, stride=CONV_STRIDE), :]
                for q in range(CONV_STRIDE):
                    if 0 <= m - q < CONV_WIDTH:
                        acc[q] = acc[q] + taps[m - q] * g_m
            for q in range(CONV_STRIDE):
                c_ref[pl.ds(r0 + q, SUBLANES, stride=CONV_STRIDE), :] = acc[q]
            return carry

        lax.fori_loop(0, ts // CONV_SUB, sub_block, 0, unroll=2)

    halo = lambda i: jnp.maximum(i * hb - 1, 0)
    return pl.pallas_call(
        body, name=name, grid=(ne, s // ts),
        in_specs=[pl.BlockSpec((ts, tc), lambda j, i: (i, j)),
                  pl.BlockSpec((ts, tc), lambda j, i: (i, ne + j)),
                  pl.BlockSpec((CONV_HALO, tc), lambda j, i: (halo(i), j)),
                  pl.BlockSpec((CONV_HALO, tc), lambda j, i: (halo(i), ne + j)),
                  pl.BlockSpec((CONV_WIDTH, tc), lambda j, i: (0, j)),
                  pl.BlockSpec((1, tc), lambda j, i: (0, j))],
        out_specs=pl.BlockSpec((ts, tc), lambda j, i: (i, j)),
        out_shape=jax.ShapeDtypeStruct((s, e), F32),
        scratch_shapes=[pltpu.VMEM((ts + CONV_HALO, tc), F32)],
        compiler_params=_params(("parallel", "parallel")),
    )(proj, proj, proj, proj, conv_w, conv_b)


def _conv_gate_fwd(name, c, proj, ln_g, ln_b):
    s, e = c.shape
    ts = _row_tile(s, e)

    def body(c_ref, z_ref, g_ref, b_ref, yg_ref):
        xhat, _ = _ln_stats(c_ref[...])
        ln = xhat * g_ref[...] + b_ref[...]
        z = z_ref[...]
        yg_ref[...] = ((ln * _sigmoid(ln)) * (z * _sigmoid(z))).astype(BF16)

    blk = pl.BlockSpec((ts, e), lambda i: (i, 0))
    vec = pl.BlockSpec((1, e), lambda i: (0, 0))
    return pl.pallas_call(
        body, name=name, grid=(s // ts,),
        in_specs=[blk, pl.BlockSpec((ts, e), lambda i: (i, 2)), vec, vec],
        out_specs=blk, out_shape=jax.ShapeDtypeStruct((s, e), BF16),
        compiler_params=_params(("parallel",)),
    )(c, proj, ln_g, ln_b)


def _conv_bwd1(name, dyg, c, proj, ln_g, ln_b):
    s, e = c.shape
    ts = _row_tile(s, e)

    def body(dyg_ref, c_ref, z_ref, g_ref, b_ref, dc_ref, dz_ref, dg_ref, db_ref, dcb_ref):
        @pl.when(pl.program_id(0) == 0)
        def _():
            dg_ref[...] = jnp.zeros_like(dg_ref)
            db_ref[...] = jnp.zeros_like(db_ref)
            dcb_ref[...] = jnp.zeros_like(dcb_ref)

        xhat, rstd = _ln_stats(c_ref[...])
        g = g_ref[...]
        ln = xhat * g + b_ref[...]
        sl = _sigmoid(ln)
        z = z_ref[...]
        sz = _sigmoid(z)
        dy = dyg_ref[...]
        dz_ref[...] = (dy * (ln * sl) * (sz * (1.0 + z * (1.0 - sz)))).astype(BF16)
        dln = dy * (z * sz) * (sl * (1.0 + ln * (1.0 - sl)))
        dg_ref[...] += _rowsum(dln * xhat)
        db_ref[...] += _rowsum(dln)
        dc = _ln_bwd(dln * g, xhat, rstd)
        dc_ref[...] = dc
        dcb_ref[...] += _rowsum(dc)

    blk = pl.BlockSpec((ts, e), lambda i: (i, 0))
    zblk = pl.BlockSpec((ts, e), lambda i: (i, 2))
    vec = pl.BlockSpec((1, e), lambda i: (0, 0))
    vshape = jax.ShapeDtypeStruct((1, e), F32)
    return pl.pallas_call(
        body, name=name, grid=(s // ts,),
        in_specs=[blk, blk, zblk, vec, vec],
        out_specs=[blk, zblk, vec, vec, vec],
        out_shape=[jax.ShapeDtypeStruct((s, e), F32), jax.ShapeDtypeStruct((s, 3 * e), BF16),
                   vshape, vshape, vshape],
        compiler_params=_params(("arbitrary",)),
    )(dyg, c, proj, ln_g, ln_b)


def _conv_bwd2(name, dc, proj, conv_w):
    s, e = dc.shape
    ts, tc = _tile(s, 512), LANES
    ne = e // tc
    n = s // ts
    hb = ts // CONV_HALO
    off = CONV_HALO - (CONV_WIDTH - 1)
    span = CONV_WIDTH + CONV_STRIDE - 1

    def body(dc_ref, nxt_ref, a_ref, gl_ref, ha_ref, hgl_ref, w_ref, dg_ref, dw_ref, gext, dcext, dwacc, taps):
        i = pl.program_id(1)

        @pl.when(i == 0)
        def _():
            dwacc[...] = jnp.zeros_like(dwacc)

        gext[CONV_HALO:, :] = a_ref[...] * _sigmoid(gl_ref[...])
        gext[:CONV_HALO, :] = jnp.where(i > 0, ha_ref[...] * _sigmoid(hgl_ref[...]), 0.0)
        dcext[:ts, :] = dc_ref[...]
        dcext[ts:, :] = jnp.where(i < n - 1, nxt_ref[...], 0.0)
        for k in range(CONV_WIDTH):
            taps[k] = jnp.broadcast_to(w_ref[k:k + 1, :], (SUBLANES, LANES))

        def sub_block(rb, carry):
            r0 = pl.multiple_of(rb * CONV_SUB, CONV_SUB)
            dg = [jnp.zeros((SUBLANES, LANES), F32)] * CONV_STRIDE
            for m in range(span):
                dc_m = dcext[pl.ds(r0 + m, SUBLANES, stride=CONV_STRIDE), :]
                for q in range(CONV_STRIDE):
                    k = CONV_WIDTH - 1 - m + q
                    if 0 <= k < CONV_WIDTH:
                        dg[q] = dg[q] + taps[k] * dc_m
            for q in range(CONV_STRIDE):
                dg_ref[pl.ds(r0 + q, SUBLANES, stride=CONV_STRIDE), :] = dg[q]
            dc_q = [dcext[pl.ds(r0 + q, SUBLANES, stride=CONV_STRIDE), :] for q in range(CONV_STRIDE)]
            win = [gext[pl.ds(r0 + off + m, SUBLANES, stride=CONV_STRIDE), :] for m in range(CONV_STRIDE - 1)]
            for k in range(CONV_WIDTH):
                win.append(gext[pl.ds(r0 + off + k + CONV_STRIDE - 1, SUBLANES, stride=CONV_STRIDE), :])
                prods = [dc_q[q] * win[q] for q in range(CONV_STRIDE)]
                while len(prods) > 1:
                    prods = [a + b for a, b in zip(prods[::2], prods[1::2])]
                dwacc[k] += prods[0]
                win.pop(0)
            return carry

        lax.fori_loop(0, ts // CONV_SUB, sub_block, 0)

        @pl.when(i == n - 1)
        def _():
            dw_ref[...] = jnp.sum(dwacc[...], axis=1)

    halo = lambda i: jnp.maximum(i * hb - 1, 0)
    nxt = lambda i: jnp.minimum((i + 1) * hb, s // CONV_HALO - 1)
    return pl.pallas_call(
        body, name=name, grid=(ne, n),
        in_specs=[pl.BlockSpec((ts, tc), lambda j, i: (i, j)),
                  pl.BlockSpec((CONV_HALO, tc), lambda j, i: (nxt(i), j)),
                  pl.BlockSpec((ts, tc), lambda j, i: (i, j)),
                  pl.BlockSpec((ts, tc), lambda j, i: (i, ne + j)),
                  pl.BlockSpec((CONV_HALO, tc), lambda j, i: (halo(i), j)),
                  pl.BlockSpec((CONV_HALO, tc), lambda j, i: (halo(i), ne + j)),
                  pl.BlockSpec((CONV_WIDTH, tc), lambda j, i: (0, j))],
        out_specs=[pl.BlockSpec((ts, tc), lambda j, i: (i, j)),
                   pl.BlockSpec((CONV_HALO, tc), lambda j, i: (0, j))],
        out_shape=[jax.ShapeDtypeStruct((s, e), F32), jax.ShapeDtypeStruct((CONV_HALO, e), F32)],
        scratch_shapes=[pltpu.VMEM((ts + CONV_HALO, tc), F32), pltpu.VMEM((ts + CONV_HALO, tc), F32),
                        pltpu.VMEM((CONV_HALO, SUBLANES, LANES), F32), pltpu.VMEM((CONV_HALO, SUBLANES, LANES), F32)],
        compiler_params=_params(("parallel", "arbitrary")),
    )(dc, dc, proj, proj, proj, proj, conv_w)


def _conv_bwd3(name, dg, proj, dproj):
    s, e = dg.shape
    ts = _row_tile(s, 2 * e)

    def body(dg_ref, agl_ref, _, out_ref):
        a = agl_ref[:, :e]
        sig = _sigmoid(agl_ref[:, e:])
        dgv = dg_ref[...]
        out_ref[:, :e] = (dgv * sig).astype(BF16)
        out_ref[:, e:] = (dgv * a * sig * (1.0 - sig)).astype(BF16)

    wide = pl.BlockSpec((ts, 2 * e), lambda i: (i, 0))
    return pl.pallas_call(
        body, name=name, grid=(s // ts,),
        in_specs=[pl.BlockSpec((ts, e), lambda i: (i, 0)), wide, pl.BlockSpec(memory_space=pl.ANY)],
        out_specs=wide, out_shape=jax.ShapeDtypeStruct((s, 3 * e), BF16),
        input_output_aliases={2: 0},
        compiler_params=_params(("parallel",)),
    )(dg, proj, dproj)


def _post_ln_bwd(name, dy_or_target, xhat, rstd, g, b, loss_head, dep=None):
    s, d = xhat.shape
    ts = _row_tile(s, d)

    def body(dy_ref, xhat_ref, rstd_ref, g_ref, b_ref, *rest):
        dr_ref, drb_ref, dg_ref, db_ref, loss_ref = rest[-5:]

        @pl.when(pl.program_id(0) == 0)
        def _():
            dg_ref[...] = jnp.zeros_like(dg_ref)
            db_ref[...] = jnp.zeros_like(db_ref)
            loss_ref[...] = jnp.zeros_like(loss_ref)

        xh = xhat_ref[...]
        gv = g_ref[...]
        if loss_head:
            err = (xh * gv + b_ref[...]) - dy_ref[...]
            per_row = jnp.mean(err * err, axis=-1, keepdims=True)
            loss_ref[...] += 0.5 * jnp.broadcast_to(_rowsum(per_row), loss_ref.shape)
            dy = err / d
        else:
            dy = dy_ref[...]
        dg_ref[...] += _rowsum(dy * xh)
        db_ref[...] += _rowsum(dy)
        dr = _ln_bwd(dy * gv, xh, rstd_ref[:, 0:1])
        dr_ref[...] = dr
        drb_ref[...] = dr.astype(BF16)

    blk = pl.BlockSpec((ts, d), lambda i: (i, 0))
    vec = pl.BlockSpec((1, d), lambda i: (0, 0))
    deps = [] if dep is None else [dep]
    return pl.pallas_call(
        body, name=name, grid=(s // ts,),
        in_specs=[blk, blk, pl.BlockSpec((ts, LANES), lambda i: (i, 0)), vec, vec] + [ANY] * len(deps),
        out_specs=[blk, blk, vec, vec, pl.BlockSpec((1, LANES), lambda i: (0, 0))],
        out_shape=[jax.ShapeDtypeStruct((s, d), F32), jax.ShapeDtypeStruct((s, d), BF16),
                   jax.ShapeDtypeStruct((1, d), F32), jax.ShapeDtypeStruct((1, d), F32),
                   jax.ShapeDtypeStruct((1, LANES), F32)],
        compiler_params=_params(("arbitrary",)),
    )(dy_or_target, xhat, rstd, g, b, *deps)


def _as2d(a):
    return a.reshape(-1, a.shape[-1])


def _cast_layer(name, a, layer, slabs=0):
    _, r, c = a.shape
    tr = _row_tile(r, c)

    def body(a_ref, o_ref):
        o_ref[...] = a_ref[...].astype(BF16)

    if slabs:
        out_spec = pl.BlockSpec((None, tr, c), lambda i: (_my_shard(), i, 0))
        out_shape = jax.ShapeDtypeStruct((slabs, r, c), BF16)
    else:
        out_spec = pl.BlockSpec((tr, c), lambda i: (i, 0))
        out_shape = jax.ShapeDtypeStruct((r, c), BF16)
    return pl.pallas_call(body, name=name, grid=(r // tr,),
                          in_specs=[pl.BlockSpec((None, tr, c), lambda i: (layer, i, 0))],
                          out_specs=out_spec, out_shape=out_shape,
                          compiler_params=_params(("parallel",)))(a)


def _my_shard():
    return 2 * lax.axis_index("x") + lax.axis_index("y")


def _pair_sum(name, own, recv):
    nsh, r, c = own.shape
    h = r // 2
    tr = _row_tile(h, c)
    nb = h // tr

    def body(own_ref, recv_ref, o_ref):
        o_ref[...] = (own_ref[...].astype(F32) + recv_ref[...].astype(F32)).astype(BF16)

    blk = pl.BlockSpec((None, tr, c), lambda j, i: (j, i, 0))
    mine = pl.BlockSpec((None, tr, c), lambda j, i: (j, lax.axis_index("c") * nb + i, 0))
    return pl.pallas_call(body, name=name, grid=(nsh, nb), in_specs=[mine, blk], out_specs=blk,
                          out_shape=jax.ShapeDtypeStruct((nsh, h, c), BF16),
                          compiler_params=_params(("parallel", "parallel")))(own, recv)


def _final_sum(name, own, recv, got, layer, n_layers, prev):
    _, r, c = own.shape
    h = r // 2
    tr = _row_tile(h, c)
    nb = h // tr

    def body(own_ref, recv_ref, q0_ref, q1_ref, q2_ref, *rest):
        o_ref = rest[-1]
        acc = own_ref[...].astype(F32) + recv_ref[...].astype(F32)
        acc = acc + q0_ref[...].astype(F32)
        acc = acc + q1_ref[...].astype(F32)
        o_ref[...] = acc + q2_ref[...].astype(F32)

    in_specs = [pl.BlockSpec((None, tr, c), lambda i: (_my_shard(), lax.axis_index("c") * nb + i, 0)),
                pl.BlockSpec((None, tr, c), lambda i: (_my_shard(), i, 0))]
    in_specs += [pl.BlockSpec((None, tr, c), functools.partial(lambda i, k: (k, i, 0), k=k)) for k in range(3)]
    operands = [own, recv, got, got, got]
    aliases = {}
    if prev is not None:
        in_specs.append(pl.BlockSpec(memory_space=pl.ANY))
        operands.append(prev)
        aliases = {5: 0}
    return pl.pallas_call(body, name=name, grid=(nb,), in_specs=in_specs,
                          out_specs=pl.BlockSpec((None, tr, c), lambda i: (layer, lax.axis_index("c") * nb + i, 0)),
                          out_shape=jax.ShapeDtypeStruct((n_layers, r, c), F32),
                          input_output_aliases=aliases,
                          compiler_params=_params(("parallel",)))(*operands)


def _adam(name, w, g, m, v):
    r, c = w.shape
    tr = _row_tile(r, c)
    c1 = 1.0 / (1.0 - ADAM_B1 ** ADAM_STEP)
    c2 = 1.0 / (1.0 - ADAM_B2 ** ADAM_STEP)

    def body(w_ref, g_ref, m_ref, v_ref, d_ref, nm_ref, nv_ref):
        gv = g_ref[...]
        nm = ADAM_B1 * m_ref[...] + (1.0 - ADAM_B1) * gv
        nv = ADAM_B2 * v_ref[...] + (1.0 - ADAM_B2) * (gv * gv)
        d_ref[...] = -ADAM_LR * ((nm * c1) / (jnp.sqrt(nv * c2) + ADAM_EPS) + ADAM_WD * w_ref[...])
        nm_ref[...] = nm
        nv_ref[...] = nv

    blk = pl.BlockSpec((tr, c), lambda i: (i, 0))
    shp = jax.ShapeDtypeStruct((r, c), F32)
    return pl.pallas_call(body, name=name, grid=(r // tr,), in_specs=[blk] * 4, out_specs=[blk] * 3,
                          out_shape=[shp] * 3, compiler_params=_params(("parallel",)))(w, g, m, v)


def _sum_devices(name, parts):
    _, r, c = parts.shape
    tr = _row_tile(r, N_DEV * c)

    def body(p_ref, o_ref):
        acc = p_ref[0]
        for k in range(1, N_DEV):
            acc = acc + p_ref[k]
        o_ref[...] = acc

    return pl.pallas_call(body, name=name, grid=(r // tr,),
                          in_specs=[pl.BlockSpec((N_DEV, tr, c), lambda i: (0, i, 0))],
                          out_specs=pl.BlockSpec((tr, c), lambda i: (i, 0)),
                          out_shape=jax.ShapeDtypeStruct((r, c), F32),
                          compiler_params=_params(("parallel",)))(parts)


def _coords():
    return lax.axis_index("x"), lax.axis_index("y"), lax.axis_index("c")


def _chip_peers(x, y):
    return [(1 - x, y, 2 * (1 - x) + y), (x, 1 - y, 2 * x + 1 - y), (1 - x, 1 - y, 2 * (1 - x) + 1 - y)]


def _remote(src, dst, ssem, rsem, dev):
    return pltpu.make_async_remote_copy(src_ref=src, dst_ref=dst, send_sem=ssem, recv_sem=rsem,
                                        device_id=dev, device_id_type=MESH)


ANY = pl.BlockSpec(memory_space=pl.ANY)


SEM = pl.BlockSpec(memory_space=pltpu.SEMAPHORE)
HBM = pl.BlockSpec(memory_space=pltpu.HBM)
DATAFLOW = pltpu.SideEffectType.DATAFLOW_SIDE_EFFECTING


def _hbm(a):
    return pltpu.with_memory_space_constraint(a, pltpu.HBM)


def _split_start(name, make_copies, n_copies, srcs, land_shapes, dep):
    n, nl = len(srcs), len(land_shapes)
    nb = n + nl

    def body(*refs):
        src_refs, land_refs = refs[:n], refs[n:nb]
        ssem, rsem = refs[nb + 1], refs[nb + 2]
        token = refs[-1]
        for send, _ in make_copies(src_refs, land_refs, ssem, rsem):
            send.start()
        token[...] = jnp.zeros_like(token)

    lands = [_hbm(lax.empty(shp, dt)) for shp, dt in land_shapes]
    out_shape = [pltpu.SemaphoreType.DMA((n_copies,)), pltpu.SemaphoreType.DMA((n_copies,))]
    out_shape += [pltpu.HBM(a.shape, a.dtype) for a in list(srcs) + lands]
    out_shape.append(jax.ShapeDtypeStruct((ROW_ALIGN, LANES), F32))
    res = pl.pallas_call(
        body, name=name, out_shape=out_shape,
        in_specs=[HBM] * nb + [ANY],
        out_specs=[SEM, SEM] + [HBM] * nb + [pl.BlockSpec(memory_space=pltpu.VMEM)],
        input_output_aliases={i: 2 + i for i in range(nb)},
        compiler_params=pltpu.CompilerParams(has_side_effects=DATAFLOW),
    )(*[_hbm(a) for a in srcs], *lands, dep)
    return res[0], res[1], res[2:2 + n], res[2 + n:2 + nb], res[-1]


def _split_wait(name, make_copies, ssem, rsem, srcs, lands, after):
    n, nb = len(srcs), len(srcs) + len(lands)

    def body(*refs):
        src_refs, land_refs = refs[:n], refs[n:nb]
        for send, recv in make_copies(src_refs, land_refs, refs[nb], refs[nb + 1]):
            send.wait_send()
            recv.wait_recv()

    res = pl.pallas_call(
        body, name=name, out_shape=[pltpu.HBM(a.shape, a.dtype) for a in list(srcs) + list(lands)],
        in_specs=[HBM] * nb + [SEM, SEM, ANY], out_specs=[HBM] * nb,
        input_output_aliases={i: i for i in range(nb)},
        compiler_params=pltpu.CompilerParams(has_side_effects=DATAFLOW),
    )(*srcs, *lands, ssem, rsem, after)
    return res[:n], res[n:]


def _gather_copies(buf_refs, _, ssem, rsem):
    x, y, c = _coords()
    j = 2 * x + y
    out = []
    for i, buf in enumerate(buf_refs):
        h = buf.shape[1] // 2
        rows = pl.ds(c * h, h)
        for k, (px, py, pj) in enumerate(_chip_peers(x, y)):
            sems = (ssem.at[3 * i + k], rsem.at[3 * i + k], (px, py, c))
            out.append((_remote(buf.at[j, rows], buf.at[j, rows], *sems),
                        _remote(buf.at[pj, rows], buf.at[pj, rows], *sems)))
    return out


def _scatter_copies(src_refs, land_refs, ssem, rsem):
    x, y, c = _coords()
    out = []
    for i, (src, land) in enumerate(zip(src_refs, land_refs)):
        for k, (px, py, pj) in enumerate(_chip_peers(x, y)):
            cp = _remote(src.at[pj], land.at[k], ssem.at[3 * i + k], rsem.at[3 * i + k], (px, py, c))
            out.append((cp, cp))
    return out


def _sibling_pass(name, lands, dep):
    n = len(lands)

    def body(*refs):
        bufs = refs[n + 1:2 * n + 1]
        ssem, rsem = refs[2 * n + 1:]
        x, y, c = _coords()
        sends = []
        for i in range(n):
            h = bufs[i].shape[1] // 2
            for k, (px, py, pj) in enumerate(_chip_peers(x, y)):
                mine = bufs[i].at[pj, pl.ds(c * h, h)]
                sends.append(_remote(mine, mine, ssem.at[3 * i + k], rsem.at[3 * i + k], (x, y, 1 - c)))
        for cp in sends:
            cp.start()
        for i in range(n):
            h = bufs[i].shape[1] // 2
            for k, (px, py, pj) in enumerate(_chip_peers(x, y)):
                theirs = bufs[i].at[pj, pl.ds((1 - c) * h, h)]
                _remote(theirs, theirs, ssem.at[3 * i + k], rsem.at[3 * i + k], (x, y, 1 - c)).wait_recv()
        for cp in sends:
            cp.wait_send()

    return pl.pallas_call(
        body, name=name, in_specs=[ANY] * (n + 1), out_specs=[ANY] * n,
        out_shape=[jax.ShapeDtypeStruct(a.shape, a.dtype) for a in lands],
        input_output_aliases={i: i for i in range(n)},
        scratch_shapes=[pltpu.SemaphoreType.DMA((3 * n,)), pltpu.SemaphoreType.DMA((3 * n,))],
    )(*lands, dep)


def _gather_small(small):
    def body(in_ref, out_ref, ssem, rsem, lsem):
        x, y, c = _coords()
        j = 2 * x + y
        local = pltpu.make_async_copy(in_ref, out_ref.at[j], lsem)
        local.start()
        peers = _chip_peers(x, y)
        sends = [_remote(in_ref, out_ref.at[j], ssem.at[k], rsem.at[k], (px, py, c))
                 for k, (px, py, pj) in enumerate(peers)]
        for cp in sends:
            cp.start()
        for k, (px, py, pj) in enumerate(peers):
            sends[k].wait_send()
            _remote(in_ref, out_ref.at[pj], ssem.at[k], rsem.at[k], (px, py, c)).wait_recv()
        local.wait()

    return pl.pallas_call(body, name="gather_small", in_specs=[ANY], out_specs=ANY,
                          out_shape=jax.ShapeDtypeStruct((N_CHIPS,) + small.shape, small.dtype),
                          scratch_shapes=[pltpu.SemaphoreType.DMA((3,)), pltpu.SemaphoreType.DMA((3,)),
                                          pltpu.SemaphoreType.DMA])(small)


def _pair_copies(src_refs, land_refs, ssem, rsem):
    x, y, c = _coords()
    out = []
    for i, (src, land) in enumerate(zip(src_refs, land_refs)):
        h = src.shape[1] // 2
        cp = _remote(src.at[:, pl.ds((1 - c) * h, h)], land, ssem.at[i], rsem.at[i], (x, y, 1 - c))
        out.append((cp, cp))
    return out


def _swap_copies(layers):
    def make(buf_refs, _, ssem, rsem):
        x, y, c = _coords()
        out = []
        for i, buf in enumerate(buf_refs):
            h = buf.shape[1] // 2
            mine = buf.at[layers[i], pl.ds(c * h, h)]
            theirs = buf.at[layers[i], pl.ds((1 - c) * h, h)]
            sems = (ssem.at[i], rsem.at[i], (x, y, 1 - c))
            out.append((_remote(mine, mine, *sems), _remote(theirs, theirs, *sems)))
        return out

    return make


def _allgather_small(part):
    def body(in_ref, out_ref, ssem, rsem, lsem):
        x, y, c = _coords()
        sibling = (x, y, 1 - c)
        chips = _chip_peers(x, y)

        def slot(px, py, pc):
            return out_ref.at[4 * px + 2 * py + pc]

        local = pltpu.make_async_copy(in_ref, slot(x, y, c), lsem)
        local.start()
        first = [_remote(in_ref, slot(x, y, c), ssem.at[0], rsem.at[0], sibling)]
        first += [_remote(in_ref, slot(x, y, c), ssem.at[1 + k], rsem.at[1 + k], (px, py, c))
                  for k, (px, py, _) in enumerate(chips)]
        for cp in first:
            cp.start()
        passed = []
        for k, (px, py, _) in enumerate(chips):
            landed = slot(px, py, c)
            _remote(landed, landed, ssem.at[1 + k], rsem.at[1 + k], (px, py, c)).wait_recv()
            fwd = _remote(landed, landed, ssem.at[4 + k], rsem.at[4 + k], sibling)
            fwd.start()
            passed.append(fwd)
        theirs = slot(x, y, 1 - c)
        _remote(theirs, theirs, ssem.at[0], rsem.at[0], sibling).wait_recv()
        for k, (px, py, _) in enumerate(chips):
            landed = slot(px, py, 1 - c)
            _remote(landed, landed, ssem.at[4 + k], rsem.at[4 + k], sibling).wait_recv()
        for cp in first + passed:
            cp.wait_send()
        local.wait()

    return pl.pallas_call(body, name="allgather_small", in_specs=[ANY], out_specs=ANY,
                          out_shape=jax.ShapeDtypeStruct((N_DEV,) + part.shape, part.dtype),
                          scratch_shapes=[pltpu.SemaphoreType.DMA((N_DEV - 1,)), pltpu.SemaphoreType.DMA((N_DEV - 1,)),
                                          pltpu.SemaphoreType.DMA])(part)


ROW_ALIGN = SUBLANES
LAYER_WEIGHTS = (("a_w_in", "a_w_out"), ("b_w_in", "b_w_pool", "b_w_out"), ("c_w_in", "c_w_out"))
SMALL_SHARDED = ("a_ln_g", "a_ln_b", "c_conv_w", "c_conv_b", "c_ln_g", "c_ln_b")
SMALL_ORDER = ("a_ln_g", "a_ln_b", "a_w_s", "a_b_s", "b_scale", "c_conv_w", "c_conv_b", "c_ln_g", "c_ln_b",
               "post_ln_g", "post_ln_b")
WEIGHTS = ("a_w_in", "a_ln_g", "a_ln_b", "a_w_s", "a_b_s", "a_w_out", "b_w_in", "b_w_pool", "b_scale", "b_w_out",
           "c_w_in", "c_conv_w", "c_conv_b", "c_ln_g", "c_ln_b", "c_w_out", "post_ln_g", "post_ln_b")


def _pad_rows(a):
    pad = -a.shape[0] % ROW_ALIGN
    return jnp.pad(a, ((0, pad), (0, 0))) if pad else a


def _pack(parts, width):
    return jnp.concatenate([_pad_rows(p.reshape(-1, width)) for p in parts], axis=0)


def _unpack(packed, shapes, width):
    out, row = [], 0
    for shp in shapes:
        size = 1
        for n in shp:
            size *= n
        rows = size // width
        out.append(packed[row:row + rows].reshape(shp))
        row += rows + (-rows % ROW_ALIGN)
    return out


def _step(x, tgt, w, m, v):
    s, d = x.shape
    cx, cy, cc = _coords()
    j_me = 2 * cx + cy
    es = w["a_ln_g"].shape[1]
    e = N_CHIPS * es

    n_a, n_c = w["a_w_in"].shape[0], w["c_w_in"].shape[0]
    ng = w["b_w_pool"].shape[1]
    as3d = lambda a: a.reshape(a.shape[0], -1, a.shape[-1])

    zones = [[_cast_layer("cast_%s%d" % (k, i // 3), as3d(w[k]), i // 3, N_CHIPS) for k in LAYER_WEIGHTS[i % 3]]
             for i in range(DEPTH)]

    def gather_start(i, dep):
        return _split_start("gather_start%d" % i, _gather_copies, 3 * len(zones[i]), zones[i], [], dep)

    small_all = _gather_small(_pack([w[k] for k in SMALL_SHARDED], es))
    ssem, rsem, lands, _, tok = gather_start(0, small_all)
    xb = _cast_layer("cast_x", x[None], 0)
    small_full = jnp.swapaxes(small_all, 0, 1).reshape(small_all.shape[1], e)
    full_shapes = [(n_a, e), (n_a, e), (CONV_WIDTH, e), (1, e), (1, e), (1, e)]
    a_ln_g, a_ln_b, conv_w, conv_b, c_ln_g, c_ln_b = _unpack(small_full, full_shapes, e)

    post_g, post_b = w["post_ln_g"], w["post_ln_b"]
    b_scale = w["b_scale"]

    saved = []
    res, gp, bp, hb = x, jnp.ones((1, d), F32), jnp.zeros((1, d), F32), xb
    for i in range(DEPTH):
        kind, l = i % 3, i // 3
        tag = "L%d" % i
        after = zones[DEPTH - 1][-1] if i == 0 else res
        lands, _ = _split_wait("gather_wait%d" % i, _gather_copies, ssem, rsem, lands, [], after)
        full = _sibling_pass("sibling_pass%d" % i, lands, tok)
        if i + 1 < DEPTH:
            ssem, rsem, lands, _, tok = gather_start(i + 1, full[0])
        wl = dict(zip(LAYER_WEIGHTS[kind], full))
        w_in = wl[LAYER_WEIGHTS[kind][0]].reshape(N_CHIPS, 1, d, -1)
        w_out = wl[LAYER_WEIGHTS[kind][-1]].reshape(N_CHIPS, 1, es, d)
        proj = _mm_in(tag + "_in", hb, w_in, 0, dep=tok)
        if kind == 0:
            b_st = w["a_b_s"][l].T
            yg = _sgu_fwd(tag + "_sgu", proj, a_ln_g[l:l + 1], a_ln_b[l:l + 1], w["a_w_s"][l], b_st)
            extra = (b_st,)
        elif kind == 1:
            w_pool = wl["b_w_pool"].reshape(N_CHIPS, ng, -1, w["b_w_pool"].shape[3])
            p = _pool_fwd(tag + "_pool", proj)
            q, yg = _pool_mm(tag + "_poolmm", p, w_pool, proj, b_scale[l:l + 1])
            extra = (p, q, w_pool)
        else:
            cv = _conv_fwd(tag + "_conv", proj, conv_w, conv_b)
            yg = _conv_gate_fwd(tag + "_gate", cv, proj, c_ln_g, c_ln_b)
            extra = (cv,)
        xhat, hb_next, rstd = _mm_out_ln(tag + "_out", yg, w_out, 0, res, gp, bp, post_g[i:i + 1], post_b[i:i + 1])
        saved.append((hb, proj, yg, xhat, rstd, extra, w_in, w_out))
        res, gp, bp, hb = xhat, post_g[i:i + 1], post_b[i:i + 1], hb_next

    sg = {k: [None] * w[k].shape[0] for k in ("a_ln_g", "a_ln_b", "a_w_s", "a_b_s", "post_ln_g", "post_ln_b")}
    scat = {}

    def finish_pair(p, after):
        i, pssem, prsem, own, recv, dep = p
        own, recv = _split_wait("pair_wait%d" % i, _pair_copies, pssem, prsem, own, recv, after)
        pair = [_pair_sum("pairsum%d_%d" % (i, n), a, r) for n, (a, r) in enumerate(zip(own, recv))]
        ssem, rsem, srcs, lands, token = _split_start("scatter_start%d" % i, _scatter_copies, 3 * len(pair), pair,
                                                      [((3,) + a.shape[1:], a.dtype) for a in pair], dep)
        scat[i] = (own, recv, ssem, rsem, srcs, lands)
        return token

    pending = None
    dr, drb, dg, db, loss_vec = _post_ln_bwd("L3_lnbwd", tgt, saved[-1][3], saved[-1][4],
                                             post_g[DEPTH - 1:DEPTH], post_b[DEPTH - 1:DEPTH], True)
    grad_x = None
    for i in reversed(range(DEPTH)):
        kind, l = i % 3, i // 3
        tag = "L%d" % i
        hb_in, proj, yg, _, _, extra, w_in, w_out = saved[i]
        sg["post_ln_g"][i], sg["post_ln_b"][i] = dg, db
        dyg = _mm_dyg(tag + "_dyg", drb, w_out, 0)
        dw_mid = []
        if kind == 0:
            dproj, dlg, dlb, dws_s, dbs_t = _sgu_bwd(tag + "_sgubwd", dyg, proj, a_ln_g[l:l + 1], a_ln_b[l:l + 1],
                                                     w["a_w_s"][l], extra[0])
            sg["a_ln_g"][l], sg["a_ln_b"][l], sg["a_w_s"][l], sg["a_b_s"][l] = dlg, dlb, dws_s, dbs_t.T
        elif kind == 1:
            p, q, w_pool = extra
            dq, dproj, dscale = _pool_bwd1(tag + "_poolbwd1", dyg, q, proj, b_scale[l:l + 1])
            dp = _pool_mm_bwd(tag + "_poolmmbwd", dq, w_pool)
            dproj = _pool_bwd2(tag + "_poolbwd2", dp, dproj)
            dw_mid = [_mm_dw_pool(tag + "_dwpool", p, dq, ng).reshape(N_CHIPS, -1, dq.shape[1] // ng)]
            sg["b_scale"] = [dscale]
        else:
            cv = extra[0]
            dc, dproj, dlg, dlb, dcb = _conv_bwd1(tag + "_convbwd1", dyg, cv, proj, c_ln_g, c_ln_b)
            dgc, dcw = _conv_bwd2(tag + "_convbwd2", dc, proj, conv_w)
            dproj = _conv_bwd3(tag + "_convbwd3", dgc, proj, dproj)
            sg["c_ln_g"], sg["c_ln_b"], sg["c_conv_b"], sg["c_conv_w"] = [dlg], [dlb], [dcb], [dcw[:CONV_WIDTH]]
        started = None if pending is None else finish_pair(pending, dproj)
        dw_out = _mm_dw_out(tag + "_dwout", yg, drb, dep=started).reshape(N_CHIPS, es, d)
        if i > 0:
            dh = _mm_dh(tag + "_dh", dproj, w_in, 0, dr)
        dw_in = _mm_dw_in(tag + "_dwin", hb_in, dproj)
        own = [dw_in] + dw_mid + [dw_out]
        pending = (i,) + _split_start("pair_start%d" % i, _pair_copies, len(own), own,
                                      [((N_CHIPS, a.shape[1] // 2, a.shape[2]), a.dtype) for a in own], tok)
        tok = pending[-1]
        if i > 0:
            prev = saved[i - 1]
            dr, drb, dg, db, _ = _post_ln_bwd("L%d_lnbwd" % (i - 1), dh, prev[3], prev[4],
                                              post_g[i - 1:i], post_b[i - 1:i], False, dep=tok)
        else:
            grad_x = _mm_dh(tag + "_dh", dproj, w_in, 0, dr, dep=tok)
            started = finish_pair(pending, grad_x)

    grads, delta, new_m, new_v = {}, {}, {}, {}

    def adam(k):
        shp = w[k].shape
        two_d = (lambda a: a.reshape(-1, shp[-1])) if shp[-1] % LANES == 0 else (lambda a: a.reshape(1, -1))
        dl, nm, nv = _adam("adam_" + k, two_d(w[k]), two_d(grads[k]), two_d(m[k]), two_d(v[k]))
        delta[k], new_m[k], new_v[k] = dl.reshape(shp), nm.reshape(shp), nv.reshape(shp)
        return dl

    reduced, swaps = {}, {}

    def reduce_layer(i, after):
        kind, l = i % 3, i // 3
        names = LAYER_WEIGHTS[kind]
        own, recv, ssem, rsem, srcs, lands = scat[i]
        _, got = _split_wait("scatter_wait%d" % i, _scatter_copies, ssem, rsem, srcs, lands, after)
        for n, k in enumerate(names):
            reduced[k] = _final_sum("finalsum%d_%d" % (i, n), own[n], recv[n], got[n], l, w[k].shape[0],
                                    reduced.get(k))
        make = _swap_copies([l] * len(names))
        ssem, rsem, bufs, _, token = _split_start("swap_start%d" % i, make, len(names), [reduced[k] for k in names],
                                                  [], got[0])
        swaps[i] = (make, ssem, rsem, bufs)
        return token

    def finish_swap(i, after, with_adam):
        names = LAYER_WEIGHTS[i % 3]
        make, ssem, rsem, bufs = swaps[i]
        bufs, _ = _split_wait("swap_wait%d" % i, make, ssem, rsem, bufs, [], after)
        out = after
        for k, g in zip(names, bufs):
            reduced[k] = g
            if with_adam:
                grads[k] = g.reshape(w[k].shape)
                out = adam(k)
        return out

    after = started
    for i in range(DEPTH - 1, 0, -1):
        after = reduce_layer(i, after)
    for i in range(DEPTH - 2, 0, -1):
        after = finish_swap(i, after, True)
    finish_swap(DEPTH - 1, after, False)

    small_parts = []
    for k in SMALL_ORDER:
        small_parts.append(jnp.concatenate([a.reshape(1, -1) for a in sg[k]], axis=0))
    width = 8 * LANES
    packed = _pack(small_parts, width)
    summed = _sum_devices("sum_small", _allgather_small(packed))
    full_small_shapes = {"a_ln_g": (n_a, e), "a_ln_b": (n_a, e), "a_w_s": w["a_w_s"].shape, "a_b_s": w["a_b_s"].shape,
                         "b_scale": w["b_scale"].shape, "c_conv_w": (n_c, CONV_WIDTH, e), "c_conv_b": (n_c, e),
                         "c_ln_g": (n_c, e), "c_ln_b": (n_c, e), "post_ln_g": post_g.shape, "post_ln_b": post_b.shape}
    for k, g in zip(SMALL_ORDER, _unpack(summed, [full_small_shapes[k] for k in SMALL_ORDER], width)):
        if k in SMALL_SHARDED:
            g = lax.dynamic_slice_in_dim(g, j_me * es, es, axis=g.ndim - 1)
        grads[k] = g
        after = adam(k)

    finish_swap(0, reduce_layer(0, after), True)

    loss = lax.psum(loss_vec[0, 0], ("x", "y", "c"))
    return loss, grad_x, grads, delta, new_m, new_v


def kernel(x, a_w_in, a_ln_g, a_ln_b, a_w_s, a_b_s, a_w_out, b_w_in, b_w_pool, b_scale, b_w_out, c_w_in, c_conv_w, c_conv_b, c_ln_g, c_ln_b, c_w_out, post_ln_g, post_ln_b, loss_target, m_a_w_in, m_a_ln_g, m_a_ln_b, m_a_w_s, m_a_b_s, m_a_w_out, m_b_w_in, m_b_w_pool, m_b_scale, m_b_w_out, m_c_w_in, m_c_conv_w, m_c_conv_b, m_c_ln_g, m_c_ln_b, m_c_w_out, m_post_ln_g, m_post_ln_b, v_a_w_in, v_a_ln_g, v_a_ln_b, v_a_w_s, v_a_b_s, v_a_w_out, v_b_w_in, v_b_w_pool, v_b_scale, v_b_w_out, v_c_w_in, v_c_conv_w, v_c_conv_b, v_c_ln_g, v_c_ln_b, v_c_w_out, v_post_ln_g, v_post_ln_b):
    args = locals()
    w = {k: args[k] for k in WEIGHTS}
    m = {k: args["m_" + k] for k in WEIGHTS}
    v = {k: args["v_" + k] for k in WEIGHTS}
    loss, grad_x, grads, delta, new_m, new_v = _step(x[0], loss_target[0], w, m, v)
    out = [loss, grad_x[None]]
    for group in (grads, delta, new_m, new_v):
        out.extend(group[k] for k in WEIGHTS)
    return tuple(out)
```

```python
import functools

import jax
import jax.numpy as jnp
from jax import lax
from jax.experimental import pallas as pl
from jax.experimental.pallas import tpu as pltpu

F32 = jnp.float32
BF16 = jnp.bfloat16
MESH = pl.DeviceIdType.MESH

DEPTH = 4
CHUNK = 128
SGU_HEADS = 8
POOL_WINDOWS = (2, 4, 8, 16)
POOL_HALO = 16
CONV_WIDTH = 31
CONV_HALO = 32
LN_EPS = 1e-5
ALPHA = (2.0 * DEPTH) ** 0.25
ADAM_LR, ADAM_B1, ADAM_B2, ADAM_EPS, ADAM_WD, ADAM_STEP = 0.001, 0.9, 0.999, 1e-08, 0.01, 10
N_CHIPS = 4
N_DEV = 8
LANES = 128
SUBLANES = 8
CONV_STRIDE = 4
CONV_SUB = SUBLANES * CONV_STRIDE
VMEM_LIMIT = 56 << 20
GELU_K = 0.7978845608028654
GELU_C = 0.044715


def _params(sem=None, vmem=VMEM_LIMIT):
    return pltpu.CompilerParams(dimension_semantics=sem, vmem_limit_bytes=vmem)


def _tile(dim, pref):
    if dim <= pref:
        return dim
    t = 1 << (pref.bit_length() - 1)
    while dim % t:
        t //= 2
    return t


ROW_BLOCK_BYTES = 2 << 20


def _row_tile(rows, cols):
    return _tile(rows, max(ROW_ALIGN, ROW_BLOCK_BYTES // (4 * cols)))


def _sigmoid(x):
    return 1.0 / (1.0 + jnp.exp(-x))


def _gelu_gate(x):
    x2 = x * x
    return _sigmoid((2.0 * GELU_K) * x * (1.0 + GELU_C * x2)), x2


def _gelu(x):
    return 0.5 * x * (1.0 + jnp.tanh(GELU_K * (x + GELU_C * x * x * x)))


def _gelu_and_grad(x):
    s, x2 = _gelu_gate(x)
    return x * s, s + x * (s * (1.0 - s)) * ((2.0 * GELU_K) * (1.0 + (3.0 * GELU_C) * x2))


def _ln_stats(x):
    mu = jnp.mean(x, axis=-1, keepdims=True)
    d = x - mu
    var = jnp.mean(d * d, axis=-1, keepdims=True)
    rstd = lax.rsqrt(var + LN_EPS)
    return d * rstd, rstd


def _ln_bwd(dy_hat, xhat, rstd):
    m1 = jnp.mean(dy_hat, axis=-1, keepdims=True)
    m2 = jnp.mean(dy_hat * xhat, axis=-1, keepdims=True)
    return rstd * (dy_hat - m1 - xhat * m2)


def _rowsum(x):
    return jnp.sum(x, axis=0, keepdims=True)


def _mm(name, a, b, *, dims, grid, a_spec, b_spec, out_shape, out_specs, acc_shape,
        extra=(), extra_specs=(), epilogue=None, dep=None):
    nk = grid[2]
    n_extra = len(extra)
    single = not isinstance(out_shape, (list, tuple))
    shapes = [out_shape] if single else list(out_shape)
    specs = [out_specs] if single else list(out_specs)
    n_out = len(shapes)
    deps = [] if dep is None else [dep]
    first_out = 2 + n_extra + len(deps)

    def body(*refs):
        a_ref, b_ref = refs[0], refs[1]
        ex = refs[2:2 + n_extra]
        outs = refs[first_out:first_out + n_out]

        def finish(acc):
            if epilogue is None:
                outs[0][...] = acc.astype(outs[0].dtype)
            else:
                epilogue(acc, ex, outs)

        part = lax.dot_general(a_ref[...].astype(BF16), b_ref[...].astype(BF16), dims, preferred_element_type=F32)
        if nk == 1:
            finish(part)
            return
        acc_ref = refs[-1]
        k = pl.program_id(2)

        @pl.when(k == 0)
        def _():
            acc_ref[...] = part

        @pl.when(jnp.logical_and(k > 0, k < nk - 1))
        def _():
            acc_ref[...] += part

        @pl.when(k == nk - 1)
        def _():
            finish(acc_ref[...] + part)

    res = pl.pallas_call(
        body, name=name, grid=grid, in_specs=[a_spec, b_spec, *extra_specs] + [ANY] * len(deps), out_specs=specs,
        out_shape=shapes, scratch_shapes=[pltpu.VMEM(acc_shape, F32)] if nk > 1 else [],
        compiler_params=_params(("parallel", "parallel", "arbitrary")),
    )(a, b, *extra, *deps)
    return res[0] if single else res


NN = (((1,), (0,)), ((), ()))
NT = (((1,), (1,)), ((), ()))
TN = (((0,), (0,)), ((), ()))


def _mm_in(name, hb, wg, layer, dep=None):
    s, d = hb.shape
    nsh = wg.shape[3]
    n = N_CHIPS * nsh
    bm, bn, bk = _tile(s, 1024), _tile(nsh, 1024), _tile(d, 2048)
    per = nsh // bn
    return _mm(name, hb, wg, dims=NN, grid=(s // bm, n // bn, d // bk),
               a_spec=pl.BlockSpec((bm, bk), lambda m, j, k: (m, k)),
               b_spec=pl.BlockSpec((None, None, bk, bn), lambda m, j, k: (j // per, layer, k, j % per)),
               out_shape=jax.ShapeDtypeStruct((s, n), F32),
               out_specs=pl.BlockSpec((bm, bn), lambda m, j, k: (m, j)),
               acc_shape=(bm, bn), dep=dep)


def _mm_out_ln(name, yg, wg, layer, res, gp, bp, g, b):
    s, e = yg.shape
    esh, d = wg.shape[2], wg.shape[3]
    bm, bk = _tile(s, 512), _tile(esh, 1024)
    per = esh // bk

    def epilogue(acc, ex, outs):
        res_ref, gp_ref, bp_ref, g_ref, b_ref = ex
        xhat_ref, hb_ref, rstd_ref = outs
        r = ALPHA * (res_ref[...] * gp_ref[...] + bp_ref[...]) + acc
        xhat, rstd = _ln_stats(r)
        xhat_ref[...] = xhat
        hb_ref[...] = (xhat * g_ref[...] + b_ref[...]).astype(BF16)
        rstd_ref[...] = jnp.broadcast_to(rstd, rstd_ref.shape)

    vec = pl.BlockSpec((1, d), lambda m, j, k: (0, 0))
    row = pl.BlockSpec((bm, d), lambda m, j, k: (m, 0))
    return _mm(name, yg, wg, dims=NN, grid=(s // bm, 1, e // bk),
               a_spec=pl.BlockSpec((bm, bk), lambda m, j, k: (m, k)),
               b_spec=pl.BlockSpec((None, None, bk, d), lambda m, j, k: (k // per, layer, k % per, 0)),
               extra=(res, gp, bp, g, b), extra_specs=(row, vec, vec, vec, vec),
               out_shape=(jax.ShapeDtypeStruct((s, d), F32), jax.ShapeDtypeStruct((s, d), BF16),
                          jax.ShapeDtypeStruct((s, LANES), F32)),
               out_specs=(row, row, pl.BlockSpec((bm, LANES), lambda m, j, k: (m, 0))),
               acc_shape=(bm, d), epilogue=epilogue)


def _mm_dyg(name, drb, wg, layer):
    s, d = drb.shape
    esh = wg.shape[2]
    e = N_CHIPS * esh
    bm, bn, bk = _tile(s, 1024), _tile(esh, 1024), _tile(d, 2048)
    per = esh // bn
    return _mm(name, drb, wg, dims=NT, grid=(s // bm, e // bn, d // bk),
               a_spec=pl.BlockSpec((bm, bk), lambda m, j, k: (m, k)),
               b_spec=pl.BlockSpec((None, None, bn, bk), lambda m, j, k: (j // per, layer, j % per, k)),
               out_shape=jax.ShapeDtypeStruct((s, e), F32),
               out_specs=pl.BlockSpec((bm, bn), lambda m, j, k: (m, j)),
               acc_shape=(bm, bn))


def _mm_dw_out(name, yg, drb, dep=None):
    s, e = yg.shape
    d = drb.shape[1]
    bm, bn, bk = _tile(e, 1024), _tile(d, 1024), _tile(s, 4096)
    return _mm(name, yg, drb, dims=TN, grid=(e // bm, d // bn, s // bk),
               a_spec=pl.BlockSpec((bk, bm), lambda m, j, k: (k, m)),
               b_spec=pl.BlockSpec((bk, bn), lambda m, j, k: (k, j)),
               out_shape=jax.ShapeDtypeStruct((e, d), BF16),
               out_specs=pl.BlockSpec((bm, bn), lambda m, j, k: (m, j)),
               acc_shape=(bm, bn), dep=dep)


def _mm_dh(name, dproj, wg, layer, dr, dep=None):
    s, n = dproj.shape
    d, nsh = wg.shape[2], wg.shape[3]
    bm, bn = _tile(s, 1024), _tile(d, 1024)
    bk = max(t for t in range(LANES, 3073, LANES) if nsh % t == 0)
    per = nsh // bk

    def epilogue(acc, ex, outs):
        outs[0][...] = ALPHA * ex[0][...] + acc

    blk = pl.BlockSpec((bm, bn), lambda m, j, k: (m, j))
    return _mm(name, dproj, wg, dims=NT, grid=(s // bm, d // bn, n // bk),
               a_spec=pl.BlockSpec((bm, bk), lambda m, j, k: (m, k)),
               b_spec=pl.BlockSpec((None, None, bn, bk), lambda m, j, k: (k // per, layer, j, k % per)),
               extra=(dr,), extra_specs=(blk,),
               out_shape=jax.ShapeDtypeStruct((s, d), F32), out_specs=blk,
               acc_shape=(bm, bn), epilogue=epilogue, dep=dep)


def _mm_dw_in(name, hb, dproj):
    s, d = hb.shape
    n = dproj.shape[1]
    nsh = n // N_CHIPS
    bm, bn, bk = _tile(d, 1024), _tile(nsh, 1024), _tile(s, 4096)
    per = nsh // bn
    return _mm(name, hb, dproj, dims=TN, grid=(d // bm, n // bn, s // bk),
               a_spec=pl.BlockSpec((bk, bm), lambda m, j, k: (k, m)),
               b_spec=pl.BlockSpec((bk, bn), lambda m, j, k: (k, j)),
               out_shape=jax.ShapeDtypeStruct((N_CHIPS, d, nsh), BF16),
               out_specs=pl.BlockSpec((None, bm, bn), lambda m, j, k: (j // per, m, j % per)),
               acc_shape=(bm, bn))


def _causal_mask():
    r = lax.broadcasted_iota(jnp.int32, (CHUNK, CHUNK), 0)
    c = lax.broadcasted_iota(jnp.int32, (CHUNK, CHUNK), 1)
    return r >= c


def _sgu_fwd(name, proj, ln_g, ln_b, w_s, b_st):
    s, n3 = proj.shape
    e = n3 // 3
    hd = e // SGU_HEADS

    def body(proj_ref, g_ref, b_ref, ws_ref, bst_ref, yg_ref):
        xhat, _ = _ln_stats(_gelu(proj_ref[:, e:2 * e]))
        vnb = (xhat * g_ref[...] + b_ref[...]).astype(BF16)
        tri = _causal_mask()
        for h in range(SGU_HEADS):
            hs = slice(h * hd, (h + 1) * hd)
            ws = jnp.where(tri, ws_ref[h], 0.0).astype(BF16)
            mixed = jnp.dot(ws, vnb[:, hs], preferred_element_type=F32) + bst_ref[:, h:h + 1]
            u = proj_ref[:, hs]
            z = proj_ref[:, 2 * e + h * hd:2 * e + (h + 1) * hd]
            yg_ref[:, hs] = (_gelu(u) * mixed * (z * _sigmoid(z))).astype(BF16)

    vec = pl.BlockSpec((1, e), lambda i: (0, 0))
    return pl.pallas_call(
        body, name=name, grid=(s // CHUNK,),
        in_specs=[pl.BlockSpec((CHUNK, n3), lambda i: (i, 0)), vec, vec,
                  pl.BlockSpec((SGU_HEADS, CHUNK, CHUNK), lambda i: (0, 0, 0)),
                  pl.BlockSpec((CHUNK, SGU_HEADS), lambda i: (0, 0))],
        out_specs=pl.BlockSpec((CHUNK, e), lambda i: (i, 0)),
        out_shape=jax.ShapeDtypeStruct((s, e), BF16),
        compiler_params=_params(("parallel",)),
    )(proj, ln_g, ln_b, w_s, b_st)


def _sgu_bwd(name, dyg, proj, ln_g, ln_b, w_s, b_st):
    s, n3 = proj.shape
    e = n3 // 3
    hd = e // SGU_HEADS

    def body(dyg_ref, proj_ref, g_ref, b_ref, ws_ref, bst_ref,
             dproj_ref, dg_ref, db_ref, dws_ref, dbs_ref, dvn_ref):
        @pl.when(pl.program_id(0) == 0)
        def _():
            dg_ref[...] = jnp.zeros_like(dg_ref)
            db_ref[...] = jnp.zeros_like(db_ref)
            dws_ref[...] = jnp.zeros_like(dws_ref)
            dbs_ref[...] = jnp.zeros_like(dbs_ref)

        gv, gv_grad = _gelu_and_grad(proj_ref[:, e:2 * e])
        xhat, rstd = _ln_stats(gv)
        g = g_ref[...]
        vnb = (xhat * g + b_ref[...]).astype(BF16)
        tri = _causal_mask()
        for h in range(SGU_HEADS):
            hs = slice(h * hd, (h + 1) * hd)
            zs = slice(2 * e + h * hd, 2 * e + (h + 1) * hd)
            ws = jnp.where(tri, ws_ref[h], 0.0).astype(BF16)
            mixed = jnp.dot(ws, vnb[:, hs], preferred_element_type=F32) + bst_ref[:, h:h + 1]
            u = proj_ref[:, hs]
            z = proj_ref[:, zs]
            gu, gu_grad = _gelu_and_grad(u)
            sig = _sigmoid(z)
            dy = dyg_ref[:, hs]
            t = dy * (z * sig)
            dmixed = t * gu
            dproj_ref[:, hs] = (t * mixed * gu_grad).astype(BF16)
            dproj_ref[:, zs] = (dy * gu * mixed * (sig * (1.0 + z * (1.0 - sig)))).astype(BF16)
            dmb = dmixed.astype(BF16)
            dvn_ref[:, hs] = lax.dot_general(ws, dmb, TN, preferred_element_type=F32)
            dws = lax.dot_general(dmb, vnb[:, hs], NT, preferred_element_type=F32)
            dws_ref[h] += jnp.where(tri, dws, 0.0)
            dbs_ref[:, h:h + 1] += jnp.sum(dmixed, axis=-1, keepdims=True)
        dvn = dvn_ref[...]
        dg_ref[...] += _rowsum(dvn * xhat)
        db_ref[...] += _rowsum(dvn)
        dgv = _ln_bwd(dvn * g, xhat, rstd)
        dproj_ref[:, e:2 * e] = (dgv * gv_grad).astype(BF16)

    vec = pl.BlockSpec((1, e), lambda i: (0, 0))
    wsp = pl.BlockSpec((SGU_HEADS, CHUNK, CHUNK), lambda i: (0, 0, 0))
    bsp = pl.BlockSpec((CHUNK, SGU_HEADS), lambda i: (0, 0))
    return pl.pallas_call(
        body, name=name, grid=(s // CHUNK,),
        in_specs=[pl.BlockSpec((CHUNK, e), lambda i: (i, 0)), pl.BlockSpec((CHUNK, n3), lambda i: (i, 0)),
                  vec, vec, wsp, bsp],
        out_specs=[pl.BlockSpec((CHUNK, n3), lambda i: (i, 0)), vec, vec, wsp, bsp],
        out_shape=[jax.ShapeDtypeStruct((s, n3), BF16), jax.ShapeDtypeStruct((1, e), F32),
                   jax.ShapeDtypeStruct((1, e), F32),
                   jax.ShapeDtypeStruct((SGU_HEADS, CHUNK, CHUNK), F32),
                   jax.ShapeDtypeStruct((CHUNK, SGU_HEADS), F32)],
        scratch_shapes=[pltpu.VMEM((CHUNK, e), F32)],
        compiler_params=_params(("arbitrary",)),
    )(dyg, proj, ln_g, ln_b, w_s, b_st)


def _pool_counts(pos, w):
    return jnp.minimum(pos + 1, w).astype(F32)


def _pool_fwd(name, proj):
    s, n2 = proj.shape
    e = n2 // 2
    gd = e // len(POOL_WINDOWS)
    ts = _row_tile(s, e)
    hb = ts // POOL_HALO

    def body(v_ref, halo_ref, p_ref, ext):
        i = pl.program_id(0)
        ext[POOL_HALO:, :] = v_ref[...]
        ext[:POOL_HALO, :] = jnp.where(i > 0, halo_ref[...], 0.0)
        pos = i * ts + lax.broadcasted_iota(jnp.int32, (ts, 1), 0)
        for g, w in enumerate(POOL_WINDOWS):
            cs = slice(g * gd, (g + 1) * gd)
            acc = ext[POOL_HALO:, cs]
            for j in range(1, w):
                acc = acc + ext[POOL_HALO - j:POOL_HALO - j + ts, cs]
            p_ref[:, cs] = (acc / _pool_counts(pos, w) - v_ref[:, cs]).astype(BF16)

    return pl.pallas_call(
        body, name=name, grid=(s // ts,),
        in_specs=[pl.BlockSpec((ts, e), lambda i: (i, 0)),
                  pl.BlockSpec((POOL_HALO, e), lambda i: (jnp.maximum(i * hb - 1, 0), 0))],
        out_specs=pl.BlockSpec((ts, e), lambda i: (i, 0)),
        out_shape=jax.ShapeDtypeStruct((s, e), BF16),
        scratch_shapes=[pltpu.VMEM((ts + POOL_HALO, e), F32)],
        compiler_params=_params(("parallel",)),
    )(proj, proj)


def _pool_mm(name, p, wg, proj, scale):
    s, e = p.shape
    ng, rsh, gd = wg.shape[1], wg.shape[2], wg.shape[3]
    bm = _tile(s, 1024)

    def body(p_ref, w0, w1, w2, w3, z_ref, sc_ref, q_ref, yg_ref):
        q = None
        for j, w_ref in enumerate((w0, w1, w2, w3)):
            part = jnp.dot(p_ref[:, j * rsh:(j + 1) * rsh], w_ref[...], preferred_element_type=F32)
            q = part if q is None else q + part
        z = z_ref[...]
        q_ref[...] = q
        yg_ref[...] = (q * sc_ref[...] * (z * _sigmoid(z))).astype(BF16)

    blk = pl.BlockSpec((bm, gd), lambda m, g: (m, g))
    w_specs = [pl.BlockSpec((None, None, rsh, gd), functools.partial(lambda m, g, j: (j, g, 0, 0), j=j))
               for j in range(N_CHIPS)]
    return pl.pallas_call(
        body, name=name, grid=(s // bm, ng),
        in_specs=[blk] + w_specs + [pl.BlockSpec((bm, gd), lambda m, g: (m, ng + g)),
                                    pl.BlockSpec((1, gd), lambda m, g: (0, g))],
        out_specs=[blk, blk],
        out_shape=[jax.ShapeDtypeStruct((s, e), F32), jax.ShapeDtypeStruct((s, e), BF16)],
        compiler_params=_params(("parallel", "parallel")),
    )(p, wg, wg, wg, wg, proj, scale)


def _pool_bwd1(name, dyg, q, proj, scale):
    s, e = q.shape
    ts = _row_tile(s, e)

    def body(dyg_ref, q_ref, z_ref, sc_ref, dq_ref, dz_ref, dsc_ref):
        @pl.when(pl.program_id(0) == 0)
        def _():
            dsc_ref[...] = jnp.zeros_like(dsc_ref)

        z = z_ref[...]
        sig = _sigmoid(z)
        dy = dyg_ref[...]
        qv = q_ref[...]
        sc = sc_ref[...]
        dp2 = dy * (z * sig)
        dsc_ref[...] += _rowsum(dp2 * qv)
        dq_ref[...] = (dp2 * sc).astype(BF16)
        dz_ref[...] = (dy * (qv * sc) * (sig * (1.0 + z * (1.0 - sig)))).astype(BF16)

    blk = pl.BlockSpec((ts, e), lambda i: (i, 0))
    vec = pl.BlockSpec((1, e), lambda i: (0, 0))
    return pl.pallas_call(
        body, name=name, grid=(s // ts,),
        in_specs=[blk, blk, pl.BlockSpec((ts, e), lambda i: (i, 1)), vec],
        out_specs=[blk, pl.BlockSpec((ts, e), lambda i: (i, 1)), vec],
        out_shape=[jax.ShapeDtypeStruct((s, e), BF16), jax.ShapeDtypeStruct((s, 2 * e), BF16),
                   jax.ShapeDtypeStruct((1, e), F32)],
        compiler_params=_params(("arbitrary",)),
    )(dyg, q, proj, scale)


def _pool_mm_bwd(name, dq, wg):
    s, e = dq.shape
    ng, rsh, gd = wg.shape[1], wg.shape[2], wg.shape[3]
    bm = _tile(s, 1024)
    return _mm(name, dq, wg, dims=NT, grid=(s // bm, ng * N_CHIPS, 1),
               a_spec=pl.BlockSpec((bm, gd), lambda m, j, k: (m, j // N_CHIPS)),
               b_spec=pl.BlockSpec((None, None, rsh, gd), lambda m, j, k: (j % N_CHIPS, j // N_CHIPS, 0, 0)),
               out_shape=jax.ShapeDtypeStruct((s, e), F32),
               out_specs=pl.BlockSpec((bm, rsh), lambda m, j, k: (m, j)),
               acc_shape=(bm, rsh))


def _pool_bwd2(name, dp, dproj):
    s, e = dp.shape
    gd = e // len(POOL_WINDOWS)
    ts = _row_tile(s, e)
    n = s // ts
    hb = ts // POOL_HALO

    def body(dp_ref, nxt_ref, _, dv_ref, ext):
        i = pl.program_id(0)
        pos = i * ts + lax.broadcasted_iota(jnp.int32, (ts + POOL_HALO, 1), 0)
        for g, w in enumerate(POOL_WINDOWS):
            cs = slice(g * gd, (g + 1) * gd)
            cnt = _pool_counts(pos, w)
            ext[:ts, cs] = dp_ref[:, cs] / cnt[:ts]
            ext[ts:, cs] = jnp.where(i < n - 1, nxt_ref[:, cs] / cnt[ts:], 0.0)
        for g, w in enumerate(POOL_WINDOWS):
            cs = slice(g * gd, (g + 1) * gd)
            acc = ext[:ts, cs]
            for j in range(1, w):
                acc = acc + ext[j:j + ts, cs]
            dv_ref[:, cs] = (acc - dp_ref[:, cs]).astype(BF16)

    return pl.pallas_call(
        body, name=name, grid=(n,),
        in_specs=[pl.BlockSpec((ts, e), lambda i: (i, 0)),
                  pl.BlockSpec((POOL_HALO, e), lambda i: (jnp.minimum((i + 1) * hb, s // POOL_HALO - 1), 0)),
                  pl.BlockSpec(memory_space=pl.ANY)],
        out_specs=pl.BlockSpec((ts, e), lambda i: (i, 0)),
        out_shape=jax.ShapeDtypeStruct((s, 2 * e), BF16),
        scratch_shapes=[pltpu.VMEM((ts + POOL_HALO, e), F32)],
        input_output_aliases={2: 0},
        compiler_params=_params(("parallel",)),
    )(dp, dp, dproj)


def _mm_dw_pool(name, p, dq, ng):
    s, e = p.shape
    gd = e // ng
    rsh = gd // N_CHIPS
    bk = _tile(s, 4096)
    return _mm(name, p, dq, dims=TN, grid=(N_CHIPS, ng, s // bk),
               a_spec=pl.BlockSpec((bk, rsh), lambda m, g, k: (k, g * N_CHIPS + m)),
               b_spec=pl.BlockSpec((bk, gd), lambda m, g, k: (k, g)),
               out_shape=jax.ShapeDtypeStruct((N_CHIPS, ng, rsh, gd), BF16),
               out_specs=pl.BlockSpec((None, None, rsh, gd), lambda m, g, k: (m, g, 0, 0)),
               acc_shape=(rsh, gd))


def _conv_fwd(name, proj, conv_w, conv_b):
    s, n3 = proj.shape
    e = n3 // 3
    ts, tc = _tile(s, 512), LANES
    ne = e // tc
    hb = ts // CONV_HALO
    off = CONV_HALO - (CONV_WIDTH - 1)

    def body(a_ref, gl_ref, ha_ref, hgl_ref, w_ref, b_ref, c_ref, gext):
        i = pl.program_id(1)
        gext[CONV_HALO:, :] = a_ref[...] * _sigmoid(gl_ref[...])
        gext[:CONV_HALO, :] = jnp.where(i > 0, ha_ref[...] * _sigmoid(hgl_ref[...]), 0.0)
        taps = [jnp.broadcast_to(w_ref[k:k + 1, :], (SUBLANES, LANES)) for k in range(CONV_WIDTH)]
        bias = jnp.broadcast_to(b_ref[...], (SUBLANES, LANES))

        def sub_block(rb, carry):
            r0 = pl.multiple_of(rb * CONV_SUB, CONV_SUB)
            acc = [bias] * CONV_STRIDE
            for m in range(CONV_WIDTH + CONV_STRIDE - 1):
                g_m = gext[pl.ds(r0 + off + m, SUBLANES, stride=CONV_STRIDE), :]
                for q in range(CONV_STRIDE):
                    if 0 <= m - q < CONV_WIDTH:
                        acc[q] = acc[q] + taps[m - q] * g_m
            for q in range(CONV_STRIDE):
                c_ref[pl.ds(r0 + q, SUBLANES, stride=CONV_STRIDE), :] = acc[q]
            return carry

        lax.fori_loop(0, ts // CONV_SUB, sub_block, 0, unroll=2)

    halo = lambda i: jnp.maximum(i * hb - 1, 0)
    return pl.pallas_call(
        body, name=name, grid=(ne, s // ts),
        in_specs=[pl.BlockSpec((ts, tc), lambda j, i: (i, j)),
                  pl.BlockSpec((ts, tc), lambda j, i: (i, ne + j)),
                  pl.BlockSpec((CONV_HALO, tc), lambda j, i: (halo(i), j)),
                  pl.BlockSpec((CONV_HALO, tc), lambda j, i: (halo(i), ne + j)),
                  pl.BlockSpec((CONV_WIDTH, tc), lambda j, i: (0, j)),
                  pl.BlockSpec((1, tc), lambda j, i: (0, j))],
        out_specs=pl.BlockSpec((ts, tc), lambda j, i: (i, j)),
        out_shape=jax.ShapeDtypeStruct((s, e), F32),
        scratch_shapes=[pltpu.VMEM((ts + CONV_HALO, tc), F32)],
        compiler_params=_params(("parallel", "parallel")),
    )(proj, proj, proj, proj, conv_w, conv_b)


def _conv_gate_fwd(name, c, proj, ln_g, ln_b):
    s, e = c.shape
    ts = _row_tile(s, e)

    def body(c_ref, z_ref, g_ref, b_ref, yg_ref):
        xhat, _ = _ln_stats(c_ref[...])
        ln = xhat * g_ref[...] + b_ref[...]
        z = z_ref[...]
        yg_ref[...] = ((ln * _sigmoid(ln)) * (z * _sigmoid(z))).astype(BF16)

    blk = pl.BlockSpec((ts, e), lambda i: (i, 0))
    vec = pl.BlockSpec((1, e), lambda i: (0, 0))
    return pl.pallas_call(
        body, name=name, grid=(s // ts,),
        in_specs=[blk, pl.BlockSpec((ts, e), lambda i: (i, 2)), vec, vec],
        out_specs=blk, out_shape=jax.ShapeDtypeStruct((s, e), BF16),
        compiler_params=_params(("parallel",)),
    )(c, proj, ln_g, ln_b)


def _conv_bwd1(name, dyg, c, proj, ln_g, ln_b):
    s, e = c.shape
    ts = _row_tile(s, e)

    def body(dyg_ref, c_ref, z_ref, g_ref, b_ref, dc_ref, dz_ref, dg_ref, db_ref, dcb_ref):
        @pl.when(pl.program_id(0) == 0)
        def _():
            dg_ref[...] = jnp.zeros_like(dg_ref)
            db_ref[...] = jnp.zeros_like(db_ref)
            dcb_ref[...] = jnp.zeros_like(dcb_ref)

        xhat, rstd = _ln_stats(c_ref[...])
        g = g_ref[...]
        ln = xhat * g + b_ref[...]
        sl = _sigmoid(ln)
        z = z_ref[...]
        sz = _sigmoid(z)
        dy = dyg_ref[...]
        dz_ref[...] = (dy * (ln * sl) * (sz * (1.0 + z * (1.0 - sz)))).astype(BF16)
        dln = dy * (z * sz) * (sl * (1.0 + ln * (1.0 - sl)))
        dg_ref[...] += _rowsum(dln * xhat)
        db_ref[...] += _rowsum(dln)
        dc = _ln_bwd(dln * g, xhat, rstd)
        dc_ref[...] = dc
        dcb_ref[...] += _rowsum(dc)

    blk = pl.BlockSpec((ts, e), lambda i: (i, 0))
    zblk = pl.BlockSpec((ts, e), lambda i: (i, 2))
    vec = pl.BlockSpec((1, e), lambda i: (0, 0))
    vshape = jax.ShapeDtypeStruct((1, e), F32)
    return pl.pallas_call(
        body, name=name, grid=(s // ts,),
        in_specs=[blk, blk, zblk, vec, vec],
        out_specs=[blk, zblk, vec, vec, vec],
        out_shape=[jax.ShapeDtypeStruct((s, e), F32), jax.ShapeDtypeStruct((s, 3 * e), BF16),
                   vshape, vshape, vshape],
        compiler_params=_params(("arbitrary",)),
    )(dyg, c, proj, ln_g, ln_b)


def _conv_bwd2(name, dc, proj, conv_w):
    s, e = dc.shape
    ts, tc = _tile(s, 512), LANES
    ne = e // tc
    n = s // ts
    hb = ts // CONV_HALO
    off = CONV_HALO - (CONV_WIDTH - 1)
    span = CONV_WIDTH + CONV_STRIDE - 1

    def body(dc_ref, nxt_ref, a_ref, gl_ref, ha_ref, hgl_ref, w_ref, dg_ref, dw_ref, gext, dcext, dwacc, taps):
        i = pl.program_id(1)

        @pl.when(i == 0)
        def _():
            dwacc[...] = jnp.zeros_like(dwacc)

        gext[CONV_HALO:, :] = a_ref[...] * _sigmoid(gl_ref[...])
        gext[:CONV_HALO, :] = jnp.where(i > 0, ha_ref[...] * _sigmoid(hgl_ref[...]), 0.0)
        dcext[:ts, :] = dc_ref[...]
        dcext[ts:, :] = jnp.where(i < n - 1, nxt_ref[...], 0.0)
        for k in range(CONV_WIDTH):
            taps[k] = jnp.broadcast_to(w_ref[k:k + 1, :], (SUBLANES, LANES))

        def sub_block(rb, carry):
            r0 = pl.multiple_of(rb * CONV_SUB, CONV_SUB)
            dg = [jnp.zeros((SUBLANES, LANES), F32)] * CONV_STRIDE
            for m in range(span):
                dc_m = dcext[pl.ds(r0 + m, SUBLANES, stride=CONV_STRIDE), :]
                for q in range(CONV_STRIDE):
                    k = CONV_WIDTH - 1 - m + q
                    if 0 <= k < CONV_WIDTH:
                        dg[q] = dg[q] + taps[k] * dc_m
            for q in range(CONV_STRIDE):
                dg_ref[pl.ds(r0 + q, SUBLANES, stride=CONV_STRIDE), :] = dg[q]
            dc_q = [dcext[pl.ds(r0 + q, SUBLANES, stride=CONV_STRIDE), :] for q in range(CONV_STRIDE)]
            win = [gext[pl.ds(r0 + off + m, SUBLANES, stride=CONV_STRIDE), :] for m in range(CONV_STRIDE - 1)]
            for k in range(CONV_WIDTH):
                win.append(gext[pl.ds(r0 + off + k + CONV_STRIDE - 1, SUBLANES, stride=CONV_STRIDE), :])
                prods = [dc_q[q] * win[q] for q in range(CONV_STRIDE)]
                while len(prods) > 1:
                    prods = [a + b for a, b in zip(prods[::2], prods[1::2])]
                dwacc[k] += prods[0]
                win.pop(0)
            return carry

        lax.fori_loop(0, ts // CONV_SUB, sub_block, 0)

        @pl.when(i == n - 1)
        def _():
            dw_ref[...] = jnp.sum(dwacc[...], axis=1)

    halo = lambda i: jnp.maximum(i * hb - 1, 0)
    nxt = lambda i: jnp.minimum((i + 1) * hb, s // CONV_HALO - 1)
    return pl.pallas_call(
        body, name=name, grid=(ne, n),
        in_specs=[pl.BlockSpec((ts, tc), lambda j, i: (i, j)),
                  pl.BlockSpec((CONV_HALO, tc), lambda j, i: (nxt(i), j)),
                  pl.BlockSpec((ts, tc), lambda j, i: (i, j)),
                  pl.BlockSpec((ts, tc), lambda j, i: (i, ne + j)),
                  pl.BlockSpec((CONV_HALO, tc), lambda j, i: (halo(i), j)),
                  pl.BlockSpec((CONV_HALO, tc), lambda j, i: (halo(i), ne + j)),
                  pl.BlockSpec((CONV_WIDTH, tc), lambda j, i: (0, j))],
        out_specs=[pl.BlockSpec((ts, tc), lambda j, i: (i, j)),
                   pl.BlockSpec((CONV_HALO, tc), lambda j, i: (0, j))],
        out_shape=[jax.ShapeDtypeStruct((s, e), F32), jax.ShapeDtypeStruct((CONV_HALO, e), F32)],
        scratch_shapes=[pltpu.VMEM((ts + CONV_HALO, tc), F32), pltpu.VMEM((ts + CONV_HALO, tc), F32),
                        pltpu.VMEM((CONV_HALO, SUBLANES, LANES), F32), pltpu.VMEM((CONV_HALO, SUBLANES, LANES), F32)],
        compiler_params=_params(("parallel", "arbitrary")),
    )(dc, dc, proj, proj, proj, proj, conv_w)


def _conv_bwd3(name, dg, proj, dproj):
    s, e = dg.shape
    ts = _row_tile(s, 2 * e)

    def body(dg_ref, agl_ref, _, out_ref):
        a = agl_ref[:, :e]
        sig = _sigmoid(agl_ref[:, e:])
        dgv = dg_ref[...]
        out_ref[:, :e] = (dgv * sig).astype(BF16)
        out_ref[:, e:] = (dgv * a * sig * (1.0 - sig)).astype(BF16)

    wide = pl.BlockSpec((ts, 2 * e), lambda i: (i, 0))
    return pl.pallas_call(
        body, name=name, grid=(s // ts,),
        in_specs=[pl.BlockSpec((ts, e), lambda i: (i, 0)), wide, pl.BlockSpec(memory_space=pl.ANY)],
        out_specs=wide, out_shape=jax.ShapeDtypeStruct((s, 3 * e), BF16),
        input_output_aliases={2: 0},
        compiler_params=_params(("parallel",)),
    )(dg, proj, dproj)


def _post_ln_bwd(name, dy_or_target, xhat, rstd, g, b, loss_head, dep=None):
    s, d = xhat.shape
    ts = _row_tile(s, d)

    def body(dy_ref, xhat_ref, rstd_ref, g_ref, b_ref, *rest):
        dr_ref, drb_ref, dg_ref, db_ref, loss_ref = rest[-5:]

        @pl.when(pl.program_id(0) == 0)
        def _():
            dg_ref[...] = jnp.zeros_like(dg_ref)
            db_ref[...] = jnp.zeros_like(db_ref)
            loss_ref[...] = jnp.zeros_like(loss_ref)

        xh = xhat_ref[...]
        gv = g_ref[...]
        if loss_head:
            err = (xh * gv + b_ref[...]) - dy_ref[...]
            per_row = jnp.mean(err * err, axis=-1, keepdims=True)
            loss_ref[...] += 0.5 * jnp.broadcast_to(_rowsum(per_row), loss_ref.shape)
            dy = err / d
        else:
            dy = dy_ref[...]
        dg_ref[...] += _rowsum(dy * xh)
        db_ref[...] += _rowsum(dy)
        dr = _ln_bwd(dy * gv, xh, rstd_ref[:, 0:1])
        dr_ref[...] = dr
        drb_ref[...] = dr.astype(BF16)

    blk = pl.BlockSpec((ts, d), lambda i: (i, 0))
    vec = pl.BlockSpec((1, d), lambda i: (0, 0))
    deps = [] if dep is None else [dep]
    return pl.pallas_call(
        body, name=name, grid=(s // ts,),
        in_specs=[blk, blk, pl.BlockSpec((ts, LANES), lambda i: (i, 0)), vec, vec] + [ANY] * len(deps),
        out_specs=[blk, blk, vec, vec, pl.BlockSpec((1, LANES), lambda i: (0, 0))],
        out_shape=[jax.ShapeDtypeStruct((s, d), F32), jax.ShapeDtypeStruct((s, d), BF16),
                   jax.ShapeDtypeStruct((1, d), F32), jax.ShapeDtypeStruct((1, d), F32),
                   jax.ShapeDtypeStruct((1, LANES), F32)],
        compiler_params=_params(("arbitrary",)),
    )(dy_or_target, xhat, rstd, g, b, *deps)


def _as2d(a):
    return a.reshape(-1, a.shape[-1])


def _cast_layer(name, a, layer, slabs=0):
    _, r, c = a.shape
    tr = _row_tile(r, c)

    def body(a_ref, o_ref):
        o_ref[...] = a_ref[...].astype(BF16)

    if slabs:
        out_spec = pl.BlockSpec((None, tr, c), lambda i: (_my_shard(), i, 0))
        out_shape = jax.ShapeDtypeStruct((slabs, r, c), BF16)
    else:
        out_spec = pl.BlockSpec((tr, c), lambda i: (i, 0))
        out_shape = jax.ShapeDtypeStruct((r, c), BF16)
    return pl.pallas_call(body, name=name, grid=(r // tr,),
                          in_specs=[pl.BlockSpec((None, tr, c), lambda i: (layer, i, 0))],
                          out_specs=out_spec, out_shape=out_shape,
                          compiler_params=_params(("parallel",)))(a)


def _my_shard():
    return 2 * lax.axis_index("x") + lax.axis_index("y")


def _pair_sum(name, own, recv):
    nsh, r, c = own.shape
    h = r // 2
    tr = _row_tile(h, c)
    nb = h // tr

    def body(own_ref, recv_ref, o_ref):
        o_ref[...] = (own_ref[...].astype(F32) + recv_ref[...].astype(F32)).astype(BF16)

    blk = pl.BlockSpec((None, tr, c), lambda j, i: (j, i, 0))
    mine = pl.BlockSpec((None, tr, c), lambda j, i: (j, lax.axis_index("c") * nb + i, 0))
    return pl.pallas_call(body, name=name, grid=(nsh, nb), in_specs=[mine, blk], out_specs=blk,
                          out_shape=jax.ShapeDtypeStruct((nsh, h, c), BF16),
                          compiler_params=_params(("parallel", "parallel")))(own, recv)


def _final_sum(name, own, recv, got, layer, n_layers, prev):
    _, r, c = own.shape
    h = r // 2
    tr = _row_tile(h, c)
    nb = h // tr

    def body(own_ref, recv_ref, q0_ref, q1_ref, q2_ref, *rest):
        o_ref = rest[-1]
        acc = own_ref[...].astype(F32) + recv_ref[...].astype(F32)
        acc = acc + q0_ref[...].astype(F32)
        acc = acc + q1_ref[...].astype(F32)
        o_ref[...] = acc + q2_ref[...].astype(F32)

    in_specs = [pl.BlockSpec((None, tr, c), lambda i: (_my_shard(), lax.axis_index("c") * nb + i, 0)),
                pl.BlockSpec((None, tr, c), lambda i: (_my_shard(), i, 0))]
    in_specs += [pl.BlockSpec((None, tr, c), functools.partial(lambda i, k: (k, i, 0), k=k)) for k in range(3)]
    operands = [own, recv, got, got, got]
    aliases = {}
    if prev is not None:
        in_specs.append(pl.BlockSpec(memory_space=pl.ANY))
        operands.append(prev)
        aliases = {5: 0}
    return pl.pallas_call(body, name=name, grid=(nb,), in_specs=in_specs,
                          out_specs=pl.BlockSpec((None, tr, c), lambda i: (layer, lax.axis_index("c") * nb + i, 0)),
                          out_shape=jax.ShapeDtypeStruct((n_layers, r, c), F32),
                          input_output_aliases=aliases,
                          compiler_params=_params(("parallel",)))(*operands)


def _adam(name, w, g, m, v):
    r, c = w.shape
    tr = _row_tile(r, c)
    c1 = 1.0 / (1.0 - ADAM_B1 ** ADAM_STEP)
    c2 = 1.0 / (1.0 - ADAM_B2 ** ADAM_STEP)

    def body(w_ref, g_ref, m_ref, v_ref, d_ref, nm_ref, nv_ref):
        gv = g_ref[...]
        nm = ADAM_B1 * m_ref[...] + (1.0 - ADAM_B1) * gv
        nv = ADAM_B2 * v_ref[...] + (1.0 - ADAM_B2) * (gv * gv)
        d_ref[...] = -ADAM_LR * ((nm * c1) / (jnp.sqrt(nv * c2) + ADAM_EPS) + ADAM_WD * w_ref[...])
        nm_ref[...] = nm
        nv_ref[...] = nv

    blk = pl.BlockSpec((tr, c), lambda i: (i, 0))
    shp = jax.ShapeDtypeStruct((r, c), F32)
    return pl.pallas_call(body, name=name, grid=(r // tr,), in_specs=[blk] * 4, out_specs=[blk] * 3,
                          out_shape=[shp] * 3, compiler_params=_params(("parallel",)))(w, g, m, v)


def _sum_devices(name, parts):
    _, r, c = parts.shape
    tr = _row_tile(r, N_DEV * c)

    def body(p_ref, o_ref):
        acc = p_ref[0]
        for k in range(1, N_DEV):
            acc = acc + p_ref[k]
        o_ref[...] = acc

    return pl.pallas_call(body, name=name, grid=(r // tr,),
                          in_specs=[pl.BlockSpec((N_DEV, tr, c), lambda i: (0, i, 0))],
                          out_specs=pl.BlockSpec((tr, c), lambda i: (i, 0)),
                          out_shape=jax.ShapeDtypeStruct((r, c), F32),
                          compiler_params=_params(("parallel",)))(parts)


def _coords():
    return lax.axis_index("x"), lax.axis_index("y"), lax.axis_index("c")


def _chip_peers(x, y):
    return [(1 - x, y, 2 * (1 - x) + y), (x, 1 - y, 2 * x + 1 - y), (1 - x, 1 - y, 2 * (1 - x) + 1 - y)]


def _remote(src, dst, ssem, rsem, dev):
    return pltpu.make_async_remote_copy(src_ref=src, dst_ref=dst, send_sem=ssem, recv_sem=rsem,
                                        device_id=dev, device_id_type=MESH)


ANY = pl.BlockSpec(memory_space=pl.ANY)


SEM = pl.BlockSpec(memory_space=pltpu.SEMAPHORE)
HBM = pl.BlockSpec(memory_space=pltpu.HBM)
DATAFLOW = pltpu.SideEffectType.DATAFLOW_SIDE_EFFECTING


def _hbm(a):
    return pltpu.with_memory_space_constraint(a, pltpu.HBM)


def _split_start(name, make_copies, n_copies, srcs, land_shapes, dep):
    n, nl = len(srcs), len(land_shapes)
    nb = n + nl

    def body(*refs):
        src_refs, land_refs = refs[:n], refs[n:nb]
        ssem, rsem = refs[nb + 1], refs[nb + 2]
        token = refs[-1]
        for send, _ in make_copies(src_refs, land_refs, ssem, rsem):
            send.start()
        token[...] = jnp.zeros_like(token)

    lands = [_hbm(lax.empty(shp, dt)) for shp, dt in land_shapes]
    out_shape = [pltpu.SemaphoreType.DMA((n_copies,)), pltpu.SemaphoreType.DMA((n_copies,))]
    out_shape += [pltpu.HBM(a.shape, a.dtype) for a in list(srcs) + lands]
    out_shape.append(jax.ShapeDtypeStruct((ROW_ALIGN, LANES), F32))
    res = pl.pallas_call(
        body, name=name, out_shape=out_shape,
        in_specs=[HBM] * nb + [ANY],
        out_specs=[SEM, SEM] + [HBM] * nb + [pl.BlockSpec(memory_space=pltpu.VMEM)],
        input_output_aliases={i: 2 + i for i in range(nb)},
        compiler_params=pltpu.CompilerParams(has_side_effects=DATAFLOW),
    )(*[_hbm(a) for a in srcs], *lands, dep)
    return res[0], res[1], res[2:2 + n], res[2 + n:2 + nb], res[-1]


def _split_wait(name, make_copies, ssem, rsem, srcs, lands, after):
    n, nb = len(srcs), len(srcs) + len(lands)

    def body(*refs):
        src_refs, land_refs = refs[:n], refs[n:nb]
        for send, recv in make_copies(src_refs, land_refs, refs[nb], refs[nb + 1]):
            send.wait_send()
            recv.wait_recv()

    res = pl.pallas_call(
        body, name=name, out_shape=[pltpu.HBM(a.shape, a.dtype) for a in list(srcs) + list(lands)],
        in_specs=[HBM] * nb + [SEM, SEM, ANY], out_specs=[HBM] * nb,
        input_output_aliases={i: i for i in range(nb)},
        compiler_params=pltpu.CompilerParams(has_side_effects=DATAFLOW),
    )(*srcs, *lands, ssem, rsem, after)
    return res[:n], res[n:]


def _gather_copies(buf_refs, _, ssem, rsem):
    x, y, c = _coords()
    j = 2 * x + y
    out = []
    for i, buf in enumerate(buf_refs):
        h = buf.shape[1] // 2
        rows = pl.ds(c * h, h)
        for k, (px, py, pj) in enumerate(_chip_peers(x, y)):
            sems = (ssem.at[3 * i + k], rsem.at[3 * i + k], (px, py, c))
            out.append((_remote(buf.at[j, rows], buf.at[j, rows], *sems),
                        _remote(buf.at[pj, rows], buf.at[pj, rows], *sems)))
    return out


def _scatter_copies(src_refs, land_refs, ssem, rsem):
    x, y, c = _coords()
    out = []
    for i, (src, land) in enumerate(zip(src_refs, land_refs)):
        for k, (px, py, pj) in enumerate(_chip_peers(x, y)):
            cp = _remote(src.at[pj], land.at[k], ssem.at[3 * i + k], rsem.at[3 * i + k], (px, py, c))
            out.append((cp, cp))
    return out


def _sibling_pass(name, lands, dep):
    n = len(lands)

    def body(*refs):
        bufs = refs[n + 1:2 * n + 1]
        ssem, rsem = refs[2 * n + 1:]
        x, y, c = _coords()
        sends = []
        for i in range(n):
            h = bufs[i].shape[1] // 2
            for k, (px, py, pj) in enumerate(_chip_peers(x, y)):
                mine = bufs[i].at[pj, pl.ds(c * h, h)]
                sends.append(_remote(mine, mine, ssem.at[3 * i + k], rsem.at[3 * i + k], (x, y, 1 - c)))
        for cp in sends:
            cp.start()
        for i in range(n):
            h = bufs[i].shape[1] // 2
            for k, (px, py, pj) in enumerate(_chip_peers(x, y)):
                theirs = bufs[i].at[pj, pl.ds((1 - c) * h, h)]
                _remote(theirs, theirs, ssem.at[3 * i + k], rsem.at[3 * i + k], (x, y, 1 - c)).wait_recv()
        for cp in sends:
            cp.wait_send()

    return pl.pallas_call(
        body, name=name, in_specs=[ANY] * (n + 1), out_specs=[ANY] * n,
        out_shape=[jax.ShapeDtypeStruct(a.shape, a.dtype) for a in lands],
        input_output_aliases={i: i for i in range(n)},
        scratch_shapes=[pltpu.SemaphoreType.DMA((3 * n,)), pltpu.SemaphoreType.DMA((3 * n,))],
    )(*lands, dep)


def _gather_small(small):
    def body(in_ref, out_ref, ssem, rsem, lsem):
        x, y, c = _coords()
        j = 2 * x + y
        local = pltpu.make_async_copy(in_ref, out_ref.at[j], lsem)
        local.start()
        peers = _chip_peers(x, y)
        sends = [_remote(in_ref, out_ref.at[j], ssem.at[k], rsem.at[k], (px, py, c))
                 for k, (px, py, pj) in enumerate(peers)]
        for cp in sends:
            cp.start()
        for k, (px, py, pj) in enumerate(peers):
            sends[k].wait_send()
            _remote(in_ref, out_ref.at[pj], ssem.at[k], rsem.at[k], (px, py, c)).wait_recv()
        local.wait()

    return pl.pallas_call(body, name="gather_small", in_specs=[ANY], out_specs=ANY,
                          out_shape=jax.ShapeDtypeStruct((N_CHIPS,) + small.shape, small.dtype),
                          scratch_shapes=[pltpu.SemaphoreType.DMA((3,)), pltpu.SemaphoreType.DMA((3,)),
                                          pltpu.SemaphoreType.DMA])(small)


def _pair_copies(src_refs, land_refs, ssem, rsem):
    x, y, c = _coords()
    out = []
    for i, (src, land) in enumerate(zip(src_refs, land_refs)):
        h = src.shape[1] // 2
        cp = _remote(src.at[:, pl.ds((1 - c) * h, h)], land, ssem.at[i], rsem.at[i], (x, y, 1 - c))
        out.append((cp, cp))
    return out


def _swap_copies(layers):
    def make(buf_refs, _, ssem, rsem):
        x, y, c = _coords()
        out = []
        for i, buf in enumerate(buf_refs):
            h = buf.shape[1] // 2
            mine = buf.at[layers[i], pl.ds(c * h, h)]
            theirs = buf.at[layers[i], pl.ds((1 - c) * h, h)]
            sems = (ssem.at[i], rsem.at[i], (x, y, 1 - c))
            out.append((_remote(mine, mine, *sems), _remote(theirs, theirs, *sems)))
        return out

    return make


def _small_copies(buf_refs, _, ssem, rsem):
    x, y, c = _coords()
    buf = buf_refs[0]
    me = 4 * x + 2 * y + c
    out = []
    for r in range(1, N_DEV):
        px = 1 - x if r & 4 else x
        py = 1 - y if r & 2 else y
        pc = 1 - c if r & 1 else c
        peer = 4 * px + 2 * py + pc
        sems = (ssem.at[r - 1], rsem.at[r - 1], (px, py, pc))
        out.append((_remote(buf.at[me], buf.at[me], *sems), _remote(buf.at[peer], buf.at[peer], *sems)))
    return out


ROW_ALIGN = SUBLANES
SMALL_WIDTH = 8 * LANES
LAYER_WEIGHTS = (("a_w_in", "a_w_out"), ("b_w_in", "b_w_pool", "b_w_out"), ("c_w_in", "c_w_out"))
SMALL_SHARDED = ("a_ln_g", "a_ln_b", "c_conv_w", "c_conv_b", "c_ln_g", "c_ln_b")
SMALL_ORDER = ("a_ln_g", "a_ln_b", "a_w_s", "a_b_s", "b_scale", "c_conv_w", "c_conv_b", "c_ln_g", "c_ln_b",
               "post_ln_g", "post_ln_b")
WEIGHTS = ("a_w_in", "a_ln_g", "a_ln_b", "a_w_s", "a_b_s", "a_w_out", "b_w_in", "b_w_pool", "b_scale", "b_w_out",
           "c_w_in", "c_conv_w", "c_conv_b", "c_ln_g", "c_ln_b", "c_w_out", "post_ln_g", "post_ln_b")


def _pad_rows(a):
    pad = -a.shape[0] % ROW_ALIGN
    return jnp.pad(a, ((0, pad), (0, 0))) if pad else a


def _pack(parts, width):
    return jnp.concatenate([_pad_rows(p.reshape(-1, width)) for p in parts], axis=0)


def _unpack(packed, shapes, width):
    out, row = [], 0
    for shp in shapes:
        size = 1
        for n in shp:
            size *= n
        rows = size // width
        out.append(packed[row:row + rows].reshape(shp))
        row += rows + (-rows % ROW_ALIGN)
    return out


def _step(x, tgt, w, m, v):
    s, d = x.shape
    cx, cy, cc = _coords()
    j_me = 2 * cx + cy
    es = w["a_ln_g"].shape[1]
    e = N_CHIPS * es

    n_a, n_c = w["a_w_in"].shape[0], w["c_w_in"].shape[0]
    ng = w["b_w_pool"].shape[1]
    as3d = lambda a: a.reshape(a.shape[0], -1, a.shape[-1])

    zones = [[_cast_layer("cast_%s%d" % (k, i // 3), as3d(w[k]), i // 3, N_CHIPS) for k in LAYER_WEIGHTS[i % 3]]
             for i in range(DEPTH)]

    def gather_start(i, dep):
        return _split_start("gather_start%d" % i, _gather_copies, 3 * len(zones[i]), zones[i], [], dep)

    small_all = _gather_small(_pack([w[k] for k in SMALL_SHARDED], es))
    ssem, rsem, lands, _, tok = gather_start(0, small_all)
    xb = _cast_layer("cast_x", x[None], 0)
    small_full = jnp.swapaxes(small_all, 0, 1).reshape(small_all.shape[1], e)
    full_shapes = [(n_a, e), (n_a, e), (CONV_WIDTH, e), (1, e), (1, e), (1, e)]
    a_ln_g, a_ln_b, conv_w, conv_b, c_ln_g, c_ln_b = _unpack(small_full, full_shapes, e)

    post_g, post_b = w["post_ln_g"], w["post_ln_b"]
    b_scale = w["b_scale"]

    saved = []
    res, gp, bp, hb = x, jnp.ones((1, d), F32), jnp.zeros((1, d), F32), xb
    for i in range(DEPTH):
        kind, l = i % 3, i // 3
        tag = "L%d" % i
        after = zones[DEPTH - 1][-1] if i == 0 else res
        lands, _ = _split_wait("gather_wait%d" % i, _gather_copies, ssem, rsem, lands, [], after)
        full = _sibling_pass("sibling_pass%d" % i, lands, tok)
        if i + 1 < DEPTH:
            ssem, rsem, lands, _, tok = gather_start(i + 1, full[0])
        wl = dict(zip(LAYER_WEIGHTS[kind], full))
        w_in = wl[LAYER_WEIGHTS[kind][0]].reshape(N_CHIPS, 1, d, -1)
        w_out = wl[LAYER_WEIGHTS[kind][-1]].reshape(N_CHIPS, 1, es, d)
        proj = _mm_in(tag + "_in", hb, w_in, 0, dep=tok)
        if kind == 0:
            b_st = w["a_b_s"][l].T
            yg = _sgu_fwd(tag + "_sgu", proj, a_ln_g[l:l + 1], a_ln_b[l:l + 1], w["a_w_s"][l], b_st)
            extra = (b_st,)
        elif kind == 1:
            w_pool = wl["b_w_pool"].reshape(N_CHIPS, ng, -1, w["b_w_pool"].shape[3])
            p = _pool_fwd(tag + "_pool", proj)
            q, yg = _pool_mm(tag + "_poolmm", p, w_pool, proj, b_scale[l:l + 1])
            extra = (p, q, w_pool)
        else:
            cv = _conv_fwd(tag + "_conv", proj, conv_w, conv_b)
            yg = _conv_gate_fwd(tag + "_gate", cv, proj, c_ln_g, c_ln_b)
            extra = (cv,)
        xhat, hb_next, rstd = _mm_out_ln(tag + "_out", yg, w_out, 0, res, gp, bp, post_g[i:i + 1], post_b[i:i + 1])
        saved.append((hb, proj, yg, xhat, rstd, extra, w_in, w_out))
        res, gp, bp, hb = xhat, post_g[i:i + 1], post_b[i:i + 1], hb_next

    sg = {k: [None] * w[k].shape[0] for k in ("a_ln_g", "a_ln_b", "a_w_s", "a_b_s", "post_ln_g", "post_ln_b")}
    scat = {}

    def finish_pair(p, after):
        i, pssem, prsem, own, recv, dep = p
        own, recv = _split_wait("pair_wait%d" % i, _pair_copies, pssem, prsem, own, recv, after)
        pair = [_pair_sum("pairsum%d_%d" % (i, n), a, r) for n, (a, r) in enumerate(zip(own, recv))]
        ssem, rsem, srcs, lands, token = _split_start("scatter_start%d" % i, _scatter_copies, 3 * len(pair), pair,
                                                      [((3,) + a.shape[1:], a.dtype) for a in pair], dep)
        scat[i] = (own, recv, ssem, rsem, srcs, lands)
        return token

    pending = None
    dr, drb, dg, db, loss_vec = _post_ln_bwd("L3_lnbwd", tgt, saved[-1][3], saved[-1][4],
                                             post_g[DEPTH - 1:DEPTH], post_b[DEPTH - 1:DEPTH], True)
    grad_x = None
    for i in reversed(range(DEPTH)):
        kind, l = i % 3, i // 3
        tag = "L%d" % i
        hb_in, proj, yg, _, _, extra, w_in, w_out = saved[i]
        sg["post_ln_g"][i], sg["post_ln_b"][i] = dg, db
        dyg = _mm_dyg(tag + "_dyg", drb, w_out, 0)
        dw_mid = []
        if kind == 0:
            dproj, dlg, dlb, dws_s, dbs_t = _sgu_bwd(tag + "_sgubwd", dyg, proj, a_ln_g[l:l + 1], a_ln_b[l:l + 1],
                                                     w["a_w_s"][l], extra[0])
            sg["a_ln_g"][l], sg["a_ln_b"][l], sg["a_w_s"][l], sg["a_b_s"][l] = dlg, dlb, dws_s, dbs_t.T
        elif kind == 1:
            p, q, w_pool = extra
            dq, dproj, dscale = _pool_bwd1(tag + "_poolbwd1", dyg, q, proj, b_scale[l:l + 1])
            dp = _pool_mm_bwd(tag + "_poolmmbwd", dq, w_pool)
            dproj = _pool_bwd2(tag + "_poolbwd2", dp, dproj)
            dw_mid = [_mm_dw_pool(tag + "_dwpool", p, dq, ng).reshape(N_CHIPS, -1, dq.shape[1] // ng)]
            sg["b_scale"] = [dscale]
        else:
            cv = extra[0]
            dc, dproj, dlg, dlb, dcb = _conv_bwd1(tag + "_convbwd1", dyg, cv, proj, c_ln_g, c_ln_b)
            dgc, dcw = _conv_bwd2(tag + "_convbwd2", dc, proj, conv_w)
            dproj = _conv_bwd3(tag + "_convbwd3", dgc, proj, dproj)
            sg["c_ln_g"], sg["c_ln_b"], sg["c_conv_b"], sg["c_conv_w"] = [dlg], [dlb], [dcb], [dcw[:CONV_WIDTH]]
        if i == 0:
            packed = _pack([jnp.concatenate([a.reshape(1, -1) for a in sg[k]], axis=0) for k in SMALL_ORDER], SMALL_WIDTH)
            slabs = lax.dynamic_update_slice(jnp.zeros((N_DEV,) + packed.shape, F32), packed[None],
                                             (4 * cx + 2 * cy + cc, 0, 0))
            small = _split_start("small_start", _small_copies, N_DEV - 1, [slabs], [], dproj)
            dproj_dep = small[-1]
        else:
            dproj_dep = dproj
        started = None if pending is None else finish_pair(pending, dproj_dep)
        dw_out = _mm_dw_out(tag + "_dwout", yg, drb, dep=started).reshape(N_CHIPS, es, d)
        if i > 0:
            dh = _mm_dh(tag + "_dh", dproj, w_in, 0, dr)
        dw_in = _mm_dw_in(tag + "_dwin", hb_in, dproj)
        own = [dw_in] + dw_mid + [dw_out]
        pending = (i,) + _split_start("pair_start%d" % i, _pair_copies, len(own), own,
                                      [((N_CHIPS, a.shape[1] // 2, a.shape[2]), a.dtype) for a in own], tok)
        tok = pending[-1]
        if i > 0:
            prev = saved[i - 1]
            dr, drb, dg, db, _ = _post_ln_bwd("L%d_lnbwd" % (i - 1), dh, prev[3], prev[4],
                                              post_g[i - 1:i], post_b[i - 1:i], False, dep=tok)
        else:
            grad_x = _mm_dh(tag + "_dh", dproj, w_in, 0, dr, dep=tok)
            started = finish_pair(pending, grad_x)

    grads, delta, new_m, new_v = {}, {}, {}, {}

    def adam(k):
        shp = w[k].shape
        two_d = (lambda a: a.reshape(-1, shp[-1])) if shp[-1] % LANES == 0 else (lambda a: a.reshape(1, -1))
        dl, nm, nv = _adam("adam_" + k, two_d(w[k]), two_d(grads[k]), two_d(m[k]), two_d(v[k]))
        delta[k], new_m[k], new_v[k] = dl.reshape(shp), nm.reshape(shp), nv.reshape(shp)
        return dl

    reduced, swaps = {}, {}

    def reduce_layer(i, after):
        kind, l = i % 3, i // 3
        names = LAYER_WEIGHTS[kind]
        own, recv, ssem, rsem, srcs, lands = scat[i]
        _, got = _split_wait("scatter_wait%d" % i, _scatter_copies, ssem, rsem, srcs, lands, after)
        for n, k in enumerate(names):
            reduced[k] = _final_sum("finalsum%d_%d" % (i, n), own[n], recv[n], got[n], l, w[k].shape[0],
                                    reduced.get(k))
        make = _swap_copies([l] * len(names))
        ssem, rsem, bufs, _, token = _split_start("swap_start%d" % i, make, len(names), [reduced[k] for k in names],
                                                  [], got[0])
        swaps[i] = (make, ssem, rsem, bufs)
        return token

    def finish_swap(i, after, with_adam):
        names = LAYER_WEIGHTS[i % 3]
        make, ssem, rsem, bufs = swaps[i]
        bufs, _ = _split_wait("swap_wait%d" % i, make, ssem, rsem, bufs, [], after)
        out = after
        for k, g in zip(names, bufs):
            reduced[k] = g
            if with_adam:
                grads[k] = g.reshape(w[k].shape)
                out = adam(k)
        return out

    after = started
    for i in range(DEPTH - 1, 0, -1):
        after = reduce_layer(i, after)
    for i in range(DEPTH - 2, 0, -1):
        after = finish_swap(i, after, True)
    finish_swap(DEPTH - 1, after, False)

    width = SMALL_WIDTH
    gathered, _ = _split_wait("small_wait", _small_copies, small[0], small[1], small[2], [], after)
    summed = _sum_devices("sum_small", gathered[0])
    full_small_shapes = {"a_ln_g": (n_a, e), "a_ln_b": (n_a, e), "a_w_s": w["a_w_s"].shape, "a_b_s": w["a_b_s"].shape,
                         "b_scale": w["b_scale"].shape, "c_conv_w": (n_c, CONV_WIDTH, e), "c_conv_b": (n_c, e),
                         "c_ln_g": (n_c, e), "c_ln_b": (n_c, e), "post_ln_g": post_g.shape, "post_ln_b": post_b.shape}
    for k, g in zip(SMALL_ORDER, _unpack(summed, [full_small_shapes[k] for k in SMALL_ORDER], width)):
        if k in SMALL_SHARDED:
            g = lax.dynamic_slice_in_dim(g, j_me * es, es, axis=g.ndim - 1)
        grads[k] = g
        after = adam(k)

    finish_swap(0, reduce_layer(0, after), True)

    loss = lax.psum(loss_vec[0, 0], ("x", "y", "c"))
    return loss, grad_x, grads, delta, new_m, new_v


def kernel(x, a_w_in, a_ln_g, a_ln_b, a_w_s, a_b_s, a_w_out, b_w_in, b_w_pool, b_scale, b_w_out, c_w_in, c_conv_w, c_conv_b, c_ln_g, c_ln_b, c_w_out, post_ln_g, post_ln_b, loss_target, m_a_w_in, m_a_ln_g, m_a_ln_b, m_a_w_s, m_a_b_s, m_a_w_out, m_b_w_in, m_b_w_pool, m_b_scale, m_b_w_out, m_c_w_in, m_c_conv_w, m_c_conv_b, m_c_ln_g, m_c_ln_b, m_c_w_out, m_post_ln_g, m_post_ln_b, v_a_w_in, v_a_ln_g, v_a_ln_b, v_a_w_s, v_a_b_s, v_a_w_out, v_b_w_in, v_b_w_pool, v_b_scale, v_b_w_out, v_c_w_in, v_c_conv_w, v_c_conv_b, v_c_ln_g, v_c_ln_b, v_c_w_out, v_post_ln_g, v_post_ln_b):
    args = locals()
    w = {k: args[k] for k in WEIGHTS}
    m = {k: args["m_" + k] for k in WEIGHTS}
    v = {k: args["v_" + k] for k in WEIGHTS}
    loss, grad_x, grads, delta, new_m, new_v = _step(x[0], loss_target[0], w, m, v)
    out = [loss, grad_x[None]]
    for group in (grads, delta, new_m, new_v):
        out.extend(group[k] for k in WEIGHTS)
    return tuple(out)
```

```python
import functools

import jax
import jax.numpy as jnp
from jax import lax
from jax.experimental import pallas as pl
from jax.experimental.pallas import tpu as pltpu

F32 = jnp.float32
BF16 = jnp.bfloat16
MESH = pl.DeviceIdType.MESH

DEPTH = 4
CHUNK = 128
SGU_HEADS = 8
POOL_WINDOWS = (2, 4, 8, 16)
POOL_HALO = 16
CONV_WIDTH = 31
CONV_HALO = 32
LN_EPS = 1e-5
ALPHA = (2.0 * DEPTH) ** 0.25
ADAM_LR, ADAM_B1, ADAM_B2, ADAM_EPS, ADAM_WD, ADAM_STEP = 0.001, 0.9, 0.999, 1e-08, 0.01, 10
N_CHIPS = 4
N_DEV = 8
LANES = 128
SUBLANES = 8
CONV_STRIDE = 4
CONV_SUB = SUBLANES * CONV_STRIDE
VMEM_LIMIT = 56 << 20
GELU_K = 0.7978845608028654
GELU_C = 0.044715


def _params(sem=None, vmem=VMEM_LIMIT):
    return pltpu.CompilerParams(dimension_semantics=sem, vmem_limit_bytes=vmem)


def _tile(dim, pref):
    if dim <= pref:
        return dim
    t = 1 << (pref.bit_length() - 1)
    while dim % t:
        t //= 2
    return t


ROW_BLOCK_BYTES = 2 << 20


def _row_tile(rows, cols):
    return _tile(rows, max(ROW_ALIGN, ROW_BLOCK_BYTES // (4 * cols)))


def _sigmoid(x):
    return 1.0 / (1.0 + jnp.exp(-x))


def _gelu_gate(x):
    x2 = x * x
    return _sigmoid((2.0 * GELU_K) * x * (1.0 + GELU_C * x2)), x2


def _gelu(x):
    return 0.5 * x * (1.0 + jnp.tanh(GELU_K * (x + GELU_C * x * x * x)))


def _gelu_and_grad(x):
    s, x2 = _gelu_gate(x)
    return x * s, s + x * (s * (1.0 - s)) * ((2.0 * GELU_K) * (1.0 + (3.0 * GELU_C) * x2))


def _ln_stats(x):
    mu = jnp.mean(x, axis=-1, keepdims=True)
    d = x - mu
    var = jnp.mean(d * d, axis=-1, keepdims=True)
    rstd = lax.rsqrt(var + LN_EPS)
    return d * rstd, rstd


def _ln_bwd(dy_hat, xhat, rstd):
    m1 = jnp.mean(dy_hat, axis=-1, keepdims=True)
    m2 = jnp.mean(dy_hat * xhat, axis=-1, keepdims=True)
    return rstd * (dy_hat - m1 - xhat * m2)


def _rowsum(x):
    return jnp.sum(x, axis=0, keepdims=True)


def _mm(name, a, b, *, dims, grid, a_spec, b_spec, out_shape, out_specs, acc_shape,
        extra=(), extra_specs=(), epilogue=None, dep=None):
    nk = grid[2]
    n_extra = len(extra)
    single = not isinstance(out_shape, (list, tuple))
    shapes = [out_shape] if single else list(out_shape)
    specs = [out_specs] if single else list(out_specs)
    n_out = len(shapes)
    deps = [] if dep is None else [dep]
    first_out = 2 + n_extra + len(deps)

    def body(*refs):
        a_ref, b_ref = refs[0], refs[1]
        ex = refs[2:2 + n_extra]
        outs = refs[first_out:first_out + n_out]

        def finish(acc):
            if epilogue is None:
                outs[0][...] = acc.astype(outs[0].dtype)
            else:
                epilogue(acc, ex, outs)

        part = lax.dot_general(a_ref[...].astype(BF16), b_ref[...].astype(BF16), dims, preferred_element_type=F32)
        if nk == 1:
            finish(part)
            return
        acc_ref = refs[-1]
        k = pl.program_id(2)

        @pl.when(k == 0)
        def _():
            acc_ref[...] = part

        @pl.when(jnp.logical_and(k > 0, k < nk - 1))
        def _():
            acc_ref[...] += part

        @pl.when(k == nk - 1)
        def _():
            finish(acc_ref[...] + part)

    res = pl.pallas_call(
        body, name=name, grid=grid, in_specs=[a_spec, b_spec, *extra_specs] + [ANY] * len(deps), out_specs=specs,
        out_shape=shapes, scratch_shapes=[pltpu.VMEM(acc_shape, F32)] if nk > 1 else [],
        compiler_params=_params(("parallel", "parallel", "arbitrary")),
    )(a, b, *extra, *deps)
    return res[0] if single else res


NN = (((1,), (0,)), ((), ()))
NT = (((1,), (1,)), ((), ()))
TN = (((0,), (0,)), ((), ()))


def _mm_in(name, hb, wg, layer, dep=None):
    s, d = hb.shape
    nsh = wg.shape[3]
    n = N_CHIPS * nsh
    bm, bn, bk = _tile(s, 1024), _tile(nsh, 1024), _tile(d, 2048)
    per = nsh // bn
    return _mm(name, hb, wg, dims=NN, grid=(s // bm, n // bn, d // bk),
               a_spec=pl.BlockSpec((bm, bk), lambda m, j, k: (m, k)),
               b_spec=pl.BlockSpec((None, None, bk, bn), lambda m, j, k: (j // per, layer, k, j % per)),
               out_shape=jax.ShapeDtypeStruct((s, n), F32),
               out_specs=pl.BlockSpec((bm, bn), lambda m, j, k: (m, j)),
               acc_shape=(bm, bn), dep=dep)


def _mm_out_ln(name, yg, wg, layer, res, gp, bp, g, b):
    s, e = yg.shape
    esh, d = wg.shape[2], wg.shape[3]
    bm, bk = _tile(s, 512), _tile(esh, 1024)
    per = esh // bk

    def epilogue(acc, ex, outs):
        res_ref, gp_ref, bp_ref, g_ref, b_ref = ex
        xhat_ref, hb_ref, rstd_ref = outs
        r = ALPHA * (res_ref[...] * gp_ref[...] + bp_ref[...]) + acc
        xhat, rstd = _ln_stats(r)
        xhat_ref[...] = xhat
        hb_ref[...] = (xhat * g_ref[...] + b_ref[...]).astype(BF16)
        rstd_ref[...] = jnp.broadcast_to(rstd, rstd_ref.shape)

    vec = pl.BlockSpec((1, d), lambda m, j, k: (0, 0))
    row = pl.BlockSpec((bm, d), lambda m, j, k: (m, 0))
    return _mm(name, yg, wg, dims=NN, grid=(s // bm, 1, e // bk),
               a_spec=pl.BlockSpec((bm, bk), lambda m, j, k: (m, k)),
               b_spec=pl.BlockSpec((None, None, bk, d), lambda m, j, k: (k // per, layer, k % per, 0)),
               extra=(res, gp, bp, g, b), extra_specs=(row, vec, vec, vec, vec),
               out_shape=(jax.ShapeDtypeStruct((s, d), F32), jax.ShapeDtypeStruct((s, d), BF16),
                          jax.ShapeDtypeStruct((s, LANES), F32)),
               out_specs=(row, row, pl.BlockSpec((bm, LANES), lambda m, j, k: (m, 0))),
               acc_shape=(bm, d), epilogue=epilogue)


def _mm_dyg(name, drb, wg, layer, dep=None):
    s, d = drb.shape
    esh = wg.shape[2]
    e = N_CHIPS * esh
    bm, bn, bk = _tile(s, 1024), _tile(esh, 1024), _tile(d, 2048)
    per = esh // bn
    return _mm(name, drb, wg, dims=NT, grid=(s // bm, e // bn, d // bk),
               a_spec=pl.BlockSpec((bm, bk), lambda m, j, k: (m, k)),
               b_spec=pl.BlockSpec((None, None, bn, bk), lambda m, j, k: (j // per, layer, j % per, k)),
               out_shape=jax.ShapeDtypeStruct((s, e), F32),
               out_specs=pl.BlockSpec((bm, bn), lambda m, j, k: (m, j)),
               acc_shape=(bm, bn), dep=dep)


def _mm_dw_out(name, yg, drb, dep=None):
    s, e = yg.shape
    d = drb.shape[1]
    bm, bn, bk = _tile(e, 1024), _tile(d, 1024), _tile(s, 4096)
    return _mm(name, yg, drb, dims=TN, grid=(e // bm, d // bn, s // bk),
               a_spec=pl.BlockSpec((bk, bm), lambda m, j, k: (k, m)),
               b_spec=pl.BlockSpec((bk, bn), lambda m, j, k: (k, j)),
               out_shape=jax.ShapeDtypeStruct((e, d), BF16),
               out_specs=pl.BlockSpec((bm, bn), lambda m, j, k: (m, j)),
               acc_shape=(bm, bn), dep=dep)


def _mm_dh(name, dproj, wg, layer, dr, dep=None):
    s, n = dproj.shape
    d, nsh = wg.shape[2], wg.shape[3]
    bm, bn = _tile(s, 1024), _tile(d, 1024)
    bk = max(t for t in range(LANES, 3073, LANES) if nsh % t == 0)
    per = nsh // bk

    def epilogue(acc, ex, outs):
        outs[0][...] = ALPHA * ex[0][...] + acc

    blk = pl.BlockSpec((bm, bn), lambda m, j, k: (m, j))
    return _mm(name, dproj, wg, dims=NT, grid=(s // bm, d // bn, n // bk),
               a_spec=pl.BlockSpec((bm, bk), lambda m, j, k: (m, k)),
               b_spec=pl.BlockSpec((None, None, bn, bk), lambda m, j, k: (k // per, layer, j, k % per)),
               extra=(dr,), extra_specs=(blk,),
               out_shape=jax.ShapeDtypeStruct((s, d), F32), out_specs=blk,
               acc_shape=(bm, bn), epilogue=epilogue, dep=dep)


def _mm_dw_in(name, hb, dproj):
    s, d = hb.shape
    n = dproj.shape[1]
    nsh = n // N_CHIPS
    bm, bn, bk = _tile(d, 1024), _tile(nsh, 1024), _tile(s, 4096)
    per = nsh // bn
    return _mm(name, hb, dproj, dims=TN, grid=(d // bm, n // bn, s // bk),
               a_spec=pl.BlockSpec((bk, bm), lambda m, j, k: (k, m)),
               b_spec=pl.BlockSpec((bk, bn), lambda m, j, k: (k, j)),
               out_shape=jax.ShapeDtypeStruct((N_CHIPS, d, nsh), BF16),
               out_specs=pl.BlockSpec((None, bm, bn), lambda m, j, k: (j // per, m, j % per)),
               acc_shape=(bm, bn))


def _causal_mask():
    r = lax.broadcasted_iota(jnp.int32, (CHUNK, CHUNK), 0)
    c = lax.broadcasted_iota(jnp.int32, (CHUNK, CHUNK), 1)
    return r >= c


def _sgu_fwd(name, proj, ln_g, ln_b, w_s, b_st):
    s, n3 = proj.shape
    e = n3 // 3
    hd = e // SGU_HEADS

    def body(proj_ref, g_ref, b_ref, ws_ref, bst_ref, yg_ref):
        xhat, _ = _ln_stats(_gelu(proj_ref[:, e:2 * e]))
        vnb = (xhat * g_ref[...] + b_ref[...]).astype(BF16)
        tri = _causal_mask()
        for h in range(SGU_HEADS):
            hs = slice(h * hd, (h + 1) * hd)
            ws = jnp.where(tri, ws_ref[h], 0.0).astype(BF16)
            mixed = jnp.dot(ws, vnb[:, hs], preferred_element_type=F32) + bst_ref[:, h:h + 1]
            u = proj_ref[:, hs]
            z = proj_ref[:, 2 * e + h * hd:2 * e + (h + 1) * hd]
            yg_ref[:, hs] = (_gelu(u) * mixed * (z * _sigmoid(z))).astype(BF16)

    vec = pl.BlockSpec((1, e), lambda i: (0, 0))
    return pl.pallas_call(
        body, name=name, grid=(s // CHUNK,),
        in_specs=[pl.BlockSpec((CHUNK, n3), lambda i: (i, 0)), vec, vec,
                  pl.BlockSpec((SGU_HEADS, CHUNK, CHUNK), lambda i: (0, 0, 0)),
                  pl.BlockSpec((CHUNK, SGU_HEADS), lambda i: (0, 0))],
        out_specs=pl.BlockSpec((CHUNK, e), lambda i: (i, 0)),
        out_shape=jax.ShapeDtypeStruct((s, e), BF16),
        compiler_params=_params(("parallel",)),
    )(proj, ln_g, ln_b, w_s, b_st)


def _sgu_bwd(name, dyg, proj, ln_g, ln_b, w_s, b_st):
    s, n3 = proj.shape
    e = n3 // 3
    hd = e // SGU_HEADS

    def body(dyg_ref, proj_ref, g_ref, b_ref, ws_ref, bst_ref,
             dproj_ref, dg_ref, db_ref, dws_ref, dbs_ref, dvn_ref):
        @pl.when(pl.program_id(0) == 0)
        def _():
            dg_ref[...] = jnp.zeros_like(dg_ref)
            db_ref[...] = jnp.zeros_like(db_ref)
            dws_ref[...] = jnp.zeros_like(dws_ref)
            dbs_ref[...] = jnp.zeros_like(dbs_ref)

        gv, gv_grad = _gelu_and_grad(proj_ref[:, e:2 * e])
        xhat, rstd = _ln_stats(gv)
        g = g_ref[...]
        vnb = (xhat * g + b_ref[...]).astype(BF16)
        tri = _causal_mask()
        for h in range(SGU_HEADS):
            hs = slice(h * hd, (h + 1) * hd)
            zs = slice(2 * e + h * hd, 2 * e + (h + 1) * hd)
            ws = jnp.where(tri, ws_ref[h], 0.0).astype(BF16)
            mixed = jnp.dot(ws, vnb[:, hs], preferred_element_type=F32) + bst_ref[:, h:h + 1]
            u = proj_ref[:, hs]
            z = proj_ref[:, zs]
            gu, gu_grad = _gelu_and_grad(u)
            sig = _sigmoid(z)
            dy = dyg_ref[:, hs]
            t = dy * (z * sig)
            dmixed = t * gu
            dproj_ref[:, hs] = (t * mixed * gu_grad).astype(BF16)
            dproj_ref[:, zs] = (dy * gu * mixed * (sig * (1.0 + z * (1.0 - sig)))).astype(BF16)
            dmb = dmixed.astype(BF16)
            dvn_ref[:, hs] = lax.dot_general(ws, dmb, TN, preferred_element_type=F32)
            dws = lax.dot_general(dmb, vnb[:, hs], NT, preferred_element_type=F32)
            dws_ref[h] += jnp.where(tri, dws, 0.0)
            dbs_ref[:, h:h + 1] += jnp.sum(dmixed, axis=-1, keepdims=True)
        dvn = dvn_ref[...]
        dg_ref[...] += _rowsum(dvn * xhat)
        db_ref[...] += _rowsum(dvn)
        dgv = _ln_bwd(dvn * g, xhat, rstd)
        dproj_ref[:, e:2 * e] = (dgv * gv_grad).astype(BF16)

    vec = pl.BlockSpec((1, e), lambda i: (0, 0))
    wsp = pl.BlockSpec((SGU_HEADS, CHUNK, CHUNK), lambda i: (0, 0, 0))
    bsp = pl.BlockSpec((CHUNK, SGU_HEADS), lambda i: (0, 0))
    return pl.pallas_call(
        body, name=name, grid=(s // CHUNK,),
        in_specs=[pl.BlockSpec((CHUNK, e), lambda i: (i, 0)), pl.BlockSpec((CHUNK, n3), lambda i: (i, 0)),
                  vec, vec, wsp, bsp],
        out_specs=[pl.BlockSpec((CHUNK, n3), lambda i: (i, 0)), vec, vec, wsp, bsp],
        out_shape=[jax.ShapeDtypeStruct((s, n3), BF16), jax.ShapeDtypeStruct((1, e), F32),
                   jax.ShapeDtypeStruct((1, e), F32),
                   jax.ShapeDtypeStruct((SGU_HEADS, CHUNK, CHUNK), F32),
                   jax.ShapeDtypeStruct((CHUNK, SGU_HEADS), F32)],
        scratch_shapes=[pltpu.VMEM((CHUNK, e), F32)],
        compiler_params=_params(("arbitrary",)),
    )(dyg, proj, ln_g, ln_b, w_s, b_st)


def _pool_counts(pos, w):
    return jnp.minimum(pos + 1, w).astype(F32)


def _pool_fwd(name, proj):
    s, n2 = proj.shape
    e = n2 // 2
    gd = e // len(POOL_WINDOWS)
    ts = _row_tile(s, e)
    hb = ts // POOL_HALO

    def body(v_ref, halo_ref, p_ref, ext):
        i = pl.program_id(0)
        ext[POOL_HALO:, :] = v_ref[...]
        ext[:POOL_HALO, :] = jnp.where(i > 0, halo_ref[...], 0.0)
        pos = i * ts + lax.broadcasted_iota(jnp.int32, (ts, 1), 0)
        for g, w in enumerate(POOL_WINDOWS):
            cs = slice(g * gd, (g + 1) * gd)
            acc = ext[POOL_HALO:, cs]
            for j in range(1, w):
                acc = acc + ext[POOL_HALO - j:POOL_HALO - j + ts, cs]
            p_ref[:, cs] = (acc / _pool_counts(pos, w) - v_ref[:, cs]).astype(BF16)

    return pl.pallas_call(
        body, name=name, grid=(s // ts,),
        in_specs=[pl.BlockSpec((ts, e), lambda i: (i, 0)),
                  pl.BlockSpec((POOL_HALO, e), lambda i: (jnp.maximum(i * hb - 1, 0), 0))],
        out_specs=pl.BlockSpec((ts, e), lambda i: (i, 0)),
        out_shape=jax.ShapeDtypeStruct((s, e), BF16),
        scratch_shapes=[pltpu.VMEM((ts + POOL_HALO, e), F32)],
        compiler_params=_params(("parallel",)),
    )(proj, proj)


def _pool_mm(name, p, wg, proj, scale):
    s, e = p.shape
    ng, rsh, gd = wg.shape[1], wg.shape[2], wg.shape[3]
    bm = _tile(s, 1024)

    def body(p_ref, w0, w1, w2, w3, z_ref, sc_ref, q_ref, yg_ref):
        q = None
        for j, w_ref in enumerate((w0, w1, w2, w3)):
            part = jnp.dot(p_ref[:, j * rsh:(j + 1) * rsh], w_ref[...], preferred_element_type=F32)
            q = part if q is None else q + part
        z = z_ref[...]
        q_ref[...] = q
        yg_ref[...] = (q * sc_ref[...] * (z * _sigmoid(z))).astype(BF16)

    blk = pl.BlockSpec((bm, gd), lambda m, g: (m, g))
    w_specs = [pl.BlockSpec((None, None, rsh, gd), functools.partial(lambda m, g, j: (j, g, 0, 0), j=j))
               for j in range(N_CHIPS)]
    return pl.pallas_call(
        body, name=name, grid=(s // bm, ng),
        in_specs=[blk] + w_specs + [pl.BlockSpec((bm, gd), lambda m, g: (m, ng + g)),
                                    pl.BlockSpec((1, gd), lambda m, g: (0, g))],
        out_specs=[blk, blk],
        out_shape=[jax.ShapeDtypeStruct((s, e), F32), jax.ShapeDtypeStruct((s, e), BF16)],
        compiler_params=_params(("parallel", "parallel")),
    )(p, wg, wg, wg, wg, proj, scale)


def _pool_bwd1(name, dyg, q, proj, scale):
    s, e = q.shape
    ts = _row_tile(s, e)

    def body(dyg_ref, q_ref, z_ref, sc_ref, dq_ref, dz_ref, dsc_ref):
        @pl.when(pl.program_id(0) == 0)
        def _():
            dsc_ref[...] = jnp.zeros_like(dsc_ref)

        z = z_ref[...]
        sig = _sigmoid(z)
        dy = dyg_ref[...]
        qv = q_ref[...]
        sc = sc_ref[...]
        dp2 = dy * (z * sig)
        dsc_ref[...] += _rowsum(dp2 * qv)
        dq_ref[...] = (dp2 * sc).astype(BF16)
        dz_ref[...] = (dy * (qv * sc) * (sig * (1.0 + z * (1.0 - sig)))).astype(BF16)

    blk = pl.BlockSpec((ts, e), lambda i: (i, 0))
    vec = pl.BlockSpec((1, e), lambda i: (0, 0))
    return pl.pallas_call(
        body, name=name, grid=(s // ts,),
        in_specs=[blk, blk, pl.BlockSpec((ts, e), lambda i: (i, 1)), vec],
        out_specs=[blk, pl.BlockSpec((ts, e), lambda i: (i, 1)), vec],
        out_shape=[jax.ShapeDtypeStruct((s, e), BF16), jax.ShapeDtypeStruct((s, 2 * e), BF16),
                   jax.ShapeDtypeStruct((1, e), F32)],
        compiler_params=_params(("arbitrary",)),
    )(dyg, q, proj, scale)


def _pool_mm_bwd(name, dq, wg):
    s, e = dq.shape
    ng, rsh, gd = wg.shape[1], wg.shape[2], wg.shape[3]
    bm = _tile(s, 1024)
    return _mm(name, dq, wg, dims=NT, grid=(s // bm, ng * N_CHIPS, 1),
               a_spec=pl.BlockSpec((bm, gd), lambda m, j, k: (m, j // N_CHIPS)),
               b_spec=pl.BlockSpec((None, None, rsh, gd), lambda m, j, k: (j % N_CHIPS, j // N_CHIPS, 0, 0)),
               out_shape=jax.ShapeDtypeStruct((s, e), F32),
               out_specs=pl.BlockSpec((bm, rsh), lambda m, j, k: (m, j)),
               acc_shape=(bm, rsh))


def _pool_bwd2(name, dp, dproj):
    s, e = dp.shape
    gd = e // len(POOL_WINDOWS)
    ts = _row_tile(s, e)
    n = s // ts
    hb = ts // POOL_HALO

    def body(dp_ref, nxt_ref, _, dv_ref, ext):
        i = pl.program_id(0)
        pos = i * ts + lax.broadcasted_iota(jnp.int32, (ts + POOL_HALO, 1), 0)
        for g, w in enumerate(POOL_WINDOWS):
            cs = slice(g * gd, (g + 1) * gd)
            cnt = _pool_counts(pos, w)
            ext[:ts, cs] = dp_ref[:, cs] / cnt[:ts]
            ext[ts:, cs] = jnp.where(i < n - 1, nxt_ref[:, cs] / cnt[ts:], 0.0)
        for g, w in enumerate(POOL_WINDOWS):
            cs = slice(g * gd, (g + 1) * gd)
            acc = ext[:ts, cs]
            for j in range(1, w):
                acc = acc + ext[j:j + ts, cs]
            dv_ref[:, cs] = (acc - dp_ref[:, cs]).astype(BF16)

    return pl.pallas_call(
        body, name=name, grid=(n,),
        in_specs=[pl.BlockSpec((ts, e), lambda i: (i, 0)),
                  pl.BlockSpec((POOL_HALO, e), lambda i: (jnp.minimum((i + 1) * hb, s // POOL_HALO - 1), 0)),
                  pl.BlockSpec(memory_space=pl.ANY)],
        out_specs=pl.BlockSpec((ts, e), lambda i: (i, 0)),
        out_shape=jax.ShapeDtypeStruct((s, 2 * e), BF16),
        scratch_shapes=[pltpu.VMEM((ts + POOL_HALO, e), F32)],
        input_output_aliases={2: 0},
        compiler_params=_params(("parallel",)),
    )(dp, dp, dproj)


def _mm_dw_pool(name, p, dq, ng):
    s, e = p.shape
    gd = e // ng
    rsh = gd // N_CHIPS
    bk = _tile(s, 4096)
    return _mm(name, p, dq, dims=TN, grid=(N_CHIPS, ng, s // bk),
               a_spec=pl.BlockSpec((bk, rsh), lambda m, g, k: (k, g * N_CHIPS + m)),
               b_spec=pl.BlockSpec((bk, gd), lambda m, g, k: (k, g)),
               out_shape=jax.ShapeDtypeStruct((N_CHIPS, ng, rsh, gd), BF16),
               out_specs=pl.BlockSpec((None, None, rsh, gd), lambda m, g, k: (m, g, 0, 0)),
               acc_shape=(rsh, gd))


def _conv_fwd(name, proj, conv_w, conv_b):
    s, n3 = proj.shape
    e = n3 // 3
    ts, tc = _tile(s, 512), LANES
    ne = e // tc
    hb = ts // CONV_HALO
    off = CONV_HALO - (CONV_WIDTH - 1)

    def body(a_ref, gl_ref, ha_ref, hgl_ref, w_ref, b_ref, c_ref, gext):
        i = pl.program_id(1)
        gext[CONV_HALO:, :] = a_ref[...] * _sigmoid(gl_ref[...])
        gext[:CONV_HALO, :] = jnp.where(i > 0, ha_ref[...] * _sigmoid(hgl_ref[...]), 0.0)
        taps = [jnp.broadcast_to(w_ref[k:k + 1, :], (SUBLANES, LANES)) for k in range(CONV_WIDTH)]
        bias = jnp.broadcast_to(b_ref[...], (SUBLANES, LANES))

        def sub_block(rb, carry):
            r0 = pl.multiple_of(rb * CONV_SUB, CONV_SUB)
            acc = [bias] * CONV_STRIDE
            for m in range(CONV_WIDTH + CONV_STRIDE - 1):
                g_m = gext[pl.ds(r0 + off + m, SUBLANES, stride=CONV_STRIDE), :]
                for q in range(CONV_STRIDE):
                    if 0 <= m - q < CONV_WIDTH:
                        acc[q] = acc[q] + taps[m - q] * g_m
            for q in range(CONV_STRIDE):
                c_ref[pl.ds(r0 + q, SUBLANES, stride=CONV_STRIDE), :] = acc[q]
            return carry

        lax.fori_loop(0, ts // CONV_SUB, sub_block, 0, unroll=2)

    halo = lambda i: jnp.maximum(i * hb - 1, 0)
    return pl.pallas_call(
        body, name=name, grid=(ne, s // ts),
        in_specs=[pl.BlockSpec((ts, tc), lambda j, i: (i, j)),
                  pl.BlockSpec((ts, tc), lambda j, i: (i, ne + j)),
                  pl.BlockSpec((CONV_HALO, tc), lambda j, i: (halo(i), j)),
                  pl.BlockSpec((CONV_HALO, tc), lambda j, i: (halo(i), ne + j)),
                  pl.BlockSpec((CONV_WIDTH, tc), lambda j, i: (0, j)),
                  pl.BlockSpec((1, tc), lambda j, i: (0, j))],
        out_specs=pl.BlockSpec((ts, tc), lambda j, i: (i, j)),
        out_shape=jax.ShapeDtypeStruct((s, e), F32),
        scratch_shapes=[pltpu.VMEM((ts + CONV_HALO, tc), F32)],
        compiler_params=_params(("parallel", "parallel")),
    )(proj, proj, proj, proj, conv_w, conv_b)


def _conv_gate_fwd(name, c, proj, ln_g, ln_b):
    s, e = c.shape
    ts = _row_tile(s, e)

    def body(c_ref, z_ref, g_ref, b_ref, yg_ref):
        xhat, _ = _ln_stats(c_ref[...])
        ln = xhat * g_ref[...] + b_ref[...]
        z = z_ref[...]
        yg_ref[...] = ((ln * _sigmoid(ln)) * (z * _sigmoid(z))).astype(BF16)

    blk = pl.BlockSpec((ts, e), lambda i: (i, 0))
    vec = pl.BlockSpec((1, e), lambda i: (0, 0))
    return pl.pallas_call(
        body, name=name, grid=(s // ts,),
        in_specs=[blk, pl.BlockSpec((ts, e), lambda i: (i, 2)), vec, vec],
        out_specs=blk, out_shape=jax.ShapeDtypeStruct((s, e), BF16),
        compiler_params=_params(("parallel",)),
    )(c, proj, ln_g, ln_b)


def _conv_bwd1(name, dyg, c, proj, ln_g, ln_b):
    s, e = c.shape
    ts = _row_tile(s, e)

    def body(dyg_ref, c_ref, z_ref, g_ref, b_ref, dc_ref, dz_ref, dg_ref, db_ref, dcb_ref):
        @pl.when(pl.program_id(0) == 0)
        def _():
            dg_ref[...] = jnp.zeros_like(dg_ref)
            db_ref[...] = jnp.zeros_like(db_ref)
            dcb_ref[...] = jnp.zeros_like(dcb_ref)

        xhat, rstd = _ln_stats(c_ref[...])
        g = g_ref[...]
        ln = xhat * g + b_ref[...]
        sl = _sigmoid(ln)
        z = z_ref[...]
        sz = _sigmoid(z)
        dy = dyg_ref[...]
        dz_ref[...] = (dy * (ln * sl) * (sz * (1.0 + z * (1.0 - sz)))).astype(BF16)
        dln = dy * (z * sz) * (sl * (1.0 + ln * (1.0 - sl)))
        dg_ref[...] += _rowsum(dln * xhat)
        db_ref[...] += _rowsum(dln)
        dc = _ln_bwd(dln * g, xhat, rstd)
        dc_ref[...] = dc
        dcb_ref[...] += _rowsum(dc)

    blk = pl.BlockSpec((ts, e), lambda i: (i, 0))
    zblk = pl.BlockSpec((ts, e), lambda i: (i, 2))
    vec = pl.BlockSpec((1, e), lambda i: (0, 0))
    vshape = jax.ShapeDtypeStruct((1, e), F32)
    return pl.pallas_call(
        body, name=name, grid=(s // ts,),
        in_specs=[blk, blk, zblk, vec, vec],
        out_specs=[blk, zblk, vec, vec, vec],
        out_shape=[jax.ShapeDtypeStruct((s, e), F32), jax.ShapeDtypeStruct((s, 3 * e), BF16),
                   vshape, vshape, vshape],
        compiler_params=_params(("arbitrary",)),
    )(dyg, c, proj, ln_g, ln_b)


def _conv_bwd2(name, dc, proj, conv_w):
    s, e = dc.shape
    ts, tc = _tile(s, 512), LANES
    ne = e // tc
    n = s // ts
    hb = ts // CONV_HALO
    off = CONV_HALO - (CONV_WIDTH - 1)
    span = CONV_WIDTH + CONV_STRIDE - 1

    def body(dc_ref, nxt_ref, a_ref, gl_ref, ha_ref, hgl_ref, w_ref, dg_ref, dw_ref, gext, dcext, dwacc, taps):
        i = pl.program_id(1)

        @pl.when(i == 0)
        def _():
            dwacc[...] = jnp.zeros_like(dwacc)

        gext[CONV_HALO:, :] = a_ref[...] * _sigmoid(gl_ref[...])
        gext[:CONV_HALO, :] = jnp.where(i > 0, ha_ref[...] * _sigmoid(hgl_ref[...]), 0.0)
        dcext[:ts, :] = dc_ref[...]
        dcext[ts:, :] = jnp.where(i < n - 1, nxt_ref[...], 0.0)
        for k in range(CONV_WIDTH):
            taps[k] = jnp.broadcast_to(w_ref[k:k + 1, :], (SUBLANES, LANES))

        def sub_block(rb, carry):
            r0 = pl.multiple_of(rb * CONV_SUB, CONV_SUB)
            dg = [jnp.zeros((SUBLANES, LANES), F32)] * CONV_STRIDE
            for m in range(span):
                dc_m = dcext[pl.ds(r0 + m, SUBLANES, stride=CONV_STRIDE), :]
                for q in range(CONV_STRIDE):
                    k = CONV_WIDTH - 1 - m + q
                    if 0 <= k < CONV_WIDTH:
                        dg[q] = dg[q] + taps[k] * dc_m
            for q in range(CONV_STRIDE):
                dg_ref[pl.ds(r0 + q, SUBLANES, stride=CONV_STRIDE), :] = dg[q]
            dc_q = [dcext[pl.ds(r0 + q, SUBLANES, stride=CONV_STRIDE), :] for q in range(CONV_STRIDE)]
            win = [gext[pl.ds(r0 + off + m, SUBLANES, stride=CONV_STRIDE), :] for m in range(CONV_STRIDE - 1)]
            for k in range(CONV_WIDTH):
                win.append(gext[pl.ds(r0 + off + k + CONV_STRIDE - 1, SUBLANES, stride=CONV_STRIDE), :])
                prods = [dc_q[q] * win[q] for q in range(CONV_STRIDE)]
                while len(prods) > 1:
                    prods = [a + b for a, b in zip(prods[::2], prods[1::2])]
                dwacc[k] += prods[0]
                win.pop(0)
            return carry

        lax.fori_loop(0, ts // CONV_SUB, sub_block, 0)

        @pl.when(i == n - 1)
        def _():
            dw_ref[...] = jnp.sum(dwacc[...], axis=1)

    halo = lambda i: jnp.maximum(i * hb - 1, 0)
    nxt = lambda i: jnp.minimum((i + 1) * hb, s // CONV_HALO - 1)
    return pl.pallas_call(
        body, name=name, grid=(ne, n),
        in_specs=[pl.BlockSpec((ts, tc), lambda j, i: (i, j)),
                  pl.BlockSpec((CONV_HALO, tc), lambda j, i: (nxt(i), j)),
                  pl.BlockSpec((ts, tc), lambda j, i: (i, j)),
                  pl.BlockSpec((ts, tc), lambda j, i: (i, ne + j)),
                  pl.BlockSpec((CONV_HALO, tc), lambda j, i: (halo(i), j)),
                  pl.BlockSpec((CONV_HALO, tc), lambda j, i: (halo(i), ne + j)),
                  pl.BlockSpec((CONV_WIDTH, tc), lambda j, i: (0, j))],
        out_specs=[pl.BlockSpec((ts, tc), lambda j, i: (i, j)),
                   pl.BlockSpec((CONV_HALO, tc), lambda j, i: (0, j))],
        out_shape=[jax.ShapeDtypeStruct((s, e), F32), jax.ShapeDtypeStruct((CONV_HALO, e), F32)],
        scratch_shapes=[pltpu.VMEM((ts + CONV_HALO, tc), F32), pltpu.VMEM((ts + CONV_HALO, tc), F32),
                        pltpu.VMEM((CONV_HALO, SUBLANES, LANES), F32), pltpu.VMEM((CONV_HALO, SUBLANES, LANES), F32)],
        compiler_params=_params(("parallel", "arbitrary")),
    )(dc, dc, proj, proj, proj, proj, conv_w)


def _conv_bwd3(name, dg, proj, dproj):
    s, e = dg.shape
    ts = _row_tile(s, 2 * e)

    def body(dg_ref, agl_ref, _, out_ref):
        a = agl_ref[:, :e]
        sig = _sigmoid(agl_ref[:, e:])
        dgv = dg_ref[...]
        out_ref[:, :e] = (dgv * sig).astype(BF16)
        out_ref[:, e:] = (dgv * a * sig * (1.0 - sig)).astype(BF16)

    wide = pl.BlockSpec((ts, 2 * e), lambda i: (i, 0))
    return pl.pallas_call(
        body, name=name, grid=(s // ts,),
        in_specs=[pl.BlockSpec((ts, e), lambda i: (i, 0)), wide, pl.BlockSpec(memory_space=pl.ANY)],
        out_specs=wide, out_shape=jax.ShapeDtypeStruct((s, 3 * e), BF16),
        input_output_aliases={2: 0},
        compiler_params=_params(("parallel",)),
    )(dg, proj, dproj)


def _post_ln_bwd(name, dy_or_target, xhat, rstd, g, b, loss_head, dep=None):
    s, d = xhat.shape
    ts = _row_tile(s, d)

    def body(dy_ref, xhat_ref, rstd_ref, g_ref, b_ref, *rest):
        dr_ref, drb_ref, dg_ref, db_ref, loss_ref = rest[-5:]

        @pl.when(pl.program_id(0) == 0)
        def _():
            dg_ref[...] = jnp.zeros_like(dg_ref)
            db_ref[...] = jnp.zeros_like(db_ref)
            loss_ref[...] = jnp.zeros_like(loss_ref)

        xh = xhat_ref[...]
        gv = g_ref[...]
        if loss_head:
            err = (xh * gv + b_ref[...]) - dy_ref[...]
            per_row = jnp.mean(err * err, axis=-1, keepdims=True)
            loss_ref[...] += 0.5 * jnp.broadcast_to(_rowsum(per_row), loss_ref.shape)
            dy = err / d
        else:
            dy = dy_ref[...]
        dg_ref[...] += _rowsum(dy * xh)
        db_ref[...] += _rowsum(dy)
        dr = _ln_bwd(dy * gv, xh, rstd_ref[:, 0:1])
        dr_ref[...] = dr
        drb_ref[...] = dr.astype(BF16)

    blk = pl.BlockSpec((ts, d), lambda i: (i, 0))
    vec = pl.BlockSpec((1, d), lambda i: (0, 0))
    deps = [] if dep is None else [dep]
    return pl.pallas_call(
        body, name=name, grid=(s // ts,),
        in_specs=[blk, blk, pl.BlockSpec((ts, LANES), lambda i: (i, 0)), vec, vec] + [ANY] * len(deps),
        out_specs=[blk, blk, vec, vec, pl.BlockSpec((1, LANES), lambda i: (0, 0))],
        out_shape=[jax.ShapeDtypeStruct((s, d), F32), jax.ShapeDtypeStruct((s, d), BF16),
                   jax.ShapeDtypeStruct((1, d), F32), jax.ShapeDtypeStruct((1, d), F32),
                   jax.ShapeDtypeStruct((1, LANES), F32)],
        compiler_params=_params(("arbitrary",)),
    )(dy_or_target, xhat, rstd, g, b, *deps)


def _as2d(a):
    return a.reshape(-1, a.shape[-1])


def _cast_layer(name, a, layer, slabs=0):
    _, r, c = a.shape
    tr = _row_tile(r, c)

    def body(a_ref, o_ref):
        o_ref[...] = a_ref[...].astype(BF16)

    if slabs:
        out_spec = pl.BlockSpec((None, tr, c), lambda i: (_my_shard(), i, 0))
        out_shape = jax.ShapeDtypeStruct((slabs, r, c), BF16)
    else:
        out_spec = pl.BlockSpec((tr, c), lambda i: (i, 0))
        out_shape = jax.ShapeDtypeStruct((r, c), BF16)
    return pl.pallas_call(body, name=name, grid=(r // tr,),
                          in_specs=[pl.BlockSpec((None, tr, c), lambda i: (layer, i, 0))],
                          out_specs=out_spec, out_shape=out_shape,
                          compiler_params=_params(("parallel",)))(a)


def _my_shard():
    return 2 * lax.axis_index("x") + lax.axis_index("y")


def _pair_sum(name, own, recv):
    nsh, r, c = own.shape
    h = r // 2
    tr = _row_tile(h, c)
    nb = h // tr

    def body(own_ref, recv_ref, o_ref):
        o_ref[...] = (own_ref[...].astype(F32) + recv_ref[...].astype(F32)).astype(BF16)

    blk = pl.BlockSpec((None, tr, c), lambda j, i: (j, i, 0))
    mine = pl.BlockSpec((None, tr, c), lambda j, i: (j, lax.axis_index("c") * nb + i, 0))
    return pl.pallas_call(body, name=name, grid=(nsh, nb), in_specs=[mine, blk], out_specs=blk,
                          out_shape=jax.ShapeDtypeStruct((nsh, h, c), BF16),
                          compiler_params=_params(("parallel", "parallel")))(own, recv)


def _final_sum(name, own, recv, got, layer, n_layers, prev):
    _, r, c = own.shape
    h = r // 2
    tr = _row_tile(h, c)
    nb = h // tr

    def body(own_ref, recv_ref, q0_ref, q1_ref, q2_ref, *rest):
        o_ref = rest[-1]
        acc = own_ref[...].astype(F32) + recv_ref[...].astype(F32)
        acc = acc + q0_ref[...].astype(F32)
        acc = acc + q1_ref[...].astype(F32)
        o_ref[...] = acc + q2_ref[...].astype(F32)

    in_specs = [pl.BlockSpec((None, tr, c), lambda i: (_my_shard(), lax.axis_index("c") * nb + i, 0)),
                pl.BlockSpec((None, tr, c), lambda i: (_my_shard(), i, 0))]
    in_specs += [pl.BlockSpec((None, tr, c), functools.partial(lambda i, k: (k, i, 0), k=k)) for k in range(3)]
    operands = [own, recv, got, got, got]
    aliases = {}
    if prev is not None:
        in_specs.append(pl.BlockSpec(memory_space=pl.ANY))
        operands.append(prev)
        aliases = {5: 0}
    return pl.pallas_call(body, name=name, grid=(nb,), in_specs=in_specs,
                          out_specs=pl.BlockSpec((None, tr, c), lambda i: (layer, lax.axis_index("c") * nb + i, 0)),
                          out_shape=jax.ShapeDtypeStruct((n_layers, r, c), F32),
                          input_output_aliases=aliases,
                          compiler_params=_params(("parallel",)))(*operands)


def _adam(name, w, g, m, v, emit_grad=False):
    r, c = w.shape
    tr = _row_tile(r, c)
    c1 = 1.0 / (1.0 - ADAM_B1 ** ADAM_STEP)
    c2 = 1.0 / (1.0 - ADAM_B2 ** ADAM_STEP)
    n_out = 4 if emit_grad else 3

    def body(w_ref, g_ref, m_ref, v_ref, d_ref, nm_ref, nv_ref, *g_out):
        gv = g_ref[...]
        if emit_grad:
            g_out[0][...] = gv
        nm = ADAM_B1 * m_ref[...] + (1.0 - ADAM_B1) * gv
        nv = ADAM_B2 * v_ref[...] + (1.0 - ADAM_B2) * (gv * gv)
        d_ref[...] = -ADAM_LR * ((nm * c1) / (jnp.sqrt(nv * c2) + ADAM_EPS) + ADAM_WD * w_ref[...])
        nm_ref[...] = nm
        nv_ref[...] = nv

    blk = pl.BlockSpec((tr, c), lambda i: (i, 0))
    shp = jax.ShapeDtypeStruct((r, c), F32)
    return pl.pallas_call(body, name=name, grid=(r // tr,), in_specs=[blk] * 4, out_specs=[blk] * n_out,
                          out_shape=[shp] * n_out, compiler_params=_params(("parallel",)))(w, g, m, v)


def _sum_devices(name, parts):
    _, r, c = parts.shape
    tr = _row_tile(r, N_DEV * c)

    def body(p_ref, o_ref):
        acc = p_ref[0]
        for k in range(1, N_DEV):
            acc = acc + p_ref[k]
        o_ref[...] = acc

    return pl.pallas_call(body, name=name, grid=(r // tr,),
                          in_specs=[pl.BlockSpec((N_DEV, tr, c), lambda i: (0, i, 0))],
                          out_specs=pl.BlockSpec((tr, c), lambda i: (i, 0)),
                          out_shape=jax.ShapeDtypeStruct((r, c), F32),
                          compiler_params=_params(("parallel",)))(parts)


def _coords():
    return lax.axis_index("x"), lax.axis_index("y"), lax.axis_index("c")


def _chip_peers(x, y):
    return [(1 - x, y, 2 * (1 - x) + y), (x, 1 - y, 2 * x + 1 - y), (1 - x, 1 - y, 2 * (1 - x) + 1 - y)]


def _remote(src, dst, ssem, rsem, dev):
    return pltpu.make_async_remote_copy(src_ref=src, dst_ref=dst, send_sem=ssem, recv_sem=rsem,
                                        device_id=dev, device_id_type=MESH)


ANY = pl.BlockSpec(memory_space=pl.ANY)


SEM = pl.BlockSpec(memory_space=pltpu.SEMAPHORE)
HBM = pl.BlockSpec(memory_space=pltpu.HBM)
DATAFLOW = pltpu.SideEffectType.DATAFLOW_SIDE_EFFECTING


def _hbm(a):
    return pltpu.with_memory_space_constraint(a, pltpu.HBM)


def _split_start(name, make_copies, n_copies, srcs, land_shapes, dep):
    n, nl = len(srcs), len(land_shapes)
    nb = n + nl

    def body(*refs):
        src_refs, land_refs = refs[:n], refs[n:nb]
        ssem, rsem = refs[nb + 1], refs[nb + 2]
        token = refs[-1]
        for send, _ in make_copies(src_refs, land_refs, ssem, rsem):
            send.start()
        token[...] = jnp.zeros_like(token)

    lands = [_hbm(lax.empty(shp, dt)) for shp, dt in land_shapes]
    out_shape = [pltpu.SemaphoreType.DMA((n_copies,)), pltpu.SemaphoreType.DMA((n_copies,))]
    out_shape += [pltpu.HBM(a.shape, a.dtype) for a in list(srcs) + lands]
    out_shape.append(jax.ShapeDtypeStruct((ROW_ALIGN, LANES), F32))
    res = pl.pallas_call(
        body, name=name, out_shape=out_shape,
        in_specs=[HBM] * nb + [ANY],
        out_specs=[SEM, SEM] + [HBM] * nb + [pl.BlockSpec(memory_space=pltpu.VMEM)],
        input_output_aliases={i: 2 + i for i in range(nb)},
        compiler_params=pltpu.CompilerParams(has_side_effects=DATAFLOW),
    )(*[_hbm(a) for a in srcs], *lands, dep)
    return res[0], res[1], res[2:2 + n], res[2 + n:2 + nb], res[-1]


def _split_wait(name, make_copies, ssem, rsem, srcs, lands, after):
    n, nb = len(srcs), len(srcs) + len(lands)

    def body(*refs):
        src_refs, land_refs = refs[:n], refs[n:nb]
        for send, recv in make_copies(src_refs, land_refs, refs[nb], refs[nb + 1]):
            send.wait_send()
            recv.wait_recv()

    res = pl.pallas_call(
        body, name=name, out_shape=[pltpu.HBM(a.shape, a.dtype) for a in list(srcs) + list(lands)],
        in_specs=[HBM] * nb + [SEM, SEM, ANY], out_specs=[HBM] * nb,
        input_output_aliases={i: i for i in range(nb)},
        compiler_params=pltpu.CompilerParams(has_side_effects=DATAFLOW),
    )(*srcs, *lands, ssem, rsem, after)
    return res[:n], res[n:]


def _gather_copies(buf_refs, _, ssem, rsem):
    x, y, c = _coords()
    j = 2 * x + y
    out = []
    for i, buf in enumerate(buf_refs):
        h = buf.shape[1] // 2
        rows = pl.ds(c * h, h)
        for k, (px, py, pj) in enumerate(_chip_peers(x, y)):
            sems = (ssem.at[3 * i + k], rsem.at[3 * i + k], (px, py, c))
            out.append((_remote(buf.at[j, rows], buf.at[j, rows], *sems),
                        _remote(buf.at[pj, rows], buf.at[pj, rows], *sems)))
    return out


def _scatter_copies(src_refs, land_refs, ssem, rsem):
    x, y, c = _coords()
    out = []
    for i, (src, land) in enumerate(zip(src_refs, land_refs)):
        for k, (px, py, pj) in enumerate(_chip_peers(x, y)):
            cp = _remote(src.at[pj], land.at[k], ssem.at[3 * i + k], rsem.at[3 * i + k], (px, py, c))
            out.append((cp, cp))
    return out


def _sibling_pass(name, lands, dep):
    n = len(lands)

    def body(*refs):
        bufs = refs[n + 1:2 * n + 1]
        ssem, rsem = refs[2 * n + 1:]
        x, y, c = _coords()
        sends = []
        for i in range(n):
            h = bufs[i].shape[1] // 2
            for k, (px, py, pj) in enumerate(_chip_peers(x, y)):
                mine = bufs[i].at[pj, pl.ds(c * h, h)]
                sends.append(_remote(mine, mine, ssem.at[3 * i + k], rsem.at[3 * i + k], (x, y, 1 - c)))
        for cp in sends:
            cp.start()
        for i in range(n):
            h = bufs[i].shape[1] // 2
            for k, (px, py, pj) in enumerate(_chip_peers(x, y)):
                theirs = bufs[i].at[pj, pl.ds((1 - c) * h, h)]
                _remote(theirs, theirs, ssem.at[3 * i + k], rsem.at[3 * i + k], (x, y, 1 - c)).wait_recv()
        for cp in sends:
            cp.wait_send()

    return pl.pallas_call(
        body, name=name, in_specs=[ANY] * (n + 1), out_specs=[ANY] * n,
        out_shape=[jax.ShapeDtypeStruct(a.shape, a.dtype) for a in lands],
        input_output_aliases={i: i for i in range(n)},
        scratch_shapes=[pltpu.SemaphoreType.DMA((3 * n,)), pltpu.SemaphoreType.DMA((3 * n,))],
    )(*lands, dep)


def _gather_small(small):
    def body(in_ref, out_ref, ssem, rsem, lsem):
        x, y, c = _coords()
        j = 2 * x + y
        local = pltpu.make_async_copy(in_ref, out_ref.at[j], lsem)
        local.start()
        peers = _chip_peers(x, y)
        sends = [_remote(in_ref, out_ref.at[j], ssem.at[k], rsem.at[k], (px, py, c))
                 for k, (px, py, pj) in enumerate(peers)]
        for cp in sends:
            cp.start()
        for k, (px, py, pj) in enumerate(peers):
            sends[k].wait_send()
            _remote(in_ref, out_ref.at[pj], ssem.at[k], rsem.at[k], (px, py, c)).wait_recv()
        local.wait()

    return pl.pallas_call(body, name="gather_small", in_specs=[ANY], out_specs=ANY,
                          out_shape=jax.ShapeDtypeStruct((N_CHIPS,) + small.shape, small.dtype),
                          scratch_shapes=[pltpu.SemaphoreType.DMA((3,)), pltpu.SemaphoreType.DMA((3,)),
                                          pltpu.SemaphoreType.DMA])(small)


def _pair_copies(src_refs, land_refs, ssem, rsem):
    x, y, c = _coords()
    out = []
    for i, (src, land) in enumerate(zip(src_refs, land_refs)):
        h = src.shape[1] // 2
        cp = _remote(src.at[:, pl.ds((1 - c) * h, h)], land, ssem.at[i], rsem.at[i], (x, y, 1 - c))
        out.append((cp, cp))
    return out


def _swap_copies(layers):
    def make(buf_refs, _, ssem, rsem):
        x, y, c = _coords()
        out = []
        for i, buf in enumerate(buf_refs):
            h = buf.shape[1] // 2
            mine = buf.at[layers[i], pl.ds(c * h, h)]
            theirs = buf.at[layers[i], pl.ds((1 - c) * h, h)]
            sems = (ssem.at[i], rsem.at[i], (x, y, 1 - c))
            out.append((_remote(mine, mine, *sems), _remote(theirs, theirs, *sems)))
        return out

    return make


def _small_copies(buf_refs, _, ssem, rsem):
    x, y, c = _coords()
    buf = buf_refs[0]
    me = 4 * x + 2 * y + c
    out = []
    for r in range(1, N_DEV):
        px = 1 - x if r & 4 else x
        py = 1 - y if r & 2 else y
        pc = 1 - c if r & 1 else c
        peer = 4 * px + 2 * py + pc
        sems = (ssem.at[r - 1], rsem.at[r - 1], (px, py, pc))
        out.append((_remote(buf.at[me], buf.at[me], *sems), _remote(buf.at[peer], buf.at[peer], *sems)))
    return out


ROW_ALIGN = SUBLANES
SMALL_WIDTH = 8 * LANES
LAYER_WEIGHTS = (("a_w_in", "a_w_out"), ("b_w_in", "b_w_pool", "b_w_out"), ("c_w_in", "c_w_out"))
SMALL_SHARDED = ("a_ln_g", "a_ln_b", "c_conv_w", "c_conv_b", "c_ln_g", "c_ln_b")
SMALL_ORDER = ("a_ln_g", "a_ln_b", "a_w_s", "a_b_s", "b_scale", "c_conv_w", "c_conv_b", "c_ln_g", "c_ln_b",
               "post_ln_g", "post_ln_b")
WEIGHTS = ("a_w_in", "a_ln_g", "a_ln_b", "a_w_s", "a_b_s", "a_w_out", "b_w_in", "b_w_pool", "b_scale", "b_w_out",
           "c_w_in", "c_conv_w", "c_conv_b", "c_ln_g", "c_ln_b", "c_w_out", "post_ln_g", "post_ln_b")


def _pad_rows(a):
    pad = -a.shape[0] % ROW_ALIGN
    return jnp.pad(a, ((0, pad), (0, 0))) if pad else a


def _pack(parts, width):
    return jnp.concatenate([_pad_rows(p.reshape(-1, width)) for p in parts], axis=0)


def _unpack(packed, shapes, width):
    out, row = [], 0
    for shp in shapes:
        size = 1
        for n in shp:
            size *= n
        rows = size // width
        out.append(packed[row:row + rows].reshape(shp))
        row += rows + (-rows % ROW_ALIGN)
    return out


def _step(x, tgt, w, m, v):
    s, d = x.shape
    cx, cy, cc = _coords()
    j_me = 2 * cx + cy
    es = w["a_ln_g"].shape[1]
    e = N_CHIPS * es

    n_a, n_c = w["a_w_in"].shape[0], w["c_w_in"].shape[0]
    ng = w["b_w_pool"].shape[1]
    as3d = lambda a: a.reshape(a.shape[0], -1, a.shape[-1])

    zones = [[_cast_layer("cast_%s%d" % (k, i // 3), as3d(w[k]), i // 3, N_CHIPS) for k in LAYER_WEIGHTS[i % 3]]
             for i in range(DEPTH)]

    def gather_start(i, dep):
        return _split_start("gather_start%d" % i, _gather_copies, 3 * len(zones[i]), zones[i], [], dep)

    small_all = _gather_small(_pack([w[k] for k in SMALL_SHARDED], es))
    ssem, rsem, lands, _, tok = gather_start(0, small_all)
    xb = _cast_layer("cast_x", x[None], 0)
    small_full = jnp.swapaxes(small_all, 0, 1).reshape(small_all.shape[1], e)
    full_shapes = [(n_a, e), (n_a, e), (CONV_WIDTH, e), (1, e), (1, e), (1, e)]
    a_ln_g, a_ln_b, conv_w, conv_b, c_ln_g, c_ln_b = _unpack(small_full, full_shapes, e)

    post_g, post_b = w["post_ln_g"], w["post_ln_b"]
    b_scale = w["b_scale"]

    saved = []
    res, gp, bp, hb = x, jnp.ones((1, d), F32), jnp.zeros((1, d), F32), xb
    for i in range(DEPTH):
        kind, l = i % 3, i // 3
        tag = "L%d" % i
        after = zones[DEPTH - 1][-1] if i == 0 else res
        lands, _ = _split_wait("gather_wait%d" % i, _gather_copies, ssem, rsem, lands, [], after)
        full = _sibling_pass("sibling_pass%d" % i, lands, tok)
        if i + 1 < DEPTH:
            ssem, rsem, lands, _, tok = gather_start(i + 1, full[0])
        wl = dict(zip(LAYER_WEIGHTS[kind], full))
        w_in = wl[LAYER_WEIGHTS[kind][0]].reshape(N_CHIPS, 1, d, -1)
        w_out = wl[LAYER_WEIGHTS[kind][-1]].reshape(N_CHIPS, 1, es, d)
        proj = _mm_in(tag + "_in", hb, w_in, 0, dep=tok)
        if kind == 0:
            b_st = w["a_b_s"][l].T
            yg = _sgu_fwd(tag + "_sgu", proj, a_ln_g[l:l + 1], a_ln_b[l:l + 1], w["a_w_s"][l], b_st)
            extra = (b_st,)
        elif kind == 1:
            w_pool = wl["b_w_pool"].reshape(N_CHIPS, ng, -1, w["b_w_pool"].shape[3])
            p = _pool_fwd(tag + "_pool", proj)
            q, yg = _pool_mm(tag + "_poolmm", p, w_pool, proj, b_scale[l:l + 1])
            extra = (p, q, w_pool)
        else:
            cv = _conv_fwd(tag + "_conv", proj, conv_w, conv_b)
            yg = _conv_gate_fwd(tag + "_gate", cv, proj, c_ln_g, c_ln_b)
            extra = (cv,)
        xhat, hb_next, rstd = _mm_out_ln(tag + "_out", yg, w_out, 0, res, gp, bp, post_g[i:i + 1], post_b[i:i + 1])
        saved.append((hb, proj, yg, xhat, rstd, extra, w_in, w_out))
        res, gp, bp, hb = xhat, post_g[i:i + 1], post_b[i:i + 1], hb_next

    sg = {k: [None] * w[k].shape[0] for k in ("a_ln_g", "a_ln_b", "a_w_s", "a_b_s", "post_ln_g", "post_ln_b")}
    scat = {}

    def finish_pair(p, after):
        i, pssem, prsem, own, recv, dep = p
        own, recv = _split_wait("pair_wait%d" % i, _pair_copies, pssem, prsem, own, recv, after)
        pair = [_pair_sum("pairsum%d_%d" % (i, n), a, r) for n, (a, r) in enumerate(zip(own, recv))]
        ssem, rsem, srcs, lands, token = _split_start("scatter_start%d" % i, _scatter_copies, 3 * len(pair), pair,
                                                      [((3,) + a.shape[1:], a.dtype) for a in pair], dep)
        scat[i] = (own, recv, ssem, rsem, srcs, lands)
        return token

    grads, delta, new_m, new_v = {}, {}, {}, {}

    def adam(k, emit_grad=False):
        shp = w[k].shape
        two_d = (lambda a: a.reshape(-1, shp[-1])) if shp[-1] % LANES == 0 else (lambda a: a.reshape(1, -1))
        res = _adam("adam_" + k, two_d(w[k]), two_d(grads[k]), two_d(m[k]), two_d(v[k]), emit_grad)
        delta[k], new_m[k], new_v[k] = (a.reshape(shp) for a in res[:3])
        if emit_grad:
            grads[k] = res[3].reshape(shp)
        return res[0]

    reduced, swaps = {}, {}

    def reduce_layer(i, after):
        kind, l = i % 3, i // 3
        names = LAYER_WEIGHTS[kind]
        own, recv, ssem, rsem, srcs, lands = scat[i]
        _, got = _split_wait("scatter_wait%d" % i, _scatter_copies, ssem, rsem, srcs, lands, after)
        for n, k in enumerate(names):
            reduced[k] = _final_sum("finalsum%d_%d" % (i, n), own[n], recv[n], got[n], l, w[k].shape[0],
                                    reduced.get(k))
        make = _swap_copies([l] * len(names))
        ssem, rsem, bufs, _, token = _split_start("swap_start%d" % i, make, len(names), [reduced[k] for k in names],
                                                  [], got[0])
        swaps[i] = (make, ssem, rsem, bufs)
        return token

    def finish_swap(i, after, with_adam):
        names = LAYER_WEIGHTS[i % 3]
        make, ssem, rsem, bufs = swaps[i]
        bufs, _ = _split_wait("swap_wait%d" % i, make, ssem, rsem, bufs, [], after)
        out = after
        for k, g in zip(names, bufs):
            reduced[k] = g
            if with_adam:
                grads[k] = g.reshape(w[k].shape)
                out = adam(k, True)
        return out

    pending = None
    dr, drb, dg, db, loss_vec = _post_ln_bwd("L3_lnbwd", tgt, saved[-1][3], saved[-1][4],
                                             post_g[DEPTH - 1:DEPTH], post_b[DEPTH - 1:DEPTH], True)
    grad_x = None
    for i in reversed(range(DEPTH)):
        kind, l = i % 3, i // 3
        tag = "L%d" % i
        hb_in, proj, yg, _, _, extra, w_in, w_out = saved[i]
        sg["post_ln_g"][i], sg["post_ln_b"][i] = dg, db
        started = None if pending is None else finish_pair(pending, drb)
        dyg = _mm_dyg(tag + "_dyg", drb, w_out, 0, dep=started)
        dw_mid = []
        if kind == 0:
            dproj, dlg, dlb, dws_s, dbs_t = _sgu_bwd(tag + "_sgubwd", dyg, proj, a_ln_g[l:l + 1], a_ln_b[l:l + 1],
                                                     w["a_w_s"][l], extra[0])
            sg["a_ln_g"][l], sg["a_ln_b"][l], sg["a_w_s"][l], sg["a_b_s"][l] = dlg, dlb, dws_s, dbs_t.T
        elif kind == 1:
            p, q, w_pool = extra
            dq, dproj, dscale = _pool_bwd1(tag + "_poolbwd1", dyg, q, proj, b_scale[l:l + 1])
            dp = _pool_mm_bwd(tag + "_poolmmbwd", dq, w_pool)
            dproj = _pool_bwd2(tag + "_poolbwd2", dp, dproj)
            dw_mid = [_mm_dw_pool(tag + "_dwpool", p, dq, ng).reshape(N_CHIPS, -1, dq.shape[1] // ng)]
            sg["b_scale"] = [dscale]
        else:
            cv = extra[0]
            dc, dproj, dlg, dlb, dcb = _conv_bwd1(tag + "_convbwd1", dyg, cv, proj, c_ln_g, c_ln_b)
            dgc, dcw = _conv_bwd2(tag + "_convbwd2", dc, proj, conv_w)
            dproj = _conv_bwd3(tag + "_convbwd3", dgc, proj, dproj)
            sg["c_ln_g"], sg["c_ln_b"], sg["c_conv_b"], sg["c_conv_w"] = [dlg], [dlb], [dcb], [dcw[:CONV_WIDTH]]
        if i == 0:
            packed = _pack([jnp.concatenate([a.reshape(1, -1) for a in sg[k]], axis=0) for k in SMALL_ORDER], SMALL_WIDTH)
            slabs = lax.dynamic_update_slice(jnp.zeros((N_DEV,) + packed.shape, F32), packed[None],
                                             (4 * cx + 2 * cy + cc, 0, 0))
            small = _split_start("small_start", _small_copies, N_DEV - 1, [slabs], [], dproj)
        dw_out = _mm_dw_out(tag + "_dwout", yg, drb, dep=small[-1] if i == 0 else None).reshape(N_CHIPS, es, d)
        if i > 0:
            dh = _mm_dh(tag + "_dh", dproj, w_in, 0, dr)
        dw_in = _mm_dw_in(tag + "_dwin", hb_in, dproj)
        own = [dw_in] + dw_mid + [dw_out]
        pending = (i,) + _split_start("pair_start%d" % i, _pair_copies, len(own), own,
                                      [((N_CHIPS, a.shape[1] // 2, a.shape[2]), a.dtype) for a in own], dproj)
        if i > 0:
            prev = saved[i - 1]
            dr, drb, dg, db, _ = _post_ln_bwd("L%d_lnbwd" % (i - 1), dh, prev[3], prev[4],
                                              post_g[i - 1:i], post_b[i - 1:i], False, dep=pending[-1])
        else:
            after = pending[-1]
            for done in range(DEPTH - 1, 1, -1):
                after = reduce_layer(done, after)
            grad_x = _mm_dh(tag + "_dh", dproj, w_in, 0, dr, dep=finish_pair(pending, after))

    after = reduce_layer(1, grad_x)
    for i in range(DEPTH - 2, 0, -1):
        after = finish_swap(i, after, True)
    finish_swap(DEPTH - 1, after, False)

    width = SMALL_WIDTH
    gathered, _ = _split_wait("small_wait", _small_copies, small[0], small[1], small[2], [], after)
    summed = _sum_devices("sum_small", gathered[0])
    full_small_shapes = {"a_ln_g": (n_a, e), "a_ln_b": (n_a, e), "a_w_s": w["a_w_s"].shape, "a_b_s": w["a_b_s"].shape,
                         "b_scale": w["b_scale"].shape, "c_conv_w": (n_c, CONV_WIDTH, e), "c_conv_b": (n_c, e),
                         "c_ln_g": (n_c, e), "c_ln_b": (n_c, e), "post_ln_g": post_g.shape, "post_ln_b": post_b.shape}
    for k, g in zip(SMALL_ORDER, _unpack(summed, [full_small_shapes[k] for k in SMALL_ORDER], width)):
        if k in SMALL_SHARDED:
            g = lax.dynamic_slice_in_dim(g, j_me * es, es, axis=g.ndim - 1)
        grads[k] = g
        after = adam(k)

    finish_swap(0, reduce_layer(0, after), True)

    loss = lax.psum(loss_vec[0, 0], ("x", "y", "c"))
    return loss, grad_x, grads, delta, new_m, new_v


def kernel(x, a_w_in, a_ln_g, a_ln_b, a_w_s, a_b_s, a_w_out, b_w_in, b_w_pool, b_scale, b_w_out, c_w_in, c_conv_w, c_conv_b, c_ln_g, c_ln_b, c_w_out, post_ln_g, post_ln_b, loss_target, m_a_w_in, m_a_ln_g, m_a_ln_b, m_a_w_s, m_a_b_s, m_a_w_out, m_b_w_in, m_b_w_pool, m_b_scale, m_b_w_out, m_c_w_in, m_c_conv_w, m_c_conv_b, m_c_ln_g, m_c_ln_b, m_c_w_out, m_post_ln_g, m_post_ln_b, v_a_w_in, v_a_ln_g, v_a_ln_b, v_a_w_s, v_a_b_s, v_a_w_out, v_b_w_in, v_b_w_pool, v_b_scale, v_b_w_out, v_c_w_in, v_c_conv_w, v_c_conv_b, v_c_ln_g, v_c_ln_b, v_c_w_out, v_post_ln_g, v_post_ln_b):
    args = locals()
    w = {k: args[k] for k in WEIGHTS}
    m = {k: args["m_" + k] for k in WEIGHTS}
    v = {k: args["v_" + k] for k in WEIGHTS}
    loss, grad_x, grads, delta, new_m, new_v = _step(x[0], loss_target[0], w, m, v)
    out = [loss, grad_x[None]]
    for group in (grads, delta, new_m, new_v):
        out.extend(group[k] for k in WEIGHTS)
    return tuple(out)
```

```python
import functools

import jax
import jax.numpy as jnp
from jax import lax
from jax.experimental import pallas as pl
from jax.experimental.pallas import tpu as pltpu

F32 = jnp.float32
BF16 = jnp.bfloat16
MESH = pl.DeviceIdType.MESH

DEPTH = 4
CHUNK = 128
SGU_HEADS = 8
POOL_WINDOWS = (2, 4, 8, 16)
POOL_HALO = 16
CONV_WIDTH = 31
CONV_HALO = 32
LN_EPS = 1e-5
ALPHA = (2.0 * DEPTH) ** 0.25
ADAM_LR, ADAM_B1, ADAM_B2, ADAM_EPS, ADAM_WD, ADAM_STEP = 0.001, 0.9, 0.999, 1e-08, 0.01, 10
N_CHIPS = 4
N_DEV = 8
LANES = 128
SUBLANES = 8
CONV_STRIDE = 4
CONV_SUB = SUBLANES * CONV_STRIDE
VMEM_LIMIT = 56 << 20
GELU_K = 0.7978845608028654
GELU_C = 0.044715


def _params(sem=None, vmem=VMEM_LIMIT):
    return pltpu.CompilerParams(dimension_semantics=sem, vmem_limit_bytes=vmem)


def _tile(dim, pref):
    if dim <= pref:
        return dim
    t = 1 << (pref.bit_length() - 1)
    while dim % t:
        t //= 2
    return t


ROW_BLOCK_BYTES = 2 << 20


def _row_tile(rows, cols):
    return _tile(rows, max(ROW_ALIGN, ROW_BLOCK_BYTES // (4 * cols)))


def _sigmoid(x):
    return 1.0 / (1.0 + jnp.exp(-x))


def _gelu_gate(x):
    x2 = x * x
    return _sigmoid((2.0 * GELU_K) * x * (1.0 + GELU_C * x2)), x2


def _gelu(x):
    return 0.5 * x * (1.0 + jnp.tanh(GELU_K * (x + GELU_C * x * x * x)))


def _gelu_and_grad(x):
    s, x2 = _gelu_gate(x)
    return x * s, s + x * (s * (1.0 - s)) * ((2.0 * GELU_K) * (1.0 + (3.0 * GELU_C) * x2))


def _ln_stats(x):
    mu = jnp.mean(x, axis=-1, keepdims=True)
    d = x - mu
    var = jnp.mean(d * d, axis=-1, keepdims=True)
    rstd = lax.rsqrt(var + LN_EPS)
    return d * rstd, rstd


def _ln_bwd(dy_hat, xhat, rstd):
    m1 = jnp.mean(dy_hat, axis=-1, keepdims=True)
    m2 = jnp.mean(dy_hat * xhat, axis=-1, keepdims=True)
    return rstd * (dy_hat - m1 - xhat * m2)


def _rowsum(x):
    return jnp.sum(x, axis=0, keepdims=True)


def _mm(name, a, b, *, dims, grid, a_spec, b_spec, out_shape, out_specs, acc_shape,
        extra=(), extra_specs=(), epilogue=None, dep=None):
    nk = grid[2]
    n_extra = len(extra)
    single = not isinstance(out_shape, (list, tuple))
    shapes = [out_shape] if single else list(out_shape)
    specs = [out_specs] if single else list(out_specs)
    n_out = len(shapes)
    deps = [] if dep is None else [dep]
    first_out = 2 + n_extra + len(deps)

    def body(*refs):
        a_ref, b_ref = refs[0], refs[1]
        ex = refs[2:2 + n_extra]
        outs = refs[first_out:first_out + n_out]

        def finish(acc):
            if epilogue is None:
                outs[0][...] = acc.astype(outs[0].dtype)
            else:
                epilogue(acc, ex, outs)

        part = lax.dot_general(a_ref[...].astype(BF16), b_ref[...].astype(BF16), dims, preferred_element_type=F32)
        if nk == 1:
            finish(part)
            return
        acc_ref = refs[-1]
        k = pl.program_id(2)

        @pl.when(k == 0)
        def _():
            acc_ref[...] = part

        @pl.when(jnp.logical_and(k > 0, k < nk - 1))
        def _():
            acc_ref[...] += part

        @pl.when(k == nk - 1)
        def _():
            finish(acc_ref[...] + part)

    res = pl.pallas_call(
        body, name=name, grid=grid, in_specs=[a_spec, b_spec, *extra_specs] + [ANY] * len(deps), out_specs=specs,
        out_shape=shapes, scratch_shapes=[pltpu.VMEM(acc_shape, F32)] if nk > 1 else [],
        compiler_params=_params(("parallel", "parallel", "arbitrary")),
    )(a, b, *extra, *deps)
    return res[0] if single else res


NN = (((1,), (0,)), ((), ()))
NT = (((1,), (1,)), ((), ()))
TN = (((0,), (0,)), ((), ()))


def _mm_in(name, hb, wg, layer, dep=None):
    s, d = hb.shape
    nsh = wg.shape[3]
    n = N_CHIPS * nsh
    bm, bn, bk = _tile(s, 1024), _tile(nsh, 1024), _tile(d, 2048)
    per = nsh // bn
    return _mm(name, hb, wg, dims=NN, grid=(s // bm, n // bn, d // bk),
               a_spec=pl.BlockSpec((bm, bk), lambda m, j, k: (m, k)),
               b_spec=pl.BlockSpec((None, None, bk, bn), lambda m, j, k: (j // per, layer, k, j % per)),
               out_shape=jax.ShapeDtypeStruct((s, n), F32),
               out_specs=pl.BlockSpec((bm, bn), lambda m, j, k: (m, j)),
               acc_shape=(bm, bn), dep=dep)


def _mm_out_ln(name, yg, wg, layer, res, gp, bp, g, b):
    s, e = yg.shape
    esh, d = wg.shape[2], wg.shape[3]
    bm, bk = _tile(s, 512), _tile(esh, 1024)
    per = esh // bk

    def epilogue(acc, ex, outs):
        res_ref, gp_ref, bp_ref, g_ref, b_ref = ex
        xhat_ref, hb_ref, rstd_ref = outs
        r = ALPHA * (res_ref[...] * gp_ref[...] + bp_ref[...]) + acc
        xhat, rstd = _ln_stats(r)
        xhat_ref[...] = xhat
        hb_ref[...] = (xhat * g_ref[...] + b_ref[...]).astype(BF16)
        rstd_ref[...] = jnp.broadcast_to(rstd, rstd_ref.shape)

    vec = pl.BlockSpec((1, d), lambda m, j, k: (0, 0))
    row = pl.BlockSpec((bm, d), lambda m, j, k: (m, 0))
    return _mm(name, yg, wg, dims=NN, grid=(s // bm, 1, e // bk),
               a_spec=pl.BlockSpec((bm, bk), lambda m, j, k: (m, k)),
               b_spec=pl.BlockSpec((None, None, bk, d), lambda m, j, k: (k // per, layer, k % per, 0)),
               extra=(res, gp, bp, g, b), extra_specs=(row, vec, vec, vec, vec),
               out_shape=(jax.ShapeDtypeStruct((s, d), F32), jax.ShapeDtypeStruct((s, d), BF16),
                          jax.ShapeDtypeStruct((s, LANES), F32)),
               out_specs=(row, row, pl.BlockSpec((bm, LANES), lambda m, j, k: (m, 0))),
               acc_shape=(bm, d), epilogue=epilogue)


def _mm_dyg(name, drb, wg, layer, dep=None):
    s, d = drb.shape
    esh = wg.shape[2]
    e = N_CHIPS * esh
    bm, bn, bk = _tile(s, 1024), _tile(esh, 1024), _tile(d, 2048)
    per = esh // bn
    return _mm(name, drb, wg, dims=NT, grid=(s // bm, e // bn, d // bk),
               a_spec=pl.BlockSpec((bm, bk), lambda m, j, k: (m, k)),
               b_spec=pl.BlockSpec((None, None, bn, bk), lambda m, j, k: (j // per, layer, j % per, k)),
               out_shape=jax.ShapeDtypeStruct((s, e), F32),
               out_specs=pl.BlockSpec((bm, bn), lambda m, j, k: (m, j)),
               acc_shape=(bm, bn), dep=dep)


def _mm_dw_out(name, yg, drb, dep=None):
    s, e = yg.shape
    d = drb.shape[1]
    bm, bn, bk = _tile(e, 1024), _tile(d, 1024), _tile(s, 4096)
    return _mm(name, yg, drb, dims=TN, grid=(e // bm, d // bn, s // bk),
               a_spec=pl.BlockSpec((bk, bm), lambda m, j, k: (k, m)),
               b_spec=pl.BlockSpec((bk, bn), lambda m, j, k: (k, j)),
               out_shape=jax.ShapeDtypeStruct((e, d), BF16),
               out_specs=pl.BlockSpec((bm, bn), lambda m, j, k: (m, j)),
               acc_shape=(bm, bn), dep=dep)


def _mm_dh(name, dproj, wg, layer, dr, dep=None):
    s, n = dproj.shape
    d, nsh = wg.shape[2], wg.shape[3]
    bm, bn = _tile(s, 1024), _tile(d, 1024)
    bk = max(t for t in range(LANES, 3073, LANES) if nsh % t == 0)
    per = nsh // bk

    def epilogue(acc, ex, outs):
        outs[0][...] = ALPHA * ex[0][...] + acc

    blk = pl.BlockSpec((bm, bn), lambda m, j, k: (m, j))
    return _mm(name, dproj, wg, dims=NT, grid=(s // bm, d // bn, n // bk),
               a_spec=pl.BlockSpec((bm, bk), lambda m, j, k: (m, k)),
               b_spec=pl.BlockSpec((None, None, bn, bk), lambda m, j, k: (k // per, layer, j, k % per)),
               extra=(dr,), extra_specs=(blk,),
               out_shape=jax.ShapeDtypeStruct((s, d), F32), out_specs=blk,
               acc_shape=(bm, bn), epilogue=epilogue, dep=dep)


def _mm_dw_in(name, hb, dproj):
    s, d = hb.shape
    n = dproj.shape[1]
    nsh = n // N_CHIPS
    bm, bn, bk = _tile(d, 1024), _tile(nsh, 1024), _tile(s, 4096)
    per = nsh // bn
    return _mm(name, hb, dproj, dims=TN, grid=(d // bm, n // bn, s // bk),
               a_spec=pl.BlockSpec((bk, bm), lambda m, j, k: (k, m)),
               b_spec=pl.BlockSpec((bk, bn), lambda m, j, k: (k, j)),
               out_shape=jax.ShapeDtypeStruct((N_CHIPS, d, nsh), BF16),
               out_specs=pl.BlockSpec((None, bm, bn), lambda m, j, k: (j // per, m, j % per)),
               acc_shape=(bm, bn))


def _causal_mask():
    r = lax.broadcasted_iota(jnp.int32, (CHUNK, CHUNK), 0)
    c = lax.broadcasted_iota(jnp.int32, (CHUNK, CHUNK), 1)
    return r >= c


def _sgu_fwd(name, proj, ln_g, ln_b, w_s, b_st):
    s, n3 = proj.shape
    e = n3 // 3
    hd = e // SGU_HEADS

    def body(proj_ref, g_ref, b_ref, ws_ref, bst_ref, yg_ref):
        xhat, _ = _ln_stats(_gelu(proj_ref[:, e:2 * e]))
        vnb = (xhat * g_ref[...] + b_ref[...]).astype(BF16)
        tri = _causal_mask()
        for h in range(SGU_HEADS):
            hs = slice(h * hd, (h + 1) * hd)
            ws = jnp.where(tri, ws_ref[h], 0.0).astype(BF16)
            mixed = jnp.dot(ws, vnb[:, hs], preferred_element_type=F32) + bst_ref[:, h:h + 1]
            u = proj_ref[:, hs]
            z = proj_ref[:, 2 * e + h * hd:2 * e + (h + 1) * hd]
            yg_ref[:, hs] = (_gelu(u) * mixed * (z * _sigmoid(z))).astype(BF16)

    vec = pl.BlockSpec((1, e), lambda i: (0, 0))
    return pl.pallas_call(
        body, name=name, grid=(s // CHUNK,),
        in_specs=[pl.BlockSpec((CHUNK, n3), lambda i: (i, 0)), vec, vec,
                  pl.BlockSpec((SGU_HEADS, CHUNK, CHUNK), lambda i: (0, 0, 0)),
                  pl.BlockSpec((CHUNK, SGU_HEADS), lambda i: (0, 0))],
        out_specs=pl.BlockSpec((CHUNK, e), lambda i: (i, 0)),
        out_shape=jax.ShapeDtypeStruct((s, e), BF16),
        compiler_params=_params(("parallel",)),
    )(proj, ln_g, ln_b, w_s, b_st)


def _sgu_bwd(name, dyg, proj, ln_g, ln_b, w_s, b_st):
    s, n3 = proj.shape
    e = n3 // 3
    hd = e // SGU_HEADS

    def body(dyg_ref, proj_ref, g_ref, b_ref, ws_ref, bst_ref,
             dproj_ref, dg_ref, db_ref, dws_ref, dbs_ref, dvn_ref):
        @pl.when(pl.program_id(0) == 0)
        def _():
            dg_ref[...] = jnp.zeros_like(dg_ref)
            db_ref[...] = jnp.zeros_like(db_ref)
            dws_ref[...] = jnp.zeros_like(dws_ref)
            dbs_ref[...] = jnp.zeros_like(dbs_ref)

        gv, gv_grad = _gelu_and_grad(proj_ref[:, e:2 * e])
        xhat, rstd = _ln_stats(gv)
        g = g_ref[...]
        vnb = (xhat * g + b_ref[...]).astype(BF16)
        tri = _causal_mask()
        for h in range(SGU_HEADS):
            hs = slice(h * hd, (h + 1) * hd)
            zs = slice(2 * e + h * hd, 2 * e + (h + 1) * hd)
            ws = jnp.where(tri, ws_ref[h], 0.0).astype(BF16)
            mixed = jnp.dot(ws, vnb[:, hs], preferred_element_type=F32) + bst_ref[:, h:h + 1]
            u = proj_ref[:, hs]
            z = proj_ref[:, zs]
            gu, gu_grad = _gelu_and_grad(u)
            sig = _sigmoid(z)
            dy = dyg_ref[:, hs]
            t = dy * (z * sig)
            dmixed = t * gu
            dproj_ref[:, hs] = (t * mixed * gu_grad).astype(BF16)
            dproj_ref[:, zs] = (dy * gu * mixed * (sig * (1.0 + z * (1.0 - sig)))).astype(BF16)
            dmb = dmixed.astype(BF16)
            dvn_ref[:, hs] = lax.dot_general(ws, dmb, TN, preferred_element_type=F32)
            dws = lax.dot_general(dmb, vnb[:, hs], NT, preferred_element_type=F32)
            dws_ref[h] += jnp.where(tri, dws, 0.0)
            dbs_ref[:, h:h + 1] += jnp.sum(dmixed, axis=-1, keepdims=True)
        dvn = dvn_ref[...]
        dg_ref[...] += _rowsum(dvn * xhat)
        db_ref[...] += _rowsum(dvn)
        dgv = _ln_bwd(dvn * g, xhat, rstd)
        dproj_ref[:, e:2 * e] = (dgv * gv_grad).astype(BF16)

    vec = pl.BlockSpec((1, e), lambda i: (0, 0))
    wsp = pl.BlockSpec((SGU_HEADS, CHUNK, CHUNK), lambda i: (0, 0, 0))
    bsp = pl.BlockSpec((CHUNK, SGU_HEADS), lambda i: (0, 0))
    return pl.pallas_call(
        body, name=name, grid=(s // CHUNK,),
        in_specs=[pl.BlockSpec((CHUNK, e), lambda i: (i, 0)), pl.BlockSpec((CHUNK, n3), lambda i: (i, 0)),
                  vec, vec, wsp, bsp],
        out_specs=[pl.BlockSpec((CHUNK, n3), lambda i: (i, 0)), vec, vec, wsp, bsp],
        out_shape=[jax.ShapeDtypeStruct((s, n3), BF16), jax.ShapeDtypeStruct((1, e), F32),
                   jax.ShapeDtypeStruct((1, e), F32),
                   jax.ShapeDtypeStruct((SGU_HEADS, CHUNK, CHUNK), F32),
                   jax.ShapeDtypeStruct((CHUNK, SGU_HEADS), F32)],
        scratch_shapes=[pltpu.VMEM((CHUNK, e), F32)],
        compiler_params=_params(("arbitrary",)),
    )(dyg, proj, ln_g, ln_b, w_s, b_st)


def _pool_counts(pos, w):
    return jnp.minimum(pos + 1, w).astype(F32)


def _pool_fwd(name, proj):
    s, n2 = proj.shape
    e = n2 // 2
    gd = e // len(POOL_WINDOWS)
    ts = _row_tile(s, e)
    hb = ts // POOL_HALO

    def body(v_ref, halo_ref, p_ref, ext):
        i = pl.program_id(0)
        ext[POOL_HALO:, :] = v_ref[...]
        ext[:POOL_HALO, :] = jnp.where(i > 0, halo_ref[...], 0.0)
        pos = i * ts + lax.broadcasted_iota(jnp.int32, (ts, 1), 0)
        for g, w in enumerate(POOL_WINDOWS):
            cs = slice(g * gd, (g + 1) * gd)
            acc = ext[POOL_HALO:, cs]
            for j in range(1, w):
                acc = acc + ext[POOL_HALO - j:POOL_HALO - j + ts, cs]
            p_ref[:, cs] = (acc / _pool_counts(pos, w) - v_ref[:, cs]).astype(BF16)

    return pl.pallas_call(
        body, name=name, grid=(s // ts,),
        in_specs=[pl.BlockSpec((ts, e), lambda i: (i, 0)),
                  pl.BlockSpec((POOL_HALO, e), lambda i: (jnp.maximum(i * hb - 1, 0), 0))],
        out_specs=pl.BlockSpec((ts, e), lambda i: (i, 0)),
        out_shape=jax.ShapeDtypeStruct((s, e), BF16),
        scratch_shapes=[pltpu.VMEM((ts + POOL_HALO, e), F32)],
        compiler_params=_params(("parallel",)),
    )(proj, proj)


def _pool_mm(name, p, wg, proj, scale):
    s, e = p.shape
    ng, rsh, gd = wg.shape[1], wg.shape[2], wg.shape[3]
    bm = _tile(s, 1024)

    def body(p_ref, w0, w1, w2, w3, z_ref, sc_ref, q_ref, yg_ref):
        q = None
        for j, w_ref in enumerate((w0, w1, w2, w3)):
            part = jnp.dot(p_ref[:, j * rsh:(j + 1) * rsh], w_ref[...], preferred_element_type=F32)
            q = part if q is None else q + part
        z = z_ref[...]
        q_ref[...] = q
        yg_ref[...] = (q * sc_ref[...] * (z * _sigmoid(z))).astype(BF16)

    blk = pl.BlockSpec((bm, gd), lambda m, g: (m, g))
    w_specs = [pl.BlockSpec((None, None, rsh, gd), functools.partial(lambda m, g, j: (j, g, 0, 0), j=j))
               for j in range(N_CHIPS)]
    return pl.pallas_call(
        body, name=name, grid=(s // bm, ng),
        in_specs=[blk] + w_specs + [pl.BlockSpec((bm, gd), lambda m, g: (m, ng + g)),
                                    pl.BlockSpec((1, gd), lambda m, g: (0, g))],
        out_specs=[blk, blk],
        out_shape=[jax.ShapeDtypeStruct((s, e), F32), jax.ShapeDtypeStruct((s, e), BF16)],
        compiler_params=_params(("parallel", "parallel")),
    )(p, wg, wg, wg, wg, proj, scale)


def _pool_bwd1(name, dyg, q, proj, scale):
    s, e = q.shape
    ts = _row_tile(s, e)

    def body(dyg_ref, q_ref, z_ref, sc_ref, dq_ref, dz_ref, dsc_ref):
        @pl.when(pl.program_id(0) == 0)
        def _():
            dsc_ref[...] = jnp.zeros_like(dsc_ref)

        z = z_ref[...]
        sig = _sigmoid(z)
        dy = dyg_ref[...]
        qv = q_ref[...]
        sc = sc_ref[...]
        dp2 = dy * (z * sig)
        dsc_ref[...] += _rowsum(dp2 * qv)
        dq_ref[...] = (dp2 * sc).astype(BF16)
        dz_ref[...] = (dy * (qv * sc) * (sig * (1.0 + z * (1.0 - sig)))).astype(BF16)

    blk = pl.BlockSpec((ts, e), lambda i: (i, 0))
    vec = pl.BlockSpec((1, e), lambda i: (0, 0))
    return pl.pallas_call(
        body, name=name, grid=(s // ts,),
        in_specs=[blk, blk, pl.BlockSpec((ts, e), lambda i: (i, 1)), vec],
        out_specs=[blk, pl.BlockSpec((ts, e), lambda i: (i, 1)), vec],
        out_shape=[jax.ShapeDtypeStruct((s, e), BF16), jax.ShapeDtypeStruct((s, 2 * e), BF16),
                   jax.ShapeDtypeStruct((1, e), F32)],
        compiler_params=_params(("arbitrary",)),
    )(dyg, q, proj, scale)


def _pool_mm_bwd(name, dq, wg):
    s, e = dq.shape
    ng, rsh, gd = wg.shape[1], wg.shape[2], wg.shape[3]
    bm = _tile(s, 1024)
    return _mm(name, dq, wg, dims=NT, grid=(s // bm, ng * N_CHIPS, 1),
               a_spec=pl.BlockSpec((bm, gd), lambda m, j, k: (m, j // N_CHIPS)),
               b_spec=pl.BlockSpec((None, None, rsh, gd), lambda m, j, k: (j % N_CHIPS, j // N_CHIPS, 0, 0)),
               out_shape=jax.ShapeDtypeStruct((s, e), F32),
               out_specs=pl.BlockSpec((bm, rsh), lambda m, j, k: (m, j)),
               acc_shape=(bm, rsh))


def _pool_bwd2(name, dp, dproj):
    s, e = dp.shape
    gd = e // len(POOL_WINDOWS)
    ts = _row_tile(s, e)
    n = s // ts
    hb = ts // POOL_HALO

    def body(dp_ref, nxt_ref, _, dv_ref, ext):
        i = pl.program_id(0)
        pos = i * ts + lax.broadcasted_iota(jnp.int32, (ts + POOL_HALO, 1), 0)
        for g, w in enumerate(POOL_WINDOWS):
            cs = slice(g * gd, (g + 1) * gd)
            cnt = _pool_counts(pos, w)
            ext[:ts, cs] = dp_ref[:, cs] / cnt[:ts]
            ext[ts:, cs] = jnp.where(i < n - 1, nxt_ref[:, cs] / cnt[ts:], 0.0)
        for g, w in enumerate(POOL_WINDOWS):
            cs = slice(g * gd, (g + 1) * gd)
            acc = ext[:ts, cs]
            for j in range(1, w):
                acc = acc + ext[j:j + ts, cs]
            dv_ref[:, cs] = (acc - dp_ref[:, cs]).astype(BF16)

    return pl.pallas_call(
        body, name=name, grid=(n,),
        in_specs=[pl.BlockSpec((ts, e), lambda i: (i, 0)),
                  pl.BlockSpec((POOL_HALO, e), lambda i: (jnp.minimum((i + 1) * hb, s // POOL_HALO - 1), 0)),
                  pl.BlockSpec(memory_space=pl.ANY)],
        out_specs=pl.BlockSpec((ts, e), lambda i: (i, 0)),
        out_shape=jax.ShapeDtypeStruct((s, 2 * e), BF16),
        scratch_shapes=[pltpu.VMEM((ts + POOL_HALO, e), F32)],
        input_output_aliases={2: 0},
        compiler_params=_params(("parallel",)),
    )(dp, dp, dproj)


def _mm_dw_pool(name, p, dq, ng):
    s, e = p.shape
    gd = e // ng
    rsh = gd // N_CHIPS
    bk = _tile(s, 4096)
    return _mm(name, p, dq, dims=TN, grid=(N_CHIPS, ng, s // bk),
               a_spec=pl.BlockSpec((bk, rsh), lambda m, g, k: (k, g * N_CHIPS + m)),
               b_spec=pl.BlockSpec((bk, gd), lambda m, g, k: (k, g)),
               out_shape=jax.ShapeDtypeStruct((N_CHIPS, ng, rsh, gd), BF16),
               out_specs=pl.BlockSpec((None, None, rsh, gd), lambda m, g, k: (m, g, 0, 0)),
               acc_shape=(rsh, gd))


def _conv_fwd(name, proj, conv_w, conv_b):
    s, n3 = proj.shape
    e = n3 // 3
    ts, tc = _tile(s, 512), LANES
    ne = e // tc
    hb = ts // CONV_HALO
    off = CONV_HALO - (CONV_WIDTH - 1)

    def body(a_ref, gl_ref, ha_ref, hgl_ref, w_ref, b_ref, c_ref, gext):
        i = pl.program_id(1)
        gext[CONV_HALO:, :] = a_ref[...] * _sigmoid(gl_ref[...])
        gext[:CONV_HALO, :] = jnp.where(i > 0, ha_ref[...] * _sigmoid(hgl_ref[...]), 0.0)
        taps = [jnp.broadcast_to(w_ref[k:k + 1, :], (SUBLANES, LANES)) for k in range(CONV_WIDTH)]
        bias = jnp.broadcast_to(b_ref[...], (SUBLANES, LANES))

        def sub_block(rb, carry):
            r0 = pl.multiple_of(rb * CONV_SUB, CONV_SUB)
            acc = [bias] * CONV_STRIDE
            for m in range(CONV_WIDTH + CONV_STRIDE - 1):
                g_m = gext[pl.ds(r0 + off + m, SUBLANES, stride=CONV_STRIDE), :]
                for q in range(CONV_STRIDE):
                    if 0 <= m - q < CONV_WIDTH:
                        acc[q] = acc[q] + taps[m - q] * g_m
            for q in range(CONV_STRIDE):
                c_ref[pl.ds(r0 + q, SUBLANES, stride=CONV_STRIDE), :] = acc[q]
            return carry

        lax.fori_loop(0, ts // CONV_SUB, sub_block, 0, unroll=2)

    halo = lambda i: jnp.maximum(i * hb - 1, 0)
    return pl.pallas_call(
        body, name=name, grid=(ne, s // ts),
        in_specs=[pl.BlockSpec((ts, tc), lambda j, i: (i, j)),
                  pl.BlockSpec((ts, tc), lambda j, i: (i, ne + j)),
                  pl.BlockSpec((CONV_HALO, tc), lambda j, i: (halo(i), j)),
                  pl.BlockSpec((CONV_HALO, tc), lambda j, i: (halo(i), ne + j)),
                  pl.BlockSpec((CONV_WIDTH, tc), lambda j, i: (0, j)),
                  pl.BlockSpec((1, tc), lambda j, i: (0, j))],
        out_specs=pl.BlockSpec((ts, tc), lambda j, i: (i, j)),
        out_shape=jax.ShapeDtypeStruct((s, e), F32),
        scratch_shapes=[pltpu.VMEM((ts + CONV_HALO, tc), F32)],
        compiler_params=_params(("parallel", "parallel")),
    )(proj, proj, proj, proj, conv_w, conv_b)


def _conv_gate_fwd(name, c, proj, ln_g, ln_b):
    s, e = c.shape
    ts = _row_tile(s, e)

    def body(c_ref, z_ref, g_ref, b_ref, yg_ref):
        xhat, _ = _ln_stats(c_ref[...])
        ln = xhat * g_ref[...] + b_ref[...]
        z = z_ref[...]
        yg_ref[...] = ((ln * _sigmoid(ln)) * (z * _sigmoid(z))).astype(BF16)

    blk = pl.BlockSpec((ts, e), lambda i: (i, 0))
    vec = pl.BlockSpec((1, e), lambda i: (0, 0))
    return pl.pallas_call(
        body, name=name, grid=(s // ts,),
        in_specs=[blk, pl.BlockSpec((ts, e), lambda i: (i, 2)), vec, vec],
        out_specs=blk, out_shape=jax.ShapeDtypeStruct((s, e), BF16),
        compiler_params=_params(("parallel",)),
    )(c, proj, ln_g, ln_b)


def _conv_bwd1(name, dyg, c, proj, ln_g, ln_b):
    s, e = c.shape
    ts = _row_tile(s, e)

    def body(dyg_ref, c_ref, z_ref, g_ref, b_ref, dc_ref, dz_ref, dg_ref, db_ref, dcb_ref):
        @pl.when(pl.program_id(0) == 0)
        def _():
            dg_ref[...] = jnp.zeros_like(dg_ref)
            db_ref[...] = jnp.zeros_like(db_ref)
            dcb_ref[...] = jnp.zeros_like(dcb_ref)

        xhat, rstd = _ln_stats(c_ref[...])
        g = g_ref[...]
        ln = xhat * g + b_ref[...]
        sl = _sigmoid(ln)
        z = z_ref[...]
        sz = _sigmoid(z)
        dy = dyg_ref[...]
        dz_ref[...] = (dy * (ln * sl) * (sz * (1.0 + z * (1.0 - sz)))).astype(BF16)
        dln = dy * (z * sz) * (sl * (1.0 + ln * (1.0 - sl)))
        dg_ref[...] += _rowsum(dln * xhat)
        db_ref[...] += _rowsum(dln)
        dc = _ln_bwd(dln * g, xhat, rstd)
        dc_ref[...] = dc
        dcb_ref[...] += _rowsum(dc)

    blk = pl.BlockSpec((ts, e), lambda i: (i, 0))
    zblk = pl.BlockSpec((ts, e), lambda i: (i, 2))
    vec = pl.BlockSpec((1, e), lambda i: (0, 0))
    vshape = jax.ShapeDtypeStruct((1, e), F32)
    return pl.pallas_call(
        body, name=name, grid=(s // ts,),
        in_specs=[blk, blk, zblk, vec, vec],
        out_specs=[blk, zblk, vec, vec, vec],
        out_shape=[jax.ShapeDtypeStruct((s, e), F32), jax.ShapeDtypeStruct((s, 3 * e), BF16),
                   vshape, vshape, vshape],
        compiler_params=_params(("arbitrary",)),
    )(dyg, c, proj, ln_g, ln_b)


def _conv_bwd2(name, dc, proj, conv_w):
    s, e = dc.shape
    ts, tc = _tile(s, 512), LANES
    ne = e // tc
    n = s // ts
    hb = ts // CONV_HALO
    off = CONV_HALO - (CONV_WIDTH - 1)
    span = CONV_WIDTH + CONV_STRIDE - 1

    def body(dc_ref, nxt_ref, a_ref, gl_ref, ha_ref, hgl_ref, w_ref, dg_ref, dw_ref, gext, dcext, dwacc, taps):
        i = pl.program_id(1)

        @pl.when(i == 0)
        def _():
            dwacc[...] = jnp.zeros_like(dwacc)

        gext[CONV_HALO:, :] = a_ref[...] * _sigmoid(gl_ref[...])
        gext[:CONV_HALO, :] = jnp.where(i > 0, ha_ref[...] * _sigmoid(hgl_ref[...]), 0.0)
        dcext[:ts, :] = dc_ref[...]
        dcext[ts:, :] = jnp.where(i < n - 1, nxt_ref[...], 0.0)
        for k in range(CONV_WIDTH):
            taps[k] = jnp.broadcast_to(w_ref[k:k + 1, :], (SUBLANES, LANES))

        def sub_block(rb, carry):
            r0 = pl.multiple_of(rb * CONV_SUB, CONV_SUB)
            dg = [jnp.zeros((SUBLANES, LANES), F32)] * CONV_STRIDE
            for m in range(span):
                dc_m = dcext[pl.ds(r0 + m, SUBLANES, stride=CONV_STRIDE), :]
                for q in range(CONV_STRIDE):
                    k = CONV_WIDTH - 1 - m + q
                    if 0 <= k < CONV_WIDTH:
                        dg[q] = dg[q] + taps[k] * dc_m
            for q in range(CONV_STRIDE):
                dg_ref[pl.ds(r0 + q, SUBLANES, stride=CONV_STRIDE), :] = dg[q]
            dc_q = [dcext[pl.ds(r0 + q, SUBLANES, stride=CONV_STRIDE), :] for q in range(CONV_STRIDE)]
            win = [gext[pl.ds(r0 + off + m, SUBLANES, stride=CONV_STRIDE), :] for m in range(CONV_STRIDE - 1)]
            for k in range(CONV_WIDTH):
                win.append(gext[pl.ds(r0 + off + k + CONV_STRIDE - 1, SUBLANES, stride=CONV_STRIDE), :])
                prods = [dc_q[q] * win[q] for q in range(CONV_STRIDE)]
                while len(prods) > 1:
                    prods = [a + b for a, b in zip(prods[::2], prods[1::2])]
                dwacc[k] += prods[0]
                win.pop(0)
            return carry

        lax.fori_loop(0, ts // CONV_SUB, sub_block, 0, unroll=2)

        @pl.when(i == n - 1)
        def _():
            dw_ref[...] = jnp.sum(dwacc[...], axis=1)

    halo = lambda i: jnp.maximum(i * hb - 1, 0)
    nxt = lambda i: jnp.minimum((i + 1) * hb, s // CONV_HALO - 1)
    return pl.pallas_call(
        body, name=name, grid=(ne, n),
        in_specs=[pl.BlockSpec((ts, tc), lambda j, i: (i, j)),
                  pl.BlockSpec((CONV_HALO, tc), lambda j, i: (nxt(i), j)),
                  pl.BlockSpec((ts, tc), lambda j, i: (i, j)),
                  pl.BlockSpec((ts, tc), lambda j, i: (i, ne + j)),
                  pl.BlockSpec((CONV_HALO, tc), lambda j, i: (halo(i), j)),
                  pl.BlockSpec((CONV_HALO, tc), lambda j, i: (halo(i), ne + j)),
                  pl.BlockSpec((CONV_WIDTH, tc), lambda j, i: (0, j))],
        out_specs=[pl.BlockSpec((ts, tc), lambda j, i: (i, j)),
                   pl.BlockSpec((CONV_HALO, tc), lambda j, i: (0, j))],
        out_shape=[jax.ShapeDtypeStruct((s, e), F32), jax.ShapeDtypeStruct((CONV_HALO, e), F32)],
        scratch_shapes=[pltpu.VMEM((ts + CONV_HALO, tc), F32), pltpu.VMEM((ts + CONV_HALO, tc), F32),
                        pltpu.VMEM((CONV_HALO, SUBLANES, LANES), F32), pltpu.VMEM((CONV_HALO, SUBLANES, LANES), F32)],
        compiler_params=_params(("parallel", "arbitrary")),
    )(dc, dc, proj, proj, proj, proj, conv_w)


def _conv_bwd3(name, dg, proj, dproj):
    s, e = dg.shape
    ts = _row_tile(s, 2 * e)

    def body(dg_ref, agl_ref, _, out_ref):
        a = agl_ref[:, :e]
        sig = _sigmoid(agl_ref[:, e:])
        dgv = dg_ref[...]
        out_ref[:, :e] = (dgv * sig).astype(BF16)
        out_ref[:, e:] = (dgv * a * sig * (1.0 - sig)).astype(BF16)

    wide = pl.BlockSpec((ts, 2 * e), lambda i: (i, 0))
    return pl.pallas_call(
        body, name=name, grid=(s // ts,),
        in_specs=[pl.BlockSpec((ts, e), lambda i: (i, 0)), wide, pl.BlockSpec(memory_space=pl.ANY)],
        out_specs=wide, out_shape=jax.ShapeDtypeStruct((s, 3 * e), BF16),
        input_output_aliases={2: 0},
        compiler_params=_params(("parallel",)),
    )(dg, proj, dproj)


def _post_ln_bwd(name, dy_or_target, xhat, rstd, g, b, loss_head, dep=None):
    s, d = xhat.shape
    ts = _row_tile(s, d)

    def body(dy_ref, xhat_ref, rstd_ref, g_ref, b_ref, *rest):
        dr_ref, drb_ref, dg_ref, db_ref, loss_ref = rest[-5:]

        @pl.when(pl.program_id(0) == 0)
        def _():
            dg_ref[...] = jnp.zeros_like(dg_ref)
            db_ref[...] = jnp.zeros_like(db_ref)
            loss_ref[...] = jnp.zeros_like(loss_ref)

        xh = xhat_ref[...]
        gv = g_ref[...]
        if loss_head:
            err = (xh * gv + b_ref[...]) - dy_ref[...]
            per_row = jnp.mean(err * err, axis=-1, keepdims=True)
            loss_ref[...] += 0.5 * jnp.broadcast_to(_rowsum(per_row), loss_ref.shape)
            dy = err / d
        else:
            dy = dy_ref[...]
        dg_ref[...] += _rowsum(dy * xh)
        db_ref[...] += _rowsum(dy)
        dr = _ln_bwd(dy * gv, xh, rstd_ref[:, 0:1])
        dr_ref[...] = dr
        drb_ref[...] = dr.astype(BF16)

    blk = pl.BlockSpec((ts, d), lambda i: (i, 0))
    vec = pl.BlockSpec((1, d), lambda i: (0, 0))
    deps = [] if dep is None else [dep]
    return pl.pallas_call(
        body, name=name, grid=(s // ts,),
        in_specs=[blk, blk, pl.BlockSpec((ts, LANES), lambda i: (i, 0)), vec, vec] + [ANY] * len(deps),
        out_specs=[blk, blk, vec, vec, pl.BlockSpec((1, LANES), lambda i: (0, 0))],
        out_shape=[jax.ShapeDtypeStruct((s, d), F32), jax.ShapeDtypeStruct((s, d), BF16),
                   jax.ShapeDtypeStruct((1, d), F32), jax.ShapeDtypeStruct((1, d), F32),
                   jax.ShapeDtypeStruct((1, LANES), F32)],
        compiler_params=_params(("arbitrary",)),
    )(dy_or_target, xhat, rstd, g, b, *deps)


def _as2d(a):
    return a.reshape(-1, a.shape[-1])


def _cast_layer(name, a, layer, slabs=0):
    _, r, c = a.shape
    tr = _row_tile(r, c)

    def body(a_ref, o_ref):
        o_ref[...] = a_ref[...].astype(BF16)

    if slabs:
        out_spec = pl.BlockSpec((None, tr, c), lambda i: (_my_shard(), i, 0))
        out_shape = jax.ShapeDtypeStruct((slabs, r, c), BF16)
    else:
        out_spec = pl.BlockSpec((tr, c), lambda i: (i, 0))
        out_shape = jax.ShapeDtypeStruct((r, c), BF16)
    return pl.pallas_call(body, name=name, grid=(r // tr,),
                          in_specs=[pl.BlockSpec((None, tr, c), lambda i: (layer, i, 0))],
                          out_specs=out_spec, out_shape=out_shape,
                          compiler_params=_params(("parallel",)))(a)


def _my_shard():
    return 2 * lax.axis_index("x") + lax.axis_index("y")


def _peer_shard(k):
    x, y = lax.axis_index("x"), lax.axis_index("y")
    px = jnp.where(k == 1, x, 1 - x)
    py = jnp.where(k == 0, y, 1 - y)
    return 2 * px + py


def _pair_sum(name, own, recv):
    _, r, c = own.shape
    h = r // 2
    tr = _row_tile(h, c)
    nb = h // tr

    def body(own_ref, recv_ref, o_ref):
        o_ref[...] = (own_ref[...].astype(F32) + recv_ref[...].astype(F32)).astype(BF16)

    mine = pl.BlockSpec((None, tr, c), lambda k, i: (_peer_shard(k), lax.axis_index("c") * nb + i, 0))
    theirs = pl.BlockSpec((None, tr, c), lambda k, i: (_peer_shard(k), i, 0))
    return pl.pallas_call(body, name=name, grid=(N_CHIPS - 1, nb), in_specs=[mine, theirs],
                          out_specs=pl.BlockSpec((None, tr, c), lambda k, i: (k, i, 0)),
                          out_shape=jax.ShapeDtypeStruct((N_CHIPS - 1, h, c), BF16),
                          compiler_params=_params(("parallel", "parallel")))(own, recv)


def _final_sum(name, own, recv, got, layer, n_layers, prev):
    _, r, c = own.shape
    h = r // 2
    tr = _row_tile(h, c)
    nb = h // tr

    def body(own_ref, recv_ref, q0_ref, q1_ref, q2_ref, *rest):
        o_ref = rest[-1]
        acc = own_ref[...].astype(F32) + recv_ref[...].astype(F32)
        acc = acc + q0_ref[...].astype(F32)
        acc = acc + q1_ref[...].astype(F32)
        o_ref[...] = acc + q2_ref[...].astype(F32)

    in_specs = [pl.BlockSpec((None, tr, c), lambda i: (_my_shard(), lax.axis_index("c") * nb + i, 0)),
                pl.BlockSpec((None, tr, c), lambda i: (_my_shard(), i, 0))]
    in_specs += [pl.BlockSpec((None, tr, c), functools.partial(lambda i, k: (k, i, 0), k=k)) for k in range(3)]
    operands = [own, recv, got, got, got]
    aliases = {}
    if prev is not None:
        in_specs.append(pl.BlockSpec(memory_space=pl.ANY))
        operands.append(prev)
        aliases = {5: 0}
    return pl.pallas_call(body, name=name, grid=(nb,), in_specs=in_specs,
                          out_specs=pl.BlockSpec((None, tr, c), lambda i: (layer, lax.axis_index("c") * nb + i, 0)),
                          out_shape=jax.ShapeDtypeStruct((n_layers, r, c), F32),
                          input_output_aliases=aliases,
                          compiler_params=_params(("parallel",)))(*operands)


def _adam(name, w, g, m, v, emit_grad=False):
    r, c = w.shape
    tr = _row_tile(r, c)
    c1 = 1.0 / (1.0 - ADAM_B1 ** ADAM_STEP)
    c2 = 1.0 / (1.0 - ADAM_B2 ** ADAM_STEP)
    n_out = 4 if emit_grad else 3

    def body(w_ref, g_ref, m_ref, v_ref, d_ref, nm_ref, nv_ref, *g_out):
        gv = g_ref[...]
        if emit_grad:
            g_out[0][...] = gv
        nm = ADAM_B1 * m_ref[...] + (1.0 - ADAM_B1) * gv
        nv = ADAM_B2 * v_ref[...] + (1.0 - ADAM_B2) * (gv * gv)
        d_ref[...] = -ADAM_LR * ((nm * c1) / (jnp.sqrt(nv * c2) + ADAM_EPS) + ADAM_WD * w_ref[...])
        nm_ref[...] = nm
        nv_ref[...] = nv

    blk = pl.BlockSpec((tr, c), lambda i: (i, 0))
    shp = jax.ShapeDtypeStruct((r, c), F32)
    return pl.pallas_call(body, name=name, grid=(r // tr,), in_specs=[blk] * 4, out_specs=[blk] * n_out,
                          out_shape=[shp] * n_out, compiler_params=_params(("parallel",)))(w, g, m, v)


def _sum_devices(name, parts):
    _, r, c = parts.shape
    tr = _row_tile(r, N_DEV * c)

    def body(p_ref, o_ref):
        acc = p_ref[0]
        for k in range(1, N_DEV):
            acc = acc + p_ref[k]
        o_ref[...] = acc

    return pl.pallas_call(body, name=name, grid=(r // tr,),
                          in_specs=[pl.BlockSpec((N_DEV, tr, c), lambda i: (0, i, 0))],
                          out_specs=pl.BlockSpec((tr, c), lambda i: (i, 0)),
                          out_shape=jax.ShapeDtypeStruct((r, c), F32),
                          compiler_params=_params(("parallel",)))(parts)


def _coords():
    return lax.axis_index("x"), lax.axis_index("y"), lax.axis_index("c")


def _chip_peers(x, y):
    return [(1 - x, y, 2 * (1 - x) + y), (x, 1 - y, 2 * x + 1 - y), (1 - x, 1 - y, 2 * (1 - x) + 1 - y)]


def _remote(src, dst, ssem, rsem, dev):
    return pltpu.make_async_remote_copy(src_ref=src, dst_ref=dst, send_sem=ssem, recv_sem=rsem,
                                        device_id=dev, device_id_type=MESH)


ANY = pl.BlockSpec(memory_space=pl.ANY)


SEM = pl.BlockSpec(memory_space=pltpu.SEMAPHORE)
HBM = pl.BlockSpec(memory_space=pltpu.HBM)
DATAFLOW = pltpu.SideEffectType.DATAFLOW_SIDE_EFFECTING


def _hbm(a):
    return pltpu.with_memory_space_constraint(a, pltpu.HBM)


def _split_start(name, make_copies, n_copies, srcs, land_shapes, dep):
    n, nl = len(srcs), len(land_shapes)
    nb = n + nl

    def body(*refs):
        src_refs, land_refs = refs[:n], refs[n:nb]
        ssem, rsem = refs[nb + 1], refs[nb + 2]
        token = refs[-1]
        for send, _ in make_copies(src_refs, land_refs, ssem, rsem):
            send.start()
        token[...] = jnp.zeros_like(token)

    lands = [_hbm(lax.empty(shp, dt)) for shp, dt in land_shapes]
    out_shape = [pltpu.SemaphoreType.DMA((n_copies,)), pltpu.SemaphoreType.DMA((n_copies,))]
    out_shape += [pltpu.HBM(a.shape, a.dtype) for a in list(srcs) + lands]
    out_shape.append(jax.ShapeDtypeStruct((ROW_ALIGN, LANES), F32))
    res = pl.pallas_call(
        body, name=name, out_shape=out_shape,
        in_specs=[HBM] * nb + [ANY],
        out_specs=[SEM, SEM] + [HBM] * nb + [pl.BlockSpec(memory_space=pltpu.VMEM)],
        input_output_aliases={i: 2 + i for i in range(nb)},
        compiler_params=pltpu.CompilerParams(has_side_effects=DATAFLOW),
    )(*[_hbm(a) for a in srcs], *lands, dep)
    return res[0], res[1], res[2:2 + n], res[2 + n:2 + nb], res[-1]


def _split_wait(name, make_copies, ssem, rsem, srcs, lands, after):
    n, nb = len(srcs), len(srcs) + len(lands)

    def body(*refs):
        src_refs, land_refs = refs[:n], refs[n:nb]
        for send, recv in make_copies(src_refs, land_refs, refs[nb], refs[nb + 1]):
            send.wait_send()
            recv.wait_recv()

    res = pl.pallas_call(
        body, name=name, out_shape=[pltpu.HBM(a.shape, a.dtype) for a in list(srcs) + list(lands)],
        in_specs=[HBM] * nb + [SEM, SEM, ANY], out_specs=[HBM] * nb,
        input_output_aliases={i: i for i in range(nb)},
        compiler_params=pltpu.CompilerParams(has_side_effects=DATAFLOW),
    )(*srcs, *lands, ssem, rsem, after)
    return res[:n], res[n:]


def _gather_copies(buf_refs, _, ssem, rsem):
    x, y, c = _coords()
    j = 2 * x + y
    out = []
    for i, buf in enumerate(buf_refs):
        h = buf.shape[1] // 2
        rows = pl.ds(c * h, h)
        for k, (px, py, pj) in enumerate(_chip_peers(x, y)):
            sems = (ssem.at[3 * i + k], rsem.at[3 * i + k], (px, py, c))
            out.append((_remote(buf.at[j, rows], buf.at[j, rows], *sems),
                        _remote(buf.at[pj, rows], buf.at[pj, rows], *sems)))
    return out


def _scatter_copies(src_refs, land_refs, ssem, rsem):
    x, y, c = _coords()
    out = []
    for i, (src, land) in enumerate(zip(src_refs, land_refs)):
        for k, (px, py, _) in enumerate(_chip_peers(x, y)):
            cp = _remote(src.at[k], land.at[k], ssem.at[3 * i + k], rsem.at[3 * i + k], (px, py, c))
            out.append((cp, cp))
    return out


def _sibling_pass(name, lands, dep):
    n = len(lands)

    def body(*refs):
        bufs = refs[n + 1:2 * n + 1]
        ssem, rsem = refs[2 * n + 1:]
        x, y, c = _coords()
        sends = []
        for i in range(n):
            h = bufs[i].shape[1] // 2
            for k, (px, py, pj) in enumerate(_chip_peers(x, y)):
                mine = bufs[i].at[pj, pl.ds(c * h, h)]
                sends.append(_remote(mine, mine, ssem.at[3 * i + k], rsem.at[3 * i + k], (x, y, 1 - c)))
        for cp in sends:
            cp.start()
        for i in range(n):
            h = bufs[i].shape[1] // 2
            for k, (px, py, pj) in enumerate(_chip_peers(x, y)):
                theirs = bufs[i].at[pj, pl.ds((1 - c) * h, h)]
                _remote(theirs, theirs, ssem.at[3 * i + k], rsem.at[3 * i + k], (x, y, 1 - c)).wait_recv()
        for cp in sends:
            cp.wait_send()

    return pl.pallas_call(
        body, name=name, in_specs=[ANY] * (n + 1), out_specs=[ANY] * n,
        out_shape=[jax.ShapeDtypeStruct(a.shape, a.dtype) for a in lands],
        input_output_aliases={i: i for i in range(n)},
        scratch_shapes=[pltpu.SemaphoreType.DMA((3 * n,)), pltpu.SemaphoreType.DMA((3 * n,))],
    )(*lands, dep)


def _gather_small(small):
    def body(in_ref, out_ref, ssem, rsem, lsem):
        x, y, c = _coords()
        j = 2 * x + y
        local = pltpu.make_async_copy(in_ref, out_ref.at[j], lsem)
        local.start()
        peers = _chip_peers(x, y)
        sends = [_remote(in_ref, out_ref.at[j], ssem.at[k], rsem.at[k], (px, py, c))
                 for k, (px, py, pj) in enumerate(peers)]
        for cp in sends:
            cp.start()
        for k, (px, py, pj) in enumerate(peers):
            sends[k].wait_send()
            _remote(in_ref, out_ref.at[pj], ssem.at[k], rsem.at[k], (px, py, c)).wait_recv()
        local.wait()

    return pl.pallas_call(body, name="gather_small", in_specs=[ANY], out_specs=ANY,
                          out_shape=jax.ShapeDtypeStruct((N_CHIPS,) + small.shape, small.dtype),
                          scratch_shapes=[pltpu.SemaphoreType.DMA((3,)), pltpu.SemaphoreType.DMA((3,)),
                                          pltpu.SemaphoreType.DMA])(small)


def _pair_copies(src_refs, land_refs, ssem, rsem):
    x, y, c = _coords()
    out = []
    for i, (src, land) in enumerate(zip(src_refs, land_refs)):
        h = src.shape[1] // 2
        cp = _remote(src.at[:, pl.ds((1 - c) * h, h)], land, ssem.at[i], rsem.at[i], (x, y, 1 - c))
        out.append((cp, cp))
    return out


def _swap_copies(layers):
    def make(buf_refs, _, ssem, rsem):
        x, y, c = _coords()
        out = []
        for i, buf in enumerate(buf_refs):
            h = buf.shape[1] // 2
            mine = buf.at[layers[i], pl.ds(c * h, h)]
            theirs = buf.at[layers[i], pl.ds((1 - c) * h, h)]
            sems = (ssem.at[i], rsem.at[i], (x, y, 1 - c))
            out.append((_remote(mine, mine, *sems), _remote(theirs, theirs, *sems)))
        return out

    return make


def _small_copies(buf_refs, _, ssem, rsem):
    x, y, c = _coords()
    buf = buf_refs[0]
    me = 4 * x + 2 * y + c
    out = []
    for r in range(1, N_DEV):
        px = 1 - x if r & 4 else x
        py = 1 - y if r & 2 else y
        pc = 1 - c if r & 1 else c
        peer = 4 * px + 2 * py + pc
        sems = (ssem.at[r - 1], rsem.at[r - 1], (px, py, pc))
        out.append((_remote(buf.at[me], buf.at[me], *sems), _remote(buf.at[peer], buf.at[peer], *sems)))
    return out


ROW_ALIGN = SUBLANES
SMALL_WIDTH = 8 * LANES
LAYER_WEIGHTS = (("a_w_in", "a_w_out"), ("b_w_in", "b_w_pool", "b_w_out"), ("c_w_in", "c_w_out"))
SMALL_SHARDED = ("a_ln_g", "a_ln_b", "c_conv_w", "c_conv_b", "c_ln_g", "c_ln_b")
SMALL_ORDER = ("a_ln_g", "a_ln_b", "a_w_s", "a_b_s", "b_scale", "c_conv_w", "c_conv_b", "c_ln_g", "c_ln_b",
               "post_ln_g", "post_ln_b")
WEIGHTS = ("a_w_in", "a_ln_g", "a_ln_b", "a_w_s", "a_b_s", "a_w_out", "b_w_in", "b_w_pool", "b_scale", "b_w_out",
           "c_w_in", "c_conv_w", "c_conv_b", "c_ln_g", "c_ln_b", "c_w_out", "post_ln_g", "post_ln_b")


def _pad_rows(a):
    pad = -a.shape[0] % ROW_ALIGN
    return jnp.pad(a, ((0, pad), (0, 0))) if pad else a


def _pack(parts, width):
    return jnp.concatenate([_pad_rows(p.reshape(-1, width)) for p in parts], axis=0)


def _unpack(packed, shapes, width):
    out, row = [], 0
    for shp in shapes:
        size = 1
        for n in shp:
            size *= n
        rows = size // width
        out.append(packed[row:row + rows].reshape(shp))
        row += rows + (-rows % ROW_ALIGN)
    return out


def _step(x, tgt, w, m, v):
    s, d = x.shape
    cx, cy, cc = _coords()
    j_me = 2 * cx + cy
    es = w["a_ln_g"].shape[1]
    e = N_CHIPS * es

    n_a, n_c = w["a_w_in"].shape[0], w["c_w_in"].shape[0]
    ng = w["b_w_pool"].shape[1]
    as3d = lambda a: a.reshape(a.shape[0], -1, a.shape[-1])

    zones = [[_cast_layer("cast_%s%d" % (k, i // 3), as3d(w[k]), i // 3, N_CHIPS) for k in LAYER_WEIGHTS[i % 3]]
             for i in range(DEPTH)]

    def gather_start(i, dep):
        return _split_start("gather_start%d" % i, _gather_copies, 3 * len(zones[i]), zones[i], [], dep)

    small_all = _gather_small(_pack([w[k] for k in SMALL_SHARDED], es))
    ssem, rsem, lands, _, tok = gather_start(0, small_all)
    xb = _cast_layer("cast_x", x[None], 0)
    small_full = jnp.swapaxes(small_all, 0, 1).reshape(small_all.shape[1], e)
    full_shapes = [(n_a, e), (n_a, e), (CONV_WIDTH, e), (1, e), (1, e), (1, e)]
    a_ln_g, a_ln_b, conv_w, conv_b, c_ln_g, c_ln_b = _unpack(small_full, full_shapes, e)

    post_g, post_b = w["post_ln_g"], w["post_ln_b"]
    b_scale = w["b_scale"]

    saved = []
    res, gp, bp, hb = x, jnp.ones((1, d), F32), jnp.zeros((1, d), F32), xb
    for i in range(DEPTH):
        kind, l = i % 3, i // 3
        tag = "L%d" % i
        after = zones[DEPTH - 1][-1] if i == 0 else res
        lands, _ = _split_wait("gather_wait%d" % i, _gather_copies, ssem, rsem, lands, [], after)
        full = _sibling_pass("sibling_pass%d" % i, lands, tok)
        if i + 1 < DEPTH:
            ssem, rsem, lands, _, tok = gather_start(i + 1, full[0])
        wl = dict(zip(LAYER_WEIGHTS[kind], full))
        w_in = wl[LAYER_WEIGHTS[kind][0]].reshape(N_CHIPS, 1, d, -1)
        w_out = wl[LAYER_WEIGHTS[kind][-1]].reshape(N_CHIPS, 1, es, d)
        proj = _mm_in(tag + "_in", hb, w_in, 0, dep=tok)
        if kind == 0:
            b_st = w["a_b_s"][l].T
            yg = _sgu_fwd(tag + "_sgu", proj, a_ln_g[l:l + 1], a_ln_b[l:l + 1], w["a_w_s"][l], b_st)
            extra = (b_st,)
        elif kind == 1:
            w_pool = wl["b_w_pool"].reshape(N_CHIPS, ng, -1, w["b_w_pool"].shape[3])
            p = _pool_fwd(tag + "_pool", proj)
            q, yg = _pool_mm(tag + "_poolmm", p, w_pool, proj, b_scale[l:l + 1])
            extra = (p, q, w_pool)
        else:
            cv = _conv_fwd(tag + "_conv", proj, conv_w, conv_b)
            yg = _conv_gate_fwd(tag + "_gate", cv, proj, c_ln_g, c_ln_b)
            extra = (cv,)
        xhat, hb_next, rstd = _mm_out_ln(tag + "_out", yg, w_out, 0, res, gp, bp, post_g[i:i + 1], post_b[i:i + 1])
        saved.append((hb, proj, yg, xhat, rstd, extra, w_in, w_out))
        res, gp, bp, hb = xhat, post_g[i:i + 1], post_b[i:i + 1], hb_next

    sg = {k: [None] * w[k].shape[0] for k in ("a_ln_g", "a_ln_b", "a_w_s", "a_b_s", "post_ln_g", "post_ln_b")}
    scat = {}

    def finish_pair(p, after):
        i, pssem, prsem, own, recv, dep = p
        own, recv = _split_wait("pair_wait%d" % i, _pair_copies, pssem, prsem, own, recv, after)
        pair = [_pair_sum("pairsum%d_%d" % (i, n), a, r) for n, (a, r) in enumerate(zip(own, recv))]
        ssem, rsem, srcs, lands, token = _split_start("scatter_start%d" % i, _scatter_copies, 3 * len(pair), pair,
                                                      [((3,) + a.shape[1:], a.dtype) for a in pair], dep)
        scat[i] = (own, recv, ssem, rsem, srcs, lands)
        return token

    grads, delta, new_m, new_v = {}, {}, {}, {}

    def adam(k, emit_grad=False):
        shp = w[k].shape
        two_d = (lambda a: a.reshape(-1, shp[-1])) if shp[-1] % LANES == 0 else (lambda a: a.reshape(1, -1))
        res = _adam("adam_" + k, two_d(w[k]), two_d(grads[k]), two_d(m[k]), two_d(v[k]), emit_grad)
        delta[k], new_m[k], new_v[k] = (a.reshape(shp) for a in res[:3])
        if emit_grad:
            grads[k] = res[3].reshape(shp)
        return res[0]

    reduced, swaps = {}, {}

    def reduce_layer(i, after):
        kind, l = i % 3, i // 3
        names = LAYER_WEIGHTS[kind]
        own, recv, ssem, rsem, srcs, lands = scat[i]
        _, got = _split_wait("scatter_wait%d" % i, _scatter_copies, ssem, rsem, srcs, lands, after)
        for n, k in enumerate(names):
            reduced[k] = _final_sum("finalsum%d_%d" % (i, n), own[n], recv[n], got[n], l, w[k].shape[0],
                                    reduced.get(k))
        make = _swap_copies([l] * len(names))
        ssem, rsem, bufs, _, token = _split_start("swap_start%d" % i, make, len(names), [reduced[k] for k in names],
                                                  [], got[0])
        swaps[i] = (make, ssem, rsem, bufs)
        return token

    def finish_swap(i, after, with_adam):
        names = LAYER_WEIGHTS[i % 3]
        make, ssem, rsem, bufs = swaps[i]
        bufs, _ = _split_wait("swap_wait%d" % i, make, ssem, rsem, bufs, [], after)
        out = after
        for k, g in zip(names, bufs):
            reduced[k] = g
            if with_adam:
                grads[k] = g.reshape(w[k].shape)
                out = adam(k, True)
        return out

    pending = None
    dr, drb, dg, db, loss_vec = _post_ln_bwd("L3_lnbwd", tgt, saved[-1][3], saved[-1][4],
                                             post_g[DEPTH - 1:DEPTH], post_b[DEPTH - 1:DEPTH], True)
    grad_x = None
    for i in reversed(range(DEPTH)):
        kind, l = i % 3, i // 3
        tag = "L%d" % i
        hb_in, proj, yg, _, _, extra, w_in, w_out = saved[i]
        sg["post_ln_g"][i], sg["post_ln_b"][i] = dg, db
        started = None if pending is None else finish_pair(pending, drb)
        dyg = _mm_dyg(tag + "_dyg", drb, w_out, 0, dep=started)
        dw_mid = []
        if kind == 0:
            dproj, dlg, dlb, dws_s, dbs_t = _sgu_bwd(tag + "_sgubwd", dyg, proj, a_ln_g[l:l + 1], a_ln_b[l:l + 1],
                                                     w["a_w_s"][l], extra[0])
            sg["a_ln_g"][l], sg["a_ln_b"][l], sg["a_w_s"][l], sg["a_b_s"][l] = dlg, dlb, dws_s, dbs_t.T
        elif kind == 1:
            p, q, w_pool = extra
            dq, dproj, dscale = _pool_bwd1(tag + "_poolbwd1", dyg, q, proj, b_scale[l:l + 1])
            dp = _pool_mm_bwd(tag + "_poolmmbwd", dq, w_pool)
            dproj = _pool_bwd2(tag + "_poolbwd2", dp, dproj)
            dw_mid = [_mm_dw_pool(tag + "_dwpool", p, dq, ng).reshape(N_CHIPS, -1, dq.shape[1] // ng)]
            sg["b_scale"] = [dscale]
        else:
            cv = extra[0]
            dc, dproj, dlg, dlb, dcb = _conv_bwd1(tag + "_convbwd1", dyg, cv, proj, c_ln_g, c_ln_b)
            dgc, dcw = _conv_bwd2(tag + "_convbwd2", dc, proj, conv_w)
            dproj = _conv_bwd3(tag + "_convbwd3", dgc, proj, dproj)
            sg["c_ln_g"], sg["c_ln_b"], sg["c_conv_b"], sg["c_conv_w"] = [dlg], [dlb], [dcb], [dcw[:CONV_WIDTH]]
        if i == 0:
            packed = _pack([jnp.concatenate([a.reshape(1, -1) for a in sg[k]], axis=0) for k in SMALL_ORDER], SMALL_WIDTH)
            slabs = lax.dynamic_update_slice(jnp.zeros((N_DEV,) + packed.shape, F32), packed[None],
                                             (4 * cx + 2 * cy + cc, 0, 0))
            small = _split_start("small_start", _small_copies, N_DEV - 1, [slabs], [], dproj)
        dw_out = _mm_dw_out(tag + "_dwout", yg, drb, dep=small[-1] if i == 0 else None).reshape(N_CHIPS, es, d)
        if i > 0:
            dh = _mm_dh(tag + "_dh", dproj, w_in, 0, dr)
        dw_in = _mm_dw_in(tag + "_dwin", hb_in, dproj)
        own = [dw_in] + dw_mid + [dw_out]
        pending = (i,) + _split_start("pair_start%d" % i, _pair_copies, len(own), own,
                                      [((N_CHIPS, a.shape[1] // 2, a.shape[2]), a.dtype) for a in own], dproj)
        if i > 0:
            prev = saved[i - 1]
            dr, drb, dg, db, _ = _post_ln_bwd("L%d_lnbwd" % (i - 1), dh, prev[3], prev[4],
                                              post_g[i - 1:i], post_b[i - 1:i], False, dep=pending[-1])
        else:
            after = pending[-1]
            for done in range(DEPTH - 1, 1, -1):
                after = reduce_layer(done, after)
            grad_x = _mm_dh(tag + "_dh", dproj, w_in, 0, dr, dep=finish_pair(pending, after))

    after = reduce_layer(1, grad_x)
    for i in range(DEPTH - 2, 0, -1):
        after = finish_swap(i, after, True)
    finish_swap(DEPTH - 1, after, False)

    width = SMALL_WIDTH
    gathered, _ = _split_wait("small_wait", _small_copies, small[0], small[1], small[2], [], after)
    summed = _sum_devices("sum_small", gathered[0])
    full_small_shapes = {"a_ln_g": (n_a, e), "a_ln_b": (n_a, e), "a_w_s": w["a_w_s"].shape, "a_b_s": w["a_b_s"].shape,
                         "b_scale": w["b_scale"].shape, "c_conv_w": (n_c, CONV_WIDTH, e), "c_conv_b": (n_c, e),
                         "c_ln_g": (n_c, e), "c_ln_b": (n_c, e), "post_ln_g": post_g.shape, "post_ln_b": post_b.shape}
    for k, g in zip(SMALL_ORDER, _unpack(summed, [full_small_shapes[k] for k in SMALL_ORDER], width)):
        if k in SMALL_SHARDED:
            g = lax.dynamic_slice_in_dim(g, j_me * es, es, axis=g.ndim - 1)
        grads[k] = g
        after = adam(k)

    finish_swap(0, reduce_layer(0, after), True)

    loss = lax.psum(loss_vec[0, 0], ("x", "y", "c"))
    return loss, grad_x, grads, delta, new_m, new_v


def kernel(x, a_w_in, a_ln_g, a_ln_b, a_w_s, a_b_s, a_w_out, b_w_in, b_w_pool, b_scale, b_w_out, c_w_in, c_conv_w, c_conv_b, c_ln_g, c_ln_b, c_w_out, post_ln_g, post_ln_b, loss_target, m_a_w_in, m_a_ln_g, m_a_ln_b, m_a_w_s, m_a_b_s, m_a_w_out, m_b_w_in, m_b_w_pool, m_b_scale, m_b_w_out, m_c_w_in, m_c_conv_w, m_c_conv_b, m_c_ln_g, m_c_ln_b, m_c_w_out, m_post_ln_g, m_post_ln_b, v_a_w_in, v_a_ln_g, v_a_ln_b, v_a_w_s, v_a_b_s, v_a_w_out, v_b_w_in, v_b_w_pool, v_b_scale, v_b_w_out, v_c_w_in, v_c_conv_w, v_c_conv_b, v_c_ln_g, v_c_ln_b, v_c_w_out, v_post_ln_g, v_post_ln_b):
    args = locals()
    w = {k: args[k] for k in WEIGHTS}
    m = {k: args["m_" + k] for k in WEIGHTS}
    v = {k: args["v_" + k] for k in WEIGHTS}
    loss, grad_x, grads, delta, new_m, new_v = _step(x[0], loss_target[0], w, m, v)
    out = [loss, grad_x[None]]
    for group in (grads, delta, new_m, new_v):
        out.extend(group[k] for k in WEIGHTS)
    return tuple(out)
```

```python
import functools

import jax
import jax.numpy as jnp
from jax import lax
from jax.experimental import pallas as pl
from jax.experimental.pallas import tpu as pltpu

F32 = jnp.float32
BF16 = jnp.bfloat16
MESH = pl.DeviceIdType.MESH

DEPTH = 4
CHUNK = 128
SGU_HEADS = 8
POOL_WINDOWS = (2, 4, 8, 16)
POOL_HALO = 16
CONV_WIDTH = 31
CONV_HALO = 32
LN_EPS = 1e-5
ALPHA = (2.0 * DEPTH) ** 0.25
ADAM_LR, ADAM_B1, ADAM_B2, ADAM_EPS, ADAM_WD, ADAM_STEP = 0.001, 0.9, 0.999, 1e-08, 0.01, 10
N_CHIPS = 4
N_DEV = 8
LANES = 128
SUBLANES = 8
CONV_STRIDE = 4
CONV_SUB = SUBLANES * CONV_STRIDE
VMEM_LIMIT = 56 << 20
GELU_K = 0.7978845608028654
GELU_C = 0.044715


def _params(sem=None, vmem=VMEM_LIMIT):
    return pltpu.CompilerParams(dimension_semantics=sem, vmem_limit_bytes=vmem)


def _tile(dim, pref):
    if dim <= pref:
        return dim
    t = 1 << (pref.bit_length() - 1)
    while dim % t:
        t //= 2
    return t


ROW_BLOCK_BYTES = 2 << 20


def _row_tile(rows, cols):
    return _tile(rows, max(ROW_ALIGN, ROW_BLOCK_BYTES // (4 * cols)))


def _sigmoid(x):
    return 1.0 / (1.0 + jnp.exp(-x))


def _gelu_gate(x):
    x2 = x * x
    return _sigmoid((2.0 * GELU_K) * x * (1.0 + GELU_C * x2)), x2


def _gelu(x):
    return 0.5 * x * (1.0 + jnp.tanh(GELU_K * (x + GELU_C * x * x * x)))


def _gelu_and_grad(x):
    s, x2 = _gelu_gate(x)
    return x * s, s + x * (s * (1.0 - s)) * ((2.0 * GELU_K) * (1.0 + (3.0 * GELU_C) * x2))


def _ln_stats(x):
    mu = jnp.mean(x, axis=-1, keepdims=True)
    d = x - mu
    var = jnp.mean(d * d, axis=-1, keepdims=True)
    rstd = lax.rsqrt(var + LN_EPS)
    return d * rstd, rstd


def _ln_bwd(dy_hat, xhat, rstd):
    m1 = jnp.mean(dy_hat, axis=-1, keepdims=True)
    m2 = jnp.mean(dy_hat * xhat, axis=-1, keepdims=True)
    return rstd * (dy_hat - m1 - xhat * m2)


def _rowsum(x):
    return jnp.sum(x, axis=0, keepdims=True)


def _mm(name, a, b, *, dims, grid, a_spec, b_spec, out_shape, out_specs, acc_shape,
        extra=(), extra_specs=(), epilogue=None, dep=None):
    nk = grid[2]
    n_extra = len(extra)
    single = not isinstance(out_shape, (list, tuple))
    shapes = [out_shape] if single else list(out_shape)
    specs = [out_specs] if single else list(out_specs)
    n_out = len(shapes)
    deps = [] if dep is None else [dep]
    first_out = 2 + n_extra + len(deps)

    def body(*refs):
        a_ref, b_ref = refs[0], refs[1]
        ex = refs[2:2 + n_extra]
        outs = refs[first_out:first_out + n_out]

        def finish(acc):
            if epilogue is None:
                outs[0][...] = acc.astype(outs[0].dtype)
            else:
                epilogue(acc, ex, outs)

        part = lax.dot_general(a_ref[...].astype(BF16), b_ref[...].astype(BF16), dims, preferred_element_type=F32)
        if nk == 1:
            finish(part)
            return
        acc_ref = refs[-1]
        k = pl.program_id(2)

        @pl.when(k == 0)
        def _():
            acc_ref[...] = part

        @pl.when(jnp.logical_and(k > 0, k < nk - 1))
        def _():
            acc_ref[...] += part

        @pl.when(k == nk - 1)
        def _():
            finish(acc_ref[...] + part)

    res = pl.pallas_call(
        body, name=name, grid=grid, in_specs=[a_spec, b_spec, *extra_specs] + [ANY] * len(deps), out_specs=specs,
        out_shape=shapes, scratch_shapes=[pltpu.VMEM(acc_shape, F32)] if nk > 1 else [],
        compiler_params=_params(("parallel", "parallel", "arbitrary")),
    )(a, b, *extra, *deps)
    return res[0] if single else res


NN = (((1,), (0,)), ((), ()))
NT = (((1,), (1,)), ((), ()))
TN = (((0,), (0,)), ((), ()))


def _mm_in(name, hb, wg, layer, dep=None):
    s, d = hb.shape
    nsh = wg.shape[3]
    n = N_CHIPS * nsh
    bm, bn, bk = _tile(s, 1024), _tile(nsh, 1024), _tile(d, 2048)
    per = nsh // bn
    return _mm(name, hb, wg, dims=NN, grid=(s // bm, n // bn, d // bk),
               a_spec=pl.BlockSpec((bm, bk), lambda m, j, k: (m, k)),
               b_spec=pl.BlockSpec((None, None, bk, bn), lambda m, j, k: (j // per, layer, k, j % per)),
               out_shape=jax.ShapeDtypeStruct((s, n), F32),
               out_specs=pl.BlockSpec((bm, bn), lambda m, j, k: (m, j)),
               acc_shape=(bm, bn), dep=dep)


def _mm_out_ln(name, yg, wg, layer, res, gp, bp, g, b):
    s, e = yg.shape
    esh, d = wg.shape[2], wg.shape[3]
    bm, bk = _tile(s, 512), _tile(esh, 1024)
    per = esh // bk

    def epilogue(acc, ex, outs):
        res_ref, gp_ref, bp_ref, g_ref, b_ref = ex
        xhat_ref, hb_ref, rstd_ref = outs
        r = ALPHA * (res_ref[...] * gp_ref[...] + bp_ref[...]) + acc
        xhat, rstd = _ln_stats(r)
        xhat_ref[...] = xhat
        hb_ref[...] = (xhat * g_ref[...] + b_ref[...]).astype(BF16)
        rstd_ref[...] = jnp.broadcast_to(rstd, rstd_ref.shape)

    vec = pl.BlockSpec((1, d), lambda m, j, k: (0, 0))
    row = pl.BlockSpec((bm, d), lambda m, j, k: (m, 0))
    return _mm(name, yg, wg, dims=NN, grid=(s // bm, 1, e // bk),
               a_spec=pl.BlockSpec((bm, bk), lambda m, j, k: (m, k)),
               b_spec=pl.BlockSpec((None, None, bk, d), lambda m, j, k: (k // per, layer, k % per, 0)),
               extra=(res, gp, bp, g, b), extra_specs=(row, vec, vec, vec, vec),
               out_shape=(jax.ShapeDtypeStruct((s, d), F32), jax.ShapeDtypeStruct((s, d), BF16),
                          jax.ShapeDtypeStruct((s, LANES), F32)),
               out_specs=(row, row, pl.BlockSpec((bm, LANES), lambda m, j, k: (m, 0))),
               acc_shape=(bm, d), epilogue=epilogue)


def _mm_dyg(name, drb, wg, layer, dep=None):
    s, d = drb.shape
    esh = wg.shape[2]
    e = N_CHIPS * esh
    bm, bn, bk = _tile(s, 1024), _tile(esh, 1024), _tile(d, 2048)
    per = esh // bn
    return _mm(name, drb, wg, dims=NT, grid=(s // bm, e // bn, d // bk),
               a_spec=pl.BlockSpec((bm, bk), lambda m, j, k: (m, k)),
               b_spec=pl.BlockSpec((None, None, bn, bk), lambda m, j, k: (j // per, layer, j % per, k)),
               out_shape=jax.ShapeDtypeStruct((s, e), F32),
               out_specs=pl.BlockSpec((bm, bn), lambda m, j, k: (m, j)),
               acc_shape=(bm, bn), dep=dep)


def _mm_dw_out(name, yg, drb, dep=None):
    s, e = yg.shape
    d = drb.shape[1]
    bm, bn, bk = _tile(e, 1024), _tile(d, 1024), _tile(s, 4096)
    return _mm(name, yg, drb, dims=TN, grid=(e // bm, d // bn, s // bk),
               a_spec=pl.BlockSpec((bk, bm), lambda m, j, k: (k, m)),
               b_spec=pl.BlockSpec((bk, bn), lambda m, j, k: (k, j)),
               out_shape=jax.ShapeDtypeStruct((e, d), BF16),
               out_specs=pl.BlockSpec((bm, bn), lambda m, j, k: (m, j)),
               acc_shape=(bm, bn), dep=dep)


def _mm_dh(name, dproj, wg, layer, dr, dep=None):
    s, n = dproj.shape
    d, nsh = wg.shape[2], wg.shape[3]
    bm, bn = _tile(s, 1024), _tile(d, 1024)
    bk = max(t for t in range(LANES, 3073, LANES) if nsh % t == 0)
    per = nsh // bk

    def epilogue(acc, ex, outs):
        outs[0][...] = ALPHA * ex[0][...] + acc

    blk = pl.BlockSpec((bm, bn), lambda m, j, k: (m, j))
    return _mm(name, dproj, wg, dims=NT, grid=(s // bm, d // bn, n // bk),
               a_spec=pl.BlockSpec((bm, bk), lambda m, j, k: (m, k)),
               b_spec=pl.BlockSpec((None, None, bn, bk), lambda m, j, k: (k // per, layer, j, k % per)),
               extra=(dr,), extra_specs=(blk,),
               out_shape=jax.ShapeDtypeStruct((s, d), F32), out_specs=blk,
               acc_shape=(bm, bn), epilogue=epilogue, dep=dep)


def _mm_dw_in(name, hb, dproj):
    s, d = hb.shape
    n = dproj.shape[1]
    nsh = n // N_CHIPS
    bm, bn, bk = _tile(d, 1024), _tile(nsh, 1024), _tile(s, 4096)
    per = nsh // bn
    return _mm(name, hb, dproj, dims=TN, grid=(d // bm, n // bn, s // bk),
               a_spec=pl.BlockSpec((bk, bm), lambda m, j, k: (k, m)),
               b_spec=pl.BlockSpec((bk, bn), lambda m, j, k: (k, j)),
               out_shape=jax.ShapeDtypeStruct((N_CHIPS, d, nsh), BF16),
               out_specs=pl.BlockSpec((None, bm, bn), lambda m, j, k: (j // per, m, j % per)),
               acc_shape=(bm, bn))


def _causal_mask():
    r = lax.broadcasted_iota(jnp.int32, (CHUNK, CHUNK), 0)
    c = lax.broadcasted_iota(jnp.int32, (CHUNK, CHUNK), 1)
    return r >= c


def _sgu_fwd(name, proj, ln_g, ln_b, w_s, b_st):
    s, n3 = proj.shape
    e = n3 // 3
    hd = e // SGU_HEADS

    def body(proj_ref, g_ref, b_ref, ws_ref, bst_ref, yg_ref):
        xhat, _ = _ln_stats(_gelu(proj_ref[:, e:2 * e]))
        vnb = (xhat * g_ref[...] + b_ref[...]).astype(BF16)
        tri = _causal_mask()
        for h in range(SGU_HEADS):
            hs = slice(h * hd, (h + 1) * hd)
            ws = jnp.where(tri, ws_ref[h], 0.0).astype(BF16)
            mixed = jnp.dot(ws, vnb[:, hs], preferred_element_type=F32) + bst_ref[:, h:h + 1]
            u = proj_ref[:, hs]
            z = proj_ref[:, 2 * e + h * hd:2 * e + (h + 1) * hd]
            yg_ref[:, hs] = (_gelu(u) * mixed * (z * _sigmoid(z))).astype(BF16)

    vec = pl.BlockSpec((1, e), lambda i: (0, 0))
    return pl.pallas_call(
        body, name=name, grid=(s // CHUNK,),
        in_specs=[pl.BlockSpec((CHUNK, n3), lambda i: (i, 0)), vec, vec,
                  pl.BlockSpec((SGU_HEADS, CHUNK, CHUNK), lambda i: (0, 0, 0)),
                  pl.BlockSpec((CHUNK, SGU_HEADS), lambda i: (0, 0))],
        out_specs=pl.BlockSpec((CHUNK, e), lambda i: (i, 0)),
        out_shape=jax.ShapeDtypeStruct((s, e), BF16),
        compiler_params=_params(("parallel",)),
    )(proj, ln_g, ln_b, w_s, b_st)


def _sgu_bwd(name, dyg, proj, ln_g, ln_b, w_s, b_st):
    s, n3 = proj.shape
    e = n3 // 3
    hd = e // SGU_HEADS

    def body(dyg_ref, proj_ref, g_ref, b_ref, ws_ref, bst_ref,
             dproj_ref, dg_ref, db_ref, dws_ref, dbs_ref, dvn_ref):
        @pl.when(pl.program_id(0) == 0)
        def _():
            dg_ref[...] = jnp.zeros_like(dg_ref)
            db_ref[...] = jnp.zeros_like(db_ref)
            dws_ref[...] = jnp.zeros_like(dws_ref)
            dbs_ref[...] = jnp.zeros_like(dbs_ref)

        gv, gv_grad = _gelu_and_grad(proj_ref[:, e:2 * e])
        xhat, rstd = _ln_stats(gv)
        g = g_ref[...]
        vnb = (xhat * g + b_ref[...]).astype(BF16)
        tri = _causal_mask()
        for h in range(SGU_HEADS):
            hs = slice(h * hd, (h + 1) * hd)
            zs = slice(2 * e + h * hd, 2 * e + (h + 1) * hd)
            ws = jnp.where(tri, ws_ref[h], 0.0).astype(BF16)
            mixed = jnp.dot(ws, vnb[:, hs], preferred_element_type=F32) + bst_ref[:, h:h + 1]
            u = proj_ref[:, hs]
            z = proj_ref[:, zs]
            gu, gu_grad = _gelu_and_grad(u)
            sig = _sigmoid(z)
            dy = dyg_ref[:, hs]
            t = dy * (z * sig)
            dmixed = t * gu
            dproj_ref[:, hs] = (t * mixed * gu_grad).astype(BF16)
            dproj_ref[:, zs] = (dy * gu * mixed * (sig * (1.0 + z * (1.0 - sig)))).astype(BF16)
            dmb = dmixed.astype(BF16)
            dvn_ref[:, hs] = lax.dot_general(ws, dmb, TN, preferred_element_type=F32)
            dws = lax.dot_general(dmb, vnb[:, hs], NT, preferred_element_type=F32)
            dws_ref[h] += jnp.where(tri, dws, 0.0)
            dbs_ref[:, h:h + 1] += jnp.sum(dmixed, axis=-1, keepdims=True)
        dvn = dvn_ref[...]
        dg_ref[...] += _rowsum(dvn * xhat)
        db_ref[...] += _rowsum(dvn)
        dgv = _ln_bwd(dvn * g, xhat, rstd)
        dproj_ref[:, e:2 * e] = (dgv * gv_grad).astype(BF16)

    vec = pl.BlockSpec((1, e), lambda i: (0, 0))
    wsp = pl.BlockSpec((SGU_HEADS, CHUNK, CHUNK), lambda i: (0, 0, 0))
    bsp = pl.BlockSpec((CHUNK, SGU_HEADS), lambda i: (0, 0))
    return pl.pallas_call(
        body, name=name, grid=(s // CHUNK,),
        in_specs=[pl.BlockSpec((CHUNK, e), lambda i: (i, 0)), pl.BlockSpec((CHUNK, n3), lambda i: (i, 0)),
                  vec, vec, wsp, bsp],
        out_specs=[pl.BlockSpec((CHUNK, n3), lambda i: (i, 0)), vec, vec, wsp, bsp],
        out_shape=[jax.ShapeDtypeStruct((s, n3), BF16), jax.ShapeDtypeStruct((1, e), F32),
                   jax.ShapeDtypeStruct((1, e), F32),
                   jax.ShapeDtypeStruct((SGU_HEADS, CHUNK, CHUNK), F32),
                   jax.ShapeDtypeStruct((CHUNK, SGU_HEADS), F32)],
        scratch_shapes=[pltpu.VMEM((CHUNK, e), F32)],
        compiler_params=_params(("arbitrary",)),
    )(dyg, proj, ln_g, ln_b, w_s, b_st)


def _pool_counts(pos, w):
    return jnp.minimum(pos + 1, w).astype(F32)


def _pool_fwd(name, proj):
    s, n2 = proj.shape
    e = n2 // 2
    gd = e // len(POOL_WINDOWS)
    ts = _row_tile(s, e)
    hb = ts // POOL_HALO

    def body(v_ref, halo_ref, p_ref, ext):
        i = pl.program_id(0)
        ext[POOL_HALO:, :] = v_ref[...]
        ext[:POOL_HALO, :] = jnp.where(i > 0, halo_ref[...], 0.0)
        pos = i * ts + lax.broadcasted_iota(jnp.int32, (ts, 1), 0)
        for g, w in enumerate(POOL_WINDOWS):
            cs = slice(g * gd, (g + 1) * gd)
            acc = ext[POOL_HALO:, cs]
            for j in range(1, w):
                acc = acc + ext[POOL_HALO - j:POOL_HALO - j + ts, cs]
            p_ref[:, cs] = (acc / _pool_counts(pos, w) - v_ref[:, cs]).astype(BF16)

    return pl.pallas_call(
        body, name=name, grid=(s // ts,),
        in_specs=[pl.BlockSpec((ts, e), lambda i: (i, 0)),
                  pl.BlockSpec((POOL_HALO, e), lambda i: (jnp.maximum(i * hb - 1, 0), 0))],
        out_specs=pl.BlockSpec((ts, e), lambda i: (i, 0)),
        out_shape=jax.ShapeDtypeStruct((s, e), BF16),
        scratch_shapes=[pltpu.VMEM((ts + POOL_HALO, e), F32)],
        compiler_params=_params(("parallel",)),
    )(proj, proj)


def _pool_mm(name, p, wg, proj, scale):
    s, e = p.shape
    ng, rsh, gd = wg.shape[1], wg.shape[2], wg.shape[3]
    bm = _tile(s, 1024)

    def body(p_ref, w0, w1, w2, w3, z_ref, sc_ref, q_ref, yg_ref):
        q = None
        for j, w_ref in enumerate((w0, w1, w2, w3)):
            part = jnp.dot(p_ref[:, j * rsh:(j + 1) * rsh], w_ref[...], preferred_element_type=F32)
            q = part if q is None else q + part
        z = z_ref[...]
        q_ref[...] = q
        yg_ref[...] = (q * sc_ref[...] * (z * _sigmoid(z))).astype(BF16)

    blk = pl.BlockSpec((bm, gd), lambda m, g: (m, g))
    w_specs = [pl.BlockSpec((None, None, rsh, gd), functools.partial(lambda m, g, j: (j, g, 0, 0), j=j))
               for j in range(N_CHIPS)]
    return pl.pallas_call(
        body, name=name, grid=(s // bm, ng),
        in_specs=[blk] + w_specs + [pl.BlockSpec((bm, gd), lambda m, g: (m, ng + g)),
                                    pl.BlockSpec((1, gd), lambda m, g: (0, g))],
        out_specs=[blk, blk],
        out_shape=[jax.ShapeDtypeStruct((s, e), F32), jax.ShapeDtypeStruct((s, e), BF16)],
        compiler_params=_params(("parallel", "parallel")),
    )(p, wg, wg, wg, wg, proj, scale)


def _pool_bwd1(name, dyg, q, proj, scale):
    s, e = q.shape
    ts = _row_tile(s, e)

    def body(dyg_ref, q_ref, z_ref, sc_ref, dq_ref, dz_ref, dsc_ref):
        @pl.when(pl.program_id(0) == 0)
        def _():
            dsc_ref[...] = jnp.zeros_like(dsc_ref)

        z = z_ref[...]
        sig = _sigmoid(z)
        dy = dyg_ref[...]
        qv = q_ref[...]
        sc = sc_ref[...]
        dp2 = dy * (z * sig)
        dsc_ref[...] += _rowsum(dp2 * qv)
        dq_ref[...] = (dp2 * sc).astype(BF16)
        dz_ref[...] = (dy * (qv * sc) * (sig * (1.0 + z * (1.0 - sig)))).astype(BF16)

    blk = pl.BlockSpec((ts, e), lambda i: (i, 0))
    vec = pl.BlockSpec((1, e), lambda i: (0, 0))
    return pl.pallas_call(
        body, name=name, grid=(s // ts,),
        in_specs=[blk, blk, pl.BlockSpec((ts, e), lambda i: (i, 1)), vec],
        out_specs=[blk, pl.BlockSpec((ts, e), lambda i: (i, 1)), vec],
        out_shape=[jax.ShapeDtypeStruct((s, e), BF16), jax.ShapeDtypeStruct((s, 2 * e), BF16),
                   jax.ShapeDtypeStruct((1, e), F32)],
        compiler_params=_params(("arbitrary",)),
    )(dyg, q, proj, scale)


def _pool_mm_bwd(name, dq, wg):
    s, e = dq.shape
    ng, rsh, gd = wg.shape[1], wg.shape[2], wg.shape[3]
    bm = _tile(s, 1024)
    return _mm(name, dq, wg, dims=NT, grid=(s // bm, ng * N_CHIPS, 1),
               a_spec=pl.BlockSpec((bm, gd), lambda m, j, k: (m, j // N_CHIPS)),
               b_spec=pl.BlockSpec((None, None, rsh, gd), lambda m, j, k: (j % N_CHIPS, j // N_CHIPS, 0, 0)),
               out_shape=jax.ShapeDtypeStruct((s, e), F32),
               out_specs=pl.BlockSpec((bm, rsh), lambda m, j, k: (m, j)),
               acc_shape=(bm, rsh))


def _pool_bwd2(name, dp, dproj):
    s, e = dp.shape
    gd = e // len(POOL_WINDOWS)
    ts = _row_tile(s, e)
    n = s // ts
    hb = ts // POOL_HALO

    def body(dp_ref, nxt_ref, _, dv_ref, ext):
        i = pl.program_id(0)
        pos = i * ts + lax.broadcasted_iota(jnp.int32, (ts + POOL_HALO, 1), 0)
        for g, w in enumerate(POOL_WINDOWS):
            cs = slice(g * gd, (g + 1) * gd)
            cnt = _pool_counts(pos, w)
            ext[:ts, cs] = dp_ref[:, cs] / cnt[:ts]
            ext[ts:, cs] = jnp.where(i < n - 1, nxt_ref[:, cs] / cnt[ts:], 0.0)
        for g, w in enumerate(POOL_WINDOWS):
            cs = slice(g * gd, (g + 1) * gd)
            acc = ext[:ts, cs]
            for j in range(1, w):
                acc = acc + ext[j:j + ts, cs]
            dv_ref[:, cs] = (acc - dp_ref[:, cs]).astype(BF16)

    return pl.pallas_call(
        body, name=name, grid=(n,),
        in_specs=[pl.BlockSpec((ts, e), lambda i: (i, 0)),
                  pl.BlockSpec((POOL_HALO, e), lambda i: (jnp.minimum((i + 1) * hb, s // POOL_HALO - 1), 0)),
                  pl.BlockSpec(memory_space=pl.ANY)],
        out_specs=pl.BlockSpec((ts, e), lambda i: (i, 0)),
        out_shape=jax.ShapeDtypeStruct((s, 2 * e), BF16),
        scratch_shapes=[pltpu.VMEM((ts + POOL_HALO, e), F32)],
        input_output_aliases={2: 0},
        compiler_params=_params(("parallel",)),
    )(dp, dp, dproj)


def _mm_dw_pool(name, p, dq, ng):
    s, e = p.shape
    gd = e // ng
    rsh = gd // N_CHIPS
    bk = _tile(s, 4096)
    return _mm(name, p, dq, dims=TN, grid=(N_CHIPS, ng, s // bk),
               a_spec=pl.BlockSpec((bk, rsh), lambda m, g, k: (k, g * N_CHIPS + m)),
               b_spec=pl.BlockSpec((bk, gd), lambda m, g, k: (k, g)),
               out_shape=jax.ShapeDtypeStruct((N_CHIPS, ng, rsh, gd), BF16),
               out_specs=pl.BlockSpec((None, None, rsh, gd), lambda m, g, k: (m, g, 0, 0)),
               acc_shape=(rsh, gd))


def _conv_fwd(name, proj, conv_w, conv_b):
    s, n3 = proj.shape
    e = n3 // 3
    ts, tc = _tile(s, 512), LANES
    ne = e // tc
    hb = ts // CONV_HALO
    off = CONV_HALO - (CONV_WIDTH - 1)

    def body(a_ref, gl_ref, ha_ref, hgl_ref, w_ref, b_ref, c_ref, gext):
        i = pl.program_id(1)
        gext[CONV_HALO:, :] = a_ref[...] * _sigmoid(gl_ref[...])
        gext[:CONV_HALO, :] = jnp.where(i > 0, ha_ref[...] * _sigmoid(hgl_ref[...]), 0.0)
        taps = [jnp.broadcast_to(w_ref[k:k + 1, :], (SUBLANES, LANES)) for k in range(CONV_WIDTH)]
        bias = jnp.broadcast_to(b_ref[...], (SUBLANES, LANES))

        def sub_block(rb, carry):
            r0 = pl.multiple_of(rb * CONV_SUB, CONV_SUB)
            acc = [bias] * CONV_STRIDE
            for m in range(CONV_WIDTH + CONV_STRIDE - 1):
                g_m = gext[pl.ds(r0 + off + m, SUBLANES, stride=CONV_STRIDE), :]
                for q in range(CONV_STRIDE):
                    if 0 <= m - q < CONV_WIDTH:
                        acc[q] = acc[q] + taps[m - q] * g_m
            for q in range(CONV_STRIDE):
                c_ref[pl.ds(r0 + q, SUBLANES, stride=CONV_STRIDE), :] = acc[q]
            return carry

        lax.fori_loop(0, ts // CONV_SUB, sub_block, 0, unroll=4)

    halo = lambda i: jnp.maximum(i * hb - 1, 0)
    return pl.pallas_call(
        body, name=name, grid=(ne, s // ts),
        in_specs=[pl.BlockSpec((ts, tc), lambda j, i: (i, j)),
                  pl.BlockSpec((ts, tc), lambda j, i: (i, ne + j)),
                  pl.BlockSpec((CONV_HALO, tc), lambda j, i: (halo(i), j)),
                  pl.BlockSpec((CONV_HALO, tc), lambda j, i: (halo(i), ne + j)),
                  pl.BlockSpec((CONV_WIDTH, tc), lambda j, i: (0, j)),
                  pl.BlockSpec((1, tc), lambda j, i: (0, j))],
        out_specs=pl.BlockSpec((ts, tc), lambda j, i: (i, j)),
        out_shape=jax.ShapeDtypeStruct((s, e), F32),
        scratch_shapes=[pltpu.VMEM((ts + CONV_HALO, tc), F32)],
        compiler_params=_params(("parallel", "parallel")),
    )(proj, proj, proj, proj, conv_w, conv_b)


def _conv_gate_fwd(name, c, proj, ln_g, ln_b):
    s, e = c.shape
    ts = _row_tile(s, e)

    def body(c_ref, z_ref, g_ref, b_ref, yg_ref):
        xhat, _ = _ln_stats(c_ref[...])
        ln = xhat * g_ref[...] + b_ref[...]
        z = z_ref[...]
        yg_ref[...] = ((ln * _sigmoid(ln)) * (z * _sigmoid(z))).astype(BF16)

    blk = pl.BlockSpec((ts, e), lambda i: (i, 0))
    vec = pl.BlockSpec((1, e), lambda i: (0, 0))
    return pl.pallas_call(
        body, name=name, grid=(s // ts,),
        in_specs=[blk, pl.BlockSpec((ts, e), lambda i: (i, 2)), vec, vec],
        out_specs=blk, out_shape=jax.ShapeDtypeStruct((s, e), BF16),
        compiler_params=_params(("parallel",)),
    )(c, proj, ln_g, ln_b)


def _conv_bwd1(name, dyg, c, proj, ln_g, ln_b):
    s, e = c.shape
    ts = _row_tile(s, e)

    def body(dyg_ref, c_ref, z_ref, g_ref, b_ref, dc_ref, dz_ref, dg_ref, db_ref, dcb_ref):
        @pl.when(pl.program_id(0) == 0)
        def _():
            dg_ref[...] = jnp.zeros_like(dg_ref)
            db_ref[...] = jnp.zeros_like(db_ref)
            dcb_ref[...] = jnp.zeros_like(dcb_ref)

        xhat, rstd = _ln_stats(c_ref[...])
        g = g_ref[...]
        ln = xhat * g + b_ref[...]
        sl = _sigmoid(ln)
        z = z_ref[...]
        sz = _sigmoid(z)
        dy = dyg_ref[...]
        dz_ref[...] = (dy * (ln * sl) * (sz * (1.0 + z * (1.0 - sz)))).astype(BF16)
        dln = dy * (z * sz) * (sl * (1.0 + ln * (1.0 - sl)))
        dg_ref[...] += _rowsum(dln * xhat)
        db_ref[...] += _rowsum(dln)
        dc = _ln_bwd(dln * g, xhat, rstd)
        dc_ref[...] = dc
        dcb_ref[...] += _rowsum(dc)

    blk = pl.BlockSpec((ts, e), lambda i: (i, 0))
    zblk = pl.BlockSpec((ts, e), lambda i: (i, 2))
    vec = pl.BlockSpec((1, e), lambda i: (0, 0))
    vshape = jax.ShapeDtypeStruct((1, e), F32)
    return pl.pallas_call(
        body, name=name, grid=(s // ts,),
        in_specs=[blk, blk, zblk, vec, vec],
        out_specs=[blk, zblk, vec, vec, vec],
        out_shape=[jax.ShapeDtypeStruct((s, e), F32), jax.ShapeDtypeStruct((s, 3 * e), BF16),
                   vshape, vshape, vshape],
        compiler_params=_params(("arbitrary",)),
    )(dyg, c, proj, ln_g, ln_b)


def _conv_bwd2(name, dc, proj, conv_w):
    s, e = dc.shape
    ts, tc = _tile(s, 512), LANES
    ne = e // tc
    n = s // ts
    hb = ts // CONV_HALO
    off = CONV_HALO - (CONV_WIDTH - 1)
    span = CONV_WIDTH + CONV_STRIDE - 1

    def body(dc_ref, nxt_ref, a_ref, gl_ref, ha_ref, hgl_ref, w_ref, dg_ref, dw_ref, gext, dcext, dwacc, taps):
        i = pl.program_id(1)

        @pl.when(i == 0)
        def _():
            dwacc[...] = jnp.zeros_like(dwacc)

        gext[CONV_HALO:, :] = a_ref[...] * _sigmoid(gl_ref[...])
        gext[:CONV_HALO, :] = jnp.where(i > 0, ha_ref[...] * _sigmoid(hgl_ref[...]), 0.0)
        dcext[:ts, :] = dc_ref[...]
        dcext[ts:, :] = jnp.where(i < n - 1, nxt_ref[...], 0.0)
        for k in range(CONV_WIDTH):
            taps[k] = jnp.broadcast_to(w_ref[k:k + 1, :], (SUBLANES, LANES))

        def sub_block(rb, carry):
            r0 = pl.multiple_of(rb * CONV_SUB, CONV_SUB)
            dg = [jnp.zeros((SUBLANES, LANES), F32)] * CONV_STRIDE
            for m in range(span):
                dc_m = dcext[pl.ds(r0 + m, SUBLANES, stride=CONV_STRIDE), :]
                for q in range(CONV_STRIDE):
                    k = CONV_WIDTH - 1 - m + q
                    if 0 <= k < CONV_WIDTH:
                        dg[q] = dg[q] + taps[k] * dc_m
            for q in range(CONV_STRIDE):
                dg_ref[pl.ds(r0 + q, SUBLANES, stride=CONV_STRIDE), :] = dg[q]
            dc_q = [dcext[pl.ds(r0 + q, SUBLANES, stride=CONV_STRIDE), :] for q in range(CONV_STRIDE)]
            win = [gext[pl.ds(r0 + off + m, SUBLANES, stride=CONV_STRIDE), :] for m in range(CONV_STRIDE - 1)]
            for k in range(CONV_WIDTH):
                win.append(gext[pl.ds(r0 + off + k + CONV_STRIDE - 1, SUBLANES, stride=CONV_STRIDE), :])
                prods = [dc_q[q] * win[q] for q in range(CONV_STRIDE)]
                while len(prods) > 1:
                    prods = [a + b for a, b in zip(prods[::2], prods[1::2])]
                dwacc[k] += prods[0]
                win.pop(0)
            return carry

        lax.fori_loop(0, ts // CONV_SUB, sub_block, 0, unroll=2)

        @pl.when(i == n - 1)
        def _():
            dw_ref[...] = jnp.sum(dwacc[...], axis=1)

    halo = lambda i: jnp.maximum(i * hb - 1, 0)
    nxt = lambda i: jnp.minimum((i + 1) * hb, s // CONV_HALO - 1)
    return pl.pallas_call(
        body, name=name, grid=(ne, n),
        in_specs=[pl.BlockSpec((ts, tc), lambda j, i: (i, j)),
                  pl.BlockSpec((CONV_HALO, tc), lambda j, i: (nxt(i), j)),
                  pl.BlockSpec((ts, tc), lambda j, i: (i, j)),
                  pl.BlockSpec((ts, tc), lambda j, i: (i, ne + j)),
                  pl.BlockSpec((CONV_HALO, tc), lambda j, i: (halo(i), j)),
                  pl.BlockSpec((CONV_HALO, tc), lambda j, i: (halo(i), ne + j)),
                  pl.BlockSpec((CONV_WIDTH, tc), lambda j, i: (0, j))],
        out_specs=[pl.BlockSpec((ts, tc), lambda j, i: (i, j)),
                   pl.BlockSpec((CONV_HALO, tc), lambda j, i: (0, j))],
        out_shape=[jax.ShapeDtypeStruct((s, e), F32), jax.ShapeDtypeStruct((CONV_HALO, e), F32)],
        scratch_shapes=[pltpu.VMEM((ts + CONV_HALO, tc), F32), pltpu.VMEM((ts + CONV_HALO, tc), F32),
                        pltpu.VMEM((CONV_HALO, SUBLANES, LANES), F32), pltpu.VMEM((CONV_HALO, SUBLANES, LANES), F32)],
        compiler_params=_params(("parallel", "arbitrary")),
    )(dc, dc, proj, proj, proj, proj, conv_w)


def _conv_bwd3(name, dg, proj, dproj):
    s, e = dg.shape
    ts = _row_tile(s, 2 * e)

    def body(dg_ref, agl_ref, _, out_ref):
        a = agl_ref[:, :e]
        sig = _sigmoid(agl_ref[:, e:])
        dgv = dg_ref[...]
        out_ref[:, :e] = (dgv * sig).astype(BF16)
        out_ref[:, e:] = (dgv * a * sig * (1.0 - sig)).astype(BF16)

    wide = pl.BlockSpec((ts, 2 * e), lambda i: (i, 0))
    return pl.pallas_call(
        body, name=name, grid=(s // ts,),
        in_specs=[pl.BlockSpec((ts, e), lambda i: (i, 0)), wide, pl.BlockSpec(memory_space=pl.ANY)],
        out_specs=wide, out_shape=jax.ShapeDtypeStruct((s, 3 * e), BF16),
        input_output_aliases={2: 0},
        compiler_params=_params(("parallel",)),
    )(dg, proj, dproj)


def _post_ln_bwd(name, dy_or_target, xhat, rstd, g, b, loss_head, dep=None):
    s, d = xhat.shape
    ts = _row_tile(s, d)

    def body(dy_ref, xhat_ref, rstd_ref, g_ref, b_ref, *rest):
        dr_ref, drb_ref, dg_ref, db_ref, loss_ref = rest[-5:]

        @pl.when(pl.program_id(0) == 0)
        def _():
            dg_ref[...] = jnp.zeros_like(dg_ref)
            db_ref[...] = jnp.zeros_like(db_ref)
            loss_ref[...] = jnp.zeros_like(loss_ref)

        xh = xhat_ref[...]
        gv = g_ref[...]
        if loss_head:
            err = (xh * gv + b_ref[...]) - dy_ref[...]
            per_row = jnp.mean(err * err, axis=-1, keepdims=True)
            loss_ref[...] += 0.5 * jnp.broadcast_to(_rowsum(per_row), loss_ref.shape)
            dy = err / d
        else:
            dy = dy_ref[...]
        dg_ref[...] += _rowsum(dy * xh)
        db_ref[...] += _rowsum(dy)
        dr = _ln_bwd(dy * gv, xh, rstd_ref[:, 0:1])
        dr_ref[...] = dr
        drb_ref[...] = dr.astype(BF16)

    blk = pl.BlockSpec((ts, d), lambda i: (i, 0))
    vec = pl.BlockSpec((1, d), lambda i: (0, 0))
    deps = [] if dep is None else [dep]
    return pl.pallas_call(
        body, name=name, grid=(s // ts,),
        in_specs=[blk, blk, pl.BlockSpec((ts, LANES), lambda i: (i, 0)), vec, vec] + [ANY] * len(deps),
        out_specs=[blk, blk, vec, vec, pl.BlockSpec((1, LANES), lambda i: (0, 0))],
        out_shape=[jax.ShapeDtypeStruct((s, d), F32), jax.ShapeDtypeStruct((s, d), BF16),
                   jax.ShapeDtypeStruct((1, d), F32), jax.ShapeDtypeStruct((1, d), F32),
                   jax.ShapeDtypeStruct((1, LANES), F32)],
        compiler_params=_params(("arbitrary",)),
    )(dy_or_target, xhat, rstd, g, b, *deps)


def _as2d(a):
    return a.reshape(-1, a.shape[-1])


def _cast_layer(name, a, layer, slabs=0):
    _, r, c = a.shape
    tr = _row_tile(r, c)

    def body(a_ref, o_ref):
        o_ref[...] = a_ref[...].astype(BF16)

    if slabs:
        out_spec = pl.BlockSpec((None, tr, c), lambda i: (_my_shard(), i, 0))
        out_shape = jax.ShapeDtypeStruct((slabs, r, c), BF16)
    else:
        out_spec = pl.BlockSpec((tr, c), lambda i: (i, 0))
        out_shape = jax.ShapeDtypeStruct((r, c), BF16)
    return pl.pallas_call(body, name=name, grid=(r // tr,),
                          in_specs=[pl.BlockSpec((None, tr, c), lambda i: (layer, i, 0))],
                          out_specs=out_spec, out_shape=out_shape,
                          compiler_params=_params(("parallel",)))(a)


def _my_shard():
    return 2 * lax.axis_index("x") + lax.axis_index("y")


def _peer_shard(k):
    x, y = lax.axis_index("x"), lax.axis_index("y")
    px = jnp.where(k == 1, x, 1 - x)
    py = jnp.where(k == 0, y, 1 - y)
    return 2 * px + py


def _pair_sum(name, own, recv):
    _, r, c = own.shape
    h = r // 2
    tr = _row_tile(h, c)
    nb = h // tr

    def body(own_ref, recv_ref, o_ref):
        o_ref[...] = (own_ref[...].astype(F32) + recv_ref[...].astype(F32)).astype(BF16)

    mine = pl.BlockSpec((None, tr, c), lambda k, i: (_peer_shard(k), lax.axis_index("c") * nb + i, 0))
    theirs = pl.BlockSpec((None, tr, c), lambda k, i: (_peer_shard(k), i, 0))
    return pl.pallas_call(body, name=name, grid=(N_CHIPS - 1, nb), in_specs=[mine, theirs],
                          out_specs=pl.BlockSpec((None, tr, c), lambda k, i: (k, i, 0)),
                          out_shape=jax.ShapeDtypeStruct((N_CHIPS - 1, h, c), BF16),
                          compiler_params=_params(("parallel", "parallel")))(own, recv)


def _final_sum(name, own, recv, got, layer, n_layers, prev):
    _, r, c = own.shape
    h = r // 2
    tr = _row_tile(h, c)
    nb = h // tr

    def body(own_ref, recv_ref, q0_ref, q1_ref, q2_ref, *rest):
        o_ref = rest[-1]
        acc = own_ref[...].astype(F32) + recv_ref[...].astype(F32)
        acc = acc + q0_ref[...].astype(F32)
        acc = acc + q1_ref[...].astype(F32)
        o_ref[...] = acc + q2_ref[...].astype(F32)

    in_specs = [pl.BlockSpec((None, tr, c), lambda i: (_my_shard(), lax.axis_index("c") * nb + i, 0)),
                pl.BlockSpec((None, tr, c), lambda i: (_my_shard(), i, 0))]
    in_specs += [pl.BlockSpec((None, tr, c), functools.partial(lambda i, k: (k, i, 0), k=k)) for k in range(3)]
    operands = [own, recv, got, got, got]
    aliases = {}
    if prev is not None:
        in_specs.append(pl.BlockSpec(memory_space=pl.ANY))
        operands.append(prev)
        aliases = {5: 0}
    return pl.pallas_call(body, name=name, grid=(nb,), in_specs=in_specs,
                          out_specs=pl.BlockSpec((None, tr, c), lambda i: (layer, lax.axis_index("c") * nb + i, 0)),
                          out_shape=jax.ShapeDtypeStruct((n_layers, r, c), F32),
                          input_output_aliases=aliases,
                          compiler_params=_params(("parallel",)))(*operands)


def _adam(name, w, g, m, v, emit_grad=False):
    r, c = w.shape
    tr = _row_tile(r, c)
    c1 = 1.0 / (1.0 - ADAM_B1 ** ADAM_STEP)
    c2 = 1.0 / (1.0 - ADAM_B2 ** ADAM_STEP)
    n_out = 4 if emit_grad else 3

    def body(w_ref, g_ref, m_ref, v_ref, d_ref, nm_ref, nv_ref, *g_out):
        gv = g_ref[...]
        if emit_grad:
            g_out[0][...] = gv
        nm = ADAM_B1 * m_ref[...] + (1.0 - ADAM_B1) * gv
        nv = ADAM_B2 * v_ref[...] + (1.0 - ADAM_B2) * (gv * gv)
        d_ref[...] = -ADAM_LR * ((nm * c1) / (jnp.sqrt(nv * c2) + ADAM_EPS) + ADAM_WD * w_ref[...])
        nm_ref[...] = nm
        nv_ref[...] = nv

    blk = pl.BlockSpec((tr, c), lambda i: (i, 0))
    shp = jax.ShapeDtypeStruct((r, c), F32)
    return pl.pallas_call(body, name=name, grid=(r // tr,), in_specs=[blk] * 4, out_specs=[blk] * n_out,
                          out_shape=[shp] * n_out, compiler_params=_params(("parallel",)))(w, g, m, v)


def _sum_devices(name, parts):
    _, r, c = parts.shape
    tr = _row_tile(r, N_DEV * c)

    def body(p_ref, o_ref):
        acc = p_ref[0]
        for k in range(1, N_DEV):
            acc = acc + p_ref[k]
        o_ref[...] = acc

    return pl.pallas_call(body, name=name, grid=(r // tr,),
                          in_specs=[pl.BlockSpec((N_DEV, tr, c), lambda i: (0, i, 0))],
                          out_specs=pl.BlockSpec((tr, c), lambda i: (i, 0)),
                          out_shape=jax.ShapeDtypeStruct((r, c), F32),
                          compiler_params=_params(("parallel",)))(parts)


def _coords():
    return lax.axis_index("x"), lax.axis_index("y"), lax.axis_index("c")


def _chip_peers(x, y):
    return [(1 - x, y, 2 * (1 - x) + y), (x, 1 - y, 2 * x + 1 - y), (1 - x, 1 - y, 2 * (1 - x) + 1 - y)]


def _remote(src, dst, ssem, rsem, dev):
    return pltpu.make_async_remote_copy(src_ref=src, dst_ref=dst, send_sem=ssem, recv_sem=rsem,
                                        device_id=dev, device_id_type=MESH)


ANY = pl.BlockSpec(memory_space=pl.ANY)


SEM = pl.BlockSpec(memory_space=pltpu.SEMAPHORE)
HBM = pl.BlockSpec(memory_space=pltpu.HBM)
DATAFLOW = pltpu.SideEffectType.DATAFLOW_SIDE_EFFECTING


def _hbm(a):
    return pltpu.with_memory_space_constraint(a, pltpu.HBM)


def _split_start(name, make_copies, n_copies, srcs, land_shapes, dep):
    n, nl = len(srcs), len(land_shapes)
    nb = n + nl

    def body(*refs):
        src_refs, land_refs = refs[:n], refs[n:nb]
        ssem, rsem = refs[nb + 1], refs[nb + 2]
        token = refs[-1]
        for send, _ in make_copies(src_refs, land_refs, ssem, rsem):
            send.start()
        token[...] = jnp.zeros_like(token)

    lands = [_hbm(lax.empty(shp, dt)) for shp, dt in land_shapes]
    out_shape = [pltpu.SemaphoreType.DMA((n_copies,)), pltpu.SemaphoreType.DMA((n_copies,))]
    out_shape += [pltpu.HBM(a.shape, a.dtype) for a in list(srcs) + lands]
    out_shape.append(jax.ShapeDtypeStruct((ROW_ALIGN, LANES), F32))
    res = pl.pallas_call(
        body, name=name, out_shape=out_shape,
        in_specs=[HBM] * nb + [ANY],
        out_specs=[SEM, SEM] + [HBM] * nb + [pl.BlockSpec(memory_space=pltpu.VMEM)],
        input_output_aliases={i: 2 + i for i in range(nb)},
        compiler_params=pltpu.CompilerParams(has_side_effects=DATAFLOW),
    )(*[_hbm(a) for a in srcs], *lands, dep)
    return res[0], res[1], res[2:2 + n], res[2 + n:2 + nb], res[-1]


def _split_wait(name, make_copies, ssem, rsem, srcs, lands, after):
    n, nb = len(srcs), len(srcs) + len(lands)

    def body(*refs):
        src_refs, land_refs = refs[:n], refs[n:nb]
        for send, recv in make_copies(src_refs, land_refs, refs[nb], refs[nb + 1]):
            send.wait_send()
            recv.wait_recv()

    res = pl.pallas_call(
        body, name=name, out_shape=[pltpu.HBM(a.shape, a.dtype) for a in list(srcs) + list(lands)],
        in_specs=[HBM] * nb + [SEM, SEM, ANY], out_specs=[HBM] * nb,
        input_output_aliases={i: i for i in range(nb)},
        compiler_params=pltpu.CompilerParams(has_side_effects=DATAFLOW),
    )(*srcs, *lands, ssem, rsem, after)
    return res[:n], res[n:]


def _gather_copies(buf_refs, _, ssem, rsem):
    x, y, c = _coords()
    j = 2 * x + y
    out = []
    for i, buf in enumerate(buf_refs):
        h = buf.shape[1] // 2
        rows = pl.ds(c * h, h)
        for k, (px, py, pj) in enumerate(_chip_peers(x, y)):
            sems = (ssem.at[3 * i + k], rsem.at[3 * i + k], (px, py, c))
            out.append((_remote(buf.at[j, rows], buf.at[j, rows], *sems),
                        _remote(buf.at[pj, rows], buf.at[pj, rows], *sems)))
    return out


def _scatter_copies(src_refs, land_refs, ssem, rsem):
    x, y, c = _coords()
    out = []
    for i, (src, land) in enumerate(zip(src_refs, land_refs)):
        for k, (px, py, _) in enumerate(_chip_peers(x, y)):
            cp = _remote(src.at[k], land.at[k], ssem.at[3 * i + k], rsem.at[3 * i + k], (px, py, c))
            out.append((cp, cp))
    return out


def _sibling_pass(name, lands, dep):
    n = len(lands)

    def body(*refs):
        bufs = refs[n + 1:2 * n + 1]
        ssem, rsem = refs[2 * n + 1:]
        x, y, c = _coords()
        sends = []
        for i in range(n):
            h = bufs[i].shape[1] // 2
            for k, (px, py, pj) in enumerate(_chip_peers(x, y)):
                mine = bufs[i].at[pj, pl.ds(c * h, h)]
                sends.append(_remote(mine, mine, ssem.at[3 * i + k], rsem.at[3 * i + k], (x, y, 1 - c)))
        for cp in sends:
            cp.start()
        for i in range(n):
            h = bufs[i].shape[1] // 2
            for k, (px, py, pj) in enumerate(_chip_peers(x, y)):
                theirs = bufs[i].at[pj, pl.ds((1 - c) * h, h)]
                _remote(theirs, theirs, ssem.at[3 * i + k], rsem.at[3 * i + k], (x, y, 1 - c)).wait_recv()
        for cp in sends:
            cp.wait_send()

    return pl.pallas_call(
        body, name=name, in_specs=[ANY] * (n + 1), out_specs=[ANY] * n,
        out_shape=[jax.ShapeDtypeStruct(a.shape, a.dtype) for a in lands],
        input_output_aliases={i: i for i in range(n)},
        scratch_shapes=[pltpu.SemaphoreType.DMA((3 * n,)), pltpu.SemaphoreType.DMA((3 * n,))],
    )(*lands, dep)


def _gather_small(small):
    def body(in_ref, out_ref, ssem, rsem, lsem):
        x, y, c = _coords()
        j = 2 * x + y
        local = pltpu.make_async_copy(in_ref, out_ref.at[j], lsem)
        local.start()
        peers = _chip_peers(x, y)
        sends = [_remote(in_ref, out_ref.at[j], ssem.at[k], rsem.at[k], (px, py, c))
                 for k, (px, py, pj) in enumerate(peers)]
        for cp in sends:
            cp.start()
        for k, (px, py, pj) in enumerate(peers):
            sends[k].wait_send()
            _remote(in_ref, out_ref.at[pj], ssem.at[k], rsem.at[k], (px, py, c)).wait_recv()
        local.wait()

    return pl.pallas_call(body, name="gather_small", in_specs=[ANY], out_specs=ANY,
                          out_shape=jax.ShapeDtypeStruct((N_CHIPS,) + small.shape, small.dtype),
                          scratch_shapes=[pltpu.SemaphoreType.DMA((3,)), pltpu.SemaphoreType.DMA((3,)),
                                          pltpu.SemaphoreType.DMA])(small)


def _pair_copies(src_refs, land_refs, ssem, rsem):
    x, y, c = _coords()
    out = []
    for i, (src, land) in enumerate(zip(src_refs, land_refs)):
        h = src.shape[1] // 2
        cp = _remote(src.at[:, pl.ds((1 - c) * h, h)], land, ssem.at[i], rsem.at[i], (x, y, 1 - c))
        out.append((cp, cp))
    return out


def _swap_copies(layers):
    def make(buf_refs, _, ssem, rsem):
        x, y, c = _coords()
        out = []
        for i, buf in enumerate(buf_refs):
            h = buf.shape[1] // 2
            mine = buf.at[layers[i], pl.ds(c * h, h)]
            theirs = buf.at[layers[i], pl.ds((1 - c) * h, h)]
            sems = (ssem.at[i], rsem.at[i], (x, y, 1 - c))
            out.append((_remote(mine, mine, *sems), _remote(theirs, theirs, *sems)))
        return out

    return make


def _small_copies(buf_refs, _, ssem, rsem):
    x, y, c = _coords()
    buf = buf_refs[0]
    me = 4 * x + 2 * y + c
    out = []
    for r in range(1, N_DEV):
        px = 1 - x if r & 4 else x
        py = 1 - y if r & 2 else y
        pc = 1 - c if r & 1 else c
        peer = 4 * px + 2 * py + pc
        sems = (ssem.at[r - 1], rsem.at[r - 1], (px, py, pc))
        out.append((_remote(buf.at[me], buf.at[me], *sems), _remote(buf.at[peer], buf.at[peer], *sems)))
    return out


ROW_ALIGN = SUBLANES
SMALL_WIDTH = 8 * LANES
LAYER_WEIGHTS = (("a_w_in", "a_w_out"), ("b_w_in", "b_w_pool", "b_w_out"), ("c_w_in", "c_w_out"))
SMALL_SHARDED = ("a_ln_g", "a_ln_b", "c_conv_w", "c_conv_b", "c_ln_g", "c_ln_b")
SMALL_ORDER = ("a_ln_g", "a_ln_b", "a_w_s", "a_b_s", "b_scale", "c_conv_w", "c_conv_b", "c_ln_g", "c_ln_b",
               "post_ln_g", "post_ln_b")
WEIGHTS = ("a_w_in", "a_ln_g", "a_ln_b", "a_w_s", "a_b_s", "a_w_out", "b_w_in", "b_w_pool", "b_scale", "b_w_out",
           "c_w_in", "c_conv_w", "c_conv_b", "c_ln_g", "c_ln_b", "c_w_out", "post_ln_g", "post_ln_b")


def _pad_rows(a):
    pad = -a.shape[0] % ROW_ALIGN
    return jnp.pad(a, ((0, pad), (0, 0))) if pad else a


def _pack(parts, width):
    return jnp.concatenate([_pad_rows(p.reshape(-1, width)) for p in parts], axis=0)


def _unpack(packed, shapes, width):
    out, row = [], 0
    for shp in shapes:
        size = 1
        for n in shp:
            size *= n
        rows = size // width
        out.append(packed[row:row + rows].reshape(shp))
        row += rows + (-rows % ROW_ALIGN)
    return out


def _step(x, tgt, w, m, v):
    s, d = x.shape
    cx, cy, cc = _coords()
    j_me = 2 * cx + cy
    es = w["a_ln_g"].shape[1]
    e = N_CHIPS * es

    n_a, n_c = w["a_w_in"].shape[0], w["c_w_in"].shape[0]
    ng = w["b_w_pool"].shape[1]
    as3d = lambda a: a.reshape(a.shape[0], -1, a.shape[-1])

    zones = [[_cast_layer("cast_%s%d" % (k, i // 3), as3d(w[k]), i // 3, N_CHIPS) for k in LAYER_WEIGHTS[i % 3]]
             for i in range(DEPTH)]

    def gather_start(i, dep):
        return _split_start("gather_start%d" % i, _gather_copies, 3 * len(zones[i]), zones[i], [], dep)

    small_all = _gather_small(_pack([w[k] for k in SMALL_SHARDED], es))
    ssem, rsem, lands, _, tok = gather_start(0, small_all)
    xb = _cast_layer("cast_x", x[None], 0)
    small_full = jnp.swapaxes(small_all, 0, 1).reshape(small_all.shape[1], e)
    full_shapes = [(n_a, e), (n_a, e), (CONV_WIDTH, e), (1, e), (1, e), (1, e)]
    a_ln_g, a_ln_b, conv_w, conv_b, c_ln_g, c_ln_b = _unpack(small_full, full_shapes, e)

    post_g, post_b = w["post_ln_g"], w["post_ln_b"]
    b_scale = w["b_scale"]

    saved = []
    res, gp, bp, hb = x, jnp.ones((1, d), F32), jnp.zeros((1, d), F32), xb
    for i in range(DEPTH):
        kind, l = i % 3, i // 3
        tag = "L%d" % i
        after = zones[DEPTH - 1][-1] if i == 0 else res
        lands, _ = _split_wait("gather_wait%d" % i, _gather_copies, ssem, rsem, lands, [], after)
        full = _sibling_pass("sibling_pass%d" % i, lands, tok)
        if i + 1 < DEPTH:
            ssem, rsem, lands, _, tok = gather_start(i + 1, full[0])
        wl = dict(zip(LAYER_WEIGHTS[kind], full))
        w_in = wl[LAYER_WEIGHTS[kind][0]].reshape(N_CHIPS, 1, d, -1)
        w_out = wl[LAYER_WEIGHTS[kind][-1]].reshape(N_CHIPS, 1, es, d)
        proj = _mm_in(tag + "_in", hb, w_in, 0, dep=tok)
        if kind == 0:
            b_st = w["a_b_s"][l].T
            yg = _sgu_fwd(tag + "_sgu", proj, a_ln_g[l:l + 1], a_ln_b[l:l + 1], w["a_w_s"][l], b_st)
            extra = (b_st,)
        elif kind == 1:
            w_pool = wl["b_w_pool"].reshape(N_CHIPS, ng, -1, w["b_w_pool"].shape[3])
            p = _pool_fwd(tag + "_pool", proj)
            q, yg = _pool_mm(tag + "_poolmm", p, w_pool, proj, b_scale[l:l + 1])
            extra = (p, q, w_pool)
        else:
            cv = _conv_fwd(tag + "_conv", proj, conv_w, conv_b)
            yg = _conv_gate_fwd(tag + "_gate", cv, proj, c_ln_g, c_ln_b)
            extra = (cv,)
        xhat, hb_next, rstd = _mm_out_ln(tag + "_out", yg, w_out, 0, res, gp, bp, post_g[i:i + 1], post_b[i:i + 1])
        saved.append((hb, proj, yg, xhat, rstd, extra, w_in, w_out))
        res, gp, bp, hb = xhat, post_g[i:i + 1], post_b[i:i + 1], hb_next

    sg = {k: [None] * w[k].shape[0] for k in ("a_ln_g", "a_ln_b", "a_w_s", "a_b_s", "post_ln_g", "post_ln_b")}
    scat = {}

    def finish_pair(p, after):
        i, pssem, prsem, own, recv, dep = p
        own, recv = _split_wait("pair_wait%d" % i, _pair_copies, pssem, prsem, own, recv, after)
        pair = [_pair_sum("pairsum%d_%d" % (i, n), a, r) for n, (a, r) in enumerate(zip(own, recv))]
        ssem, rsem, srcs, lands, token = _split_start("scatter_start%d" % i, _scatter_copies, 3 * len(pair), pair,
                                                      [((3,) + a.shape[1:], a.dtype) for a in pair], dep)
        scat[i] = (own, recv, ssem, rsem, srcs, lands)
        return token

    grads, delta, new_m, new_v = {}, {}, {}, {}

    def adam(k, emit_grad=False):
        shp = w[k].shape
        two_d = (lambda a: a.reshape(-1, shp[-1])) if shp[-1] % LANES == 0 else (lambda a: a.reshape(1, -1))
        res = _adam("adam_" + k, two_d(w[k]), two_d(grads[k]), two_d(m[k]), two_d(v[k]), emit_grad)
        delta[k], new_m[k], new_v[k] = (a.reshape(shp) for a in res[:3])
        if emit_grad:
            grads[k] = res[3].reshape(shp)
        return res[0]

    reduced, swaps = {}, {}

    def reduce_layer(i, after):
        kind, l = i % 3, i // 3
        names = LAYER_WEIGHTS[kind]
        own, recv, ssem, rsem, srcs, lands = scat[i]
        _, got = _split_wait("scatter_wait%d" % i, _scatter_copies, ssem, rsem, srcs, lands, after)
        for n, k in enumerate(names):
            reduced[k] = _final_sum("finalsum%d_%d" % (i, n), own[n], recv[n], got[n], l, w[k].shape[0],
                                    reduced.get(k))
        make = _swap_copies([l] * len(names))
        ssem, rsem, bufs, _, token = _split_start("swap_start%d" % i, make, len(names), [reduced[k] for k in names],
                                                  [], got[0])
        swaps[i] = (make, ssem, rsem, bufs)
        return token

    def finish_swap(i, after, with_adam):
        names = LAYER_WEIGHTS[i % 3]
        make, ssem, rsem, bufs = swaps[i]
        bufs, _ = _split_wait("swap_wait%d" % i, make, ssem, rsem, bufs, [], after)
        out = after
        for k, g in zip(names, bufs):
            reduced[k] = g
            if with_adam:
                grads[k] = g.reshape(w[k].shape)
                out = adam(k, True)
        return out

    pending = None
    dr, drb, dg, db, loss_vec = _post_ln_bwd("L3_lnbwd", tgt, saved[-1][3], saved[-1][4],
                                             post_g[DEPTH - 1:DEPTH], post_b[DEPTH - 1:DEPTH], True)
    grad_x = None
    for i in reversed(range(DEPTH)):
        kind, l = i % 3, i // 3
        tag = "L%d" % i
        hb_in, proj, yg, _, _, extra, w_in, w_out = saved[i]
        sg["post_ln_g"][i], sg["post_ln_b"][i] = dg, db
        started = None if pending is None else finish_pair(pending, drb)
        dyg = _mm_dyg(tag + "_dyg", drb, w_out, 0, dep=started)
        dw_mid = []
        if kind == 0:
            dproj, dlg, dlb, dws_s, dbs_t = _sgu_bwd(tag + "_sgubwd", dyg, proj, a_ln_g[l:l + 1], a_ln_b[l:l + 1],
                                                     w["a_w_s"][l], extra[0])
            sg["a_ln_g"][l], sg["a_ln_b"][l], sg["a_w_s"][l], sg["a_b_s"][l] = dlg, dlb, dws_s, dbs_t.T
        elif kind == 1:
            p, q, w_pool = extra
            dq, dproj, dscale = _pool_bwd1(tag + "_poolbwd1", dyg, q, proj, b_scale[l:l + 1])
            dp = _pool_mm_bwd(tag + "_poolmmbwd", dq, w_pool)
            dproj = _pool_bwd2(tag + "_poolbwd2", dp, dproj)
            dw_mid = [_mm_dw_pool(tag + "_dwpool", p, dq, ng).reshape(N_CHIPS, -1, dq.shape[1] // ng)]
            sg["b_scale"] = [dscale]
        else:
            cv = extra[0]
            dc, dproj, dlg, dlb, dcb = _conv_bwd1(tag + "_convbwd1", dyg, cv, proj, c_ln_g, c_ln_b)
            dgc, dcw = _conv_bwd2(tag + "_convbwd2", dc, proj, conv_w)
            dproj = _conv_bwd3(tag + "_convbwd3", dgc, proj, dproj)
            sg["c_ln_g"], sg["c_ln_b"], sg["c_conv_b"], sg["c_conv_w"] = [dlg], [dlb], [dcb], [dcw[:CONV_WIDTH]]
        if i == 0:
            packed = _pack([jnp.concatenate([a.reshape(1, -1) for a in sg[k]], axis=0) for k in SMALL_ORDER], SMALL_WIDTH)
            slabs = lax.dynamic_update_slice(jnp.zeros((N_DEV,) + packed.shape, F32), packed[None],
                                             (4 * cx + 2 * cy + cc, 0, 0))
            small = _split_start("small_start", _small_copies, N_DEV - 1, [slabs], [], dproj)
        dw_out = _mm_dw_out(tag + "_dwout", yg, drb, dep=small[-1] if i == 0 else None).reshape(N_CHIPS, es, d)
        if i > 0:
            dh = _mm_dh(tag + "_dh", dproj, w_in, 0, dr)
        dw_in = _mm_dw_in(tag + "_dwin", hb_in, dproj)
        own = [dw_in] + dw_mid + [dw_out]
        pending = (i,) + _split_start("pair_start%d" % i, _pair_copies, len(own), own,
                                      [((N_CHIPS, a.shape[1] // 2, a.shape[2]), a.dtype) for a in own], dproj)
        if i > 0:
            prev = saved[i - 1]
            dr, drb, dg, db, _ = _post_ln_bwd("L%d_lnbwd" % (i - 1), dh, prev[3], prev[4],
                                              post_g[i - 1:i], post_b[i - 1:i], False, dep=pending[-1])
        else:
            after = pending[-1]
            for done in range(DEPTH - 1, 1, -1):
                after = reduce_layer(done, after)
            grad_x = _mm_dh(tag + "_dh", dproj, w_in, 0, dr, dep=finish_pair(pending, after))

    after = reduce_layer(1, grad_x)
    for i in range(DEPTH - 2, 0, -1):
        after = finish_swap(i, after, True)
    finish_swap(DEPTH - 1, after, False)

    width = SMALL_WIDTH
    gathered, _ = _split_wait("small_wait", _small_copies, small[0], small[1], small[2], [], after)
    summed = _sum_devices("sum_small", gathered[0])
    full_small_shapes = {"a_ln_g": (n_a, e), "a_ln_b": (n_a, e), "a_w_s": w["a_w_s"].shape, "a_b_s": w["a_b_s"].shape,
                         "b_scale": w["b_scale"].shape, "c_conv_w": (n_c, CONV_WIDTH, e), "c_conv_b": (n_c, e),
                         "c_ln_g": (n_c, e), "c_ln_b": (n_c, e), "post_ln_g": post_g.shape, "post_ln_b": post_b.shape}
    for k, g in zip(SMALL_ORDER, _unpack(summed, [full_small_shapes[k] for k in SMALL_ORDER], width)):
        if k in SMALL_SHARDED:
            g = lax.dynamic_slice_in_dim(g, j_me * es, es, axis=g.ndim - 1)
        grads[k] = g
        after = adam(k)

    finish_swap(0, reduce_layer(0, after), True)

    loss = lax.psum(loss_vec[0, 0], ("x", "y", "c"))
    return loss, grad_x, grads, delta, new_m, new_v


def kernel(x, a_w_in, a_ln_g, a_ln_b, a_w_s, a_b_s, a_w_out, b_w_in, b_w_pool, b_scale, b_w_out, c_w_in, c_conv_w, c_conv_b, c_ln_g, c_ln_b, c_w_out, post_ln_g, post_ln_b, loss_target, m_a_w_in, m_a_ln_g, m_a_ln_b, m_a_w_s, m_a_b_s, m_a_w_out, m_b_w_in, m_b_w_pool, m_b_scale, m_b_w_out, m_c_w_in, m_c_conv_w, m_c_conv_b, m_c_ln_g, m_c_ln_b, m_c_w_out, m_post_ln_g, m_post_ln_b, v_a_w_in, v_a_ln_g, v_a_ln_b, v_a_w_s, v_a_b_s, v_a_w_out, v_b_w_in, v_b_w_pool, v_b_scale, v_b_w_out, v_c_w_in, v_c_conv_w, v_c_conv_b, v_c_ln_g, v_c_ln_b, v_c_w_out, v_post_ln_g, v_post_ln_b):
    args = locals()
    w = {k: args[k] for k in WEIGHTS}
    m = {k: args["m_" + k] for k in WEIGHTS}
    v = {k: args["v_" + k] for k in WEIGHTS}
    loss, grad_x, grads, delta, new_m, new_v = _step(x[0], loss_target[0], w, m, v)
    out = [loss, grad_x[None]]
    for group in (grads, delta, new_m, new_v):
        out.extend(group[k] for k in WEIGHTS)
    return tuple(out)
```

```python
import functools

import jax
import jax.numpy as jnp
from jax import lax
from jax.experimental import pallas as pl
from jax.experimental.pallas import tpu as pltpu

F32 = jnp.float32
BF16 = jnp.bfloat16
MESH = pl.DeviceIdType.MESH

DEPTH = 4
CHUNK = 128
SGU_HEADS = 8
POOL_WINDOWS = (2, 4, 8, 16)
POOL_HALO = 16
CONV_WIDTH = 31
CONV_HALO = 32
LN_EPS = 1e-5
ALPHA = (2.0 * DEPTH) ** 0.25
ADAM_LR, ADAM_B1, ADAM_B2, ADAM_EPS, ADAM_WD, ADAM_STEP = 0.001, 0.9, 0.999, 1e-08, 0.01, 10
N_CHIPS = 4
N_DEV = 8
LANES = 128
SUBLANES = 8
CONV_STRIDE = 4
CONV_SUB = SUBLANES * CONV_STRIDE
VMEM_LIMIT = 56 << 20
GELU_K = 0.7978845608028654
GELU_C = 0.044715


def _params(sem=None, vmem=VMEM_LIMIT):
    return pltpu.CompilerParams(dimension_semantics=sem, vmem_limit_bytes=vmem)


def _tile(dim, pref):
    if dim <= pref:
        return dim
    t = 1 << (pref.bit_length() - 1)
    while dim % t:
        t //= 2
    return t


ROW_BLOCK_BYTES = 2 << 20


def _row_tile(rows, cols):
    return _tile(rows, max(ROW_ALIGN, ROW_BLOCK_BYTES // (4 * cols)))


def _sigmoid(x):
    return 1.0 / (1.0 + jnp.exp(-x))


def _gelu_gate(x):
    x2 = x * x
    return _sigmoid((2.0 * GELU_K) * x * (1.0 + GELU_C * x2)), x2


def _gelu(x):
    return 0.5 * x * (1.0 + jnp.tanh(GELU_K * (x + GELU_C * x * x * x)))


def _gelu_and_grad(x):
    s, x2 = _gelu_gate(x)
    return x * s, s + x * (s * (1.0 - s)) * ((2.0 * GELU_K) * (1.0 + (3.0 * GELU_C) * x2))


def _ln_stats(x):
    mu = jnp.mean(x, axis=-1, keepdims=True)
    d = x - mu
    var = jnp.mean(d * d, axis=-1, keepdims=True)
    rstd = lax.rsqrt(var + LN_EPS)
    return d * rstd, rstd


def _ln_bwd(dy_hat, xhat, rstd):
    m1 = jnp.mean(dy_hat, axis=-1, keepdims=True)
    m2 = jnp.mean(dy_hat * xhat, axis=-1, keepdims=True)
    return rstd * (dy_hat - m1 - xhat * m2)


def _rowsum(x):
    return jnp.sum(x, axis=0, keepdims=True)


def _mm(name, a, b, *, dims, grid, a_spec, b_spec, out_shape, out_specs, acc_shape,
        extra=(), extra_specs=(), epilogue=None, dep=None):
    nk = grid[2]
    n_extra = len(extra)
    single = not isinstance(out_shape, (list, tuple))
    shapes = [out_shape] if single else list(out_shape)
    specs = [out_specs] if single else list(out_specs)
    n_out = len(shapes)
    deps = [] if dep is None else [dep]
    first_out = 2 + n_extra + len(deps)

    def body(*refs):
        a_ref, b_ref = refs[0], refs[1]
        ex = refs[2:2 + n_extra]
        outs = refs[first_out:first_out + n_out]

        def finish(acc):
            if epilogue is None:
                outs[0][...] = acc.astype(outs[0].dtype)
            else:
                epilogue(acc, ex, outs)

        part = lax.dot_general(a_ref[...].astype(BF16), b_ref[...].astype(BF16), dims, preferred_element_type=F32)
        if nk == 1:
            finish(part)
            return
        acc_ref = refs[-1]
        k = pl.program_id(2)

        @pl.when(k == 0)
        def _():
            acc_ref[...] = part

        @pl.when(jnp.logical_and(k > 0, k < nk - 1))
        def _():
            acc_ref[...] += part

        @pl.when(k == nk - 1)
        def _():
            finish(acc_ref[...] + part)

    res = pl.pallas_call(
        body, name=name, grid=grid, in_specs=[a_spec, b_spec, *extra_specs] + [ANY] * len(deps), out_specs=specs,
        out_shape=shapes, scratch_shapes=[pltpu.VMEM(acc_shape, F32)] if nk > 1 else [],
        compiler_params=_params(("parallel", "parallel", "arbitrary")),
    )(a, b, *extra, *deps)
    return res[0] if single else res


NN = (((1,), (0,)), ((), ()))
NT = (((1,), (1,)), ((), ()))
TN = (((0,), (0,)), ((), ()))


def _mm_in(name, hb, wg, layer, dep=None):
    s, d = hb.shape
    nsh = wg.shape[3]
    n = N_CHIPS * nsh
    bm, bn, bk = _tile(s, 1024), _tile(nsh, 1024), _tile(d, 2048)
    per = nsh // bn
    return _mm(name, hb, wg, dims=NN, grid=(s // bm, n // bn, d // bk),
               a_spec=pl.BlockSpec((bm, bk), lambda m, j, k: (m, k)),
               b_spec=pl.BlockSpec((None, None, bk, bn), lambda m, j, k: (j // per, layer, k, j % per)),
               out_shape=jax.ShapeDtypeStruct((s, n), F32),
               out_specs=pl.BlockSpec((bm, bn), lambda m, j, k: (m, j)),
               acc_shape=(bm, bn), dep=dep)


def _mm_out_ln(name, yg, wg, layer, res, gp, bp, g, b):
    s, e = yg.shape
    esh, d = wg.shape[2], wg.shape[3]
    bm, bk = _tile(s, 512), _tile(esh, 1024)
    per = esh // bk

    def epilogue(acc, ex, outs):
        res_ref, gp_ref, bp_ref, g_ref, b_ref = ex
        xhat_ref, hb_ref, rstd_ref = outs
        r = ALPHA * (res_ref[...] * gp_ref[...] + bp_ref[...]) + acc
        xhat, rstd = _ln_stats(r)
        xhat_ref[...] = xhat
        hb_ref[...] = (xhat * g_ref[...] + b_ref[...]).astype(BF16)
        rstd_ref[...] = jnp.broadcast_to(rstd, rstd_ref.shape)

    vec = pl.BlockSpec((1, d), lambda m, j, k: (0, 0))
    row = pl.BlockSpec((bm, d), lambda m, j, k: (m, 0))
    return _mm(name, yg, wg, dims=NN, grid=(s // bm, 1, e // bk),
               a_spec=pl.BlockSpec((bm, bk), lambda m, j, k: (m, k)),
               b_spec=pl.BlockSpec((None, None, bk, d), lambda m, j, k: (k // per, layer, k % per, 0)),
               extra=(res, gp, bp, g, b), extra_specs=(row, vec, vec, vec, vec),
               out_shape=(jax.ShapeDtypeStruct((s, d), F32), jax.ShapeDtypeStruct((s, d), BF16),
                          jax.ShapeDtypeStruct((s, LANES), F32)),
               out_specs=(row, row, pl.BlockSpec((bm, LANES), lambda m, j, k: (m, 0))),
               acc_shape=(bm, d), epilogue=epilogue)


def _mm_dyg(name, drb, wg, layer, dep=None):
    s, d = drb.shape
    esh = wg.shape[2]
    e = N_CHIPS * esh
    bm, bn, bk = _tile(s, 1024), _tile(esh, 1024), _tile(d, 2048)
    per = esh // bn
    return _mm(name, drb, wg, dims=NT, grid=(s // bm, e // bn, d // bk),
               a_spec=pl.BlockSpec((bm, bk), lambda m, j, k: (m, k)),
               b_spec=pl.BlockSpec((None, None, bn, bk), lambda m, j, k: (j // per, layer, j % per, k)),
               out_shape=jax.ShapeDtypeStruct((s, e), BF16),
               out_specs=pl.BlockSpec((bm, bn), lambda m, j, k: (m, j)),
               acc_shape=(bm, bn), dep=dep)


def _mm_dw_out(name, yg, drb, dep=None):
    s, e = yg.shape
    d = drb.shape[1]
    bm, bn, bk = _tile(e, 1024), _tile(d, 1024), _tile(s, 4096)
    return _mm(name, yg, drb, dims=TN, grid=(e // bm, d // bn, s // bk),
               a_spec=pl.BlockSpec((bk, bm), lambda m, j, k: (k, m)),
               b_spec=pl.BlockSpec((bk, bn), lambda m, j, k: (k, j)),
               out_shape=jax.ShapeDtypeStruct((e, d), BF16),
               out_specs=pl.BlockSpec((bm, bn), lambda m, j, k: (m, j)),
               acc_shape=(bm, bn), dep=dep)


def _mm_dh(name, dproj, wg, layer, dr, dep=None):
    s, n = dproj.shape
    d, nsh = wg.shape[2], wg.shape[3]
    bm, bn = _tile(s, 1024), _tile(d, 1024)
    bk = max(t for t in range(LANES, 3073, LANES) if nsh % t == 0)
    per = nsh // bk

    def epilogue(acc, ex, outs):
        outs[0][...] = ALPHA * ex[0][...] + acc

    blk = pl.BlockSpec((bm, bn), lambda m, j, k: (m, j))
    return _mm(name, dproj, wg, dims=NT, grid=(s // bm, d // bn, n // bk),
               a_spec=pl.BlockSpec((bm, bk), lambda m, j, k: (m, k)),
               b_spec=pl.BlockSpec((None, None, bn, bk), lambda m, j, k: (k // per, layer, j, k % per)),
               extra=(dr,), extra_specs=(blk,),
               out_shape=jax.ShapeDtypeStruct((s, d), F32), out_specs=blk,
               acc_shape=(bm, bn), epilogue=epilogue, dep=dep)


def _mm_dw_in(name, hb, dproj):
    s, d = hb.shape
    n = dproj.shape[1]
    nsh = n // N_CHIPS
    bm, bn, bk = _tile(d, 1024), _tile(nsh, 1024), _tile(s, 4096)
    per = nsh // bn
    return _mm(name, hb, dproj, dims=TN, grid=(d // bm, n // bn, s // bk),
               a_spec=pl.BlockSpec((bk, bm), lambda m, j, k: (k, m)),
               b_spec=pl.BlockSpec((bk, bn), lambda m, j, k: (k, j)),
               out_shape=jax.ShapeDtypeStruct((N_CHIPS, d, nsh), BF16),
               out_specs=pl.BlockSpec((None, bm, bn), lambda m, j, k: (j // per, m, j % per)),
               acc_shape=(bm, bn))


def _causal_mask():
    r = lax.broadcasted_iota(jnp.int32, (CHUNK, CHUNK), 0)
    c = lax.broadcasted_iota(jnp.int32, (CHUNK, CHUNK), 1)
    return r >= c


def _sgu_fwd(name, proj, ln_g, ln_b, w_s, b_st):
    s, n3 = proj.shape
    e = n3 // 3
    hd = e // SGU_HEADS

    def body(proj_ref, g_ref, b_ref, ws_ref, bst_ref, yg_ref):
        xhat, _ = _ln_stats(_gelu(proj_ref[:, e:2 * e]))
        vnb = (xhat * g_ref[...] + b_ref[...]).astype(BF16)
        tri = _causal_mask()
        for h in range(SGU_HEADS):
            hs = slice(h * hd, (h + 1) * hd)
            ws = jnp.where(tri, ws_ref[h], 0.0).astype(BF16)
            mixed = jnp.dot(ws, vnb[:, hs], preferred_element_type=F32) + bst_ref[:, h:h + 1]
            u = proj_ref[:, hs]
            z = proj_ref[:, 2 * e + h * hd:2 * e + (h + 1) * hd]
            yg_ref[:, hs] = (_gelu(u) * mixed * (z * _sigmoid(z))).astype(BF16)

    vec = pl.BlockSpec((1, e), lambda i: (0, 0))
    return pl.pallas_call(
        body, name=name, grid=(s // CHUNK,),
        in_specs=[pl.BlockSpec((CHUNK, n3), lambda i: (i, 0)), vec, vec,
                  pl.BlockSpec((SGU_HEADS, CHUNK, CHUNK), lambda i: (0, 0, 0)),
                  pl.BlockSpec((CHUNK, SGU_HEADS), lambda i: (0, 0))],
        out_specs=pl.BlockSpec((CHUNK, e), lambda i: (i, 0)),
        out_shape=jax.ShapeDtypeStruct((s, e), BF16),
        compiler_params=_params(("parallel",)),
    )(proj, ln_g, ln_b, w_s, b_st)


def _sgu_bwd(name, dyg, proj, ln_g, ln_b, w_s, b_st):
    s, n3 = proj.shape
    e = n3 // 3
    hd = e // SGU_HEADS

    def body(dyg_ref, proj_ref, g_ref, b_ref, ws_ref, bst_ref,
             dproj_ref, dg_ref, db_ref, dws_ref, dbs_ref, dvn_ref):
        @pl.when(pl.program_id(0) == 0)
        def _():
            dg_ref[...] = jnp.zeros_like(dg_ref)
            db_ref[...] = jnp.zeros_like(db_ref)
            dws_ref[...] = jnp.zeros_like(dws_ref)
            dbs_ref[...] = jnp.zeros_like(dbs_ref)

        gv, gv_grad = _gelu_and_grad(proj_ref[:, e:2 * e])
        xhat, rstd = _ln_stats(gv)
        g = g_ref[...]
        vnb = (xhat * g + b_ref[...]).astype(BF16)
        tri = _causal_mask()
        for h in range(SGU_HEADS):
            hs = slice(h * hd, (h + 1) * hd)
            zs = slice(2 * e + h * hd, 2 * e + (h + 1) * hd)
            ws = jnp.where(tri, ws_ref[h], 0.0).astype(BF16)
            mixed = jnp.dot(ws, vnb[:, hs], preferred_element_type=F32) + bst_ref[:, h:h + 1]
            u = proj_ref[:, hs]
            z = proj_ref[:, zs]
            gu, gu_grad = _gelu_and_grad(u)
            sig = _sigmoid(z)
            dy = dyg_ref[:, hs]
            t = dy * (z * sig)
            dmixed = t * gu
            dproj_ref[:, hs] = (t * mixed * gu_grad).astype(BF16)
            dproj_ref[:, zs] = (dy * gu * mixed * (sig * (1.0 + z * (1.0 - sig)))).astype(BF16)
            dmb = dmixed.astype(BF16)
            dvn_ref[:, hs] = lax.dot_general(ws, dmb, TN, preferred_element_type=F32)
            dws = lax.dot_general(dmb, vnb[:, hs], NT, preferred_element_type=F32)
            dws_ref[h] += jnp.where(tri, dws, 0.0)
            dbs_ref[:, h:h + 1] += jnp.sum(dmixed, axis=-1, keepdims=True)
        dvn = dvn_ref[...]
        dg_ref[...] += _rowsum(dvn * xhat)
        db_ref[...] += _rowsum(dvn)
        dgv = _ln_bwd(dvn * g, xhat, rstd)
        dproj_ref[:, e:2 * e] = (dgv * gv_grad).astype(BF16)

    vec = pl.BlockSpec((1, e), lambda i: (0, 0))
    wsp = pl.BlockSpec((SGU_HEADS, CHUNK, CHUNK), lambda i: (0, 0, 0))
    bsp = pl.BlockSpec((CHUNK, SGU_HEADS), lambda i: (0, 0))
    return pl.pallas_call(
        body, name=name, grid=(s // CHUNK,),
        in_specs=[pl.BlockSpec((CHUNK, e), lambda i: (i, 0)), pl.BlockSpec((CHUNK, n3), lambda i: (i, 0)),
                  vec, vec, wsp, bsp],
        out_specs=[pl.BlockSpec((CHUNK, n3), lambda i: (i, 0)), vec, vec, wsp, bsp],
        out_shape=[jax.ShapeDtypeStruct((s, n3), BF16), jax.ShapeDtypeStruct((1, e), F32),
                   jax.ShapeDtypeStruct((1, e), F32),
                   jax.ShapeDtypeStruct((SGU_HEADS, CHUNK, CHUNK), F32),
                   jax.ShapeDtypeStruct((CHUNK, SGU_HEADS), F32)],
        scratch_shapes=[pltpu.VMEM((CHUNK, e), F32)],
        compiler_params=_params(("arbitrary",)),
    )(dyg, proj, ln_g, ln_b, w_s, b_st)


def _pool_counts(pos, w):
    return jnp.minimum(pos + 1, w).astype(F32)


def _pool_fwd(name, proj):
    s, n2 = proj.shape
    e = n2 // 2
    gd = e // len(POOL_WINDOWS)
    ts = _row_tile(s, e)
    hb = ts // POOL_HALO

    def body(v_ref, halo_ref, p_ref, ext):
        i = pl.program_id(0)
        ext[POOL_HALO:, :] = v_ref[...]
        ext[:POOL_HALO, :] = jnp.where(i > 0, halo_ref[...], 0.0)
        pos = i * ts + lax.broadcasted_iota(jnp.int32, (ts, 1), 0)
        for g, w in enumerate(POOL_WINDOWS):
            cs = slice(g * gd, (g + 1) * gd)
            acc = ext[POOL_HALO:, cs]
            for j in range(1, w):
                acc = acc + ext[POOL_HALO - j:POOL_HALO - j + ts, cs]
            p_ref[:, cs] = (acc / _pool_counts(pos, w) - v_ref[:, cs]).astype(BF16)

    return pl.pallas_call(
        body, name=name, grid=(s // ts,),
        in_specs=[pl.BlockSpec((ts, e), lambda i: (i, 0)),
                  pl.BlockSpec((POOL_HALO, e), lambda i: (jnp.maximum(i * hb - 1, 0), 0))],
        out_specs=pl.BlockSpec((ts, e), lambda i: (i, 0)),
        out_shape=jax.ShapeDtypeStruct((s, e), BF16),
        scratch_shapes=[pltpu.VMEM((ts + POOL_HALO, e), F32)],
        compiler_params=_params(("parallel",)),
    )(proj, proj)


def _pool_mm(name, p, wg, proj, scale):
    s, e = p.shape
    ng, rsh, gd = wg.shape[1], wg.shape[2], wg.shape[3]
    bm = _tile(s, 1024)

    def body(p_ref, w0, w1, w2, w3, z_ref, sc_ref, q_ref, yg_ref):
        q = None
        for j, w_ref in enumerate((w0, w1, w2, w3)):
            part = jnp.dot(p_ref[:, j * rsh:(j + 1) * rsh], w_ref[...], preferred_element_type=F32)
            q = part if q is None else q + part
        z = z_ref[...]
        q_ref[...] = q
        yg_ref[...] = (q * sc_ref[...] * (z * _sigmoid(z))).astype(BF16)

    blk = pl.BlockSpec((bm, gd), lambda m, g: (m, g))
    w_specs = [pl.BlockSpec((None, None, rsh, gd), functools.partial(lambda m, g, j: (j, g, 0, 0), j=j))
               for j in range(N_CHIPS)]
    return pl.pallas_call(
        body, name=name, grid=(s // bm, ng),
        in_specs=[blk] + w_specs + [pl.BlockSpec((bm, gd), lambda m, g: (m, ng + g)),
                                    pl.BlockSpec((1, gd), lambda m, g: (0, g))],
        out_specs=[blk, blk],
        out_shape=[jax.ShapeDtypeStruct((s, e), F32), jax.ShapeDtypeStruct((s, e), BF16)],
        compiler_params=_params(("parallel", "parallel")),
    )(p, wg, wg, wg, wg, proj, scale)


def _pool_bwd1(name, dyg, q, proj, scale):
    s, e = q.shape
    ts = _row_tile(s, e)

    def body(dyg_ref, q_ref, z_ref, sc_ref, dq_ref, dz_ref, dsc_ref):
        @pl.when(pl.program_id(0) == 0)
        def _():
            dsc_ref[...] = jnp.zeros_like(dsc_ref)

        z = z_ref[...]
        sig = _sigmoid(z)
        dy = dyg_ref[...]
        qv = q_ref[...]
        sc = sc_ref[...]
        dp2 = dy * (z * sig)
        dsc_ref[...] += _rowsum(dp2 * qv)
        dq_ref[...] = (dp2 * sc).astype(BF16)
        dz_ref[...] = (dy * (qv * sc) * (sig * (1.0 + z * (1.0 - sig)))).astype(BF16)

    blk = pl.BlockSpec((ts, e), lambda i: (i, 0))
    vec = pl.BlockSpec((1, e), lambda i: (0, 0))
    return pl.pallas_call(
        body, name=name, grid=(s // ts,),
        in_specs=[blk, blk, pl.BlockSpec((ts, e), lambda i: (i, 1)), vec],
        out_specs=[blk, pl.BlockSpec((ts, e), lambda i: (i, 1)), vec],
        out_shape=[jax.ShapeDtypeStruct((s, e), BF16), jax.ShapeDtypeStruct((s, 2 * e), BF16),
                   jax.ShapeDtypeStruct((1, e), F32)],
        compiler_params=_params(("arbitrary",)),
    )(dyg, q, proj, scale)


def _pool_mm_bwd(name, dq, wg):
    s, e = dq.shape
    ng, rsh, gd = wg.shape[1], wg.shape[2], wg.shape[3]
    bm = _tile(s, 1024)
    return _mm(name, dq, wg, dims=NT, grid=(s // bm, ng * N_CHIPS, 1),
               a_spec=pl.BlockSpec((bm, gd), lambda m, j, k: (m, j // N_CHIPS)),
               b_spec=pl.BlockSpec((None, None, rsh, gd), lambda m, j, k: (j % N_CHIPS, j // N_CHIPS, 0, 0)),
               out_shape=jax.ShapeDtypeStruct((s, e), F32),
               out_specs=pl.BlockSpec((bm, rsh), lambda m, j, k: (m, j)),
               acc_shape=(bm, rsh))


def _pool_bwd2(name, dp, dproj):
    s, e = dp.shape
    gd = e // len(POOL_WINDOWS)
    ts = _row_tile(s, e)
    n = s // ts
    hb = ts // POOL_HALO

    def body(dp_ref, nxt_ref, _, dv_ref, ext):
        i = pl.program_id(0)
        pos = i * ts + lax.broadcasted_iota(jnp.int32, (ts + POOL_HALO, 1), 0)
        for g, w in enumerate(POOL_WINDOWS):
            cs = slice(g * gd, (g + 1) * gd)
            cnt = _pool_counts(pos, w)
            ext[:ts, cs] = dp_ref[:, cs] / cnt[:ts]
            ext[ts:, cs] = jnp.where(i < n - 1, nxt_ref[:, cs] / cnt[ts:], 0.0)
        for g, w in enumerate(POOL_WINDOWS):
            cs = slice(g * gd, (g + 1) * gd)
            acc = ext[:ts, cs]
            for j in range(1, w):
                acc = acc + ext[j:j + ts, cs]
            dv_ref[:, cs] = (acc - dp_ref[:, cs]).astype(BF16)

    return pl.pallas_call(
        body, name=name, grid=(n,),
        in_specs=[pl.BlockSpec((ts, e), lambda i: (i, 0)),
                  pl.BlockSpec((POOL_HALO, e), lambda i: (jnp.minimum((i + 1) * hb, s // POOL_HALO - 1), 0)),
                  pl.BlockSpec(memory_space=pl.ANY)],
        out_specs=pl.BlockSpec((ts, e), lambda i: (i, 0)),
        out_shape=jax.ShapeDtypeStruct((s, 2 * e), BF16),
        scratch_shapes=[pltpu.VMEM((ts + POOL_HALO, e), F32)],
        input_output_aliases={2: 0},
        compiler_params=_params(("parallel",)),
    )(dp, dp, dproj)


def _mm_dw_pool(name, p, dq, ng):
    s, e = p.shape
    gd = e // ng
    rsh = gd // N_CHIPS
    bk = _tile(s, 4096)
    return _mm(name, p, dq, dims=TN, grid=(N_CHIPS, ng, s // bk),
               a_spec=pl.BlockSpec((bk, rsh), lambda m, g, k: (k, g * N_CHIPS + m)),
               b_spec=pl.BlockSpec((bk, gd), lambda m, g, k: (k, g)),
               out_shape=jax.ShapeDtypeStruct((N_CHIPS, ng, rsh, gd), BF16),
               out_specs=pl.BlockSpec((None, None, rsh, gd), lambda m, g, k: (m, g, 0, 0)),
               acc_shape=(rsh, gd))


def _conv_fwd(name, proj, conv_w, conv_b):
    s, n3 = proj.shape
    e = n3 // 3
    ts, tc = _tile(s, 512), LANES
    ne = e // tc
    hb = ts // CONV_HALO
    off = CONV_HALO - (CONV_WIDTH - 1)

    def body(a_ref, gl_ref, ha_ref, hgl_ref, w_ref, b_ref, c_ref, gext):
        i = pl.program_id(1)
        gext[CONV_HALO:, :] = a_ref[...] * _sigmoid(gl_ref[...])
        gext[:CONV_HALO, :] = jnp.where(i > 0, ha_ref[...] * _sigmoid(hgl_ref[...]), 0.0)
        taps = [jnp.broadcast_to(w_ref[k:k + 1, :], (SUBLANES, LANES)) for k in range(CONV_WIDTH)]
        bias = jnp.broadcast_to(b_ref[...], (SUBLANES, LANES))

        def sub_block(rb, carry):
            r0 = pl.multiple_of(rb * CONV_SUB, CONV_SUB)
            acc = [bias] * CONV_STRIDE
            for m in range(CONV_WIDTH + CONV_STRIDE - 1):
                g_m = gext[pl.ds(r0 + off + m, SUBLANES, stride=CONV_STRIDE), :]
                for q in range(CONV_STRIDE):
                    if 0 <= m - q < CONV_WIDTH:
                        acc[q] = acc[q] + taps[m - q] * g_m
            for q in range(CONV_STRIDE):
                c_ref[pl.ds(r0 + q, SUBLANES, stride=CONV_STRIDE), :] = acc[q]
            return carry

        lax.fori_loop(0, ts // CONV_SUB, sub_block, 0, unroll=4)

    halo = lambda i: jnp.maximum(i * hb - 1, 0)
    return pl.pallas_call(
        body, name=name, grid=(ne, s // ts),
        in_specs=[pl.BlockSpec((ts, tc), lambda j, i: (i, j)),
                  pl.BlockSpec((ts, tc), lambda j, i: (i, ne + j)),
                  pl.BlockSpec((CONV_HALO, tc), lambda j, i: (halo(i), j)),
                  pl.BlockSpec((CONV_HALO, tc), lambda j, i: (halo(i), ne + j)),
                  pl.BlockSpec((CONV_WIDTH, tc), lambda j, i: (0, j)),
                  pl.BlockSpec((1, tc), lambda j, i: (0, j))],
        out_specs=pl.BlockSpec((ts, tc), lambda j, i: (i, j)),
        out_shape=jax.ShapeDtypeStruct((s, e), F32),
        scratch_shapes=[pltpu.VMEM((ts + CONV_HALO, tc), F32)],
        compiler_params=_params(("parallel", "parallel")),
    )(proj, proj, proj, proj, conv_w, conv_b)


def _conv_gate_fwd(name, c, proj, ln_g, ln_b):
    s, e = c.shape
    ts = _row_tile(s, e)

    def body(c_ref, z_ref, g_ref, b_ref, yg_ref):
        xhat, _ = _ln_stats(c_ref[...])
        ln = xhat * g_ref[...] + b_ref[...]
        z = z_ref[...]
        yg_ref[...] = ((ln * _sigmoid(ln)) * (z * _sigmoid(z))).astype(BF16)

    blk = pl.BlockSpec((ts, e), lambda i: (i, 0))
    vec = pl.BlockSpec((1, e), lambda i: (0, 0))
    return pl.pallas_call(
        body, name=name, grid=(s // ts,),
        in_specs=[blk, pl.BlockSpec((ts, e), lambda i: (i, 2)), vec, vec],
        out_specs=blk, out_shape=jax.ShapeDtypeStruct((s, e), BF16),
        compiler_params=_params(("parallel",)),
    )(c, proj, ln_g, ln_b)


def _conv_bwd1(name, dyg, c, proj, ln_g, ln_b):
    s, e = c.shape
    ts = _row_tile(s, e)

    def body(dyg_ref, c_ref, z_ref, g_ref, b_ref, dc_ref, dz_ref, dg_ref, db_ref, dcb_ref):
        @pl.when(pl.program_id(0) == 0)
        def _():
            dg_ref[...] = jnp.zeros_like(dg_ref)
            db_ref[...] = jnp.zeros_like(db_ref)
            dcb_ref[...] = jnp.zeros_like(dcb_ref)

        xhat, rstd = _ln_stats(c_ref[...])
        g = g_ref[...]
        ln = xhat * g + b_ref[...]
        sl = _sigmoid(ln)
        z = z_ref[...]
        sz = _sigmoid(z)
        dy = dyg_ref[...]
        dz_ref[...] = (dy * (ln * sl) * (sz * (1.0 + z * (1.0 - sz)))).astype(BF16)
        dln = dy * (z * sz) * (sl * (1.0 + ln * (1.0 - sl)))
        dg_ref[...] += _rowsum(dln * xhat)
        db_ref[...] += _rowsum(dln)
        dc = _ln_bwd(dln * g, xhat, rstd)
        dc_ref[...] = dc
        dcb_ref[...] += _rowsum(dc)

    blk = pl.BlockSpec((ts, e), lambda i: (i, 0))
    zblk = pl.BlockSpec((ts, e), lambda i: (i, 2))
    vec = pl.BlockSpec((1, e), lambda i: (0, 0))
    vshape = jax.ShapeDtypeStruct((1, e), F32)
    return pl.pallas_call(
        body, name=name, grid=(s // ts,),
        in_specs=[blk, blk, zblk, vec, vec],
        out_specs=[blk, zblk, vec, vec, vec],
        out_shape=[jax.ShapeDtypeStruct((s, e), F32), jax.ShapeDtypeStruct((s, 3 * e), BF16),
                   vshape, vshape, vshape],
        compiler_params=_params(("arbitrary",)),
    )(dyg, c, proj, ln_g, ln_b)


def _conv_bwd2(name, dc, proj, conv_w):
    s, e = dc.shape
    ts, tc = _tile(s, 512), LANES
    ne = e // tc
    n = s // ts
    hb = ts // CONV_HALO
    off = CONV_HALO - (CONV_WIDTH - 1)
    span = CONV_WIDTH + CONV_STRIDE - 1

    def body(dc_ref, nxt_ref, a_ref, gl_ref, ha_ref, hgl_ref, w_ref, dg_ref, dw_ref, gext, dcext, dwacc, taps):
        i = pl.program_id(1)

        @pl.when(i == 0)
        def _():
            dwacc[...] = jnp.zeros_like(dwacc)

        gext[CONV_HALO:, :] = a_ref[...] * _sigmoid(gl_ref[...])
        gext[:CONV_HALO, :] = jnp.where(i > 0, ha_ref[...] * _sigmoid(hgl_ref[...]), 0.0)
        dcext[:ts, :] = dc_ref[...]
        dcext[ts:, :] = jnp.where(i < n - 1, nxt_ref[...], 0.0)
        for k in range(CONV_WIDTH):
            taps[k] = jnp.broadcast_to(w_ref[k:k + 1, :], (SUBLANES, LANES))

        def sub_block(rb, carry):
            r0 = pl.multiple_of(rb * CONV_SUB, CONV_SUB)
            dg = [jnp.zeros((SUBLANES, LANES), F32)] * CONV_STRIDE
            for m in range(span):
                dc_m = dcext[pl.ds(r0 + m, SUBLANES, stride=CONV_STRIDE), :]
                for q in range(CONV_STRIDE):
                    k = CONV_WIDTH - 1 - m + q
                    if 0 <= k < CONV_WIDTH:
                        dg[q] = dg[q] + taps[k] * dc_m
            for q in range(CONV_STRIDE):
                dg_ref[pl.ds(r0 + q, SUBLANES, stride=CONV_STRIDE), :] = dg[q]
            dc_q = [dcext[pl.ds(r0 + q, SUBLANES, stride=CONV_STRIDE), :] for q in range(CONV_STRIDE)]
            win = [gext[pl.ds(r0 + off + m, SUBLANES, stride=CONV_STRIDE), :] for m in range(CONV_STRIDE - 1)]
            for k in range(CONV_WIDTH):
                win.append(gext[pl.ds(r0 + off + k + CONV_STRIDE - 1, SUBLANES, stride=CONV_STRIDE), :])
                prods = [dc_q[q] * win[q] for q in range(CONV_STRIDE)]
                while len(prods) > 1:
                    prods = [a + b for a, b in zip(prods[::2], prods[1::2])]
                dwacc[k] += prods[0]
                win.pop(0)
            return carry

        lax.fori_loop(0, ts // CONV_SUB, sub_block, 0, unroll=2)

        @pl.when(i == n - 1)
        def _():
            dw_ref[...] = jnp.sum(dwacc[...], axis=1)

    halo = lambda i: jnp.maximum(i * hb - 1, 0)
    nxt = lambda i: jnp.minimum((i + 1) * hb, s // CONV_HALO - 1)
    return pl.pallas_call(
        body, name=name, grid=(ne, n),
        in_specs=[pl.BlockSpec((ts, tc), lambda j, i: (i, j)),
                  pl.BlockSpec((CONV_HALO, tc), lambda j, i: (nxt(i), j)),
                  pl.BlockSpec((ts, tc), lambda j, i: (i, j)),
                  pl.BlockSpec((ts, tc), lambda j, i: (i, ne + j)),
                  pl.BlockSpec((CONV_HALO, tc), lambda j, i: (halo(i), j)),
                  pl.BlockSpec((CONV_HALO, tc), lambda j, i: (halo(i), ne + j)),
                  pl.BlockSpec((CONV_WIDTH, tc), lambda j, i: (0, j))],
        out_specs=[pl.BlockSpec((ts, tc), lambda j, i: (i, j)),
                   pl.BlockSpec((CONV_HALO, tc), lambda j, i: (0, j))],
        out_shape=[jax.ShapeDtypeStruct((s, e), F32), jax.ShapeDtypeStruct((CONV_HALO, e), F32)],
        scratch_shapes=[pltpu.VMEM((ts + CONV_HALO, tc), F32), pltpu.VMEM((ts + CONV_HALO, tc), F32),
                        pltpu.VMEM((CONV_HALO, SUBLANES, LANES), F32), pltpu.VMEM((CONV_HALO, SUBLANES, LANES), F32)],
        compiler_params=_params(("parallel", "arbitrary")),
    )(dc, dc, proj, proj, proj, proj, conv_w)


def _conv_bwd3(name, dg, proj, dproj):
    s, e = dg.shape
    ts = _row_tile(s, 2 * e)

    def body(dg_ref, agl_ref, _, out_ref):
        a = agl_ref[:, :e]
        sig = _sigmoid(agl_ref[:, e:])
        dgv = dg_ref[...]
        out_ref[:, :e] = (dgv * sig).astype(BF16)
        out_ref[:, e:] = (dgv * a * sig * (1.0 - sig)).astype(BF16)

    wide = pl.BlockSpec((ts, 2 * e), lambda i: (i, 0))
    return pl.pallas_call(
        body, name=name, grid=(s // ts,),
        in_specs=[pl.BlockSpec((ts, e), lambda i: (i, 0)), wide, pl.BlockSpec(memory_space=pl.ANY)],
        out_specs=wide, out_shape=jax.ShapeDtypeStruct((s, 3 * e), BF16),
        input_output_aliases={2: 0},
        compiler_params=_params(("parallel",)),
    )(dg, proj, dproj)


def _post_ln_bwd(name, dy_or_target, xhat, rstd, g, b, loss_head, dep=None):
    s, d = xhat.shape
    ts = _row_tile(s, d)

    def body(dy_ref, xhat_ref, rstd_ref, g_ref, b_ref, *rest):
        dr_ref, drb_ref, dg_ref, db_ref, loss_ref = rest[-5:]

        @pl.when(pl.program_id(0) == 0)
        def _():
            dg_ref[...] = jnp.zeros_like(dg_ref)
            db_ref[...] = jnp.zeros_like(db_ref)
            loss_ref[...] = jnp.zeros_like(loss_ref)

        xh = xhat_ref[...]
        gv = g_ref[...]
        if loss_head:
            err = (xh * gv + b_ref[...]) - dy_ref[...]
            per_row = jnp.mean(err * err, axis=-1, keepdims=True)
            loss_ref[...] += 0.5 * jnp.broadcast_to(_rowsum(per_row), loss_ref.shape)
            dy = err / d
        else:
            dy = dy_ref[...]
        dg_ref[...] += _rowsum(dy * xh)
        db_ref[...] += _rowsum(dy)
        dr = _ln_bwd(dy * gv, xh, rstd_ref[:, 0:1])
        dr_ref[...] = dr
        drb_ref[...] = dr.astype(BF16)

    blk = pl.BlockSpec((ts, d), lambda i: (i, 0))
    vec = pl.BlockSpec((1, d), lambda i: (0, 0))
    deps = [] if dep is None else [dep]
    return pl.pallas_call(
        body, name=name, grid=(s // ts,),
        in_specs=[blk, blk, pl.BlockSpec((ts, LANES), lambda i: (i, 0)), vec, vec] + [ANY] * len(deps),
        out_specs=[blk, blk, vec, vec, pl.BlockSpec((1, LANES), lambda i: (0, 0))],
        out_shape=[jax.ShapeDtypeStruct((s, d), F32), jax.ShapeDtypeStruct((s, d), BF16),
                   jax.ShapeDtypeStruct((1, d), F32), jax.ShapeDtypeStruct((1, d), F32),
                   jax.ShapeDtypeStruct((1, LANES), F32)],
        compiler_params=_params(("arbitrary",)),
    )(dy_or_target, xhat, rstd, g, b, *deps)


def _as2d(a):
    return a.reshape(-1, a.shape[-1])


def _cast_layer(name, a, layer, slabs=0):
    _, r, c = a.shape
    tr = _row_tile(r, c)

    def body(a_ref, o_ref):
        o_ref[...] = a_ref[...].astype(BF16)

    if slabs:
        out_spec = pl.BlockSpec((None, tr, c), lambda i: (_my_shard(), i, 0))
        out_shape = jax.ShapeDtypeStruct((slabs, r, c), BF16)
    else:
        out_spec = pl.BlockSpec((tr, c), lambda i: (i, 0))
        out_shape = jax.ShapeDtypeStruct((r, c), BF16)
    return pl.pallas_call(body, name=name, grid=(r // tr,),
                          in_specs=[pl.BlockSpec((None, tr, c), lambda i: (layer, i, 0))],
                          out_specs=out_spec, out_shape=out_shape,
                          compiler_params=_params(("parallel",)))(a)


def _my_shard():
    return 2 * lax.axis_index("x") + lax.axis_index("y")


def _peer_shard(k):
    x, y = lax.axis_index("x"), lax.axis_index("y")
    px = jnp.where(k == 1, x, 1 - x)
    py = jnp.where(k == 0, y, 1 - y)
    return 2 * px + py


def _pair_sum(name, own, recv):
    _, r, c = own.shape
    h = r // 2
    tr = _row_tile(h, c)
    nb = h // tr

    def body(own_ref, recv_ref, o_ref):
        o_ref[...] = (own_ref[...].astype(F32) + recv_ref[...].astype(F32)).astype(BF16)

    mine = pl.BlockSpec((None, tr, c), lambda k, i: (_peer_shard(k), lax.axis_index("c") * nb + i, 0))
    theirs = pl.BlockSpec((None, tr, c), lambda k, i: (_peer_shard(k), i, 0))
    return pl.pallas_call(body, name=name, grid=(N_CHIPS - 1, nb), in_specs=[mine, theirs],
                          out_specs=pl.BlockSpec((None, tr, c), lambda k, i: (k, i, 0)),
                          out_shape=jax.ShapeDtypeStruct((N_CHIPS - 1, h, c), BF16),
                          compiler_params=_params(("parallel", "parallel")))(own, recv)


def _final_sum(name, own, recv, got, layer, n_layers, prev):
    _, r, c = own.shape
    h = r // 2
    tr = _row_tile(h, c)
    nb = h // tr

    def body(own_ref, recv_ref, q0_ref, q1_ref, q2_ref, *rest):
        o_ref = rest[-1]
        acc = own_ref[...].astype(F32) + recv_ref[...].astype(F32)
        acc = acc + q0_ref[...].astype(F32)
        acc = acc + q1_ref[...].astype(F32)
        o_ref[...] = acc + q2_ref[...].astype(F32)

    in_specs = [pl.BlockSpec((None, tr, c), lambda i: (_my_shard(), lax.axis_index("c") * nb + i, 0)),
                pl.BlockSpec((None, tr, c), lambda i: (_my_shard(), i, 0))]
    in_specs += [pl.BlockSpec((None, tr, c), functools.partial(lambda i, k: (k, i, 0), k=k)) for k in range(3)]
    operands = [own, recv, got, got, got]
    aliases = {}
    if prev is not None:
        in_specs.append(pl.BlockSpec(memory_space=pl.ANY))
        operands.append(prev)
        aliases = {5: 0}
    return pl.pallas_call(body, name=name, grid=(nb,), in_specs=in_specs,
                          out_specs=pl.BlockSpec((None, tr, c), lambda i: (layer, lax.axis_index("c") * nb + i, 0)),
                          out_shape=jax.ShapeDtypeStruct((n_layers, r, c), F32),
                          input_output_aliases=aliases,
                          compiler_params=_params(("parallel",)))(*operands)


def _adam(name, w, g, m, v, emit_grad=False):
    r, c = w.shape
    tr = _row_tile(r, c)
    c1 = 1.0 / (1.0 - ADAM_B1 ** ADAM_STEP)
    c2 = 1.0 / (1.0 - ADAM_B2 ** ADAM_STEP)
    n_out = 4 if emit_grad else 3

    def body(w_ref, g_ref, m_ref, v_ref, d_ref, nm_ref, nv_ref, *g_out):
        gv = g_ref[...]
        if emit_grad:
            g_out[0][...] = gv
        nm = ADAM_B1 * m_ref[...] + (1.0 - ADAM_B1) * gv
        nv = ADAM_B2 * v_ref[...] + (1.0 - ADAM_B2) * (gv * gv)
        d_ref[...] = -ADAM_LR * ((nm * c1) / (jnp.sqrt(nv * c2) + ADAM_EPS) + ADAM_WD * w_ref[...])
        nm_ref[...] = nm
        nv_ref[...] = nv

    blk = pl.BlockSpec((tr, c), lambda i: (i, 0))
    shp = jax.ShapeDtypeStruct((r, c), F32)
    return pl.pallas_call(body, name=name, grid=(r // tr,), in_specs=[blk] * 4, out_specs=[blk] * n_out,
                          out_shape=[shp] * n_out, compiler_params=_params(("parallel",)))(w, g, m, v)


def _sum_devices(name, parts):
    _, r, c = parts.shape
    tr = _row_tile(r, N_DEV * c)

    def body(p_ref, o_ref):
        acc = p_ref[0]
        for k in range(1, N_DEV):
            acc = acc + p_ref[k]
        o_ref[...] = acc

    return pl.pallas_call(body, name=name, grid=(r // tr,),
                          in_specs=[pl.BlockSpec((N_DEV, tr, c), lambda i: (0, i, 0))],
                          out_specs=pl.BlockSpec((tr, c), lambda i: (i, 0)),
                          out_shape=jax.ShapeDtypeStruct((r, c), F32),
                          compiler_params=_params(("parallel",)))(parts)


def _coords():
    return lax.axis_index("x"), lax.axis_index("y"), lax.axis_index("c")


def _chip_peers(x, y):
    return [(1 - x, y, 2 * (1 - x) + y), (x, 1 - y, 2 * x + 1 - y), (1 - x, 1 - y, 2 * (1 - x) + 1 - y)]


def _remote(src, dst, ssem, rsem, dev):
    return pltpu.make_async_remote_copy(src_ref=src, dst_ref=dst, send_sem=ssem, recv_sem=rsem,
                                        device_id=dev, device_id_type=MESH)


ANY = pl.BlockSpec(memory_space=pl.ANY)


SEM = pl.BlockSpec(memory_space=pltpu.SEMAPHORE)
HBM = pl.BlockSpec(memory_space=pltpu.HBM)
DATAFLOW = pltpu.SideEffectType.DATAFLOW_SIDE_EFFECTING


def _hbm(a):
    return pltpu.with_memory_space_constraint(a, pltpu.HBM)


def _split_start(name, make_copies, n_copies, srcs, land_shapes, dep):
    n, nl = len(srcs), len(land_shapes)
    nb = n + nl

    def body(*refs):
        src_refs, land_refs = refs[:n], refs[n:nb]
        ssem, rsem = refs[nb + 1], refs[nb + 2]
        token = refs[-1]
        for send, _ in make_copies(src_refs, land_refs, ssem, rsem):
            send.start()
        token[...] = jnp.zeros_like(token)

    lands = [_hbm(lax.empty(shp, dt)) for shp, dt in land_shapes]
    out_shape = [pltpu.SemaphoreType.DMA((n_copies,)), pltpu.SemaphoreType.DMA((n_copies,))]
    out_shape += [pltpu.HBM(a.shape, a.dtype) for a in list(srcs) + lands]
    out_shape.append(jax.ShapeDtypeStruct((ROW_ALIGN, LANES), F32))
    res = pl.pallas_call(
        body, name=name, out_shape=out_shape,
        in_specs=[HBM] * nb + [ANY],
        out_specs=[SEM, SEM] + [HBM] * nb + [pl.BlockSpec(memory_space=pltpu.VMEM)],
        input_output_aliases={i: 2 + i for i in range(nb)},
        compiler_params=pltpu.CompilerParams(has_side_effects=DATAFLOW),
    )(*[_hbm(a) for a in srcs], *lands, dep)
    return res[0], res[1], res[2:2 + n], res[2 + n:2 + nb], res[-1]


def _split_wait(name, make_copies, ssem, rsem, srcs, lands, after):
    n, nb = len(srcs), len(srcs) + len(lands)

    def body(*refs):
        src_refs, land_refs = refs[:n], refs[n:nb]
        for send, recv in make_copies(src_refs, land_refs, refs[nb], refs[nb + 1]):
            send.wait_send()
            recv.wait_recv()

    res = pl.pallas_call(
        body, name=name, out_shape=[pltpu.HBM(a.shape, a.dtype) for a in list(srcs) + list(lands)],
        in_specs=[HBM] * nb + [SEM, SEM, ANY], out_specs=[HBM] * nb,
        input_output_aliases={i: i for i in range(nb)},
        compiler_params=pltpu.CompilerParams(has_side_effects=DATAFLOW),
    )(*srcs, *lands, ssem, rsem, after)
    return res[:n], res[n:]


def _gather_copies(buf_refs, _, ssem, rsem):
    x, y, c = _coords()
    j = 2 * x + y
    out = []
    for i, buf in enumerate(buf_refs):
        h = buf.shape[1] // 2
        rows = pl.ds(c * h, h)
        for k, (px, py, pj) in enumerate(_chip_peers(x, y)):
            sems = (ssem.at[3 * i + k], rsem.at[3 * i + k], (px, py, c))
            out.append((_remote(buf.at[j, rows], buf.at[j, rows], *sems),
                        _remote(buf.at[pj, rows], buf.at[pj, rows], *sems)))
    return out


def _scatter_copies(src_refs, land_refs, ssem, rsem):
    x, y, c = _coords()
    out = []
    for i, (src, land) in enumerate(zip(src_refs, land_refs)):
        for k, (px, py, _) in enumerate(_chip_peers(x, y)):
            cp = _remote(src.at[k], land.at[k], ssem.at[3 * i + k], rsem.at[3 * i + k], (px, py, c))
            out.append((cp, cp))
    return out


def _sibling_pass(name, lands, dep):
    n = len(lands)

    def body(*refs):
        bufs = refs[n + 1:2 * n + 1]
        ssem, rsem = refs[2 * n + 1:]
        x, y, c = _coords()
        sends = []
        for i in range(n):
            h = bufs[i].shape[1] // 2
            for k, (px, py, pj) in enumerate(_chip_peers(x, y)):
                mine = bufs[i].at[pj, pl.ds(c * h, h)]
                sends.append(_remote(mine, mine, ssem.at[3 * i + k], rsem.at[3 * i + k], (x, y, 1 - c)))
        for cp in sends:
            cp.start()
        for i in range(n):
            h = bufs[i].shape[1] // 2
            for k, (px, py, pj) in enumerate(_chip_peers(x, y)):
                theirs = bufs[i].at[pj, pl.ds((1 - c) * h, h)]
                _remote(theirs, theirs, ssem.at[3 * i + k], rsem.at[3 * i + k], (x, y, 1 - c)).wait_recv()
        for cp in sends:
            cp.wait_send()

    return pl.pallas_call(
        body, name=name, in_specs=[ANY] * (n + 1), out_specs=[ANY] * n,
        out_shape=[jax.ShapeDtypeStruct(a.shape, a.dtype) for a in lands],
        input_output_aliases={i: i for i in range(n)},
        scratch_shapes=[pltpu.SemaphoreType.DMA((3 * n,)), pltpu.SemaphoreType.DMA((3 * n,))],
    )(*lands, dep)


def _gather_small(small):
    def body(in_ref, out_ref, ssem, rsem, lsem):
        x, y, c = _coords()
        j = 2 * x + y
        local = pltpu.make_async_copy(in_ref, out_ref.at[j], lsem)
        local.start()
        peers = _chip_peers(x, y)
        sends = [_remote(in_ref, out_ref.at[j], ssem.at[k], rsem.at[k], (px, py, c))
                 for k, (px, py, pj) in enumerate(peers)]
        for cp in sends:
            cp.start()
        for k, (px, py, pj) in enumerate(peers):
            sends[k].wait_send()
            _remote(in_ref, out_ref.at[pj], ssem.at[k], rsem.at[k], (px, py, c)).wait_recv()
        local.wait()

    return pl.pallas_call(body, name="gather_small", in_specs=[ANY], out_specs=ANY,
                          out_shape=jax.ShapeDtypeStruct((N_CHIPS,) + small.shape, small.dtype),
                          scratch_shapes=[pltpu.SemaphoreType.DMA((3,)), pltpu.SemaphoreType.DMA((3,)),
                                          pltpu.SemaphoreType.DMA])(small)


def _pair_copies(src_refs, land_refs, ssem, rsem):
    x, y, c = _coords()
    out = []
    for i, (src, land) in enumerate(zip(src_refs, land_refs)):
        h = src.shape[1] // 2
        cp = _remote(src.at[:, pl.ds((1 - c) * h, h)], land, ssem.at[i], rsem.at[i], (x, y, 1 - c))
        out.append((cp, cp))
    return out


def _swap_copies(layers):
    def make(buf_refs, _, ssem, rsem):
        x, y, c = _coords()
        out = []
        for i, buf in enumerate(buf_refs):
            h = buf.shape[1] // 2
            mine = buf.at[layers[i], pl.ds(c * h, h)]
            theirs = buf.at[layers[i], pl.ds((1 - c) * h, h)]
            sems = (ssem.at[i], rsem.at[i], (x, y, 1 - c))
            out.append((_remote(mine, mine, *sems), _remote(theirs, theirs, *sems)))
        return out

    return make


def _small_copies(buf_refs, _, ssem, rsem):
    x, y, c = _coords()
    buf = buf_refs[0]
    me = 4 * x + 2 * y + c
    out = []
    for r in range(1, N_DEV):
        px = 1 - x if r & 4 else x
        py = 1 - y if r & 2 else y
        pc = 1 - c if r & 1 else c
        peer = 4 * px + 2 * py + pc
        sems = (ssem.at[r - 1], rsem.at[r - 1], (px, py, pc))
        out.append((_remote(buf.at[me], buf.at[me], *sems), _remote(buf.at[peer], buf.at[peer], *sems)))
    return out


ROW_ALIGN = SUBLANES
SMALL_WIDTH = 8 * LANES
LAYER_WEIGHTS = (("a_w_in", "a_w_out"), ("b_w_in", "b_w_pool", "b_w_out"), ("c_w_in", "c_w_out"))
SMALL_SHARDED = ("a_ln_g", "a_ln_b", "c_conv_w", "c_conv_b", "c_ln_g", "c_ln_b")
SMALL_ORDER = ("a_ln_g", "a_ln_b", "a_w_s", "a_b_s", "b_scale", "c_conv_w", "c_conv_b", "c_ln_g", "c_ln_b",
               "post_ln_g", "post_ln_b")
WEIGHTS = ("a_w_in", "a_ln_g", "a_ln_b", "a_w_s", "a_b_s", "a_w_out", "b_w_in", "b_w_pool", "b_scale", "b_w_out",
           "c_w_in", "c_conv_w", "c_conv_b", "c_ln_g", "c_ln_b", "c_w_out", "post_ln_g", "post_ln_b")


def _pad_rows(a):
    pad = -a.shape[0] % ROW_ALIGN
    return jnp.pad(a, ((0, pad), (0, 0))) if pad else a


def _pack(parts, width):
    return jnp.concatenate([_pad_rows(p.reshape(-1, width)) for p in parts], axis=0)


def _unpack(packed, shapes, width):
    out, row = [], 0
    for shp in shapes:
        size = 1
        for n in shp:
            size *= n
        rows = size // width
        out.append(packed[row:row + rows].reshape(shp))
        row += rows + (-rows % ROW_ALIGN)
    return out


def _step(x, tgt, w, m, v):
    s, d = x.shape
    cx, cy, cc = _coords()
    j_me = 2 * cx + cy
    es = w["a_ln_g"].shape[1]
    e = N_CHIPS * es

    n_a, n_c = w["a_w_in"].shape[0], w["c_w_in"].shape[0]
    ng = w["b_w_pool"].shape[1]
    as3d = lambda a: a.reshape(a.shape[0], -1, a.shape[-1])

    zones = [[_cast_layer("cast_%s%d" % (k, i // 3), as3d(w[k]), i // 3, N_CHIPS) for k in LAYER_WEIGHTS[i % 3]]
             for i in range(DEPTH)]

    def gather_start(i, dep):
        return _split_start("gather_start%d" % i, _gather_copies, 3 * len(zones[i]), zones[i], [], dep)

    small_all = _gather_small(_pack([w[k] for k in SMALL_SHARDED], es))
    ssem, rsem, lands, _, tok = gather_start(0, small_all)
    xb = _cast_layer("cast_x", x[None], 0)
    small_full = jnp.swapaxes(small_all, 0, 1).reshape(small_all.shape[1], e)
    full_shapes = [(n_a, e), (n_a, e), (CONV_WIDTH, e), (1, e), (1, e), (1, e)]
    a_ln_g, a_ln_b, conv_w, conv_b, c_ln_g, c_ln_b = _unpack(small_full, full_shapes, e)

    post_g, post_b = w["post_ln_g"], w["post_ln_b"]
    b_scale = w["b_scale"]

    saved = []
    res, gp, bp, hb = x, jnp.ones((1, d), F32), jnp.zeros((1, d), F32), xb
    for i in range(DEPTH):
        kind, l = i % 3, i // 3
        tag = "L%d" % i
        after = zones[DEPTH - 1][-1] if i == 0 else res
        lands, _ = _split_wait("gather_wait%d" % i, _gather_copies, ssem, rsem, lands, [], after)
        full = _sibling_pass("sibling_pass%d" % i, lands, tok)
        if i + 1 < DEPTH:
            ssem, rsem, lands, _, tok = gather_start(i + 1, full[0])
        wl = dict(zip(LAYER_WEIGHTS[kind], full))
        w_in = wl[LAYER_WEIGHTS[kind][0]].reshape(N_CHIPS, 1, d, -1)
        w_out = wl[LAYER_WEIGHTS[kind][-1]].reshape(N_CHIPS, 1, es, d)
        proj = _mm_in(tag + "_in", hb, w_in, 0, dep=tok)
        if kind == 0:
            b_st = w["a_b_s"][l].T
            yg = _sgu_fwd(tag + "_sgu", proj, a_ln_g[l:l + 1], a_ln_b[l:l + 1], w["a_w_s"][l], b_st)
            extra = (b_st,)
        elif kind == 1:
            w_pool = wl["b_w_pool"].reshape(N_CHIPS, ng, -1, w["b_w_pool"].shape[3])
            p = _pool_fwd(tag + "_pool", proj)
            q, yg = _pool_mm(tag + "_poolmm", p, w_pool, proj, b_scale[l:l + 1])
            extra = (p, q, w_pool)
        else:
            cv = _conv_fwd(tag + "_conv", proj, conv_w, conv_b)
            yg = _conv_gate_fwd(tag + "_gate", cv, proj, c_ln_g, c_ln_b)
            extra = (cv,)
        xhat, hb_next, rstd = _mm_out_ln(tag + "_out", yg, w_out, 0, res, gp, bp, post_g[i:i + 1], post_b[i:i + 1])
        saved.append((hb, proj, yg, xhat, rstd, extra, w_in, w_out))
        res, gp, bp, hb = xhat, post_g[i:i + 1], post_b[i:i + 1], hb_next

    sg = {k: [None] * w[k].shape[0] for k in ("a_ln_g", "a_ln_b", "a_w_s", "a_b_s", "post_ln_g", "post_ln_b")}
    scat = {}

    def finish_pair(p, after):
        i, pssem, prsem, own, recv, dep = p
        own, recv = _split_wait("pair_wait%d" % i, _pair_copies, pssem, prsem, own, recv, after)
        pair = [_pair_sum("pairsum%d_%d" % (i, n), a, r) for n, (a, r) in enumerate(zip(own, recv))]
        ssem, rsem, srcs, lands, token = _split_start("scatter_start%d" % i, _scatter_copies, 3 * len(pair), pair,
                                                      [((3,) + a.shape[1:], a.dtype) for a in pair], dep)
        scat[i] = (own, recv, ssem, rsem, srcs, lands)
        return token

    grads, delta, new_m, new_v = {}, {}, {}, {}

    def adam(k, emit_grad=False):
        shp = w[k].shape
        two_d = (lambda a: a.reshape(-1, shp[-1])) if shp[-1] % LANES == 0 else (lambda a: a.reshape(1, -1))
        res = _adam("adam_" + k, two_d(w[k]), two_d(grads[k]), two_d(m[k]), two_d(v[k]), emit_grad)
        delta[k], new_m[k], new_v[k] = (a.reshape(shp) for a in res[:3])
        if emit_grad:
            grads[k] = res[3].reshape(shp)
        return res[0]

    reduced, swaps = {}, {}

    def reduce_layer(i, after):
        kind, l = i % 3, i // 3
        names = LAYER_WEIGHTS[kind]
        own, recv, ssem, rsem, srcs, lands = scat[i]
        _, got = _split_wait("scatter_wait%d" % i, _scatter_copies, ssem, rsem, srcs, lands, after)
        for n, k in enumerate(names):
            reduced[k] = _final_sum("finalsum%d_%d" % (i, n), own[n], recv[n], got[n], l, w[k].shape[0],
                                    reduced.get(k))
        make = _swap_copies([l] * len(names))
        ssem, rsem, bufs, _, token = _split_start("swap_start%d" % i, make, len(names), [reduced[k] for k in names],
                                                  [], got[0])
        swaps[i] = (make, ssem, rsem, bufs)
        return token

    def finish_swap(i, after, with_adam):
        names = LAYER_WEIGHTS[i % 3]
        make, ssem, rsem, bufs = swaps[i]
        bufs, _ = _split_wait("swap_wait%d" % i, make, ssem, rsem, bufs, [], after)
        out = after
        for k, g in zip(names, bufs):
            reduced[k] = g
            if with_adam:
                grads[k] = g.reshape(w[k].shape)
                out = adam(k, True)
        return out

    pending = None
    dr, drb, dg, db, loss_vec = _post_ln_bwd("L3_lnbwd", tgt, saved[-1][3], saved[-1][4],
                                             post_g[DEPTH - 1:DEPTH], post_b[DEPTH - 1:DEPTH], True)
    grad_x = None
    for i in reversed(range(DEPTH)):
        kind, l = i % 3, i // 3
        tag = "L%d" % i
        hb_in, proj, yg, _, _, extra, w_in, w_out = saved[i]
        sg["post_ln_g"][i], sg["post_ln_b"][i] = dg, db
        started = None if pending is None else finish_pair(pending, drb)
        dyg = _mm_dyg(tag + "_dyg", drb, w_out, 0, dep=started)
        dw_mid = []
        if kind == 0:
            dproj, dlg, dlb, dws_s, dbs_t = _sgu_bwd(tag + "_sgubwd", dyg, proj, a_ln_g[l:l + 1], a_ln_b[l:l + 1],
                                                     w["a_w_s"][l], extra[0])
            sg["a_ln_g"][l], sg["a_ln_b"][l], sg["a_w_s"][l], sg["a_b_s"][l] = dlg, dlb, dws_s, dbs_t.T
        elif kind == 1:
            p, q, w_pool = extra
            dq, dproj, dscale = _pool_bwd1(tag + "_poolbwd1", dyg, q, proj, b_scale[l:l + 1])
            dp = _pool_mm_bwd(tag + "_poolmmbwd", dq, w_pool)
            dproj = _pool_bwd2(tag + "_poolbwd2", dp, dproj)
            dw_mid = [_mm_dw_pool(tag + "_dwpool", p, dq, ng).reshape(N_CHIPS, -1, dq.shape[1] // ng)]
            sg["b_scale"] = [dscale]
        else:
            cv = extra[0]
            dc, dproj, dlg, dlb, dcb = _conv_bwd1(tag + "_convbwd1", dyg, cv, proj, c_ln_g, c_ln_b)
            dgc, dcw = _conv_bwd2(tag + "_convbwd2", dc, proj, conv_w)
            dproj = _conv_bwd3(tag + "_convbwd3", dgc, proj, dproj)
            sg["c_ln_g"], sg["c_ln_b"], sg["c_conv_b"], sg["c_conv_w"] = [dlg], [dlb], [dcb], [dcw[:CONV_WIDTH]]
        if i == 0:
            packed = _pack([jnp.concatenate([a.reshape(1, -1) for a in sg[k]], axis=0) for k in SMALL_ORDER], SMALL_WIDTH)
            slabs = lax.dynamic_update_slice(jnp.zeros((N_DEV,) + packed.shape, F32), packed[None],
                                             (4 * cx + 2 * cy + cc, 0, 0))
            small = _split_start("small_start", _small_copies, N_DEV - 1, [slabs], [], dproj)
        dw_out = _mm_dw_out(tag + "_dwout", yg, drb, dep=small[-1] if i == 0 else None).reshape(N_CHIPS, es, d)
        if i > 0:
            dh = _mm_dh(tag + "_dh", dproj, w_in, 0, dr)
        dw_in = _mm_dw_in(tag + "_dwin", hb_in, dproj)
        own = [dw_in] + dw_mid + [dw_out]
        pending = (i,) + _split_start("pair_start%d" % i, _pair_copies, len(own), own,
                                      [((N_CHIPS, a.shape[1] // 2, a.shape[2]), a.dtype) for a in own], dproj)
        if i > 0:
            prev = saved[i - 1]
            dr, drb, dg, db, _ = _post_ln_bwd("L%d_lnbwd" % (i - 1), dh, prev[3], prev[4],
                                              post_g[i - 1:i], post_b[i - 1:i], False, dep=pending[-1])
        else:
            after = pending[-1]
            for done in range(DEPTH - 1, 1, -1):
                after = reduce_layer(done, after)
            grad_x = _mm_dh(tag + "_dh", dproj, w_in, 0, dr, dep=finish_pair(pending, after))

    after = reduce_layer(1, grad_x)
    for i in range(DEPTH - 2, 0, -1):
        after = finish_swap(i, after, True)
    finish_swap(DEPTH - 1, after, False)

    width = SMALL_WIDTH
    gathered, _ = _split_wait("small_wait", _small_copies, small[0], small[1], small[2], [], after)
    summed = _sum_devices("sum_small", gathered[0])
    full_small_shapes = {"a_ln_g": (n_a, e), "a_ln_b": (n_a, e), "a_w_s": w["a_w_s"].shape, "a_b_s": w["a_b_s"].shape,
                         "b_scale": w["b_scale"].shape, "c_conv_w": (n_c, CONV_WIDTH, e), "c_conv_b": (n_c, e),
                         "c_ln_g": (n_c, e), "c_ln_b": (n_c, e), "post_ln_g": post_g.shape, "post_ln_b": post_b.shape}
    for k, g in zip(SMALL_ORDER, _unpack(summed, [full_small_shapes[k] for k in SMALL_ORDER], width)):
        if k in SMALL_SHARDED:
            g = lax.dynamic_slice_in_dim(g, j_me * es, es, axis=g.ndim - 1)
        grads[k] = g
        after = adam(k)

    finish_swap(0, reduce_layer(0, after), True)

    loss = lax.psum(loss_vec[0, 0], ("x", "y", "c"))
    return loss, grad_x, grads, delta, new_m, new_v


def kernel(x, a_w_in, a_ln_g, a_ln_b, a_w_s, a_b_s, a_w_out, b_w_in, b_w_pool, b_scale, b_w_out, c_w_in, c_conv_w, c_conv_b, c_ln_g, c_ln_b, c_w_out, post_ln_g, post_ln_b, loss_target, m_a_w_in, m_a_ln_g, m_a_ln_b, m_a_w_s, m_a_b_s, m_a_w_out, m_b_w_in, m_b_w_pool, m_b_scale, m_b_w_out, m_c_w_in, m_c_conv_w, m_c_conv_b, m_c_ln_g, m_c_ln_b, m_c_w_out, m_post_ln_g, m_post_ln_b, v_a_w_in, v_a_ln_g, v_a_ln_b, v_a_w_s, v_a_b_s, v_a_w_out, v_b_w_in, v_b_w_pool, v_b_scale, v_b_w_out, v_c_w_in, v_c_conv_w, v_c_conv_b, v_c_ln_g, v_c_ln_b, v_c_w_out, v_post_ln_g, v_post_ln_b):
    args = locals()
    w = {k: args[k] for k in WEIGHTS}
    m = {k: args["m_" + k] for k in WEIGHTS}
    v = {k: args["v_" + k] for k in WEIGHTS}
    loss, grad_x, grads, delta, new_m, new_v = _step(x[0], loss_target[0], w, m, v)
    out = [loss, grad_x[None]]
    for group in (grads, delta, new_m, new_v):
        out.extend(group[k] for k in WEIGHTS)
    return tuple(out)
```
